```python
import math
import jax, jax.numpy as jnp
from jax import lax
import numpy as np

D_MODEL = 2048
BATCH = 8
SEQ = 4096
DEPTH = 1
DEC_BATCH = 32
DEC_SEQ = 16
PAST_LEN = 4096

CHUNK = 64
Q_BLOCK = 128
A_HEADS = D_MODEL // 256
A_HEAD_DIM = 64
A_V_DIM = 2 * A_HEAD_DIM
A_WIDTH = A_HEADS * A_V_DIM
A_QK_WIDTH = A_HEADS * 2 * A_HEAD_DIM
ROT_DIM = A_HEAD_DIM // 4
ROPE_THETA = 500000.0
B_HEADS = D_MODEL // 256
B_KEY_DIM = 128
B_VAL_DIM = 128
B_KEY_WIDTH = B_HEADS * B_KEY_DIM
B_WIDTH = B_HEADS * B_VAL_DIM
N_MEM = 256
C_HEADS = 4
C_HEAD_DIM = D_MODEL // 8
C_WIDTH = C_HEADS * C_HEAD_DIM
DN_ALPHA = (2 * DEPTH) ** 0.25
DN_BETA = (8 * DEPTH) ** -0.25
NORM_EPS = 1e-5
IN_WIDTHS = (A_QK_WIDTH, A_QK_WIDTH, A_WIDTH, A_WIDTH,
             B_KEY_WIDTH, B_KEY_WIDTH, B_WIDTH, B_WIDTH, B_WIDTH,
             C_WIDTH, C_WIDTH,
             D_MODEL, D_MODEL, D_MODEL)
N_IN = sum(IN_WIDTHS)

kernel_name = 'stream_diffattn_hgrn2_mem_step'


def _rms_norm(x, g):
    xf = x.astype(jnp.float32)
    return xf * lax.rsqrt(jnp.mean(xf * xf, axis=-1, keepdims=True) + NORM_EPS) * g.astype(jnp.float32)


def _layer_norm(x, g, b):
    xf = x.astype(jnp.float32)
    mu = jnp.mean(xf, axis=-1, keepdims=True)
    var = jnp.mean(jnp.square(xf - mu), axis=-1, keepdims=True)
    return (xf - mu) * lax.rsqrt(var + NORM_EPS) * g.astype(jnp.float32) + b.astype(jnp.float32)


def _rope_partial(x, pos):
    inv_freq = jnp.power(ROPE_THETA, -jnp.arange(0, ROT_DIM, 2, dtype=jnp.float32) / ROT_DIM)
    ang = pos.astype(jnp.float32)[:, None] * inv_freq[None, :]
    cos = jnp.cos(ang)[:, None, None, :]
    sin = jnp.sin(ang)[:, None, None, :]
    xr = x[..., :ROT_DIM].astype(jnp.float32)
    x1, x2 = xr[..., :ROT_DIM // 2], xr[..., ROT_DIM // 2:]
    rot = jnp.concatenate([x1 * cos - x2 * sin, x2 * cos + x1 * sin], axis=-1)
    return jnp.concatenate([rot.astype(x.dtype), x[..., ROT_DIM:]], axis=-1)


def _diff_weights(s, lam):
    p = jax.nn.softmax(s, axis=-1)
    return p[:, :, 0] - lam * p[:, :, 1]


def _diff_attn_prompt(q, k, v, lam):
    n_b, seq = q.shape[:2]
    n_blk = seq // Q_BLOCK
    q_blocks = jnp.moveaxis(q.reshape(n_b, n_blk, Q_BLOCK, A_HEADS, 2, A_HEAD_DIM), 1, 0)
    key_chunk = jnp.arange(seq) // CHUNK
    vf = v.astype(jnp.float32)
    scale = A_HEAD_DIM ** -0.5

    def one_block(args):
        qb, start = args
        q_chunk = (start + jnp.arange(Q_BLOCK)) // CHUNK
        s = jnp.einsum('bqhmd,bkhmd->bhmqk', qb, k).astype(jnp.float32) * scale
        allowed = key_chunk[None, :] <= q_chunk[:, None]
        s = jnp.where(allowed, s, -jnp.inf)
        w = _diff_weights(s, lam)
        return jnp.einsum('bhqk,bkhe->bqhe', w, vf)

    o = lax.map(one_block, (q_blocks, jnp.arange(n_blk) * Q_BLOCK))
    return jnp.moveaxis(o, 0, 1).reshape(n_b, seq, A_HEADS, A_V_DIM)


def _diff_attn_sample(q, k_all, v_all, lam):
    s = jnp.einsum('bqhmd,bkhmd->bhmqk', q, k_all).astype(jnp.float32) * (A_HEAD_DIM ** -0.5)
    w = _diff_weights(s, lam)
    return jnp.einsum('bhqk,bkhe->bqhe', w, v_all.astype(jnp.float32))


def _hgrn2_chunkwise(q, k, v, g, s0, blk):
    n_b, t = q.shape[:2]
    n_c = t // blk

    def to_chunks(a):
        return a.reshape(n_b, n_c, blk, B_HEADS, a.shape[-1]).transpose(1, 0, 3, 2, 4)

    qc, kc, vc, gc = to_chunks(q), to_chunks(k), to_chunks(v), to_chunks(g)
    b = jnp.cumsum(gc, axis=3)
    mid = (blk - 1) // 2
    b_mid = b[:, :, :, mid:mid + 1, :]
    b_last = b[:, :, :, -1:, :]
    scores = jnp.einsum('cnhtd,cnhsd->cnhts', qc * jnp.exp(b - b_mid), kc * jnp.exp(b_mid - b))
    causal = jnp.tril(jnp.ones((blk, blk), dtype=bool))
    o_intra = jnp.einsum('cnhts,cnhse->cnhte', jnp.where(causal, scores, 0.0), vc)
    q_from_state = qc * jnp.exp(b)
    k_to_state = kc * jnp.exp(b_last - b)
    chunk_decay = jnp.exp(b_last[:, :, :, 0, :])

    def step(state, inp):
        q_i, k_i, v_i, d_i = inp
        o_i = jnp.einsum('nhtd,nhde->nhte', q_i, state)
        state = state * d_i[..., None] + jnp.einsum('nhtd,nhte->nhde', k_i, v_i)
        return state, o_i

    s_fin, o_inter = lax.scan(step, s0, (q_from_state, k_to_state, vc, chunk_decay))
    o = (o_intra + o_inter).transpose(1, 0, 3, 2, 4).reshape(n_b, t, B_HEADS, v.shape[-1])
    return o, s_fin


def _mem_attn(q, mem_k, mem_v):
    s = jnp.einsum('bthd,bmhd->bhtm', q, mem_k).astype(jnp.float32) * (C_HEAD_DIM ** -0.5)
    p = jax.nn.softmax(s, axis=-1)
    return jnp.einsum('bhtm,bmhd->bthd', p, mem_v.astype(jnp.float32))


def _layer(x, pos, past_k, past_v, hgrn_s0, mem_k, mem_v, rec_block, layer_idx, lower_bound, params):
    (w_in, lq1, lk1, lq2, lk2, sub_norm, hgrn_gain, w_a, w_b, w_c, w_o, ln_g, ln_b) = params
    f32 = jnp.float32
    n_b, t, _ = x.shape
    split_at = [int(c) for c in np.cumsum(IN_WIDTHS)[:-1]]
    (qa, ka, va, za, qb, fb, ib, ogb, zb, qc, zc,
     gate_a, gate_b, gate_c) = jnp.split(x @ w_in, split_at, axis=-1)

    qa = _rope_partial(qa.reshape(n_b, t, A_HEADS, 2, A_HEAD_DIM), pos)
    ka = _rope_partial(ka.reshape(n_b, t, A_HEADS, 2, A_HEAD_DIM), pos)
    va = va.reshape(n_b, t, A_HEADS, A_V_DIM)
    lam_init = 0.8 - 0.6 * math.exp(-0.3 * layer_idx)
    lam = (jnp.exp(jnp.sum(lq1.astype(f32) * lk1.astype(f32)))
           - jnp.exp(jnp.sum(lq2.astype(f32) * lk2.astype(f32))) + lam_init)
    if past_k is None:
        oa = _diff_attn_prompt(qa, ka, va, lam)
    else:
        oa = _diff_attn_sample(qa, jnp.concatenate([past_k, ka], axis=1),
                               jnp.concatenate([past_v, va], axis=1), lam)
    ya = (_rms_norm(oa, sub_norm) * (1.0 - lam_init)).reshape(n_b, t, A_WIDTH) * jax.nn.silu(za.astype(f32))

    fgate = (lower_bound + (1.0 - lower_bound) * jax.nn.sigmoid(fb.astype(f32))).reshape(n_b, t, B_HEADS, B_KEY_DIM)
    qh = jax.nn.silu(qb.astype(f32)).reshape(n_b, t, B_HEADS, B_KEY_DIM)
    vh = ib.astype(f32).reshape(n_b, t, B_HEADS, B_VAL_DIM)
    ob, s_fin = _hgrn2_chunkwise(qh, 1.0 - fgate, vh, jnp.log(fgate), hgrn_s0.astype(f32), rec_block)
    ob = _rms_norm(ob, hgrn_gain.reshape(B_HEADS, B_VAL_DIM)) * jax.nn.sigmoid(
        ogb.astype(f32).reshape(n_b, t, B_HEADS, B_VAL_DIM))
    yb = ob.reshape(n_b, t, B_WIDTH) * jax.nn.silu(zb.astype(f32))

    oc = _mem_attn(qc.reshape(n_b, t, C_HEADS, C_HEAD_DIM), mem_k, mem_v)
    yc = oc.reshape(n_b, t, C_WIDTH) * jax.nn.silu(zc.astype(f32))

    merged = (jax.nn.sigmoid(gate_a.astype(f32)) * (ya @ w_a.astype(f32))
              + jax.nn.sigmoid(gate_b.astype(f32)) * (yb @ w_b.astype(f32))
              + jax.nn.sigmoid(gate_c.astype(f32)) * (yc @ w_c.astype(f32)))
    sub = merged @ w_o.astype(f32)
    y = _layer_norm(DN_ALPHA * x.astype(f32) + sub, ln_g, ln_b).astype(x.dtype)
    return y, ka, va, s_fin.astype(x.dtype)


def setup_inputs(seed: int = 0) -> dict:
    key = jax.random.key(seed)
    ks = jax.random.split(key, 26)

    def nrm(k, shape, s):
        return jax.random.normal(k, shape, jnp.float32) * s

    return {
        'x_prompt': nrm(ks[0], (BATCH, SEQ, D_MODEL), 1.0),
        'x_sample': nrm(ks[1], (DEC_BATCH, DEC_SEQ, D_MODEL), 1.0),
        'cache_attn_k': nrm(ks[2], (DEPTH, DEC_BATCH, PAST_LEN, A_HEADS, 2, A_HEAD_DIM), 1.0),
        'cache_attn_v': nrm(ks[3], (DEPTH, DEC_BATCH, PAST_LEN, A_HEADS, A_V_DIM), 1.0),
        'state_hgrn': nrm(ks[4], (DEPTH, DEC_BATCH, B_HEADS, B_KEY_DIM, B_VAL_DIM), 0.3),
        'cache_mem_k': nrm(ks[5], (DEPTH, DEC_BATCH, N_MEM, C_HEADS, C_HEAD_DIM), 1.0),
        'cache_mem_v': nrm(ks[6], (DEPTH, DEC_BATCH, N_MEM, C_HEADS, C_HEAD_DIM), 1.0),
        'mem_prompt': nrm(ks[7], (BATCH, N_MEM, D_MODEL), 1.0),
        'w_in': nrm(ks[8], (DEPTH, D_MODEL, N_IN), D_MODEL ** -0.5),
        'lambda_q1': nrm(ks[9], (DEPTH, A_HEAD_DIM), 0.1),
        'lambda_k1': nrm(ks[10], (DEPTH, A_HEAD_DIM), 0.1),
        'lambda_q2': nrm(ks[11], (DEPTH, A_HEAD_DIM), 0.1),
        'lambda_k2': nrm(ks[12], (DEPTH, A_HEAD_DIM), 0.1),
        'attn_sub_norm': 1.0 + nrm(ks[13], (DEPTH, A_V_DIM), 0.02),
        'hgrn_lb_logits': nrm(ks[14], (DEPTH + 1, B_KEY_WIDTH), 0.5),
        'hgrn_norm': 1.0 + nrm(ks[15], (DEPTH, B_WIDTH), 0.02),
        'w_mem_k': nrm(ks[16], (DEPTH, D_MODEL, C_WIDTH), D_MODEL ** -0.5),
        'w_mem_v': nrm(ks[17], (DEPTH, D_MODEL, C_WIDTH), D_MODEL ** -0.5),
        'w_branch_a': nrm(ks[18], (DEPTH, A_WIDTH, D_MODEL), A_WIDTH ** -0.5 * DN_BETA),
        'w_branch_b': nrm(ks[19], (DEPTH, B_WIDTH, D_MODEL), B_WIDTH ** -0.5 * DN_BETA),
        'w_branch_c': nrm(ks[20], (DEPTH, C_WIDTH, D_MODEL), C_WIDTH ** -0.5 * DN_BETA),
        'w_out': nrm(ks[21], (DEPTH, D_MODEL, D_MODEL), D_MODEL ** -0.5 * DN_BETA),
        'ln_gamma': 1.0 + nrm(ks[22], (DEPTH, D_MODEL), 0.02),
        'ln_beta': nrm(ks[23], (DEPTH, D_MODEL), 0.02),
    }


def reference(x_prompt, x_sample, cache_attn_k, cache_attn_v, state_hgrn, cache_mem_k, cache_mem_v,
              mem_prompt, w_in, lambda_q1, lambda_k1, lambda_q2, lambda_k2, attn_sub_norm,
              hgrn_lb_logits, hgrn_norm, w_mem_k, w_mem_v, w_branch_a, w_branch_b, w_branch_c,
              w_out, ln_gamma, ln_beta):
    bp, seq = x_prompt.shape[:2]
    t_new = x_sample.shape[1]
    past = cache_attn_k.shape[2]
    pos_prompt = jnp.arange(seq)
    pos_sample = past + jnp.arange(t_new)
    lower_bounds = jnp.cumsum(jax.nn.softmax(hgrn_lb_logits.astype(jnp.float32), axis=0), axis=0)

    h_p, h_s = x_prompt, x_sample
    kp_l, vp_l, sp_l, mkp_l, mvp_l, ks_l, vs_l, ss_l = [], [], [], [], [], [], [], []
    for l in range(DEPTH):
        params = (w_in[l], lambda_q1[l], lambda_k1[l], lambda_q2[l], lambda_k2[l], attn_sub_norm[l],
                  hgrn_norm[l], w_branch_a[l], w_branch_b[l], w_branch_c[l], w_out[l],
                  ln_gamma[l], ln_beta[l])
        mk_p = (mem_prompt @ w_mem_k[l]).reshape(bp, N_MEM, C_HEADS, C_HEAD_DIM)
        mv_p = (mem_prompt @ w_mem_v[l]).reshape(bp, N_MEM, C_HEADS, C_HEAD_DIM)
        s0_p = jnp.zeros((bp, B_HEADS, B_KEY_DIM, B_VAL_DIM), jnp.float32)
        h_p, k_p, v_p, s_p = _layer(h_p, pos_prompt, None, None, s0_p, mk_p, mv_p, CHUNK, l,
                                    lower_bounds[l], params)
        h_s, k_s, v_s, s_s = _layer(h_s, pos_sample, cache_attn_k[l], cache_attn_v[l], state_hgrn[l],
                                    cache_mem_k[l], cache_mem_v[l], t_new, l, lower_bounds[l], params)
        kp_l.append(k_p); vp_l.append(v_p); sp_l.append(s_p); mkp_l.append(mk_p); mvp_l.append(mv_p)
        ks_l.append(k_s); vs_l.append(v_s); ss_l.append(s_s)

    k_prompt = jnp.stack(kp_l)
    v_prompt = jnp.stack(vp_l)
    hgrn_prompt = jnp.stack(sp_l)
    mem_k_prompt = jnp.stack(mkp_l)
    mem_v_prompt = jnp.stack(mvp_l)
    k_sample = jnp.stack(ks_l)
    v_sample = jnp.stack(vs_l)
    hgrn_sample = jnp.stack(ss_l)
    return (h_p, h_s, k_prompt, v_prompt, hgrn_prompt, mem_k_prompt, mem_v_prompt,
            k_sample, v_sample, hgrn_sample)
```

```python
import functools
import math

import jax
import jax.numpy as jnp
from jax import lax
from jax.experimental import pallas as pl
from jax.experimental.pallas import tpu as pltpu

F32 = jnp.float32
BF16 = jnp.bfloat16

CHUNK = 64
A_HEAD_DIM = 64
A_V_DIM = 128
ROT_DIM = 16
ROPE_THETA = 500000.0
B_KEY_DIM = 128
C_HEAD_DIM = 256
NORM_EPS = 1e-5
LANES = 128
VMEM_LIMIT = 56 * 1024 * 1024


def _cparams(sem):
    return pltpu.CompilerParams(dimension_semantics=sem, vmem_limit_bytes=VMEM_LIMIT)


def _silu(z):
    return z * jax.nn.sigmoid(z)


def _rope_tables(pos):
    half = ROT_DIM // 2
    inv_freq = jnp.power(ROPE_THETA, -jnp.arange(0, ROT_DIM, 2, dtype=F32) / ROT_DIM)
    ang = pos.astype(F32)[:, None] * inv_freq[None, :]
    cos, sin = jnp.cos(ang), jnp.sin(ang)
    t = pos.shape[0]
    pad = jnp.zeros((t, A_HEAD_DIM - ROT_DIM), F32)
    zero = jnp.zeros((t, half), F32)
    c64 = jnp.concatenate([cos, cos, pad + 1.0], axis=1)
    sa64 = jnp.concatenate([-sin, zero, pad], axis=1)
    sb64 = jnp.concatenate([zero, sin, pad], axis=1)
    rep = LANES // A_HEAD_DIM
    return tuple(jnp.tile(a, (1, rep)) for a in (c64, sa64, sb64))


def _inproj_kernel(x_ref, w_ref, c_ref, sa_ref, sb_ref, q_ref, k_ref, v_ref, r_ref, xb_ref,
                   *, tn, nq, nk, nv, q_scale):
    j = pl.program_id(1)

    @pl.when(j == 0)
    def _():
        xb_ref[...] = x_ref[...].astype(BF16)

    acc = jnp.dot(xb_ref[...], w_ref[...], preferred_element_type=F32)

    def rope_group(g):
        xg = acc[:, g * LANES:(g + 1) * LANES]
        up = pltpu.roll(xg, LANES - ROT_DIM // 2, 1)
        dn = pltpu.roll(xg, ROT_DIM // 2, 1)
        return xg * c_ref[...] + up * sa_ref[...] + dn * sb_ref[...]

    @pl.when(j < nq)
    def _():
        for g in range(tn // LANES):
            q_ref[:, g * LANES:(g + 1) * LANES] = (rope_group(g) * q_scale).astype(BF16)

    @pl.when((j >= nq) & (j < nq + nk))
    def _():
        for g in range(tn // LANES):
            k_ref[:, g * LANES:(g + 1) * LANES] = rope_group(g)

    @pl.when((j >= nq + nk) & (j < nq + nk + nv))
    def _():
        v_ref[...] = acc

    @pl.when(j >= nq + nk + nv)
    def _():
        r_ref[...] = acc.astype(BF16)


def _inproj(x2, w_bf, tables, t_len, a_width):
    n, d = x2.shape
    n_in = w_bf.shape[1]
    tm = min(n, 1024)
    tn = 512
    assert n % tm == 0 and a_width % tn == 0 and n_in % tn == 0
    if t_len >= tm:
        assert t_len % tm == 0
        tabs = tables
        per = t_len // tm
        tab_map = lambda i, j: (i % per, 0)
    else:
        assert tm % t_len == 0
        tabs = tuple(jnp.tile(a, (tm // t_len, 1)) for a in tables)
        tab_map = lambda i, j: (0, 0)
    nq = nk = nv = a_width // tn
    nr = n_in // tn - 3 * nq
    tab_spec = pl.BlockSpec((tm, LANES), tab_map)
    kern = functools.partial(_inproj_kernel, tn=tn, nq=nq, nk=nk, nv=nv, q_scale=A_HEAD_DIM ** -0.5)
    return pl.pallas_call(
        kern,
        grid=(n // tm, n_in // tn),
        in_specs=[
            pl.BlockSpec((tm, d), lambda i, j: (i, 0)),
            pl.BlockSpec((d, tn), lambda i, j: (0, j)),
            tab_spec, tab_spec, tab_spec,
        ],
        out_specs=[
            pl.BlockSpec((tm, tn), lambda i, j: (i, jnp.minimum(j, nq - 1))),
            pl.BlockSpec((tm, tn), lambda i, j: (i, jnp.clip(j - nq, 0, nk - 1))),
            pl.BlockSpec((tm, tn), lambda i, j: (i, jnp.clip(j - nq - nk, 0, nv - 1))),
            pl.BlockSpec((tm, tn), lambda i, j: (i, jnp.maximum(j - nq - nk - nv, 0))),
        ],
        out_shape=[
            jax.ShapeDtypeStruct((n, a_width), BF16),
            jax.ShapeDtypeStruct((n, a_width), F32),
            jax.ShapeDtypeStruct((n, a_width), F32),
            jax.ShapeDtypeStruct((n, nr * tn), BF16),
        ],
        scratch_shapes=[pltpu.VMEM((tm, d), BF16)],
        compiler_params=_cparams(("parallel", "arbitrary")),
        name="inproj",
    )(x2, w_bf, *tabs)


def _memkv_kernel(x_ref, wk_ref, wv_ref, k_ref, v_ref):
    xb = x_ref[...].astype(BF16)
    k_ref[...] = jnp.dot(xb, wk_ref[...], preferred_element_type=F32)
    v_ref[...] = jnp.dot(xb, wv_ref[...], preferred_element_type=F32)


def _memkv(x2, wk_bf, wv_bf):
    n, d = x2.shape
    c = wk_bf.shape[1]
    tm = min(n, 512)
    assert n % tm == 0
    wspec = pl.BlockSpec((d, c), lambda i: (0, 0))
    ospec = pl.BlockSpec((tm, c), lambda i: (i, 0))
    return pl.pallas_call(
        _memkv_kernel,
        grid=(n // tm,),
        in_specs=[pl.BlockSpec((tm, d), lambda i: (i, 0)), wspec, wspec],
        out_specs=[ospec, ospec],
        out_shape=[jax.ShapeDtypeStruct((n, c), F32)] * 2,
        compiler_params=_cparams(("parallel",)),
        name="memkv",
    )(x2, wk_bf, wv_bf)


def _attn_epilogue(o, lam_init, gain, z):
    ms = jnp.mean(o * o, axis=1, keepdims=True)
    y = o * lax.rsqrt(ms + NORM_EPS) * gain * (1.0 - lam_init)
    return y * _silu(z)


def _attn_prompt_kernel(lam_ref, q_ref, k_ref, v_ref, z_ref, g_ref, o_ref, kb_ref, vb_ref,
                        *, tq, lam_init):
    qi = pl.program_id(2)

    @pl.when(qi == 0)
    def _():
        kb_ref[...] = k_ref[...].astype(BF16)
        vb_ref[...] = v_ref[...].astype(BF16)

    qf = q_ref[...].astype(F32)
    lane = lax.broadcasted_iota(jnp.int32, qf.shape, 1)
    q1 = jnp.where(lane < A_HEAD_DIM, qf, 0.0).astype(BF16)
    q2 = jnp.where(lane >= A_HEAD_DIM, qf, 0.0).astype(BF16)
    nt = (((1,), (1,)), ((), ()))

    def update(s, vt, m, l, a):
        mn = jnp.maximum(m, jnp.max(s, axis=1, keepdims=True))
        al = jnp.exp(m - mn)
        p = jnp.exp(s - mn)
        l = al * l + jnp.sum(p, axis=1, keepdims=True)
        a = al * a + jnp.dot(p.astype(BF16), vt, preferred_element_type=F32)
        return mn, l, a

    def tile(j, carry, masked):
        m1, l1, a1, m2, l2, a2 = carry
        start = pl.multiple_of(j * tq, tq)
        kt = kb_ref[pl.ds(start, tq), :]
        vt = vb_ref[pl.ds(start, tq), :]
        s1 = lax.dot_general(q1, kt, nt, preferred_element_type=F32)
        s2 = lax.dot_general(q2, kt, nt, preferred_element_type=F32)
        if masked:
            rc = lax.broadcasted_iota(jnp.int32, (tq, tq), 0) // CHUNK
            cc = lax.broadcasted_iota(jnp.int32, (tq, tq), 1) // CHUNK
            ok = cc <= rc
            s1 = jnp.where(ok, s1, -jnp.inf)
            s2 = jnp.where(ok, s2, -jnp.inf)
        m1, l1, a1 = update(s1, vt, m1, l1, a1)
        m2, l2, a2 = update(s2, vt, m2, l2, a2)
        return m1, l1, a1, m2, l2, a2

    neg = jnp.full((tq, 1), -jnp.inf, F32)
    zero1 = jnp.zeros((tq, 1), F32)
    zacc = jnp.zeros((tq, A_V_DIM), F32)
    carry = (neg, zero1, zacc, neg, zero1, zacc)
    carry = lax.fori_loop(0, qi, lambda j, c: tile(j, c, False), carry)
    m1, l1, a1, m2, l2, a2 = tile(qi, carry, True)
    o = a1 / l1 - lam_ref[0] * (a2 / l2)
    y = _attn_epilogue(o, lam_init, g_ref[...], z_ref[...].astype(F32))
    o_ref[...] = y.astype(BF16)


def _attn_prompt(lam, q3, k3, v3, rest3, gain, lam_init):
    b, s, w = q3.shape
    h = w // A_V_DIM
    tq = min(s, 256)
    assert s % tq == 0 and tq % CHUNK == 0
    kern = functools.partial(_attn_prompt_kernel, tq=tq, lam_init=lam_init)
    qspec = pl.BlockSpec((None, tq, A_V_DIM), lambda bi, hi, qi: (bi, qi, hi))
    kvspec = pl.BlockSpec((None, s, A_V_DIM), lambda bi, hi, qi: (bi, 0, hi))
    return pl.pallas_call(
        kern,
        grid=(b, h, s // tq),
        in_specs=[
            pl.BlockSpec(memory_space=pltpu.SMEM),
            qspec, kvspec, kvspec, qspec,
            pl.BlockSpec((1, A_V_DIM), lambda bi, hi, qi: (0, 0)),
        ],
        out_specs=qspec,
        out_shape=jax.ShapeDtypeStruct((b, s, w), BF16),
        scratch_shapes=[pltpu.VMEM((s, A_V_DIM), BF16), pltpu.VMEM((s, A_V_DIM), BF16)],
        compiler_params=_cparams(("parallel", "parallel", "arbitrary")),
        name="attn_prompt",
    )(lam, q3, k3, v3, rest3, gain)


def _attn_sample_kernel(lam_ref, qbd_ref, kc_ref, vc_ref, kn_ref, vn_ref, z_ref, g_ref, o_ref,
                        s_ref, w_ref, acc_ref, *, tk, nkt, t_new, heads, lam_init):
    j = pl.program_id(1)
    past = tk * nkt
    nt = (((1,), (1,)), ((), ()))
    width = kn_ref.shape[1]

    def pad_new(ref):
        new = ref[...]
        return jnp.concatenate([new, jnp.zeros((LANES - t_new, width), F32)], axis=0).astype(BF16)

    @pl.when(j < nkt)
    def _():
        kt = kc_ref[...].astype(BF16)
        s = lax.dot_general(qbd_ref[...], kt, nt, preferred_element_type=F32)
        s_ref[:, pl.ds(pl.multiple_of(j * tk, tk), tk)] = s

    @pl.when(j == nkt - 1)
    def _():
        sn = lax.dot_general(qbd_ref[...], pad_new(kn_ref), nt, preferred_element_type=F32)
        col = lax.broadcasted_iota(jnp.int32, sn.shape, 1)
        s_ref[:, past:past + LANES] = jnp.where(col < t_new, sn, -jnp.inf)
        lam = lam_ref[0]
        for hd in range(heads):
            r0 = hd * 2 * t_new
            p = []
            for mp in range(2):
                s = s_ref[r0 + mp * t_new:r0 + (mp + 1) * t_new, :]
                e = jnp.exp(s - jnp.max(s, axis=1, keepdims=True))
                p.append(e / jnp.sum(e, axis=1, keepdims=True))
            w_ref[hd * t_new:(hd + 1) * t_new, :] = (p[0] - lam * p[1]).astype(BF16)

    @pl.when(j == nkt)
    def _():
        acc_ref[...] = jnp.dot(w_ref[:, past:past + LANES], pad_new(vn_ref),
                               preferred_element_type=F32)

    @pl.when(j >= nkt)
    def _():
        vt = vc_ref[...].astype(BF16)
        wt = w_ref[:, pl.ds(pl.multiple_of((j - nkt) * tk, tk), tk)]
        acc_ref[...] += jnp.dot(wt, vt, preferred_element_type=F32)

    @pl.when(j == 2 * nkt - 1)
    def _():
        for hd in range(heads):
            cs = slice(hd * A_V_DIM, (hd + 1) * A_V_DIM)
            o = acc_ref[hd * t_new:(hd + 1) * t_new, cs]
            y = _attn_epilogue(o, lam_init, g_ref[...], z_ref[:, cs].astype(F32))
            o_ref[:, cs] = y.astype(BF16)


def _attn_sample(lam, qbd, kc, vc, kn, vn, rest3, gain, lam_init):
    b, past, w = kc.shape
    t_new = kn.shape[1]
    heads = w // A_V_DIM
    rows = qbd.shape[1]
    tk = min(past, 512)
    assert past % tk == 0 and t_new % 16 == 0 and t_new <= LANES
    nkt = past // tk
    kern = functools.partial(_attn_sample_kernel, tk=tk, nkt=nkt, t_new=t_new, heads=heads,
                             lam_init=lam_init)
    newspec = pl.BlockSpec((None, t_new, w), lambda bi, j: (bi, 0, 0))
    return pl.pallas_call(
        kern,
        grid=(b, 2 * nkt),
        in_specs=[
            pl.BlockSpec(memory_space=pltpu.SMEM),
            pl.BlockSpec((None, rows, w), lambda bi, j: (bi, 0, 0)),
            pl.BlockSpec((None, tk, w), lambda bi, j: (bi, jnp.minimum(j, nkt - 1), 0)),
            pl.BlockSpec((None, tk, w), lambda bi, j: (bi, jnp.maximum(j - nkt, 0), 0)),
            newspec, newspec, newspec,
            pl.BlockSpec((1, A_V_DIM), lambda bi, j: (0, 0)),
        ],
        out_specs=newspec,
        out_shape=jax.ShapeDtypeStruct((b, t_new, w), BF16),
        scratch_shapes=[
            pltpu.VMEM((rows, past + LANES), F32),
            pltpu.VMEM((rows // 2, past + LANES), BF16),
            pltpu.VMEM((rows // 2, w), F32),
        ],
        compiler_params=_cparams(("parallel", "arbitrary")),
        name="attn_sample",
    )(lam, qbd, kc, vc, kn, vn, rest3, gain)


def _block_diag_queries(q3, heads):
    b, t, w = q3.shape
    hm = 2 * heads
    q5 = q3.reshape(b, t, hm, A_HEAD_DIM).transpose(0, 2, 1, 3)
    eye = jnp.eye(hm, dtype=q3.dtype)
    qbd = q5[:, :, :, None, :] * eye[None, :, None, :, None]
    return qbd.reshape(b, hm * t, w)


def _hgrn_kernel(*refs, blk, nchunk, has_s0):
    if has_s0:
        q_ref, f_ref, i_ref, og_ref, z_ref, lb_ref, gain_ref, s0_ref, y_ref, sf_ref, st_ref = refs
    else:
        q_ref, f_ref, i_ref, og_ref, z_ref, lb_ref, gain_ref, y_ref, sf_ref, st_ref = refs
    t = pl.program_id(2)

    @pl.when(t == 0)
    def _():
        st_ref[...] = s0_ref[...] if has_s0 else jnp.zeros_like(st_ref)

    lb = lb_ref[...]
    gain = gain_ref[...]
    row = lax.broadcasted_iota(jnp.int32, (blk, blk), 0)
    col = lax.broadcasted_iota(jnp.int32, (blk, blk), 1)
    causal = col <= row
    tril = jnp.where(causal, 1.0, 0.0).astype(BF16)
    ones = jnp.ones((blk, B_KEY_DIM), BF16)
    mid = (blk - 1) // 2
    nt = (((1,), (1,)), ((), ()))
    tn = (((0,), (0,)), ((), ()))

    def split3(g):
        hi = g.astype(BF16)
        r1 = g - hi.astype(F32)
        md = r1.astype(BF16)
        lo = (r1 - md.astype(F32)).astype(BF16)
        return hi, md, lo

    for c in range(nchunk):
        sl = slice(c * blk, (c + 1) * blk)
        f = lb + (1.0 - lb) * jax.nn.sigmoid(f_ref[sl, :].astype(F32))
        parts = split3(jnp.log(f))
        b = sum(jnp.dot(tril, p, preferred_element_type=F32) for p in parts)
        tot = sum(lax.dot_general(p, ones, tn, preferred_element_type=F32) for p in parts)
        b_mid = b[mid:mid + 1, :]
        b_last = b[blk - 1:blk, :]
        qv = _silu(q_ref[sl, :].astype(F32))
        kv = 1.0 - f
        vv = i_ref[sl, :].astype(BF16)
        qe = (qv * jnp.exp(b - b_mid)).astype(BF16)
        ke = (kv * jnp.exp(b_mid - b)).astype(BF16)
        sc = lax.dot_general(qe, ke, nt, preferred_element_type=F32)
        sc = jnp.where(causal, sc, 0.0).astype(BF16)
        state = st_ref[...]
        o = jnp.dot(sc, vv, preferred_element_type=F32)
        o = o + jnp.dot((qv * jnp.exp(b)).astype(BF16), state.astype(BF16),
                        preferred_element_type=F32)
        ks = (kv * jnp.exp(b_last - b)).astype(BF16)
        st_ref[...] = state * jnp.exp(tot) + lax.dot_general(ks, vv, tn, preferred_element_type=F32)
        ms = jnp.mean(o * o, axis=1, keepdims=True)
        y = o * lax.rsqrt(ms + NORM_EPS) * gain * jax.nn.sigmoid(og_ref[sl, :].astype(F32))
        y_ref[sl, :] = (y * _silu(z_ref[sl, :].astype(F32))).astype(BF16)

    @pl.when(t == pl.num_programs(2) - 1)
    def _():
        sf_ref[...] = st_ref[...]


def _hgrn(rest3, lb, gain, s0, blk, col0):
    b, t, _ = rest3.shape
    heads = lb.shape[1] // B_KEY_DIM
    tt = min(t, 512)
    assert t % tt == 0 and tt % blk == 0
    has_s0 = s0 is not None
    kern = functools.partial(_hgrn_kernel, blk=blk, nchunk=tt // blk, has_s0=has_s0)

    def colspec(k):
        return pl.BlockSpec((None, tt, B_KEY_DIM), lambda bi, hi, ti: (bi, ti, col0 + k * heads + hi))

    vecspec = pl.BlockSpec((1, B_KEY_DIM), lambda bi, hi, ti: (0, hi))
    stspec = pl.BlockSpec((None, None, B_KEY_DIM, B_KEY_DIM), lambda bi, hi, ti: (bi, hi, 0, 0))
    in_specs = [colspec(k) for k in range(5)] + [vecspec, vecspec]
    args = [rest3] * 5 + [lb, gain]
    if has_s0:
        in_specs.append(stspec)
        args.append(s0)
    return pl.pallas_call(
        kern,
        grid=(b, heads, t // tt),
        in_specs=in_specs,
        out_specs=[
            pl.BlockSpec((None, tt, B_KEY_DIM), lambda bi, hi, ti: (bi, ti, hi)),
            stspec,
        ],
        out_shape=[
            jax.ShapeDtypeStruct((b, t, heads * B_KEY_DIM), BF16),
            jax.ShapeDtypeStruct((b, heads, B_KEY_DIM, B_KEY_DIM), F32),
        ],
        scratch_shapes=[pltpu.VMEM((B_KEY_DIM, B_KEY_DIM), F32)],
        compiler_params=_cparams(("parallel", "parallel", "arbitrary")),
        name="hgrn",
    )(*args)


def _memattn_kernel(q_ref, z_ref, mk_ref, mv_ref, o_ref, kb_ref, vb_ref, *, heads):
    @pl.when(pl.program_id(1) == 0)
    def _():
        kb_ref[...] = mk_ref[...].astype(BF16)
        vb_ref[...] = mv_ref[...].astype(BF16)

    nt = (((1,), (1,)), ((), ()))
    for hd in range(heads):
        cs = slice(hd * C_HEAD_DIM, (hd + 1) * C_HEAD_DIM)
        s = lax.dot_general(q_ref[:, cs], kb_ref[:, cs], nt, preferred_element_type=F32)
        s = s * (C_HEAD_DIM ** -0.5)
        e = jnp.exp(s - jnp.max(s, axis=1, keepdims=True))
        l = jnp.sum(e, axis=1, keepdims=True)
        o = jnp.dot(e.astype(BF16), vb_ref[:, cs], preferred_element_type=F32) / l
        o_ref[:, cs] = (o * _silu(z_ref[:, cs].astype(F32))).astype(BF16)


def _memattn(rest3, mk3, mv3, qblock, zblock):
    b, t, _ = rest3.shape
    _, m, c = mk3.shape
    tq = min(t, 512)
    assert t % tq == 0
    kern = functools.partial(_memattn_kernel, heads=c // C_HEAD_DIM)
    memspec = pl.BlockSpec((None, m, c), lambda bi, ti: (bi, 0, 0))
    return pl.pallas_call(
        kern,
        grid=(b, t // tq),
        in_specs=[
            pl.BlockSpec((None, tq, c), lambda bi, ti: (bi, ti, qblock)),
            pl.BlockSpec((None, tq, c), lambda bi, ti: (bi, ti, zblock)),
            memspec, memspec,
        ],
        out_specs=pl.BlockSpec((None, tq, c), lambda bi, ti: (bi, ti, 0)),
        out_shape=jax.ShapeDtypeStruct((b, t, c), BF16),
        scratch_shapes=[pltpu.VMEM((m, c), BF16), pltpu.VMEM((m, c), BF16)],
        compiler_params=_cparams(("parallel", "arbitrary")),
        name="memattn",
    )(rest3, rest3, mk3, mv3)


def _merge_kernel(ya_ref, yb_ref, yc_ref, ga_ref, gb_ref, gc_ref, x_ref,
                  wa_ref, wb_ref, wc_ref, wo_ref, lng_ref, lnb_ref, o_ref, *, alpha):
    def branch(y_ref, w_ref, g_ref):
        return jax.nn.sigmoid(g_ref[...].astype(F32)) * jnp.dot(
            y_ref[...], w_ref[...], preferred_element_type=F32)

    merged = branch(ya_ref, wa_ref, ga_ref) + branch(yb_ref, wb_ref, gb_ref)
    merged = merged + branch(yc_ref, wc_ref, gc_ref)
    sub = jnp.dot(merged.astype(BF16), wo_ref[...], preferred_element_type=F32)
    hres = alpha * x_ref[...] + sub
    mu = jnp.mean(hres, axis=1, keepdims=True)
    cen = hres - mu
    var = jnp.mean(cen * cen, axis=1, keepdims=True)
    o_ref[...] = cen * lax.rsqrt(var + NORM_EPS) * lng_ref[...] + lnb_ref[...]


def _merge(ya, yb, yc, rest2, x2, wa, wb, wc, wo, lng, lnb, gate_block0, alpha):
    n, d = x2.shape
    w = ya.shape[1]
    tm = min(n, 256)
    assert n % tm == 0
    rows = lambda width, blk: pl.BlockSpec((tm, width), lambda i: (i, blk))
    const = lambda shape: pl.BlockSpec(shape, lambda i: (0, 0), pipeline_mode=pl.Buffered(1))
    return pl.pallas_call(
        functools.partial(_merge_kernel, alpha=alpha),
        grid=(n // tm,),
        in_specs=[
            rows(w, 0), rows(w, 0), rows(w, 0),
            rows(d, gate_block0), rows(d, gate_block0 + 1), rows(d, gate_block0 + 2),
            rows(d, 0),
            const((w, d)), const((w, d)), const((w, d)), const((d, d)),
            const((1, d)), const((1, d)),
        ],
        out_specs=rows(d, 0),
        out_shape=jax.ShapeDtypeStruct((n, d), F32),
        compiler_params=_cparams(("parallel",)),
        name="merge",
    )(ya, yb, yc, rest2, rest2, rest2, x2, wa, wb, wc, wo, lng, lnb)


def _layer(x, pos, past_k, past_v, s0, mk3, mv3, rec_block, layer_idx, lb, p):
    (w_in, lam, sub_norm, hgrn_gain, wa, wb, wc, wo, lng, lnb, alpha) = p
    n_b, t, d = x.shape
    a_width = wa.shape[0]
    heads = a_width // A_V_DIM
    x2 = x.reshape(n_b * t, d)
    q2, k2, v2, rest2 = _inproj(x2, w_in, _rope_tables(pos), t, a_width)
    r = rest2.shape[1]
    rest3 = rest2.reshape(n_b, t, r)
    q3 = q2.reshape(n_b, t, a_width)
    k3 = k2.reshape(n_b, t, a_width)
    v3 = v2.reshape(n_b, t, a_width)
    lam_init = 0.8 - 0.6 * math.exp(-0.3 * layer_idx)
    if past_k is None:
        ya = _attn_prompt(lam, q3, k3, v3, rest3, sub_norm, lam_init)
    else:
        ya = _attn_sample(lam, _block_diag_queries(q3, heads), past_k, past_v, k3, v3, rest3,
                          sub_norm, lam_init)
    yb, s_fin = _hgrn(rest3, lb, hgrn_gain, s0, rec_block, a_width // B_KEY_DIM)
    c_width = mk3.shape[2]
    c_off = a_width + 5 * lb.shape[1]
    assert c_off % c_width == 0
    yc = _memattn(rest3, mk3, mv3, c_off // c_width, c_off // c_width + 1)
    g_off = c_off + 2 * c_width
    assert g_off % d == 0
    y2 = _merge(ya.reshape(n_b * t, a_width), yb.reshape(n_b * t, -1), yc.reshape(n_b * t, c_width),
                rest2, x2, wa, wb, wc, wo, lng, lnb, g_off // d, alpha)
    return y2.reshape(n_b, t, d), k3, v3, s_fin


def kernel(x_prompt, x_sample, cache_attn_k, cache_attn_v, state_hgrn, cache_mem_k, cache_mem_v, mem_prompt, w_in, lambda_q1, lambda_k1, lambda_q2, lambda_k2, attn_sub_norm, hgrn_lb_logits, hgrn_norm, w_mem_k, w_mem_v, w_branch_a, w_branch_b, w_branch_c, w_out, ln_gamma, ln_beta):
    bp, seq, d = x_prompt.shape
    bs, t_new, _ = x_sample.shape
    depth = w_in.shape[0]
    past = cache_attn_k.shape[2]
    heads = cache_attn_k.shape[3]
    a_width = heads * A_V_DIM
    n_mem = mem_prompt.shape[1]
    c_heads = cache_mem_k.shape[3]
    c_width = c_heads * C_HEAD_DIM
    b_heads = state_hgrn.shape[2]
    alpha = (2 * depth) ** 0.25
    pos_prompt = jnp.arange(seq)
    pos_sample = past + jnp.arange(t_new)
    lower_bounds = jnp.cumsum(jax.nn.softmax(hgrn_lb_logits.astype(F32), axis=0), axis=0)

    h_p, h_s = x_prompt, x_sample
    outs = [[] for _ in range(8)]
    for l in range(depth):
        lam_init = 0.8 - 0.6 * math.exp(-0.3 * l)
        lam = (jnp.exp(jnp.sum(lambda_q1[l].astype(F32) * lambda_k1[l].astype(F32)))
               - jnp.exp(jnp.sum(lambda_q2[l].astype(F32) * lambda_k2[l].astype(F32))) + lam_init)
        params = (w_in[l].astype(BF16), lam.reshape(1), attn_sub_norm[l].reshape(1, -1),
                  hgrn_norm[l].reshape(1, -1), w_branch_a[l].astype(BF16), w_branch_b[l].astype(BF16),
                  w_branch_c[l].astype(BF16), w_out[l].astype(BF16), ln_gamma[l].reshape(1, -1),
                  ln_beta[l].reshape(1, -1), alpha)
        lb = lower_bounds[l].reshape(1, -1)
        mk_p, mv_p = _memkv(mem_prompt.reshape(bp * n_mem, d), w_mem_k[l].astype(BF16),
                            w_mem_v[l].astype(BF16))
        mk_p = mk_p.reshape(bp, n_mem, c_width)
        mv_p = mv_p.reshape(bp, n_mem, c_width)
        h_p, k_p, v_p, s_p = _layer(h_p, pos_prompt, None, None, None, mk_p, mv_p, CHUNK, l, lb, params)
        h_s, k_s, v_s, s_s = _layer(
            h_s, pos_sample,
            cache_attn_k[l].reshape(bs, past, a_width), cache_attn_v[l].reshape(bs, past, a_width),
            state_hgrn[l], cache_mem_k[l].reshape(bs, n_mem, c_width),
            cache_mem_v[l].reshape(bs, n_mem, c_width), t_new, l, lb, params)
        new = (k_p.reshape(bp, seq, heads, 2, A_HEAD_DIM), v_p.reshape(bp, seq, heads, A_V_DIM),
               s_p.astype(x_prompt.dtype), mk_p.reshape(bp, n_mem, c_heads, C_HEAD_DIM),
               mv_p.reshape(bp, n_mem, c_heads, C_HEAD_DIM),
               k_s.reshape(bs, t_new, heads, 2, A_HEAD_DIM), v_s.reshape(bs, t_new, heads, A_V_DIM),
               s_s.astype(x_sample.dtype))
        for acc, val in zip(outs, new):
            acc.append(val)
    return (h_p, h_s) + tuple(jnp.stack(o) for o in outs)
```

```python
import functools
import math

import jax
import jax.numpy as jnp
from jax import lax
from jax.experimental import pallas as pl
from jax.experimental.pallas import tpu as pltpu

F32 = jnp.float32
BF16 = jnp.bfloat16

CHUNK = 64
A_HEAD_DIM = 64
A_V_DIM = 128
ROT_DIM = 16
ROPE_THETA = 500000.0
B_KEY_DIM = 128
C_HEAD_DIM = 256
NORM_EPS = 1e-5
LANES = 128
VMEM_LIMIT = 56 * 1024 * 1024


def _cparams(sem):
    return pltpu.CompilerParams(dimension_semantics=sem, vmem_limit_bytes=VMEM_LIMIT)


def _silu(z):
    return z * jax.nn.sigmoid(z)


def _rope_tables(pos):
    half = ROT_DIM // 2
    inv_freq = jnp.power(ROPE_THETA, -jnp.arange(0, ROT_DIM, 2, dtype=F32) / ROT_DIM)
    ang = pos.astype(F32)[:, None] * inv_freq[None, :]
    cos, sin = jnp.cos(ang), jnp.sin(ang)
    t = pos.shape[0]
    pad = jnp.zeros((t, A_HEAD_DIM - ROT_DIM), F32)
    zero = jnp.zeros((t, half), F32)
    c64 = jnp.concatenate([cos, cos, pad + 1.0], axis=1)
    sa64 = jnp.concatenate([-sin, zero, pad], axis=1)
    sb64 = jnp.concatenate([zero, sin, pad], axis=1)
    rep = LANES // A_HEAD_DIM
    rows = tuple(jnp.tile(a, (1, rep)) for a in (c64, sa64, sb64))
    return rows, (cos.T, sin.T)


def _inproj_kernel(x_ref, w_ref, wkt_ref, c_ref, sa_ref, sb_ref, ct_ref, st_ref,
                   q_ref, k_ref, v_ref, r_ref, xb_ref, *, tn, nq, nk, nv, q_scale, k_transposed):
    j = pl.program_id(1)
    half = ROT_DIM // 2

    @pl.when(j == 0)
    def _():
        xb_ref[...] = x_ref[...].astype(BF16)

    def project():
        return jnp.dot(xb_ref[...], w_ref[...], preferred_element_type=F32)

    def rope_group(acc, g):
        xg = acc[:, g * LANES:(g + 1) * LANES]
        up = pltpu.roll(xg, LANES - half, 1)
        dn = pltpu.roll(xg, half, 1)
        return xg * c_ref[...] + up * sa_ref[...] + dn * sb_ref[...]

    @pl.when(j < nq)
    def _():
        acc = project()
        for g in range(tn // LANES):
            q_ref[:, g * LANES:(g + 1) * LANES] = (rope_group(acc, g) * q_scale).astype(BF16)

    @pl.when((j >= nq) & (j < nq + nk))
    def _():
        if k_transposed:
            kt = lax.dot_general(wkt_ref[...].astype(BF16), xb_ref[...], (((1,), (1,)), ((), ())),
                                 preferred_element_type=F32)
            k_ref[...] = kt
            cos, sin = ct_ref[...], st_ref[...]
            for g in range(tn // A_HEAD_DIM):
                r0 = g * A_HEAD_DIM
                lo = kt[r0:r0 + half, :]
                hi = kt[r0 + half:r0 + ROT_DIM, :]
                k_ref[r0:r0 + half, :] = lo * cos - hi * sin
                k_ref[r0 + half:r0 + ROT_DIM, :] = hi * cos + lo * sin
        else:
            acc = project()
            for g in range(tn // LANES):
                k_ref[:, g * LANES:(g + 1) * LANES] = rope_group(acc, g)

    @pl.when((j >= nq + nk) & (j < nq + nk + nv))
    def _():
        v_ref[...] = project()

    @pl.when(j >= nq + nk + nv)
    def _():
        r_ref[...] = project().astype(BF16)


def _inproj(x2, w_bf, wkt_bf, tables, t_len, a_width, q_scale, k_transposed):
    n, d = x2.shape
    n_in = w_bf.shape[1]
    tm = min(n, 1024)
    tn = 512
    assert n % tm == 0 and a_width % tn == 0 and n_in % tn == 0
    row_tabs, col_tabs = tables
    if t_len >= tm:
        assert t_len % tm == 0
        per = t_len // tm
    else:
        assert tm % t_len == 0 and not k_transposed
        row_tabs = tuple(jnp.tile(a, (tm // t_len, 1)) for a in row_tabs)
        col_tabs = tuple(jnp.tile(a, (1, tm // t_len)) for a in col_tabs)
        per = 1
    nq = nk = nv = a_width // tn
    nr = n_in // tn - 3 * nq
    kblk = lambda j: jnp.clip(j - nq, 0, nk - 1)
    row_spec = pl.BlockSpec((tm, LANES), lambda i, j: (i % per, 0))
    col_spec = pl.BlockSpec((ROT_DIM // 2, tm), lambda i, j: (0, i % per))
    if k_transposed:
        k_spec = pl.BlockSpec((None, tn, tm), lambda i, j: (i // per, kblk(j), i % per))
        k_shape = jax.ShapeDtypeStruct((n // t_len, a_width, t_len), F32)
    else:
        k_spec = pl.BlockSpec((tm, tn), lambda i, j: (i, kblk(j)))
        k_shape = jax.ShapeDtypeStruct((n, a_width), F32)
    kern = functools.partial(_inproj_kernel, tn=tn, nq=nq, nk=nk, nv=nv, q_scale=q_scale,
                             k_transposed=k_transposed)
    return pl.pallas_call(
        kern,
        grid=(n // tm, n_in // tn),
        in_specs=[
            pl.BlockSpec((tm, d), lambda i, j: (i, 0)),
            pl.BlockSpec((d, tn), lambda i, j: (0, j)),
            pl.BlockSpec((tn, d), lambda i, j: (kblk(j), 0)),
            row_spec, row_spec, row_spec, col_spec, col_spec,
        ],
        out_specs=[
            pl.BlockSpec((tm, tn), lambda i, j: (i, jnp.minimum(j, nq - 1))),
            k_spec,
            pl.BlockSpec((tm, tn), lambda i, j: (i, jnp.clip(j - nq - nk, 0, nv - 1))),
            pl.BlockSpec((tm, tn), lambda i, j: (i, jnp.maximum(j - nq - nk - nv, 0))),
        ],
        out_shape=[
            jax.ShapeDtypeStruct((n, a_width), BF16),
            k_shape,
            jax.ShapeDtypeStruct((n, a_width), F32),
            jax.ShapeDtypeStruct((n, nr * tn), BF16),
        ],
        scratch_shapes=[pltpu.VMEM((tm, d), BF16)],
        compiler_params=_cparams(("parallel", "arbitrary")),
        name="inproj",
    )(x2, w_bf, wkt_bf, *row_tabs, *col_tabs)


def _memkv_kernel(x_ref, wk_ref, wv_ref, k_ref, v_ref):
    xb = x_ref[...].astype(BF16)
    k_ref[...] = jnp.dot(xb, wk_ref[...], preferred_element_type=F32)
    v_ref[...] = jnp.dot(xb, wv_ref[...], preferred_element_type=F32)


def _memkv(x2, wk_bf, wv_bf):
    n, d = x2.shape
    c = wk_bf.shape[1]
    tm = min(n, 512)
    assert n % tm == 0
    wspec = pl.BlockSpec((d, c), lambda i: (0, 0))
    ospec = pl.BlockSpec((tm, c), lambda i: (i, 0))
    return pl.pallas_call(
        _memkv_kernel,
        grid=(n // tm,),
        in_specs=[pl.BlockSpec((tm, d), lambda i: (i, 0)), wspec, wspec],
        out_specs=[ospec, ospec],
        out_shape=[jax.ShapeDtypeStruct((n, c), F32)] * 2,
        compiler_params=_cparams(("parallel",)),
        name="memkv",
    )(x2, wk_bf, wv_bf)


def _attn_epilogue(o, lam_init, gain, z):
    ms = jnp.mean(o * o, axis=1, keepdims=True)
    y = o * lax.rsqrt(ms + NORM_EPS) * gain * (1.0 - lam_init)
    return y * _silu(z)


def _attn_prompt_kernel(lam_ref, q_ref, kt_ref, v_ref, z_ref, g_ref, o_ref,
                        kb_ref, vb_ref, s_ref, p_ref, acc_ref, m_ref, al_ref, *, tq, lam_init):
    s_len = q_ref.shape[0]
    strip = 32
    kb_ref[...] = kt_ref[...].astype(BF16)
    vb_ref[:, :A_V_DIM] = v_ref[...].astype(BF16)
    vb_ref[:, A_V_DIM:] = jnp.ones((s_len, A_V_DIM), BF16)
    lane = lax.broadcasted_iota(jnp.int32, (tq, A_V_DIM), 1)

    def tile(qa, j, masked):
        start = pl.multiple_of(j * tq, tq)
        kt = kb_ref[:, pl.ds(start, tq)]
        vt = vb_ref[pl.ds(start, tq), :]
        for mp in range(2):
            s_ref[mp] = jnp.dot(qa[mp], kt, preferred_element_type=F32)
        for mp in range(2):
            for r in range(0, tq, strip):
                rs = slice(r, r + strip)
                sv = s_ref[mp, rs, :]
                if masked:
                    cc = lax.broadcasted_iota(jnp.int32, (strip, tq), 1) // CHUNK
                    sv = jnp.where(cc <= r // CHUNK, sv, -jnp.inf)
                m_old = m_ref[mp, rs, :]
                mn = jnp.maximum(m_old, jnp.max(sv, axis=1, keepdims=True))
                m_ref[mp, rs, :] = mn
                al_ref[mp, rs, :] = jnp.exp2(m_old - mn)
                for c in range(tq // LANES):
                    cs = slice(c * LANES, (c + 1) * LANES)
                    p_ref[mp, rs, cs] = jnp.exp2(sv[:, cs] - mn).astype(BF16)
            pv = jnp.dot(p_ref[mp], vt, preferred_element_type=F32)
            for c in range(2):
                cs = slice(c * A_V_DIM, (c + 1) * A_V_DIM)
                acc_ref[mp, :, cs] = acc_ref[mp, :, cs] * al_ref[mp] + pv[:, cs]

    def q_tile(qi, _):
        rows = pl.ds(pl.multiple_of(qi * tq, tq), tq)
        qf = q_ref[rows, :].astype(F32)
        qa = (jnp.where(lane < A_HEAD_DIM, qf, 0.0).astype(BF16),
              jnp.where(lane >= A_HEAD_DIM, qf, 0.0).astype(BF16))
        m_ref[...] = jnp.full(m_ref.shape, -jnp.inf, F32)
        acc_ref[...] = jnp.zeros(acc_ref.shape, F32)
        lax.fori_loop(0, qi, lambda j, c: (tile(qa, j, False), c)[1], 0)
        tile(qa, qi, True)
        o = (acc_ref[0, :, :A_V_DIM] / acc_ref[0, :, A_V_DIM:]
             - lam_ref[0] * (acc_ref[1, :, :A_V_DIM] / acc_ref[1, :, A_V_DIM:]))
        y = _attn_epilogue(o, lam_init, g_ref[...], z_ref[rows, :].astype(F32))
        o_ref[rows, :] = y.astype(BF16)
        return 0

    lax.fori_loop(0, s_len // tq, q_tile, 0)


def _attn_prompt(lam, q3, kt3, v3, rest3, gain, lam_init):
    b, s, w = q3.shape
    h = w // A_V_DIM
    tq = min(s, 512)
    assert s % tq == 0 and tq % CHUNK == 0
    kern = functools.partial(_attn_prompt_kernel, tq=tq, lam_init=lam_init)
    qspec = pl.BlockSpec((None, s, A_V_DIM), lambda bi, hi: (bi, 0, hi))
    return pl.pallas_call(
        kern,
        grid=(b, h),
        in_specs=[
            pl.BlockSpec(memory_space=pltpu.SMEM),
            qspec,
            pl.BlockSpec((None, A_V_DIM, s), lambda bi, hi: (bi, hi, 0)),
            qspec, qspec,
            pl.BlockSpec((1, A_V_DIM), lambda bi, hi: (0, 0)),
        ],
        out_specs=qspec,
        out_shape=jax.ShapeDtypeStruct((b, s, w), BF16),
        scratch_shapes=[
            pltpu.VMEM((A_V_DIM, s), BF16),
            pltpu.VMEM((s, 2 * A_V_DIM), BF16),
            pltpu.VMEM((2, tq, tq), F32),
            pltpu.VMEM((2, tq, tq), BF16),
            pltpu.VMEM((2, tq, 2 * A_V_DIM), F32),
            pltpu.VMEM((2, tq, LANES), F32),
            pltpu.VMEM((2, tq, LANES), F32),
        ],
        compiler_params=_cparams(("parallel", "parallel")),
        name="attn_prompt",
    )(lam, q3, kt3, v3, rest3, gain)


def _attn_sample_kernel(lam_ref, qbd_ref, kc_ref, vc_ref, kn_ref, vn_ref, z_ref, g_ref, o_ref,
                        s_ref, w_ref, acc_ref, *, tk, nkt, t_new, heads, lam_init):
    j = pl.program_id(1)
    past = tk * nkt
    nt = (((1,), (1,)), ((), ()))
    width = kn_ref.shape[1]

    def pad_new(ref):
        new = ref[...]
        return jnp.concatenate([new, jnp.zeros((LANES - t_new, width), F32)], axis=0).astype(BF16)

    @pl.when(j < nkt)
    def _():
        s = jnp.dot(qbd_ref[...], kc_ref[...].astype(BF16), preferred_element_type=F32)
        s_ref[:, pl.ds(pl.multiple_of(j * tk, tk), tk)] = s

    @pl.when(j == nkt - 1)
    def _():
        sn = lax.dot_general(qbd_ref[...], pad_new(kn_ref), nt, preferred_element_type=F32)
        col = lax.broadcasted_iota(jnp.int32, sn.shape, 1)
        s_ref[:, past:past + LANES] = jnp.where(col < t_new, sn, -jnp.inf)
        lam = lam_ref[0]
        for hd in range(heads):
            r0 = hd * 2 * t_new
            p = []
            for mp in range(2):
                s = s_ref[r0 + mp * t_new:r0 + (mp + 1) * t_new, :]
                e = jnp.exp2(s - jnp.max(s, axis=1, keepdims=True))
                p.append(e / jnp.sum(e, axis=1, keepdims=True))
            w_ref[hd * t_new:(hd + 1) * t_new, :] = (p[0] - lam * p[1]).astype(BF16)

    @pl.when(j == nkt)
    def _():
        full = jnp.dot(w_ref[:, past:past + LANES], pad_new(vn_ref), preferred_element_type=F32)
        for hd in range(heads):
            rs = slice(hd * t_new, (hd + 1) * t_new)
            acc_ref[rs, :] = full[rs, hd * A_V_DIM:(hd + 1) * A_V_DIM]

    @pl.when(j >= nkt)
    def _():
        start = pl.multiple_of((j - nkt) * tk, tk)
        for hd in range(heads):
            rs = slice(hd * t_new, (hd + 1) * t_new)
            vh = vc_ref[:, hd, :].astype(BF16)
            acc_ref[rs, :] += jnp.dot(w_ref[rs, pl.ds(start, tk)], vh, preferred_element_type=F32)

    @pl.when(j == 2 * nkt - 1)
    def _():
        for hd in range(heads):
            cs = slice(hd * A_V_DIM, (hd + 1) * A_V_DIM)
            o = acc_ref[hd * t_new:(hd + 1) * t_new, :]
            y = _attn_epilogue(o, lam_init, g_ref[...], z_ref[:, cs].astype(F32))
            o_ref[:, cs] = y.astype(BF16)


def _attn_sample(lam, qbd, kct, vc4, kn, vn, rest3, gain, lam_init):
    b, w, past = kct.shape
    t_new = kn.shape[1]
    heads = w // A_V_DIM
    rows = qbd.shape[1]
    tk = min(past, 512)
    assert past % tk == 0 and t_new % 16 == 0 and t_new <= LANES
    nkt = past // tk
    kern = functools.partial(_attn_sample_kernel, tk=tk, nkt=nkt, t_new=t_new, heads=heads,
                             lam_init=lam_init)
    newspec = pl.BlockSpec((None, t_new, w), lambda bi, j: (bi, 0, 0))
    return pl.pallas_call(
        kern,
        grid=(b, 2 * nkt),
        in_specs=[
            pl.BlockSpec(memory_space=pltpu.SMEM),
            pl.BlockSpec((None, rows, w), lambda bi, j: (bi, 0, 0)),
            pl.BlockSpec((None, w, tk), lambda bi, j: (bi, 0, jnp.minimum(j, nkt - 1))),
            pl.BlockSpec((None, tk, heads, A_V_DIM),
                         lambda bi, j: (bi, jnp.maximum(j - nkt, 0), 0, 0)),
            newspec, newspec, newspec,
            pl.BlockSpec((1, A_V_DIM), lambda bi, j: (0, 0)),
        ],
        out_specs=newspec,
        out_shape=jax.ShapeDtypeStruct((b, t_new, w), BF16),
        scratch_shapes=[
            pltpu.VMEM((rows, past + LANES), F32),
            pltpu.VMEM((rows // 2, past + LANES), BF16),
            pltpu.VMEM((rows // 2, A_V_DIM), F32),
        ],
        compiler_params=_cparams(("parallel", "arbitrary")),
        name="attn_sample",
    )(lam, qbd, kct, vc4, kn, vn, rest3, gain)


def _block_diag_queries(q3, heads):
    b, t, w = q3.shape
    hm = 2 * heads
    q5 = q3.reshape(b, t, hm, A_HEAD_DIM).transpose(0, 2, 1, 3)
    eye = jnp.eye(hm, dtype=q3.dtype)
    qbd = q5[:, :, :, None, :] * eye[None, :, None, :, None]
    return qbd.reshape(b, hm * t, w)


def _hgrn_kernel(*refs, blk, nchunk, has_s0):
    if has_s0:
        q_ref, f_ref, i_ref, og_ref, z_ref, lb_ref, gain_ref, s0_ref, y_ref, sf_ref, st_ref = refs
    else:
        q_ref, f_ref, i_ref, og_ref, z_ref, lb_ref, gain_ref, y_ref, sf_ref, st_ref = refs
    t = pl.program_id(2)

    @pl.when(t == 0)
    def _():
        st_ref[...] = s0_ref[...] if has_s0 else jnp.zeros_like(st_ref)

    lb = lb_ref[...]
    gain = gain_ref[...]
    row = lax.broadcasted_iota(jnp.int32, (blk, blk), 0)
    col = lax.broadcasted_iota(jnp.int32, (blk, blk), 1)
    causal = col <= row
    tril = jnp.where(causal, 1.0, 0.0).astype(BF16)
    ones = jnp.ones((blk, B_KEY_DIM), BF16)
    mid = (blk - 1) // 2
    nt = (((1,), (1,)), ((), ()))
    tn = (((0,), (0,)), ((), ()))

    def split3(g):
        hi = g.astype(BF16)
        r1 = g - hi.astype(F32)
        md = r1.astype(BF16)
        lo = (r1 - md.astype(F32)).astype(BF16)
        return hi, md, lo

    chunks = [slice(c * blk, (c + 1) * blk) for c in range(nchunk)]
    f = lb + (1.0 - lb) * jax.nn.sigmoid(f_ref[...].astype(F32))
    parts = split3(jnp.log(f))
    qv = _silu(q_ref[...].astype(F32))
    kv = 1.0 - f
    vv = i_ref[...]
    b = [sum(jnp.dot(tril, p[sl, :], preferred_element_type=F32) for p in parts) for sl in chunks]
    tot = [sum(lax.dot_general(p[sl, :], ones, tn, preferred_element_type=F32) for p in parts)
           for sl in chunks]
    qe, ke, qs, ks = [], [], [], []
    for sl, bc in zip(chunks, b):
        b_mid = bc[mid:mid + 1, :]
        b_last = bc[blk - 1:blk, :]
        qe.append((qv[sl, :] * jnp.exp(bc - b_mid)).astype(BF16))
        ke.append((kv[sl, :] * jnp.exp(b_mid - bc)).astype(BF16))
        qs.append((qv[sl, :] * jnp.exp(bc)).astype(BF16))
        ks.append((kv[sl, :] * jnp.exp(b_last - bc)).astype(BF16))
    sc = [lax.dot_general(a, k, nt, preferred_element_type=F32) for a, k in zip(qe, ke)]
    sc = [jnp.where(causal, s, 0.0).astype(BF16) for s in sc]
    o_intra = [jnp.dot(s, vv[sl, :], preferred_element_type=F32) for s, sl in zip(sc, chunks)]
    kvs = [lax.dot_general(k, vv[sl, :], tn, preferred_element_type=F32) for k, sl in zip(ks, chunks)]
    decay = [jnp.exp(tc) for tc in tot]
    state = st_ref[...]
    outs = []
    for c in range(nchunk):
        outs.append(o_intra[c] + jnp.dot(qs[c], state.astype(BF16), preferred_element_type=F32))
        state = state * decay[c] + kvs[c]
    st_ref[...] = state
    o = outs[0] if nchunk == 1 else jnp.concatenate(outs, axis=0)
    ms = jnp.mean(o * o, axis=1, keepdims=True)
    y = o * lax.rsqrt(ms + NORM_EPS) * gain * jax.nn.sigmoid(og_ref[...].astype(F32))
    y_ref[...] = (y * _silu(z_ref[...].astype(F32))).astype(BF16)

    @pl.when(t == pl.num_programs(2) - 1)
    def _():
        sf_ref[...] = st_ref[...]


def _hgrn(rest3, lb, gain, s0, blk, col0):
    b, t, _ = rest3.shape
    heads = lb.shape[1] // B_KEY_DIM
    tt = min(t, 512)
    assert t % tt == 0 and tt % blk == 0
    has_s0 = s0 is not None
    kern = functools.partial(_hgrn_kernel, blk=blk, nchunk=tt // blk, has_s0=has_s0)

    def colspec(k):
        return pl.BlockSpec((None, tt, B_KEY_DIM), lambda bi, hi, ti: (bi, ti, col0 + k * heads + hi))

    vecspec = pl.BlockSpec((1, B_KEY_DIM), lambda bi, hi, ti: (0, hi))
    stspec = pl.BlockSpec((None, None, B_KEY_DIM, B_KEY_DIM), lambda bi, hi, ti: (bi, hi, 0, 0))
    in_specs = [colspec(k) for k in range(5)] + [vecspec, vecspec]
    args = [rest3] * 5 + [lb, gain]
    if has_s0:
        in_specs.append(stspec)
        args.append(s0)
    return pl.pallas_call(
        kern,
        grid=(b, heads, t // tt),
        in_specs=in_specs,
        out_specs=[
            pl.BlockSpec((None, tt, B_KEY_DIM), lambda bi, hi, ti: (bi, ti, hi)),
            stspec,
        ],
        out_shape=[
            jax.ShapeDtypeStruct((b, t, heads * B_KEY_DIM), BF16),
            jax.ShapeDtypeStruct((b, heads, B_KEY_DIM, B_KEY_DIM), F32),
        ],
        scratch_shapes=[pltpu.VMEM((B_KEY_DIM, B_KEY_DIM), F32)],
        compiler_params=_cparams(("parallel", "parallel", "arbitrary")),
        name="hgrn",
    )(*args)


def _memattn_kernel(q_ref, z_ref, mk_ref, mv_ref, o_ref, kb_ref, vb_ref, *, heads):
    @pl.when(pl.program_id(1) == 0)
    def _():
        kb_ref[...] = mk_ref[...].astype(BF16)
        vb_ref[...] = mv_ref[...].astype(BF16)

    nt = (((1,), (1,)), ((), ()))
    for hd in range(heads):
        cs = slice(hd * C_HEAD_DIM, (hd + 1) * C_HEAD_DIM)
        s = lax.dot_general(q_ref[:, cs], kb_ref[:, cs], nt, preferred_element_type=F32)
        s = s * (C_HEAD_DIM ** -0.5)
        e = jnp.exp(s - jnp.max(s, axis=1, keepdims=True))
        l = jnp.sum(e, axis=1, keepdims=True)
        o = jnp.dot(e.astype(BF16), vb_ref[:, cs], preferred_element_type=F32) / l
        o_ref[:, cs] = (o * _silu(z_ref[:, cs].astype(F32))).astype(BF16)


def _memattn(rest3, mk3, mv3, qblock, zblock):
    b, t, _ = rest3.shape
    _, m, c = mk3.shape
    tq = min(t, 512)
    assert t % tq == 0
    kern = functools.partial(_memattn_kernel, heads=c // C_HEAD_DIM)
    memspec = pl.BlockSpec((None, m, c), lambda bi, ti: (bi, 0, 0))
    return pl.pallas_call(
        kern,
        grid=(b, t // tq),
        in_specs=[
            pl.BlockSpec((None, tq, c), lambda bi, ti: (bi, ti, qblock)),
            pl.BlockSpec((None, tq, c), lambda bi, ti: (bi, ti, zblock)),
            memspec, memspec,
        ],
        out_specs=pl.BlockSpec((None, tq, c), lambda bi, ti: (bi, ti, 0)),
        out_shape=jax.ShapeDtypeStruct((b, t, c), BF16),
        scratch_shapes=[pltpu.VMEM((m, c), BF16), pltpu.VMEM((m, c), BF16)],
        compiler_params=_cparams(("parallel", "arbitrary")),
        name="memattn",
    )(rest3, rest3, mk3, mv3)


def _merge_kernel(ya_ref, yb_ref, yc_ref, ga_ref, gb_ref, gc_ref, x_ref,
                  wa_ref, wb_ref, wc_ref, wo_ref, lng_ref, lnb_ref, o_ref, *, alpha):
    def branch(y_ref, w_ref, g_ref):
        return jax.nn.sigmoid(g_ref[...].astype(F32)) * jnp.dot(
            y_ref[...], w_ref[...], preferred_element_type=F32)

    merged = branch(ya_ref, wa_ref, ga_ref) + branch(yb_ref, wb_ref, gb_ref)
    merged = merged + branch(yc_ref, wc_ref, gc_ref)
    sub = jnp.dot(merged.astype(BF16), wo_ref[...], preferred_element_type=F32)
    hres = alpha * x_ref[...] + sub
    mu = jnp.mean(hres, axis=1, keepdims=True)
    cen = hres - mu
    var = jnp.mean(cen * cen, axis=1, keepdims=True)
    o_ref[...] = cen * lax.rsqrt(var + NORM_EPS) * lng_ref[...] + lnb_ref[...]


def _merge(ya, yb, yc, rest2, x2, wa, wb, wc, wo, lng, lnb, gate_block0, alpha):
    n, d = x2.shape
    w = ya.shape[1]
    tm = min(n, 256)
    assert n % tm == 0
    rows = lambda width, blk: pl.BlockSpec((tm, width), lambda i: (i, blk))
    const = lambda shape: pl.BlockSpec(shape, lambda i: (0, 0), pipeline_mode=pl.Buffered(1))
    return pl.pallas_call(
        functools.partial(_merge_kernel, alpha=alpha),
        grid=(n // tm,),
        in_specs=[
            rows(w, 0), rows(w, 0), rows(w, 0),
            rows(d, gate_block0), rows(d, gate_block0 + 1), rows(d, gate_block0 + 2),
            rows(d, 0),
            const((w, d)), const((w, d)), const((w, d)), const((d, d)),
            const((1, d)), const((1, d)),
        ],
        out_specs=rows(d, 0),
        out_shape=jax.ShapeDtypeStruct((n, d), F32),
        compiler_params=_cparams(("parallel",)),
        name="merge",
    )(ya, yb, yc, rest2, rest2, rest2, x2, wa, wb, wc, wo, lng, lnb)


def _layer(x, pos, past_k, past_v, s0, mk3, mv3, rec_block, layer_idx, lb, p):
    (w_in, wkt, lam, sub_norm, hgrn_gain, wa, wb, wc, wo, lng, lnb, alpha) = p
    n_b, t, d = x.shape
    a_width = wa.shape[0]
    heads = a_width // A_V_DIM
    x2 = x.reshape(n_b * t, d)
    prompt = past_k is None
    q_scale = A_HEAD_DIM ** -0.5 * math.log2(math.e)
    q2, k_out, v2, rest2 = _inproj(x2, w_in, wkt, _rope_tables(pos), t, a_width, q_scale,
                                   k_transposed=prompt)
    r = rest2.shape[1]
    rest3 = rest2.reshape(n_b, t, r)
    q3 = q2.reshape(n_b, t, a_width)
    v3 = v2.reshape(n_b, t, a_width)
    lam_init = 0.8 - 0.6 * math.exp(-0.3 * layer_idx)
    if prompt:
        ya = _attn_prompt(lam, q3, k_out, v3, rest3, sub_norm, lam_init)
        k5 = k_out.reshape(n_b, heads, 2, A_HEAD_DIM, t).transpose(0, 4, 1, 2, 3)
    else:
        k3 = k_out.reshape(n_b, t, a_width)
        ya = _attn_sample(lam, _block_diag_queries(q3, heads), past_k, past_v, k3, v3, rest3,
                          sub_norm, lam_init)
        k5 = k3.reshape(n_b, t, heads, 2, A_HEAD_DIM)
    yb, s_fin = _hgrn(rest3, lb, hgrn_gain, s0, rec_block, a_width // B_KEY_DIM)
    c_width = mk3.shape[2]
    c_off = a_width + 5 * lb.shape[1]
    assert c_off % c_width == 0
    yc = _memattn(rest3, mk3, mv3, c_off // c_width, c_off // c_width + 1)
    g_off = c_off + 2 * c_width
    assert g_off % d == 0
    y2 = _merge(ya.reshape(n_b * t, a_width), yb.reshape(n_b * t, -1), yc.reshape(n_b * t, c_width),
                rest2, x2, wa, wb, wc, wo, lng, lnb, g_off // d, alpha)
    return y2.reshape(n_b, t, d), k5, v3, s_fin


def kernel(x_prompt, x_sample, cache_attn_k, cache_attn_v, state_hgrn, cache_mem_k, cache_mem_v, mem_prompt, w_in, lambda_q1, lambda_k1, lambda_q2, lambda_k2, attn_sub_norm, hgrn_lb_logits, hgrn_norm, w_mem_k, w_mem_v, w_branch_a, w_branch_b, w_branch_c, w_out, ln_gamma, ln_beta):
    bp, seq, d = x_prompt.shape
    bs, t_new, _ = x_sample.shape
    depth = w_in.shape[0]
    past = cache_attn_k.shape[2]
    heads = cache_attn_k.shape[3]
    a_width = heads * A_V_DIM
    n_mem = mem_prompt.shape[1]
    c_heads = cache_mem_k.shape[3]
    c_width = c_heads * C_HEAD_DIM
    b_heads = state_hgrn.shape[2]
    alpha = (2 * depth) ** 0.25
    pos_prompt = jnp.arange(seq)
    pos_sample = past + jnp.arange(t_new)
    lower_bounds = jnp.cumsum(jax.nn.softmax(hgrn_lb_logits.astype(F32), axis=0), axis=0)

    h_p, h_s = x_prompt, x_sample
    outs = [[] for _ in range(8)]
    for l in range(depth):
        lam_init = 0.8 - 0.6 * math.exp(-0.3 * l)
        lam = (jnp.exp(jnp.sum(lambda_q1[l].astype(F32) * lambda_k1[l].astype(F32)))
               - jnp.exp(jnp.sum(lambda_q2[l].astype(F32) * lambda_k2[l].astype(F32))) + lam_init)
        wkt = w_in[l][:, a_width:2 * a_width].T
        params = (w_in[l].astype(BF16), wkt, lam.reshape(1), attn_sub_norm[l].reshape(1, -1),
                  hgrn_norm[l].reshape(1, -1), w_branch_a[l].astype(BF16), w_branch_b[l].astype(BF16),
                  w_branch_c[l].astype(BF16), w_out[l].astype(BF16), ln_gamma[l].reshape(1, -1),
                  ln_beta[l].reshape(1, -1), alpha)
        lb = lower_bounds[l].reshape(1, -1)
        mk_p, mv_p = _memkv(mem_prompt.reshape(bp * n_mem, d), w_mem_k[l].astype(BF16),
                            w_mem_v[l].astype(BF16))
        mk_p = mk_p.reshape(bp, n_mem, c_width)
        mv_p = mv_p.reshape(bp, n_mem, c_width)
        h_p, k_p, v_p, s_p = _layer(h_p, pos_prompt, None, None, None, mk_p, mv_p, CHUNK, l, lb, params)
        h_s, k_s, v_s, s_s = _layer(
            h_s, pos_sample,
            cache_attn_k[l].transpose(0, 2, 3, 4, 1).reshape(bs, a_width, past), cache_attn_v[l],
            state_hgrn[l], cache_mem_k[l].reshape(bs, n_mem, c_width),
            cache_mem_v[l].reshape(bs, n_mem, c_width), t_new, l, lb, params)
        new = (k_p, v_p.reshape(bp, seq, heads, A_V_DIM),
               s_p.astype(x_prompt.dtype), mk_p.reshape(bp, n_mem, c_heads, C_HEAD_DIM),
               mv_p.reshape(bp, n_mem, c_heads, C_HEAD_DIM),
               k_s, v_s.reshape(bs, t_new, heads, A_V_DIM),
               s_s.astype(x_sample.dtype))
        for acc, val in zip(outs, new):
            acc.append(val)
    return (h_p, h_s) + tuple(jnp.stack(o) for o in outs)
```

```python
import functools
import math

import jax
import jax.numpy as jnp
from jax import lax
from jax.experimental import pallas as pl
from jax.experimental.pallas import tpu as pltpu

F32 = jnp.float32
BF16 = jnp.bfloat16

CHUNK = 64
A_HEAD_DIM = 64
A_V_DIM = 128
ROT_DIM = 16
ROPE_THETA = 500000.0
B_KEY_DIM = 128
C_HEAD_DIM = 256
NORM_EPS = 1e-5
LANES = 128
VMEM_LIMIT = 56 * 1024 * 1024
ATTN_TILE = 512


def _cparams(sem):
    return pltpu.CompilerParams(dimension_semantics=sem, vmem_limit_bytes=VMEM_LIMIT)


def _silu(z):
    return z * jax.nn.sigmoid(z)


def _rope_tables(pos):
    half = ROT_DIM // 2
    inv_freq = jnp.power(ROPE_THETA, -jnp.arange(0, ROT_DIM, 2, dtype=F32) / ROT_DIM)
    ang = pos.astype(F32)[:, None] * inv_freq[None, :]
    cos, sin = jnp.cos(ang), jnp.sin(ang)
    t = pos.shape[0]
    pad = jnp.zeros((t, A_HEAD_DIM - ROT_DIM), F32)
    zero = jnp.zeros((t, half), F32)
    c64 = jnp.concatenate([cos, cos, pad + 1.0], axis=1)
    sa64 = jnp.concatenate([-sin, zero, pad], axis=1)
    sb64 = jnp.concatenate([zero, sin, pad], axis=1)
    rep = LANES // A_HEAD_DIM
    rows = tuple(jnp.tile(a, (1, rep)) for a in (c64, sa64, sb64))
    return rows, (cos.T, sin.T)


def _inproj_kernel(x_ref, w_ref, c_ref, sa_ref, sb_ref, ct_ref, st_ref,
                   q_ref, k_ref, v_ref, r_ref, xb_ref, t_ref, *, tn, nq, nk, nv, q_scale, k_transposed):
    j = pl.program_id(1)
    half = ROT_DIM // 2

    @pl.when(j == 0)
    def _():
        xb_ref[...] = x_ref[...].astype(BF16)

    def project():
        return jnp.dot(xb_ref[...], w_ref[...], preferred_element_type=F32)

    def rope_group(acc, g):
        xg = acc[:, g * LANES:(g + 1) * LANES]
        up = pltpu.roll(xg, LANES - half, 1)
        dn = pltpu.roll(xg, half, 1)
        return xg * c_ref[...] + up * sa_ref[...] + dn * sb_ref[...]

    @pl.when(j < nq)
    def _():
        acc = project()
        for g in range(tn // LANES):
            q_ref[:, g * LANES:(g + 1) * LANES] = (rope_group(acc, g) * q_scale).astype(BF16)

    @pl.when((j >= nq) & (j < nq + nk))
    def _():
        if k_transposed:
            t_ref[...] = project()
            kt = t_ref[...].T
            cos, sin = ct_ref[...], st_ref[...]
            for g in range(tn // A_HEAD_DIM):
                r0 = g * A_HEAD_DIM
                lo = kt[r0:r0 + half, :]
                hi = kt[r0 + half:r0 + ROT_DIM, :]
                k_ref[r0:r0 + half, :] = lo * cos - hi * sin
                k_ref[r0 + half:r0 + ROT_DIM, :] = hi * cos + lo * sin
                k_ref[r0 + ROT_DIM:r0 + A_HEAD_DIM, :] = kt[r0 + ROT_DIM:r0 + A_HEAD_DIM, :]
        else:
            acc = project()
            for g in range(tn // LANES):
                k_ref[:, g * LANES:(g + 1) * LANES] = rope_group(acc, g)

    @pl.when((j >= nq + nk) & (j < nq + nk + nv))
    def _():
        v_ref[...] = project()

    @pl.when(j >= nq + nk + nv)
    def _():
        r_ref[...] = project().astype(BF16)


def _inproj(x2, w_bf, tables, t_len, a_width, q_scale, k_transposed):
    n, d = x2.shape
    n_in = w_bf.shape[1]
    tm = min(n, 1024)
    tn = 512
    assert n % tm == 0 and a_width % tn == 0 and n_in % tn == 0
    row_tabs, col_tabs = tables
    if t_len >= tm:
        assert t_len % tm == 0
        per = t_len // tm
    else:
        assert tm % t_len == 0 and not k_transposed
        row_tabs = tuple(jnp.tile(a, (tm // t_len, 1)) for a in row_tabs)
        col_tabs = tuple(jnp.tile(a, (1, tm // t_len)) for a in col_tabs)
        per = 1
    nq = nk = nv = a_width // tn
    nr = n_in // tn - 3 * nq
    kblk = lambda j: jnp.clip(j - nq, 0, nk - 1)
    row_spec = pl.BlockSpec((tm, LANES), lambda i, j: (i % per, 0))
    col_spec = pl.BlockSpec((ROT_DIM // 2, tm), lambda i, j: (0, i % per))
    if k_transposed:
        k_spec = pl.BlockSpec((None, tn, tm), lambda i, j: (i // per, kblk(j), i % per))
        k_shape = jax.ShapeDtypeStruct((n // t_len, a_width, t_len), F32)
    else:
        k_spec = pl.BlockSpec((tm, tn), lambda i, j: (i, kblk(j)))
        k_shape = jax.ShapeDtypeStruct((n, a_width), F32)
    kern = functools.partial(_inproj_kernel, tn=tn, nq=nq, nk=nk, nv=nv, q_scale=q_scale,
                             k_transposed=k_transposed)
    return pl.pallas_call(
        kern,
        grid=(n // tm, n_in // tn),
        in_specs=[
            pl.BlockSpec((tm, d), lambda i, j: (i, 0)),
            pl.BlockSpec((d, tn), lambda i, j: (0, j)),
            row_spec, row_spec, row_spec, col_spec, col_spec,
        ],
        out_specs=[
            pl.BlockSpec((tm, tn), lambda i, j: (i, jnp.minimum(j, nq - 1))),
            k_spec,
            pl.BlockSpec((tm, tn), lambda i, j: (i, jnp.clip(j - nq - nk, 0, nv - 1))),
            pl.BlockSpec((tm, tn), lambda i, j: (i, jnp.maximum(j - nq - nk - nv, 0))),
        ],
        out_shape=[
            jax.ShapeDtypeStruct((n, a_width), BF16),
            k_shape,
            jax.ShapeDtypeStruct((n, a_width), F32),
            jax.ShapeDtypeStruct((n, nr * tn), BF16),
        ],
        scratch_shapes=[pltpu.VMEM((tm, d), BF16), pltpu.VMEM((tm, tn), F32)],
        compiler_params=_cparams(("parallel", "arbitrary")),
        name="inproj",
    )(x2, w_bf, *row_tabs, *col_tabs)


def _memkv_kernel(x_ref, wk_ref, wv_ref, k_ref, v_ref):
    xb = x_ref[...].astype(BF16)
    k_ref[...] = jnp.dot(xb, wk_ref[...], preferred_element_type=F32)
    v_ref[...] = jnp.dot(xb, wv_ref[...], preferred_element_type=F32)


def _memkv(x2, wk_bf, wv_bf):
    n, d = x2.shape
    c = wk_bf.shape[1]
    tm = min(n, 512)
    assert n % tm == 0
    wspec = pl.BlockSpec((d, c), lambda i: (0, 0))
    ospec = pl.BlockSpec((tm, c), lambda i: (i, 0))
    return pl.pallas_call(
        _memkv_kernel,
        grid=(n // tm,),
        in_specs=[pl.BlockSpec((tm, d), lambda i: (i, 0)), wspec, wspec],
        out_specs=[ospec, ospec],
        out_shape=[jax.ShapeDtypeStruct((n, c), F32)] * 2,
        compiler_params=_cparams(("parallel",)),
        name="memkv",
    )(x2, wk_bf, wv_bf)


def _attn_epilogue(o, lam_init, gain, z):
    ms = jnp.mean(o * o, axis=1, keepdims=True)
    y = o * lax.rsqrt(ms + NORM_EPS) * gain * (1.0 - lam_init)
    return y * _silu(z)


def _attn_prompt_kernel(lam_ref, q_ref, kt_ref, v_ref, z_ref, g_ref, o_ref,
                        kb_ref, vb_ref, qa_ref, sa_ref, sb_ref, p_ref, acc_ref, m_ref, al_ref,
                        *, tq, lam_init):
    s_len = q_ref.shape[0]
    strip = 32
    kb_ref[...] = kt_ref[...].astype(BF16)
    vb_ref[:, :A_V_DIM] = v_ref[...].astype(BF16)
    vb_ref[:, A_V_DIM:] = jnp.ones((s_len, A_V_DIM), BF16)
    lane = lax.broadcasted_iota(jnp.int32, (tq, A_V_DIM), 1)

    def scores(j, s_ref):
        kt = kb_ref[:, pl.ds(pl.multiple_of(j * tq, tq), tq)]
        for mp in range(2):
            s_ref[mp] = jnp.dot(qa_ref[mp], kt, preferred_element_type=F32)

    def softmax_pv(j, s_ref, masked):
        vt = vb_ref[pl.ds(pl.multiple_of(j * tq, tq), tq), :]
        for mp in range(2):
            for r in range(0, tq, strip):
                rs = slice(r, r + strip)
                sv = s_ref[mp, rs, :]
                if masked:
                    cc = lax.broadcasted_iota(jnp.int32, (strip, tq), 1) // CHUNK
                    sv = jnp.where(cc <= r // CHUNK, sv, -jnp.inf)
                m_old = m_ref[mp, rs, :]
                mn = jnp.maximum(m_old, jnp.max(sv, axis=1, keepdims=True))
                m_ref[mp, rs, :] = mn
                al_ref[mp, rs, :] = jnp.exp2(m_old - mn)
                for c in range(tq // LANES):
                    cs = slice(c * LANES, (c + 1) * LANES)
                    p_ref[mp, rs, cs] = jnp.exp2(sv[:, cs] - mn).astype(BF16)
            pv = jnp.dot(p_ref[mp], vt, preferred_element_type=F32)
            for c in range(2):
                cs = slice(c * A_V_DIM, (c + 1) * A_V_DIM)
                acc_ref[mp, :, cs] = acc_ref[mp, :, cs] * al_ref[mp] + pv[:, cs]

    def q_tile(qi, _):
        rows = pl.ds(pl.multiple_of(qi * tq, tq), tq)
        qf = q_ref[rows, :].astype(F32)
        qa_ref[0] = jnp.where(lane < A_HEAD_DIM, qf, 0.0).astype(BF16)
        qa_ref[1] = jnp.where(lane >= A_HEAD_DIM, qf, 0.0).astype(BF16)
        m_ref[...] = jnp.full(m_ref.shape, -jnp.inf, F32)
        acc_ref[...] = jnp.zeros(acc_ref.shape, F32)
        scores(0, sa_ref)

        def full_tile(j, carry):
            @pl.when(j % 2 == 0)
            def _():
                scores(j + 1, sb_ref)
                softmax_pv(j, sa_ref, False)

            @pl.when(j % 2 == 1)
            def _():
                scores(j + 1, sa_ref)
                softmax_pv(j, sb_ref, False)

            return carry

        lax.fori_loop(0, qi, full_tile, 0)

        @pl.when(qi % 2 == 0)
        def _():
            softmax_pv(qi, sa_ref, True)

        @pl.when(qi % 2 == 1)
        def _():
            softmax_pv(qi, sb_ref, True)

        o = (acc_ref[0, :, :A_V_DIM] / acc_ref[0, :, A_V_DIM:]
             - lam_ref[0] * (acc_ref[1, :, :A_V_DIM] / acc_ref[1, :, A_V_DIM:]))
        y = _attn_epilogue(o, lam_init, g_ref[...], z_ref[rows, :].astype(F32))
        o_ref[rows, :] = y.astype(BF16)
        return 0

    lax.fori_loop(0, s_len // tq, q_tile, 0)


def _attn_prompt(lam, q3, kt3, v3, rest3, gain, lam_init):
    b, s, w = q3.shape
    h = w // A_V_DIM
    tq = min(s, ATTN_TILE)
    assert s % tq == 0 and tq % CHUNK == 0
    kern = functools.partial(_attn_prompt_kernel, tq=tq, lam_init=lam_init)
    qspec = pl.BlockSpec((None, s, A_V_DIM), lambda bi, hi: (bi, 0, hi))
    return pl.pallas_call(
        kern,
        grid=(b, h),
        in_specs=[
            pl.BlockSpec(memory_space=pltpu.SMEM),
            qspec,
            pl.BlockSpec((None, A_V_DIM, s), lambda bi, hi: (bi, hi, 0)),
            qspec, qspec,
            pl.BlockSpec((1, A_V_DIM), lambda bi, hi: (0, 0)),
        ],
        out_specs=qspec,
        out_shape=jax.ShapeDtypeStruct((b, s, w), BF16),
        scratch_shapes=[
            pltpu.VMEM((A_V_DIM, s), BF16),
            pltpu.VMEM((s, 2 * A_V_DIM), BF16),
            pltpu.VMEM((2, tq, A_V_DIM), BF16),
            pltpu.VMEM((2, tq, tq), F32),
            pltpu.VMEM((2, tq, tq), F32),
            pltpu.VMEM((2, tq, tq), BF16),
            pltpu.VMEM((2, tq, 2 * A_V_DIM), F32),
            pltpu.VMEM((2, tq, LANES), F32),
            pltpu.VMEM((2, tq, LANES), F32),
        ],
        compiler_params=_cparams(("parallel", "parallel")),
        name="attn_prompt",
    )(lam, q3, kt3, v3, rest3, gain)


def _attn_sample_kernel(lam_ref, qbd_ref, kc_ref, vc_ref, kn_ref, vn_ref, z_ref, g_ref, o_ref,
                        s_ref, w_ref, acc_ref, *, tk, nkt, t_new, heads, lam_init):
    j = pl.program_id(1)
    past = tk * nkt
    nt = (((1,), (1,)), ((), ()))
    width = kn_ref.shape[1]

    def pad_new(ref):
        new = ref[...]
        return jnp.concatenate([new, jnp.zeros((LANES - t_new, width), F32)], axis=0).astype(BF16)

    @pl.when(j < nkt)
    def _():
        s = jnp.dot(qbd_ref[...], kc_ref[...].astype(BF16), preferred_element_type=F32)
        s_ref[:, pl.ds(pl.multiple_of(j * tk, tk), tk)] = s

    @pl.when(j == nkt - 1)
    def _():
        sn = lax.dot_general(qbd_ref[...], pad_new(kn_ref), nt, preferred_element_type=F32)
        col = lax.broadcasted_iota(jnp.int32, sn.shape, 1)
        s_ref[:, past:past + LANES] = jnp.where(col < t_new, sn, -jnp.inf)
        lam = lam_ref[0]
        for hd in range(heads):
            r0 = hd * 2 * t_new
            p = []
            for mp in range(2):
                s = s_ref[r0 + mp * t_new:r0 + (mp + 1) * t_new, :]
                e = jnp.exp2(s - jnp.max(s, axis=1, keepdims=True))
                p.append(e / jnp.sum(e, axis=1, keepdims=True))
            w_ref[hd * t_new:(hd + 1) * t_new, :] = (p[0] - lam * p[1]).astype(BF16)

    @pl.when(j == nkt)
    def _():
        full = jnp.dot(w_ref[:, past:past + LANES], pad_new(vn_ref), preferred_element_type=F32)
        for hd in range(heads):
            rs = slice(hd * t_new, (hd + 1) * t_new)
            acc_ref[rs, :] = full[rs, hd * A_V_DIM:(hd + 1) * A_V_DIM]

    @pl.when(j >= nkt)
    def _():
        start = pl.multiple_of((j - nkt) * tk, tk)
        for hd in range(heads):
            rs = slice(hd * t_new, (hd + 1) * t_new)
            vh = vc_ref[pl.ds(hd, tk, stride=heads), :].astype(BF16)
            acc_ref[rs, :] += jnp.dot(w_ref[rs, pl.ds(start, tk)], vh, preferred_element_type=F32)

    @pl.when(j == 2 * nkt - 1)
    def _():
        for hd in range(heads):
            cs = slice(hd * A_V_DIM, (hd + 1) * A_V_DIM)
            o = acc_ref[hd * t_new:(hd + 1) * t_new, :]
            y = _attn_epilogue(o, lam_init, g_ref[...], z_ref[:, cs].astype(F32))
            o_ref[:, cs] = y.astype(BF16)


def _attn_sample(lam, qbd, kct, vc3, kn, vn, rest3, gain, lam_init):
    b, w, past = kct.shape
    t_new = kn.shape[1]
    heads = w // A_V_DIM
    rows = qbd.shape[1]
    tk = min(past, 512)
    assert past % tk == 0 and t_new % 16 == 0 and t_new <= LANES
    nkt = past // tk
    kern = functools.partial(_attn_sample_kernel, tk=tk, nkt=nkt, t_new=t_new, heads=heads,
                             lam_init=lam_init)
    newspec = pl.BlockSpec((None, t_new, w), lambda bi, j: (bi, 0, 0))
    return pl.pallas_call(
        kern,
        grid=(b, 2 * nkt),
        in_specs=[
            pl.BlockSpec(memory_space=pltpu.SMEM),
            pl.BlockSpec((None, rows, w), lambda bi, j: (bi, 0, 0)),
            pl.BlockSpec((None, w, tk), lambda bi, j: (bi, 0, jnp.minimum(j, nkt - 1))),
            pl.BlockSpec((None, tk * heads, A_V_DIM),
                         lambda bi, j: (bi, jnp.maximum(j - nkt, 0), 0)),
            newspec, newspec, newspec,
            pl.BlockSpec((1, A_V_DIM), lambda bi, j: (0, 0)),
        ],
        out_specs=newspec,
        out_shape=jax.ShapeDtypeStruct((b, t_new, w), BF16),
        scratch_shapes=[
            pltpu.VMEM((rows, past + LANES), F32),
            pltpu.VMEM((rows // 2, past + LANES), BF16),
            pltpu.VMEM((rows // 2, A_V_DIM), F32),
        ],
        compiler_params=_cparams(("parallel", "arbitrary")),
        name="attn_sample",
    )(lam, qbd, kct, vc3, kn, vn, rest3, gain)


def _block_diag_queries(q3, heads):
    b, t, w = q3.shape
    hm = 2 * heads
    q5 = q3.reshape(b, t, hm, A_HEAD_DIM).transpose(0, 2, 1, 3)
    eye = jnp.eye(hm, dtype=q3.dtype)
    qbd = q5[:, :, :, None, :] * eye[None, :, None, :, None]
    return qbd.reshape(b, hm * t, w)


def _hgrn_kernel(*refs, blk, nchunk, has_s0):
    if has_s0:
        q_ref, f_ref, i_ref, og_ref, z_ref, lb_ref, gain_ref, s0_ref, y_ref, sf_ref, st_ref = refs
    else:
        q_ref, f_ref, i_ref, og_ref, z_ref, lb_ref, gain_ref, y_ref, sf_ref, st_ref = refs
    t = pl.program_id(2)

    @pl.when(t == 0)
    def _():
        st_ref[...] = s0_ref[...] if has_s0 else jnp.zeros_like(st_ref)

    lb = lb_ref[...]
    gain = gain_ref[...]
    row = lax.broadcasted_iota(jnp.int32, (blk, blk), 0)
    col = lax.broadcasted_iota(jnp.int32, (blk, blk), 1)
    causal = col <= row
    tril = jnp.where(causal, 1.0, 0.0).astype(BF16)
    ones = jnp.ones((blk, B_KEY_DIM), BF16)
    mid = (blk - 1) // 2
    nt = (((1,), (1,)), ((), ()))
    tn = (((0,), (0,)), ((), ()))

    def split3(g):
        hi = g.astype(BF16)
        r1 = g - hi.astype(F32)
        md = r1.astype(BF16)
        lo = (r1 - md.astype(F32)).astype(BF16)
        return hi, md, lo

    chunks = [slice(c * blk, (c + 1) * blk) for c in range(nchunk)]
    f = lb + (1.0 - lb) * jax.nn.sigmoid(f_ref[...].astype(F32))
    parts = split3(jnp.log(f))
    qv = _silu(q_ref[...].astype(F32))
    kv = 1.0 - f
    vv = i_ref[...]
    b = [sum(jnp.dot(tril, p[sl, :], preferred_element_type=F32) for p in parts) for sl in chunks]
    tot = [sum(lax.dot_general(p[sl, :], ones, tn, preferred_element_type=F32) for p in parts)
           for sl in chunks]
    qe, ke, qs, ks = [], [], [], []
    for sl, bc in zip(chunks, b):
        b_mid = bc[mid:mid + 1, :]
        b_last = bc[blk - 1:blk, :]
        qe.append((qv[sl, :] * jnp.exp(bc - b_mid)).astype(BF16))
        ke.append((kv[sl, :] * jnp.exp(b_mid - bc)).astype(BF16))
        qs.append((qv[sl, :] * jnp.exp(bc)).astype(BF16))
        ks.append((kv[sl, :] * jnp.exp(b_last - bc)).astype(BF16))
    sc = [lax.dot_general(a, k, nt, preferred_element_type=F32) for a, k in zip(qe, ke)]
    sc = [jnp.where(causal, s, 0.0).astype(BF16) for s in sc]
    o_intra = [jnp.dot(s, vv[sl, :], preferred_element_type=F32) for s, sl in zip(sc, chunks)]
    kvs = [lax.dot_general(k, vv[sl, :], tn, preferred_element_type=F32) for k, sl in zip(ks, chunks)]
    decay = [jnp.exp(tc) for tc in tot]
    state = st_ref[...]
    outs = []
    for c in range(nchunk):
        outs.append(o_intra[c] + jnp.dot(qs[c], state.astype(BF16), preferred_element_type=F32))
        state = state * decay[c] + kvs[c]
    st_ref[...] = state
    o = outs[0] if nchunk == 1 else jnp.concatenate(outs, axis=0)
    ms = jnp.mean(o * o, axis=1, keepdims=True)
    y = o * lax.rsqrt(ms + NORM_EPS) * gain * jax.nn.sigmoid(og_ref[...].astype(F32))
    y_ref[...] = (y * _silu(z_ref[...].astype(F32))).astype(BF16)

    @pl.when(t == pl.num_programs(2) - 1)
    def _():
        sf_ref[...] = st_ref[...]


def _hgrn(rest3, lb, gain, s0, blk, col0):
    b, t, _ = rest3.shape
    heads = lb.shape[1] // B_KEY_DIM
    tt = min(t, 512)
    assert t % tt == 0 and tt % blk == 0
    has_s0 = s0 is not None
    kern = functools.partial(_hgrn_kernel, blk=blk, nchunk=tt // blk, has_s0=has_s0)

    def colspec(k):
        return pl.BlockSpec((None, tt, B_KEY_DIM), lambda bi, hi, ti: (bi, ti, col0 + k * heads + hi))

    vecspec = pl.BlockSpec((1, B_KEY_DIM), lambda bi, hi, ti: (0, hi))
    stspec = pl.BlockSpec((None, None, B_KEY_DIM, B_KEY_DIM), lambda bi, hi, ti: (bi, hi, 0, 0))
    in_specs = [colspec(k) for k in range(5)] + [vecspec, vecspec]
    args = [rest3] * 5 + [lb, gain]
    if has_s0:
        in_specs.append(stspec)
        args.append(s0)
    return pl.pallas_call(
        kern,
        grid=(b, heads, t // tt),
        in_specs=in_specs,
        out_specs=[
            pl.BlockSpec((None, tt, B_KEY_DIM), lambda bi, hi, ti: (bi, ti, hi)),
            stspec,
        ],
        out_shape=[
            jax.ShapeDtypeStruct((b, t, heads * B_KEY_DIM), BF16),
            jax.ShapeDtypeStruct((b, heads, B_KEY_DIM, B_KEY_DIM), F32),
        ],
        scratch_shapes=[pltpu.VMEM((B_KEY_DIM, B_KEY_DIM), F32)],
        compiler_params=_cparams(("parallel", "parallel", "arbitrary")),
        name="hgrn",
    )(*args)


def _memattn_kernel(q_ref, z_ref, mk_ref, mv_ref, o_ref, kb_ref, vb_ref, *, heads):
    @pl.when(pl.program_id(1) == 0)
    def _():
        kb_ref[...] = mk_ref[...].astype(BF16)
        vb_ref[...] = mv_ref[...].astype(BF16)

    nt = (((1,), (1,)), ((), ()))
    for hd in range(heads):
        cs = slice(hd * C_HEAD_DIM, (hd + 1) * C_HEAD_DIM)
        s = lax.dot_general(q_ref[:, cs], kb_ref[:, cs], nt, preferred_element_type=F32)
        s = s * (C_HEAD_DIM ** -0.5)
        e = jnp.exp(s - jnp.max(s, axis=1, keepdims=True))
        l = jnp.sum(e, axis=1, keepdims=True)
        o = jnp.dot(e.astype(BF16), vb_ref[:, cs], preferred_element_type=F32) / l
        o_ref[:, cs] = (o * _silu(z_ref[:, cs].astype(F32))).astype(BF16)


def _memattn(rest3, mk3, mv3, qblock, zblock):
    b, t, _ = rest3.shape
    _, m, c = mk3.shape
    tq = min(t, 512)
    assert t % tq == 0
    kern = functools.partial(_memattn_kernel, heads=c // C_HEAD_DIM)
    memspec = pl.BlockSpec((None, m, c), lambda bi, ti: (bi, 0, 0))
    return pl.pallas_call(
        kern,
        grid=(b, t // tq),
        in_specs=[
            pl.BlockSpec((None, tq, c), lambda bi, ti: (bi, ti, qblock)),
            pl.BlockSpec((None, tq, c), lambda bi, ti: (bi, ti, zblock)),
            memspec, memspec,
        ],
        out_specs=pl.BlockSpec((None, tq, c), lambda bi, ti: (bi, ti, 0)),
        out_shape=jax.ShapeDtypeStruct((b, t, c), BF16),
        scratch_shapes=[pltpu.VMEM((m, c), BF16), pltpu.VMEM((m, c), BF16)],
        compiler_params=_cparams(("parallel", "arbitrary")),
        name="memattn",
    )(rest3, rest3, mk3, mv3)


def _merge_kernel(ya_ref, yb_ref, yc_ref, ga_ref, gb_ref, gc_ref, x_ref,
                  wa_ref, wb_ref, wc_ref, wo_ref, lng_ref, lnb_ref, o_ref, *, alpha):
    def branch(y_ref, w_ref, g_ref):
        return jax.nn.sigmoid(g_ref[...].astype(F32)) * jnp.dot(
            y_ref[...], w_ref[...], preferred_element_type=F32)

    merged = branch(ya_ref, wa_ref, ga_ref) + branch(yb_ref, wb_ref, gb_ref)
    merged = merged + branch(yc_ref, wc_ref, gc_ref)
    sub = jnp.dot(merged.astype(BF16), wo_ref[...], preferred_element_type=F32)
    hres = alpha * x_ref[...] + sub
    mu = jnp.mean(hres, axis=1, keepdims=True)
    cen = hres - mu
    var = jnp.mean(cen * cen, axis=1, keepdims=True)
    o_ref[...] = cen * lax.rsqrt(var + NORM_EPS) * lng_ref[...] + lnb_ref[...]


def _merge(ya, yb, yc, rest2, x2, wa, wb, wc, wo, lng, lnb, gate_block0, alpha):
    n, d = x2.shape
    w = ya.shape[1]
    tm = min(n, 256)
    assert n % tm == 0
    rows = lambda width, blk: pl.BlockSpec((tm, width), lambda i: (i, blk))
    const = lambda shape: pl.BlockSpec(shape, lambda i: (0, 0), pipeline_mode=pl.Buffered(1))
    return pl.pallas_call(
        functools.partial(_merge_kernel, alpha=alpha),
        grid=(n // tm,),
        in_specs=[
            rows(w, 0), rows(w, 0), rows(w, 0),
            rows(d, gate_block0), rows(d, gate_block0 + 1), rows(d, gate_block0 + 2),
            rows(d, 0),
            const((w, d)), const((w, d)), const((w, d)), const((d, d)),
            const((1, d)), const((1, d)),
        ],
        out_specs=rows(d, 0),
        out_shape=jax.ShapeDtypeStruct((n, d), F32),
        compiler_params=_cparams(("parallel",)),
        name="merge",
    )(ya, yb, yc, rest2, rest2, rest2, x2, wa, wb, wc, wo, lng, lnb)


def _layer(x, pos, past_k, past_v, s0, mk3, mv3, rec_block, layer_idx, lb, p):
    (w_in, lam, sub_norm, hgrn_gain, wa, wb, wc, wo, lng, lnb, alpha) = p
    n_b, t, d = x.shape
    a_width = wa.shape[0]
    heads = a_width // A_V_DIM
    x2 = x.reshape(n_b * t, d)
    prompt = past_k is None
    q_scale = A_HEAD_DIM ** -0.5 * math.log2(math.e)
    q2, k_out, v2, rest2 = _inproj(x2, w_in, _rope_tables(pos), t, a_width, q_scale,
                                   k_transposed=prompt)
    r = rest2.shape[1]
    rest3 = rest2.reshape(n_b, t, r)
    q3 = q2.reshape(n_b, t, a_width)
    v3 = v2.reshape(n_b, t, a_width)
    lam_init = 0.8 - 0.6 * math.exp(-0.3 * layer_idx)
    if prompt:
        ya = _attn_prompt(lam, q3, k_out, v3, rest3, sub_norm, lam_init)
        k5 = k_out.reshape(n_b, heads, 2, A_HEAD_DIM, t).transpose(0, 4, 1, 2, 3)
    else:
        k3 = k_out.reshape(n_b, t, a_width)
        ya = _attn_sample(lam, _block_diag_queries(q3, heads), past_k, past_v, k3, v3, rest3,
                          sub_norm, lam_init)
        k5 = k3.reshape(n_b, t, heads, 2, A_HEAD_DIM)
    yb, s_fin = _hgrn(rest3, lb, hgrn_gain, s0, rec_block, a_width // B_KEY_DIM)
    c_width = mk3.shape[2]
    c_off = a_width + 5 * lb.shape[1]
    assert c_off % c_width == 0
    yc = _memattn(rest3, mk3, mv3, c_off // c_width, c_off // c_width + 1)
    g_off = c_off + 2 * c_width
    assert g_off % d == 0
    y2 = _merge(ya.reshape(n_b * t, a_width), yb.reshape(n_b * t, -1), yc.reshape(n_b * t, c_width),
                rest2, x2, wa, wb, wc, wo, lng, lnb, g_off // d, alpha)
    return y2.reshape(n_b, t, d), k5, v3, s_fin


def kernel(x_prompt, x_sample, cache_attn_k, cache_attn_v, state_hgrn, cache_mem_k, cache_mem_v, mem_prompt, w_in, lambda_q1, lambda_k1, lambda_q2, lambda_k2, attn_sub_norm, hgrn_lb_logits, hgrn_norm, w_mem_k, w_mem_v, w_branch_a, w_branch_b, w_branch_c, w_out, ln_gamma, ln_beta):
    bp, seq, d = x_prompt.shape
    bs, t_new, _ = x_sample.shape
    depth = w_in.shape[0]
    past = cache_attn_k.shape[2]
    heads = cache_attn_k.shape[3]
    a_width = heads * A_V_DIM
    n_mem = mem_prompt.shape[1]
    c_heads = cache_mem_k.shape[3]
    c_width = c_heads * C_HEAD_DIM
    b_heads = state_hgrn.shape[2]
    alpha = (2 * depth) ** 0.25
    pos_prompt = jnp.arange(seq)
    pos_sample = past + jnp.arange(t_new)
    lower_bounds = jnp.cumsum(jax.nn.softmax(hgrn_lb_logits.astype(F32), axis=0), axis=0)

    h_p, h_s = x_prompt, x_sample
    outs = [[] for _ in range(8)]
    for l in range(depth):
        lam_init = 0.8 - 0.6 * math.exp(-0.3 * l)
        lam = (jnp.exp(jnp.sum(lambda_q1[l].astype(F32) * lambda_k1[l].astype(F32)))
               - jnp.exp(jnp.sum(lambda_q2[l].astype(F32) * lambda_k2[l].astype(F32))) + lam_init)
        params = (w_in[l].astype(BF16), lam.reshape(1), attn_sub_norm[l].reshape(1, -1),
                  hgrn_norm[l].reshape(1, -1), w_branch_a[l].astype(BF16), w_branch_b[l].astype(BF16),
                  w_branch_c[l].astype(BF16), w_out[l].astype(BF16), ln_gamma[l].reshape(1, -1),
                  ln_beta[l].reshape(1, -1), alpha)
        lb = lower_bounds[l].reshape(1, -1)
        mk_p, mv_p = _memkv(mem_prompt.reshape(bp * n_mem, d), w_mem_k[l].astype(BF16),
                            w_mem_v[l].astype(BF16))
        mk_p = mk_p.reshape(bp, n_mem, c_width)
        mv_p = mv_p.reshape(bp, n_mem, c_width)
        h_p, k_p, v_p, s_p = _layer(h_p, pos_prompt, None, None, None, mk_p, mv_p, CHUNK, l, lb, params)
        h_s, k_s, v_s, s_s = _layer(
            h_s, pos_sample,
            cache_attn_k[l].transpose(0, 2, 3, 4, 1).reshape(bs, a_width, past),
            cache_attn_v[l].reshape(bs, past * heads, A_V_DIM),
            state_hgrn[l], cache_mem_k[l].reshape(bs, n_mem, c_width),
            cache_mem_v[l].reshape(bs, n_mem, c_width), t_new, l, lb, params)
        new = (k_p, v_p.reshape(bp, seq, heads, A_V_DIM),
               s_p.astype(x_prompt.dtype), mk_p.reshape(bp, n_mem, c_heads, C_HEAD_DIM),
               mv_p.reshape(bp, n_mem, c_heads, C_HEAD_DIM),
               k_s, v_s.reshape(bs, t_new, heads, A_V_DIM),
               s_s.astype(x_sample.dtype))
        for acc, val in zip(outs, new):
            acc.append(val)
    return (h_p, h_s) + tuple(jnp.stack(o) for o in outs)
```

```python
import functools
import math

import jax
import jax.numpy as jnp
from jax import lax
from jax.experimental import pallas as pl
from jax.experimental.pallas import tpu as pltpu

F32 = jnp.float32
BF16 = jnp.bfloat16

CHUNK = 64
A_HEAD_DIM = 64
A_V_DIM = 128
ROT_DIM = 16
ROPE_THETA = 500000.0
B_KEY_DIM = 128
C_HEAD_DIM = 256
NORM_EPS = 1e-5
LANES = 128
VMEM_LIMIT = 56 * 1024 * 1024
ATTN_TILE = 512
PV_SPLIT = 4
HGRN_HEADS_PER_STEP = 4


def _cparams(sem):
    return pltpu.CompilerParams(dimension_semantics=sem, vmem_limit_bytes=VMEM_LIMIT)


def _silu(z):
    return z * jax.nn.sigmoid(z)


def _rope_tables(pos):
    half = ROT_DIM // 2
    inv_freq = jnp.power(ROPE_THETA, -jnp.arange(0, ROT_DIM, 2, dtype=F32) / ROT_DIM)
    ang = pos.astype(F32)[:, None] * inv_freq[None, :]
    cos, sin = jnp.cos(ang), jnp.sin(ang)
    t = pos.shape[0]
    pad = jnp.zeros((t, A_HEAD_DIM - ROT_DIM), F32)
    zero = jnp.zeros((t, half), F32)
    c64 = jnp.concatenate([cos, cos, pad + 1.0], axis=1)
    sa64 = jnp.concatenate([-sin, zero, pad], axis=1)
    sb64 = jnp.concatenate([zero, sin, pad], axis=1)
    rep = LANES // A_HEAD_DIM
    rows = tuple(jnp.tile(a, (1, rep)) for a in (c64, sa64, sb64))
    return rows, (cos.T, sin.T)


def _inproj_kernel(x_ref, w_ref, c_ref, sa_ref, sb_ref, ct_ref, st_ref,
                   q_ref, k_ref, v_ref, r_ref, xb_ref, t_ref, *, tn, nq, nk, nv, q_scale, k_transposed):
    j = pl.program_id(1)
    half = ROT_DIM // 2

    @pl.when(j == 0)
    def _():
        xb_ref[...] = x_ref[...].astype(BF16)

    def project():
        return jnp.dot(xb_ref[...], w_ref[...], preferred_element_type=F32)

    def rope_group(acc, g):
        xg = acc[:, g * LANES:(g + 1) * LANES]
        up = pltpu.roll(xg, LANES - half, 1)
        dn = pltpu.roll(xg, half, 1)
        return xg * c_ref[...] + up * sa_ref[...] + dn * sb_ref[...]

    @pl.when(j < nq)
    def _():
        acc = project()
        for g in range(tn // LANES):
            q_ref[:, g * LANES:(g + 1) * LANES] = (rope_group(acc, g) * q_scale).astype(BF16)

    @pl.when((j >= nq) & (j < nq + nk))
    def _():
        if k_transposed:
            t_ref[...] = project()
            kt = t_ref[...].T
            cos, sin = ct_ref[...], st_ref[...]
            for g in range(tn // A_HEAD_DIM):
                r0 = g * A_HEAD_DIM
                lo = kt[r0:r0 + half, :]
                hi = kt[r0 + half:r0 + ROT_DIM, :]
                k_ref[r0:r0 + half, :] = lo * cos - hi * sin
                k_ref[r0 + half:r0 + ROT_DIM, :] = hi * cos + lo * sin
                k_ref[r0 + ROT_DIM:r0 + A_HEAD_DIM, :] = kt[r0 + ROT_DIM:r0 + A_HEAD_DIM, :]
        else:
            acc = project()
            for g in range(tn // LANES):
                k_ref[:, g * LANES:(g + 1) * LANES] = rope_group(acc, g)

    @pl.when((j >= nq + nk) & (j < nq + nk + nv))
    def _():
        v_ref[...] = project()

    @pl.when(j >= nq + nk + nv)
    def _():
        r_ref[...] = project().astype(BF16)


def _inproj(x2, w_bf, tables, t_len, a_width, q_scale, k_transposed):
    n, d = x2.shape
    n_in = w_bf.shape[1]
    tm = min(n, 1024)
    tn = 512
    assert n % tm == 0 and a_width % tn == 0 and n_in % tn == 0
    row_tabs, col_tabs = tables
    if t_len >= tm:
        assert t_len % tm == 0
        per = t_len // tm
    else:
        assert tm % t_len == 0 and not k_transposed
        row_tabs = tuple(jnp.tile(a, (tm // t_len, 1)) for a in row_tabs)
        col_tabs = tuple(jnp.tile(a, (1, tm // t_len)) for a in col_tabs)
        per = 1
    nq = nk = nv = a_width // tn
    nr = n_in // tn - 3 * nq
    kblk = lambda j: jnp.clip(j - nq, 0, nk - 1)
    row_spec = pl.BlockSpec((tm, LANES), lambda i, j: (i % per, 0))
    col_spec = pl.BlockSpec((ROT_DIM // 2, tm), lambda i, j: (0, i % per))
    if k_transposed:
        k_spec = pl.BlockSpec((None, tn, tm), lambda i, j: (i // per, kblk(j), i % per))
        k_shape = jax.ShapeDtypeStruct((n // t_len, a_width, t_len), F32)
    else:
        k_spec = pl.BlockSpec((tm, tn), lambda i, j: (i, kblk(j)))
        k_shape = jax.ShapeDtypeStruct((n, a_width), F32)
    kern = functools.partial(_inproj_kernel, tn=tn, nq=nq, nk=nk, nv=nv, q_scale=q_scale,
                             k_transposed=k_transposed)
    return pl.pallas_call(
        kern,
        grid=(n // tm, n_in // tn),
        in_specs=[
            pl.BlockSpec((tm, d), lambda i, j: (i, 0)),
            pl.BlockSpec((d, tn), lambda i, j: (0, j)),
            row_spec, row_spec, row_spec, col_spec, col_spec,
        ],
        out_specs=[
            pl.BlockSpec((tm, tn), lambda i, j: (i, jnp.minimum(j, nq - 1))),
            k_spec,
            pl.BlockSpec((tm, tn), lambda i, j: (i, jnp.clip(j - nq - nk, 0, nv - 1))),
            pl.BlockSpec((tm, tn), lambda i, j: (i, jnp.maximum(j - nq - nk - nv, 0))),
        ],
        out_shape=[
            jax.ShapeDtypeStruct((n, a_width), BF16),
            k_shape,
            jax.ShapeDtypeStruct((n, a_width), F32),
            jax.ShapeDtypeStruct((n, nr * tn), BF16),
        ],
        scratch_shapes=[pltpu.VMEM((tm, d), BF16), pltpu.VMEM((tm, tn), F32)],
        compiler_params=_cparams(("parallel", "arbitrary")),
        name="inproj",
    )(x2, w_bf, *row_tabs, *col_tabs)


def _memkv_kernel(x_ref, wk_ref, wv_ref, k_ref, v_ref):
    xb = x_ref[...].astype(BF16)
    k_ref[...] = jnp.dot(xb, wk_ref[...], preferred_element_type=F32)
    v_ref[...] = jnp.dot(xb, wv_ref[...], preferred_element_type=F32)


def _memkv(x2, wk_bf, wv_bf):
    n, d = x2.shape
    c = wk_bf.shape[1]
    tm = min(n, 512)
    assert n % tm == 0
    wspec = pl.BlockSpec((d, c), lambda i: (0, 0))
    ospec = pl.BlockSpec((tm, c), lambda i: (i, 0))
    return pl.pallas_call(
        _memkv_kernel,
        grid=(n // tm,),
        in_specs=[pl.BlockSpec((tm, d), lambda i: (i, 0)), wspec, wspec],
        out_specs=[ospec, ospec],
        out_shape=[jax.ShapeDtypeStruct((n, c), F32)] * 2,
        compiler_params=_cparams(("parallel",)),
        name="memkv",
    )(x2, wk_bf, wv_bf)


def _attn_epilogue(o, lam_init, gain, z):
    ms = jnp.mean(o * o, axis=1, keepdims=True)
    y = o * lax.rsqrt(ms + NORM_EPS) * gain * (1.0 - lam_init)
    return y * _silu(z)


def _attn_prompt_kernel(lam_ref, q_ref, kt_ref, v_ref, z_ref, g_ref, o_ref,
                        kb_ref, vb_ref, qa_ref, sa_ref, sb_ref, p_ref, acc_ref, m_ref, al_ref,
                        *, tq, lam_init):
    s_len = q_ref.shape[0]
    strip = 32
    kb_ref[...] = kt_ref[...].astype(BF16)
    vb_ref[:, :A_V_DIM] = v_ref[...].astype(BF16)
    vb_ref[:, A_V_DIM:] = jnp.ones((s_len, A_V_DIM), BF16)
    lane = lax.broadcasted_iota(jnp.int32, (tq, A_V_DIM), 1)

    def scores(j, s_ref):
        kt = kb_ref[:, pl.ds(pl.multiple_of(j * tq, tq), tq)]
        for mp in range(2):
            s_ref[mp] = jnp.dot(qa_ref[mp], kt, preferred_element_type=F32)

    def softmax_pv(j, s_ref, masked):
        start = pl.multiple_of(j * tq, tq)
        grp = tq // PV_SPLIT
        for mp in range(2):
            for g0 in range(0, tq, grp):
                nk = g0 + grp if masked else tq
                for r in range(g0, g0 + grp, strip):
                    rs = slice(r, r + strip)
                    sv = s_ref[mp, rs, :nk]
                    if masked:
                        cc = lax.broadcasted_iota(jnp.int32, (strip, nk), 1) // CHUNK
                        sv = jnp.where(cc <= r // CHUNK, sv, -jnp.inf)
                    m_old = m_ref[mp, rs, :]
                    mn = jnp.maximum(m_old, jnp.max(sv, axis=1, keepdims=True))
                    m_ref[mp, rs, :] = mn
                    al_ref[mp, rs, :] = jnp.exp2(m_old - mn)
                    for c in range(nk // LANES):
                        cs = slice(c * LANES, (c + 1) * LANES)
                        p_ref[mp, rs, cs] = jnp.exp2(sv[:, cs] - mn).astype(BF16)
                gs = slice(g0, g0 + grp)
                pv = jnp.dot(p_ref[mp, gs, :nk], vb_ref[pl.ds(start, nk), :],
                             preferred_element_type=F32)
                for c in range(2):
                    cs = slice(c * A_V_DIM, (c + 1) * A_V_DIM)
                    acc_ref[mp, gs, cs] = acc_ref[mp, gs, cs] * al_ref[mp, gs, :] + pv[:, cs]

    def q_tile(qi, _):
        rows = pl.ds(pl.multiple_of(qi * tq, tq), tq)
        qf = q_ref[rows, :].astype(F32)
        qa_ref[0] = jnp.where(lane < A_HEAD_DIM, qf, 0.0).astype(BF16)
        qa_ref[1] = jnp.where(lane >= A_HEAD_DIM, qf, 0.0).astype(BF16)
        m_ref[...] = jnp.full(m_ref.shape, -jnp.inf, F32)
        acc_ref[...] = jnp.zeros(acc_ref.shape, F32)
        scores(0, sa_ref)

        def full_tile(j, carry):
            @pl.when(j % 2 == 0)
            def _():
                scores(j + 1, sb_ref)
                softmax_pv(j, sa_ref, False)

            @pl.when(j % 2 == 1)
            def _():
                scores(j + 1, sa_ref)
                softmax_pv(j, sb_ref, False)

            return carry

        lax.fori_loop(0, qi, full_tile, 0)

        @pl.when(qi % 2 == 0)
        def _():
            softmax_pv(qi, sa_ref, True)

        @pl.when(qi % 2 == 1)
        def _():
            softmax_pv(qi, sb_ref, True)

        o = (acc_ref[0, :, :A_V_DIM] / acc_ref[0, :, A_V_DIM:]
             - lam_ref[0] * (acc_ref[1, :, :A_V_DIM] / acc_ref[1, :, A_V_DIM:]))
        y = _attn_epilogue(o, lam_init, g_ref[...], z_ref[rows, :].astype(F32))
        o_ref[rows, :] = y.astype(BF16)
        return 0

    lax.fori_loop(0, s_len // tq, q_tile, 0)


def _attn_prompt(lam, q3, kt3, v3, rest3, gain, lam_init):
    b, s, w = q3.shape
    h = w // A_V_DIM
    tq = min(s, ATTN_TILE)
    assert s % tq == 0 and tq % (PV_SPLIT * LANES) == 0
    kern = functools.partial(_attn_prompt_kernel, tq=tq, lam_init=lam_init)
    qspec = pl.BlockSpec((None, s, A_V_DIM), lambda bi, hi: (bi, 0, hi))
    return pl.pallas_call(
        kern,
        grid=(b, h),
        in_specs=[
            pl.BlockSpec(memory_space=pltpu.SMEM),
            qspec,
            pl.BlockSpec((None, A_V_DIM, s), lambda bi, hi: (bi, hi, 0)),
            qspec, qspec,
            pl.BlockSpec((1, A_V_DIM), lambda bi, hi: (0, 0)),
        ],
        out_specs=qspec,
        out_shape=jax.ShapeDtypeStruct((b, s, w), BF16),
        scratch_shapes=[
            pltpu.VMEM((A_V_DIM, s), BF16),
            pltpu.VMEM((s, 2 * A_V_DIM), BF16),
            pltpu.VMEM((2, tq, A_V_DIM), BF16),
            pltpu.VMEM((2, tq, tq), F32),
            pltpu.VMEM((2, tq, tq), F32),
            pltpu.VMEM((2, tq, tq), BF16),
            pltpu.VMEM((2, tq, 2 * A_V_DIM), F32),
            pltpu.VMEM((2, tq, LANES), F32),
            pltpu.VMEM((2, tq, LANES), F32),
        ],
        compiler_params=_cparams(("parallel", "parallel")),
        name="attn_prompt",
    )(lam, q3, kt3, v3, rest3, gain)


def _attn_sample_kernel(lam_ref, qbd_ref, kc_ref, vc_ref, kn_ref, vn_ref, z_ref, g_ref, o_ref,
                        s_ref, w_ref, acc_ref, *, tk, nkt, t_new, heads, lam_init):
    j = pl.program_id(1)
    past = tk * nkt
    nt = (((1,), (1,)), ((), ()))
    width = kn_ref.shape[1]

    def pad_new(ref):
        new = ref[...]
        return jnp.concatenate([new, jnp.zeros((LANES - t_new, width), F32)], axis=0).astype(BF16)

    @pl.when(j < nkt)
    def _():
        s = jnp.dot(qbd_ref[...], kc_ref[...].astype(BF16), preferred_element_type=F32)
        s_ref[:, pl.ds(pl.multiple_of(j * tk, tk), tk)] = s

    @pl.when(j == nkt - 1)
    def _():
        sn = lax.dot_general(qbd_ref[...], pad_new(kn_ref), nt, preferred_element_type=F32)
        col = lax.broadcasted_iota(jnp.int32, sn.shape, 1)
        s_ref[:, past:past + LANES] = jnp.where(col < t_new, sn, -jnp.inf)
        lam = lam_ref[0]
        for hd in range(heads):
            r0 = hd * 2 * t_new
            p = []
            for mp in range(2):
                s = s_ref[r0 + mp * t_new:r0 + (mp + 1) * t_new, :]
                e = jnp.exp2(s - jnp.max(s, axis=1, keepdims=True))
                p.append(e / jnp.sum(e, axis=1, keepdims=True))
            w_ref[hd * t_new:(hd + 1) * t_new, :] = (p[0] - lam * p[1]).astype(BF16)

    @pl.when(j == nkt)
    def _():
        full = jnp.dot(w_ref[:, past:past + LANES], pad_new(vn_ref), preferred_element_type=F32)
        for hd in range(heads):
            rs = slice(hd * t_new, (hd + 1) * t_new)
            acc_ref[rs, :] = full[rs, hd * A_V_DIM:(hd + 1) * A_V_DIM]

    @pl.when(j >= nkt)
    def _():
        start = pl.multiple_of((j - nkt) * tk, tk)
        for hd in range(heads):
            rs = slice(hd * t_new, (hd + 1) * t_new)
            vh = vc_ref[pl.ds(hd, tk, stride=heads), :].astype(BF16)
            acc_ref[rs, :] += jnp.dot(w_ref[rs, pl.ds(start, tk)], vh, preferred_element_type=F32)

    @pl.when(j == 2 * nkt - 1)
    def _():
        for hd in range(heads):
            cs = slice(hd * A_V_DIM, (hd + 1) * A_V_DIM)
            o = acc_ref[hd * t_new:(hd + 1) * t_new, :]
            y = _attn_epilogue(o, lam_init, g_ref[...], z_ref[:, cs].astype(F32))
            o_ref[:, cs] = y.astype(BF16)


def _attn_sample(lam, qbd, kct, vc3, kn, vn, rest3, gain, lam_init):
    b, w, past = kct.shape
    t_new = kn.shape[1]
    heads = w // A_V_DIM
    rows = qbd.shape[1]
    tk = min(past, 1024)
    assert past % tk == 0 and t_new % 16 == 0 and t_new <= LANES
    nkt = past // tk
    kern = functools.partial(_attn_sample_kernel, tk=tk, nkt=nkt, t_new=t_new, heads=heads,
                             lam_init=lam_init)
    newspec = pl.BlockSpec((None, t_new, w), lambda bi, j: (bi, 0, 0))
    return pl.pallas_call(
        kern,
        grid=(b, 2 * nkt),
        in_specs=[
            pl.BlockSpec(memory_space=pltpu.SMEM),
            pl.BlockSpec((None, rows, w), lambda bi, j: (bi, 0, 0)),
            pl.BlockSpec((None, w, tk), lambda bi, j: (bi, 0, jnp.minimum(j, nkt - 1))),
            pl.BlockSpec((None, tk * heads, A_V_DIM),
                         lambda bi, j: (bi, jnp.maximum(j - nkt, 0), 0)),
            newspec, newspec, newspec,
            pl.BlockSpec((1, A_V_DIM), lambda bi, j: (0, 0)),
        ],
        out_specs=newspec,
        out_shape=jax.ShapeDtypeStruct((b, t_new, w), BF16),
        scratch_shapes=[
            pltpu.VMEM((rows, past + LANES), F32),
            pltpu.VMEM((rows // 2, past + LANES), BF16),
            pltpu.VMEM((rows // 2, A_V_DIM), F32),
        ],
        compiler_params=_cparams(("parallel", "arbitrary")),
        name="attn_sample",
    )(lam, qbd, kct, vc3, kn, vn, rest3, gain)


def _block_diag_queries(q3, heads):
    b, t, w = q3.shape
    hm = 2 * heads
    q5 = q3.reshape(b, t, hm, A_HEAD_DIM).transpose(0, 2, 1, 3)
    eye = jnp.eye(hm, dtype=q3.dtype)
    qbd = q5[:, :, :, None, :] * eye[None, :, None, :, None]
    return qbd.reshape(b, hm * t, w)


def _hgrn_kernel(*refs, blk, nchunk, hp, has_s0):
    if has_s0:
        q_ref, f_ref, i_ref, og_ref, z_ref, lb_ref, gain_ref, s0_ref, y_ref, sf_ref, st_ref = refs
    else:
        q_ref, f_ref, i_ref, og_ref, z_ref, lb_ref, gain_ref, y_ref, sf_ref, st_ref = refs
    t = pl.program_id(2)

    @pl.when(t == 0)
    def _():
        st_ref[...] = s0_ref[...] if has_s0 else jnp.zeros_like(st_ref)

    lb = lb_ref[...]
    gain = gain_ref[...]
    row = lax.broadcasted_iota(jnp.int32, (blk, blk), 0)
    col = lax.broadcasted_iota(jnp.int32, (blk, blk), 1)
    causal = col <= row
    tril = jnp.where(causal, 1.0, 0.0).astype(BF16)
    ones = jnp.ones((blk, B_KEY_DIM), BF16)
    mid = (blk - 1) // 2
    nt = (((1,), (1,)), ((), ()))
    tn = (((0,), (0,)), ((), ()))

    def split3(g):
        hi = g.astype(BF16)
        r1 = g - hi.astype(F32)
        md = r1.astype(BF16)
        lo = (r1 - md.astype(F32)).astype(BF16)
        return hi, md, lo

    chunks = [slice(c * blk, (c + 1) * blk) for c in range(nchunk)]
    f = lb + (1.0 - lb) * jax.nn.sigmoid(f_ref[...].astype(F32))
    parts = split3(jnp.log(f))
    qv = _silu(q_ref[...].astype(F32))
    kv = 1.0 - f
    vv = i_ref[...]
    b = [sum(jnp.dot(tril, p[sl, :], preferred_element_type=F32) for p in parts) for sl in chunks]
    tot = [sum(lax.dot_general(p[sl, :], ones, tn, preferred_element_type=F32) for p in parts)
           for sl in chunks]
    qe, ke, qs, ks = [], [], [], []
    for sl, bc in zip(chunks, b):
        b_mid = bc[mid:mid + 1, :]
        b_last = bc[blk - 1:blk, :]
        qe.append((qv[sl, :] * jnp.exp(bc - b_mid)).astype(BF16))
        ke.append((kv[sl, :] * jnp.exp(b_mid - bc)).astype(BF16))
        qs.append((qv[sl, :] * jnp.exp(bc)).astype(BF16))
        ks.append((kv[sl, :] * jnp.exp(b_last - bc)).astype(BF16))
    lanes = [slice(h * B_KEY_DIM, (h + 1) * B_KEY_DIM) for h in range(hp)]
    o_intra, kvs = [], []
    for c, sl in enumerate(chunks):
        sc = [lax.dot_general(qe[c][:, hl], ke[c][:, hl], nt, preferred_element_type=F32)
              for hl in lanes]
        sc = [jnp.where(causal, s, 0.0).astype(BF16) for s in sc]
        o_intra.append([jnp.dot(s, vv[sl, hl], preferred_element_type=F32)
                        for s, hl in zip(sc, lanes)])
        kvs.append([lax.dot_general(ks[c][:, hl], vv[sl, hl], tn, preferred_element_type=F32)
                    for hl in lanes])
    decay = [jnp.exp(tc) for tc in tot]
    cols = []
    for h, hl in enumerate(lanes):
        state = st_ref[h]
        outs = []
        for c in range(nchunk):
            outs.append(o_intra[c][h] + jnp.dot(qs[c][:, hl], state.astype(BF16),
                                                preferred_element_type=F32))
            state = state * decay[c][hl, :] + kvs[c][h]
        st_ref[h] = state
        o = outs[0] if nchunk == 1 else jnp.concatenate(outs, axis=0)
        ms = jnp.mean(o * o, axis=1, keepdims=True)
        cols.append(o * lax.rsqrt(ms + NORM_EPS))
    o = cols[0] if hp == 1 else jnp.concatenate(cols, axis=1)
    y = o * gain * jax.nn.sigmoid(og_ref[...].astype(F32))
    y_ref[...] = (y * _silu(z_ref[...].astype(F32))).astype(BF16)

    @pl.when(t == pl.num_programs(2) - 1)
    def _():
        sf_ref[...] = st_ref[...]


def _hgrn(rest3, lb, gain, s0, blk, col0):
    b, t, _ = rest3.shape
    heads = lb.shape[1] // B_KEY_DIM
    tt = min(t, 512)
    assert t % tt == 0 and tt % blk == 0
    has_s0 = s0 is not None
    hp = heads if tt // blk == 1 else HGRN_HEADS_PER_STEP
    assert heads % hp == 0 and col0 % hp == 0
    width = hp * B_KEY_DIM
    kern = functools.partial(_hgrn_kernel, blk=blk, nchunk=tt // blk, hp=hp, has_s0=has_s0)

    def colspec(k):
        return pl.BlockSpec((None, tt, width),
                            lambda bi, hi, ti: (bi, ti, (col0 + k * heads) // hp + hi))

    vecspec = pl.BlockSpec((1, width), lambda bi, hi, ti: (0, hi))
    stspec = pl.BlockSpec((None, hp, B_KEY_DIM, B_KEY_DIM), lambda bi, hi, ti: (bi, hi, 0, 0))
    in_specs = [colspec(k) for k in range(5)] + [vecspec, vecspec]
    args = [rest3] * 5 + [lb, gain]
    if has_s0:
        in_specs.append(stspec)
        args.append(s0)
    return pl.pallas_call(
        kern,
        grid=(b, heads // hp, t // tt),
        in_specs=in_specs,
        out_specs=[
            pl.BlockSpec((None, tt, width), lambda bi, hi, ti: (bi, ti, hi)),
            stspec,
        ],
        out_shape=[
            jax.ShapeDtypeStruct((b, t, heads * B_KEY_DIM), BF16),
            jax.ShapeDtypeStruct((b, heads, B_KEY_DIM, B_KEY_DIM), F32),
        ],
        scratch_shapes=[pltpu.VMEM((hp, B_KEY_DIM, B_KEY_DIM), F32)],
        compiler_params=_cparams(("parallel", "parallel", "arbitrary")),
        name="hgrn",
    )(*args)


def _memattn_kernel(q_ref, z_ref, mk_ref, mv_ref, o_ref, kb_ref, vb_ref, *, heads):
    @pl.when(pl.program_id(1) == 0)
    def _():
        kb_ref[...] = mk_ref[...].astype(BF16)
        vb_ref[...] = mv_ref[...].astype(BF16)

    nt = (((1,), (1,)), ((), ()))
    for hd in range(heads):
        cs = slice(hd * C_HEAD_DIM, (hd + 1) * C_HEAD_DIM)
        s = lax.dot_general(q_ref[:, cs], kb_ref[:, cs], nt, preferred_element_type=F32)
        s = s * (C_HEAD_DIM ** -0.5)
        e = jnp.exp(s - jnp.max(s, axis=1, keepdims=True))
        l = jnp.sum(e, axis=1, keepdims=True)
        o = jnp.dot(e.astype(BF16), vb_ref[:, cs], preferred_element_type=F32) / l
        o_ref[:, cs] = (o * _silu(z_ref[:, cs].astype(F32))).astype(BF16)


def _memattn(rest3, mk3, mv3, qblock, zblock):
    b, t, _ = rest3.shape
    _, m, c = mk3.shape
    tq = min(t, 512)
    assert t % tq == 0
    kern = functools.partial(_memattn_kernel, heads=c // C_HEAD_DIM)
    memspec = pl.BlockSpec((None, m, c), lambda bi, ti: (bi, 0, 0))
    return pl.pallas_call(
        kern,
        grid=(b, t // tq),
        in_specs=[
            pl.BlockSpec((None, tq, c), lambda bi, ti: (bi, ti, qblock)),
            pl.BlockSpec((None, tq, c), lambda bi, ti: (bi, ti, zblock)),
            memspec, memspec,
        ],
        out_specs=pl.BlockSpec((None, tq, c), lambda bi, ti: (bi, ti, 0)),
        out_shape=jax.ShapeDtypeStruct((b, t, c), BF16),
        scratch_shapes=[pltpu.VMEM((m, c), BF16), pltpu.VMEM((m, c), BF16)],
        compiler_params=_cparams(("parallel", "arbitrary")),
        name="memattn",
    )(rest3, rest3, mk3, mv3)


def _merge_kernel(ya_ref, yb_ref, yc_ref, ga_ref, gb_ref, gc_ref, x_ref,
                  wa_ref, wb_ref, wc_ref, wo_ref, lng_ref, lnb_ref, o_ref, *, alpha):
    def branch(y_ref, w_ref, g_ref):
        return jax.nn.sigmoid(g_ref[...].astype(F32)) * jnp.dot(
            y_ref[...], w_ref[...], preferred_element_type=F32)

    merged = branch(ya_ref, wa_ref, ga_ref) + branch(yb_ref, wb_ref, gb_ref)
    merged = merged + branch(yc_ref, wc_ref, gc_ref)
    sub = jnp.dot(merged.astype(BF16), wo_ref[...], preferred_element_type=F32)
    hres = alpha * x_ref[...] + sub
    mu = jnp.mean(hres, axis=1, keepdims=True)
    cen = hres - mu
    var = jnp.mean(cen * cen, axis=1, keepdims=True)
    o_ref[...] = cen * lax.rsqrt(var + NORM_EPS) * lng_ref[...] + lnb_ref[...]


def _merge(ya, yb, yc, rest2, x2, wa, wb, wc, wo, lng, lnb, gate_block0, alpha):
    n, d = x2.shape
    w = ya.shape[1]
    tm = min(n, 256)
    assert n % tm == 0
    rows = lambda width, blk: pl.BlockSpec((tm, width), lambda i: (i, blk))
    const = lambda shape: pl.BlockSpec(shape, lambda i: (0, 0), pipeline_mode=pl.Buffered(1))
    return pl.pallas_call(
        functools.partial(_merge_kernel, alpha=alpha),
        grid=(n // tm,),
        in_specs=[
            rows(w, 0), rows(w, 0), rows(w, 0),
            rows(d, gate_block0), rows(d, gate_block0 + 1), rows(d, gate_block0 + 2),
            rows(d, 0),
            const((w, d)), const((w, d)), const((w, d)), const((d, d)),
            const((1, d)), const((1, d)),
        ],
        out_specs=rows(d, 0),
        out_shape=jax.ShapeDtypeStruct((n, d), F32),
        compiler_params=_cparams(("parallel",)),
        name="merge",
    )(ya, yb, yc, rest2, rest2, rest2, x2, wa, wb, wc, wo, lng, lnb)


def _layer(x, pos, past_k, past_v, s0, mk3, mv3, rec_block, layer_idx, lb, p):
    (w_in, lam, sub_norm, hgrn_gain, wa, wb, wc, wo, lng, lnb, alpha) = p
    n_b, t, d = x.shape
    a_width = wa.shape[0]
    heads = a_width // A_V_DIM
    x2 = x.reshape(n_b * t, d)
    prompt = past_k is None
    q_scale = A_HEAD_DIM ** -0.5 * math.log2(math.e)
    q2, k_out, v2, rest2 = _inproj(x2, w_in, _rope_tables(pos), t, a_width, q_scale,
                                   k_transposed=prompt)
    r = rest2.shape[1]
    rest3 = rest2.reshape(n_b, t, r)
    q3 = q2.reshape(n_b, t, a_width)
    v3 = v2.reshape(n_b, t, a_width)
    lam_init = 0.8 - 0.6 * math.exp(-0.3 * layer_idx)
    if prompt:
        ya = _attn_prompt(lam, q3, k_out, v3, rest3, sub_norm, lam_init)
        k5 = k_out.reshape(n_b, heads, 2, A_HEAD_DIM, t).transpose(0, 4, 1, 2, 3)
    else:
        k3 = k_out.reshape(n_b, t, a_width)
        ya = _attn_sample(lam, _block_diag_queries(q3, heads), past_k, past_v, k3, v3, rest3,
                          sub_norm, lam_init)
        k5 = k3.reshape(n_b, t, heads, 2, A_HEAD_DIM)
    yb, s_fin = _hgrn(rest3, lb, hgrn_gain, s0, rec_block, a_width // B_KEY_DIM)
    c_width = mk3.shape[2]
    c_off = a_width + 5 * lb.shape[1]
    assert c_off % c_width == 0
    yc = _memattn(rest3, mk3, mv3, c_off // c_width, c_off // c_width + 1)
    g_off = c_off + 2 * c_width
    assert g_off % d == 0
    y2 = _merge(ya.reshape(n_b * t, a_width), yb.reshape(n_b * t, -1), yc.reshape(n_b * t, c_width),
                rest2, x2, wa, wb, wc, wo, lng, lnb, g_off // d, alpha)
    return y2.reshape(n_b, t, d), k5, v3, s_fin


def kernel(x_prompt, x_sample, cache_attn_k, cache_attn_v, state_hgrn, cache_mem_k, cache_mem_v, mem_prompt, w_in, lambda_q1, lambda_k1, lambda_q2, lambda_k2, attn_sub_norm, hgrn_lb_logits, hgrn_norm, w_mem_k, w_mem_v, w_branch_a, w_branch_b, w_branch_c, w_out, ln_gamma, ln_beta):
    bp, seq, d = x_prompt.shape
    bs, t_new, _ = x_sample.shape
    depth = w_in.shape[0]
    past = cache_attn_k.shape[2]
    heads = cache_attn_k.shape[3]
    a_width = heads * A_V_DIM
    n_mem = mem_prompt.shape[1]
    c_heads = cache_mem_k.shape[3]
    c_width = c_heads * C_HEAD_DIM
    b_heads = state_hgrn.shape[2]
    alpha = (2 * depth) ** 0.25
    pos_prompt = jnp.arange(seq)
    pos_sample = past + jnp.arange(t_new)
    lower_bounds = jnp.cumsum(jax.nn.softmax(hgrn_lb_logits.astype(F32), axis=0), axis=0)

    h_p, h_s = x_prompt, x_sample
    outs = [[] for _ in range(8)]
    for l in range(depth):
        lam_init = 0.8 - 0.6 * math.exp(-0.3 * l)
        lam = (jnp.exp(jnp.sum(lambda_q1[l].astype(F32) * lambda_k1[l].astype(F32)))
               - jnp.exp(jnp.sum(lambda_q2[l].astype(F32) * lambda_k2[l].astype(F32))) + lam_init)
        params = (w_in[l].astype(BF16), lam.reshape(1), attn_sub_norm[l].reshape(1, -1),
                  hgrn_norm[l].reshape(1, -1), w_branch_a[l].astype(BF16), w_branch_b[l].astype(BF16),
                  w_branch_c[l].astype(BF16), w_out[l].astype(BF16), ln_gamma[l].reshape(1, -1),
                  ln_beta[l].reshape(1, -1), alpha)
        lb = lower_bounds[l].reshape(1, -1)
        mk_p, mv_p = _memkv(mem_prompt.reshape(bp * n_mem, d), w_mem_k[l].astype(BF16),
                            w_mem_v[l].astype(BF16))
        mk_p = mk_p.reshape(bp, n_mem, c_width)
        mv_p = mv_p.reshape(bp, n_mem, c_width)
        h_p, k_p, v_p, s_p = _layer(h_p, pos_prompt, None, None, None, mk_p, mv_p, CHUNK, l, lb, params)
        h_s, k_s, v_s, s_s = _layer(
            h_s, pos_sample,
            cache_attn_k[l].transpose(0, 2, 3, 4, 1).reshape(bs, a_width, past),
            cache_attn_v[l].reshape(bs, past * heads, A_V_DIM),
            state_hgrn[l], cache_mem_k[l].reshape(bs, n_mem, c_width),
            cache_mem_v[l].reshape(bs, n_mem, c_width), t_new, l, lb, params)
        new = (k_p, v_p.reshape(bp, seq, heads, A_V_DIM),
               s_p.astype(x_prompt.dtype), mk_p.reshape(bp, n_mem, c_heads, C_HEAD_DIM),
               mv_p.reshape(bp, n_mem, c_heads, C_HEAD_DIM),
               k_s, v_s.reshape(bs, t_new, heads, A_V_DIM),
               s_s.astype(x_sample.dtype))
        for acc, val in zip(outs, new):
            acc.append(val)
    return (h_p, h_s) + tuple(jnp.stack(o) for o in outs)
```

```python
import functools
import math

import jax
import jax.numpy as jnp
from jax import lax
from jax.experimental import pallas as pl
from jax.experimental.pallas import tpu as pltpu

F32 = jnp.float32
BF16 = jnp.bfloat16

CHUNK = 64
A_HEAD_DIM = 64
A_V_DIM = 128
ROT_DIM = 16
ROPE_THETA = 500000.0
B_KEY_DIM = 128
C_HEAD_DIM = 256
NORM_EPS = 1e-5
LANES = 128
VMEM_LIMIT = 56 * 1024 * 1024
ATTN_TILE = 512
PV_SPLIT = 4
NEXT_SCORES_AT = 3
HGRN_HEADS_PER_STEP = 4


def _cparams(sem):
    return pltpu.CompilerParams(dimension_semantics=sem, vmem_limit_bytes=VMEM_LIMIT)


def _silu(z):
    return z * jax.nn.sigmoid(z)


def _rope_tables(pos):
    half = ROT_DIM // 2
    inv_freq = jnp.power(ROPE_THETA, -jnp.arange(0, ROT_DIM, 2, dtype=F32) / ROT_DIM)
    ang = pos.astype(F32)[:, None] * inv_freq[None, :]
    cos, sin = jnp.cos(ang), jnp.sin(ang)
    t = pos.shape[0]
    pad = jnp.zeros((t, A_HEAD_DIM - ROT_DIM), F32)
    zero = jnp.zeros((t, half), F32)
    c64 = jnp.concatenate([cos, cos, pad + 1.0], axis=1)
    sa64 = jnp.concatenate([-sin, zero, pad], axis=1)
    sb64 = jnp.concatenate([zero, sin, pad], axis=1)
    rep = LANES // A_HEAD_DIM
    rows = tuple(jnp.tile(a, (1, rep)) for a in (c64, sa64, sb64))
    return rows, (cos.T, sin.T)


def _inproj_kernel(x_ref, w_ref, c_ref, sa_ref, sb_ref, ct_ref, st_ref,
                   q_ref, k_ref, v_ref, r_ref, xb_ref, t_ref, *, tn, nq, nk, nv, q_scale, k_transposed):
    j = pl.program_id(1)
    half = ROT_DIM // 2

    @pl.when(j == 0)
    def _():
        xb_ref[...] = x_ref[...].astype(BF16)

    def project():
        return jnp.dot(xb_ref[...], w_ref[...], preferred_element_type=F32)

    def rope_group(acc, g):
        xg = acc[:, g * LANES:(g + 1) * LANES]
        up = pltpu.roll(xg, LANES - half, 1)
        dn = pltpu.roll(xg, half, 1)
        return xg * c_ref[...] + up * sa_ref[...] + dn * sb_ref[...]

    @pl.when(j < nq)
    def _():
        acc = project()
        for g in range(tn // LANES):
            q_ref[:, g * LANES:(g + 1) * LANES] = (rope_group(acc, g) * q_scale).astype(BF16)

    @pl.when((j >= nq) & (j < nq + nk))
    def _():
        if k_transposed:
            t_ref[...] = project()
            kt = t_ref[...].T
            cos, sin = ct_ref[...], st_ref[...]
            for g in range(tn // A_HEAD_DIM):
                r0 = g * A_HEAD_DIM
                lo = kt[r0:r0 + half, :]
                hi = kt[r0 + half:r0 + ROT_DIM, :]
                k_ref[r0:r0 + half, :] = lo * cos - hi * sin
                k_ref[r0 + half:r0 + ROT_DIM, :] = hi * cos + lo * sin
                k_ref[r0 + ROT_DIM:r0 + A_HEAD_DIM, :] = kt[r0 + ROT_DIM:r0 + A_HEAD_DIM, :]
        else:
            acc = project()
            for g in range(tn // LANES):
                k_ref[:, g * LANES:(g + 1) * LANES] = rope_group(acc, g)

    @pl.when((j >= nq + nk) & (j < nq + nk + nv))
    def _():
        v_ref[...] = project()

    @pl.when(j >= nq + nk + nv)
    def _():
        r_ref[...] = project().astype(BF16)


def _inproj(x2, w_bf, tables, t_len, a_width, q_scale, k_transposed):
    n, d = x2.shape
    n_in = w_bf.shape[1]
    tm = min(n, 1024)
    tn = 512
    assert n % tm == 0 and a_width % tn == 0 and n_in % tn == 0
    row_tabs, col_tabs = tables
    if t_len >= tm:
        assert t_len % tm == 0
        per = t_len // tm
    else:
        assert tm % t_len == 0 and not k_transposed
        row_tabs = tuple(jnp.tile(a, (tm // t_len, 1)) for a in row_tabs)
        col_tabs = tuple(jnp.tile(a, (1, tm // t_len)) for a in col_tabs)
        per = 1
    nq = nk = nv = a_width // tn
    nr = n_in // tn - 3 * nq
    kblk = lambda j: jnp.clip(j - nq, 0, nk - 1)
    row_spec = pl.BlockSpec((tm, LANES), lambda i, j: (i % per, 0))
    col_spec = pl.BlockSpec((ROT_DIM // 2, tm), lambda i, j: (0, i % per))
    if k_transposed:
        k_spec = pl.BlockSpec((None, tn, tm), lambda i, j: (i // per, kblk(j), i % per))
        k_shape = jax.ShapeDtypeStruct((n // t_len, a_width, t_len), F32)
    else:
        k_spec = pl.BlockSpec((tm, tn), lambda i, j: (i, kblk(j)))
        k_shape = jax.ShapeDtypeStruct((n, a_width), F32)
    kern = functools.partial(_inproj_kernel, tn=tn, nq=nq, nk=nk, nv=nv, q_scale=q_scale,
                             k_transposed=k_transposed)
    return pl.pallas_call(
        kern,
        grid=(n // tm, n_in // tn),
        in_specs=[
            pl.BlockSpec((tm, d), lambda i, j: (i, 0)),
            pl.BlockSpec((d, tn), lambda i, j: (0, j)),
            row_spec, row_spec, row_spec, col_spec, col_spec,
        ],
        out_specs=[
            pl.BlockSpec((tm, tn), lambda i, j: (i, jnp.minimum(j, nq - 1))),
            k_spec,
            pl.BlockSpec((tm, tn), lambda i, j: (i, jnp.clip(j - nq - nk, 0, nv - 1))),
            pl.BlockSpec((tm, tn), lambda i, j: (i, jnp.maximum(j - nq - nk - nv, 0))),
        ],
        out_shape=[
            jax.ShapeDtypeStruct((n, a_width), BF16),
            k_shape,
            jax.ShapeDtypeStruct((n, a_width), F32),
            jax.ShapeDtypeStruct((n, nr * tn), BF16),
        ],
        scratch_shapes=[pltpu.VMEM((tm, d), BF16), pltpu.VMEM((tm, tn), F32)],
        compiler_params=_cparams(("parallel", "arbitrary")),
        name="inproj",
    )(x2, w_bf, *row_tabs, *col_tabs)


def _memkv_kernel(x_ref, wk_ref, wv_ref, k_ref, v_ref):
    xb = x_ref[...].astype(BF16)
    k_ref[...] = jnp.dot(xb, wk_ref[...], preferred_element_type=F32)
    v_ref[...] = jnp.dot(xb, wv_ref[...], preferred_element_type=F32)


def _memkv(x2, wk_bf, wv_bf):
    n, d = x2.shape
    c = wk_bf.shape[1]
    tm = min(n, 512)
    assert n % tm == 0
    wspec = pl.BlockSpec((d, c), lambda i: (0, 0))
    ospec = pl.BlockSpec((tm, c), lambda i: (i, 0))
    return pl.pallas_call(
        _memkv_kernel,
        grid=(n // tm,),
        in_specs=[pl.BlockSpec((tm, d), lambda i: (i, 0)), wspec, wspec],
        out_specs=[ospec, ospec],
        out_shape=[jax.ShapeDtypeStruct((n, c), F32)] * 2,
        compiler_params=_cparams(("parallel",)),
        name="memkv",
    )(x2, wk_bf, wv_bf)


def _attn_epilogue(o, lam_init, gain, z):
    ms = jnp.mean(o * o, axis=1, keepdims=True)
    y = o * lax.rsqrt(ms + NORM_EPS) * gain * (1.0 - lam_init)
    return y * _silu(z)


def _attn_prompt_kernel(lam_ref, q_ref, kt_ref, v_ref, z_ref, g_ref, o_ref,
                        kb_ref, vb_ref, qa_ref, sa_ref, sb_ref, p_ref, acc_ref, m_ref, al_ref,
                        *, tq, lam_init):
    s_len = q_ref.shape[0]
    strip = 32
    kb_ref[...] = kt_ref[...].astype(BF16)
    vb_ref[:, :A_V_DIM] = v_ref[...].astype(BF16)
    vb_ref[:, A_V_DIM:] = jnp.ones((s_len, A_V_DIM), BF16)
    lane = lax.broadcasted_iota(jnp.int32, (tq, A_V_DIM), 1)

    def scores(j, s_ref, maps=(0, 1)):
        kt = kb_ref[:, pl.ds(pl.multiple_of(j * tq, tq), tq)]
        for mp in maps:
            s_ref[mp] = jnp.dot(qa_ref[mp], kt, preferred_element_type=F32)

    def softmax_pv(j, s_ref, masked, nxt_ref=None):
        start = pl.multiple_of(j * tq, tq)
        grp = tq // PV_SPLIT
        if nxt_ref is not None:
            scores(j + 1, nxt_ref, (0,))
        for mp in range(2):
            for g0 in range(0, tq, grp):
                if nxt_ref is not None and mp * tq + g0 == NEXT_SCORES_AT * grp:
                    scores(j + 1, nxt_ref, (1,))
                nk = g0 + grp if masked else tq
                for r in range(g0, g0 + grp, strip):
                    rs = slice(r, r + strip)
                    sv = s_ref[mp, rs, :nk]
                    if masked:
                        cc = lax.broadcasted_iota(jnp.int32, (strip, nk), 1) // CHUNK
                        sv = jnp.where(cc <= r // CHUNK, sv, -jnp.inf)
                    m_old = m_ref[mp, rs, :]
                    mn = jnp.maximum(m_old, jnp.max(sv, axis=1, keepdims=True))
                    m_ref[mp, rs, :] = mn
                    al_ref[mp, rs, :] = jnp.exp2(m_old - mn)
                    for c in range(nk // LANES):
                        cs = slice(c * LANES, (c + 1) * LANES)
                        p_ref[mp, rs, cs] = jnp.exp2(sv[:, cs] - mn).astype(BF16)
                gs = slice(g0, g0 + grp)
                pv = jnp.dot(p_ref[mp, gs, :nk], vb_ref[pl.ds(start, nk), :],
                             preferred_element_type=F32)
                for c in range(2):
                    cs = slice(c * A_V_DIM, (c + 1) * A_V_DIM)
                    acc_ref[mp, gs, cs] = acc_ref[mp, gs, cs] * al_ref[mp, gs, :] + pv[:, cs]

    def first_scores(qi):
        qf = q_ref[pl.ds(pl.multiple_of(qi * tq, tq), tq), :].astype(F32)
        qa_ref[0] = jnp.where(lane < A_HEAD_DIM, qf, 0.0).astype(BF16)
        qa_ref[1] = jnp.where(lane >= A_HEAD_DIM, qf, 0.0).astype(BF16)
        scores(0, sa_ref)

    def q_tile(qi, _):
        rows = pl.ds(pl.multiple_of(qi * tq, tq), tq)
        m_ref[...] = jnp.full(m_ref.shape, -jnp.inf, F32)
        acc_ref[...] = jnp.zeros(acc_ref.shape, F32)

        def full_tile(j, carry):
            @pl.when(j % 2 == 0)
            def _():
                softmax_pv(j, sa_ref, False, sb_ref)

            @pl.when(j % 2 == 1)
            def _():
                softmax_pv(j, sb_ref, False, sa_ref)

            return carry

        lax.fori_loop(0, qi, full_tile, 0)

        @pl.when(qi % 2 == 0)
        def _():
            softmax_pv(qi, sa_ref, True)

        @pl.when(qi % 2 == 1)
        def _():
            softmax_pv(qi, sb_ref, True)

        first_scores(jnp.minimum(qi + 1, nq - 1))
        o = (acc_ref[0, :, :A_V_DIM] / acc_ref[0, :, A_V_DIM:]
             - lam_ref[0] * (acc_ref[1, :, :A_V_DIM] / acc_ref[1, :, A_V_DIM:]))
        y = _attn_epilogue(o, lam_init, g_ref[...], z_ref[rows, :].astype(F32))
        o_ref[rows, :] = y.astype(BF16)
        return 0

    nq = s_len // tq
    first_scores(0)
    lax.fori_loop(0, nq, q_tile, 0)


def _attn_prompt(lam, q3, kt3, v3, rest3, gain, lam_init):
    b, s, w = q3.shape
    h = w // A_V_DIM
    tq = min(s, ATTN_TILE)
    assert s % tq == 0 and tq % (PV_SPLIT * LANES) == 0
    kern = functools.partial(_attn_prompt_kernel, tq=tq, lam_init=lam_init)
    qspec = pl.BlockSpec((None, s, A_V_DIM), lambda bi, hi: (bi, 0, hi))
    return pl.pallas_call(
        kern,
        grid=(b, h),
        in_specs=[
            pl.BlockSpec(memory_space=pltpu.SMEM),
            qspec,
            pl.BlockSpec((None, A_V_DIM, s), lambda bi, hi: (bi, hi, 0)),
            qspec, qspec,
            pl.BlockSpec((1, A_V_DIM), lambda bi, hi: (0, 0)),
        ],
        out_specs=qspec,
        out_shape=jax.ShapeDtypeStruct((b, s, w), BF16),
        scratch_shapes=[
            pltpu.VMEM((A_V_DIM, s), BF16),
            pltpu.VMEM((s, 2 * A_V_DIM), BF16),
            pltpu.VMEM((2, tq, A_V_DIM), BF16),
            pltpu.VMEM((2, tq, tq), F32),
            pltpu.VMEM((2, tq, tq), F32),
            pltpu.VMEM((2, tq, tq), BF16),
            pltpu.VMEM((2, tq, 2 * A_V_DIM), F32),
            pltpu.VMEM((2, tq, LANES), F32),
            pltpu.VMEM((2, tq, LANES), F32),
        ],
        compiler_params=_cparams(("parallel", "parallel")),
        name="attn_prompt",
    )(lam, q3, kt3, v3, rest3, gain)


def _attn_sample_kernel(lam_ref, q_ref, kc_ref, vc_ref, kn_ref, vn_ref, z_ref, g_ref, o_ref,
                        qbd_ref, s_ref, w_ref, acc_ref, *, tk, nkt, t_new, heads, lam_init):
    j = pl.program_id(1)
    past = tk * nkt
    nt = (((1,), (1,)), ((), ()))
    width = kn_ref.shape[1]

    @pl.when(j == 0)
    def _():
        lane = lax.broadcasted_iota(jnp.int32, (t_new, A_V_DIM), 1)
        zero = jnp.zeros((t_new, A_V_DIM), BF16)
        for hd in range(heads):
            qh = q_ref[:, hd * A_V_DIM:(hd + 1) * A_V_DIM].astype(F32)
            for mp in range(2):
                r0 = (2 * hd + mp) * t_new
                mine = lane >= A_HEAD_DIM if mp == 1 else lane < A_HEAD_DIM
                own = jnp.where(mine, qh, 0.0).astype(BF16)
                for hb in range(heads):
                    qbd_ref[r0:r0 + t_new, hb * A_V_DIM:(hb + 1) * A_V_DIM] = own if hb == hd else zero

    def pad_new(ref):
        new = ref[...]
        return jnp.concatenate([new, jnp.zeros((LANES - t_new, width), F32)], axis=0).astype(BF16)

    @pl.when(j < nkt)
    def _():
        s = jnp.dot(qbd_ref[...], kc_ref[...].astype(BF16), preferred_element_type=F32)
        s_ref[:, pl.ds(pl.multiple_of(j * tk, tk), tk)] = s

    @pl.when(j == nkt - 1)
    def _():
        sn = lax.dot_general(qbd_ref[...], pad_new(kn_ref), nt, preferred_element_type=F32)
        col = lax.broadcasted_iota(jnp.int32, sn.shape, 1)
        s_ref[:, past:past + LANES] = jnp.where(col < t_new, sn, -jnp.inf)
        lam = lam_ref[0]
        for hd in range(heads):
            r0 = hd * 2 * t_new
            p = []
            for mp in range(2):
                s = s_ref[r0 + mp * t_new:r0 + (mp + 1) * t_new, :]
                e = jnp.exp2(s - jnp.max(s, axis=1, keepdims=True))
                p.append(e / jnp.sum(e, axis=1, keepdims=True))
            w_ref[hd * t_new:(hd + 1) * t_new, :] = (p[0] - lam * p[1]).astype(BF16)

    @pl.when(j == nkt)
    def _():
        full = jnp.dot(w_ref[:, past:past + LANES], pad_new(vn_ref), preferred_element_type=F32)
        for hd in range(heads):
            rs = slice(hd * t_new, (hd + 1) * t_new)
            acc_ref[rs, :] = full[rs, hd * A_V_DIM:(hd + 1) * A_V_DIM]

    @pl.when(j >= nkt)
    def _():
        start = pl.multiple_of((j - nkt) * tk, tk)
        for hd in range(heads):
            rs = slice(hd * t_new, (hd + 1) * t_new)
            vh = vc_ref[pl.ds(hd, tk, stride=heads), :].astype(BF16)
            acc_ref[rs, :] += jnp.dot(w_ref[rs, pl.ds(start, tk)], vh, preferred_element_type=F32)

    @pl.when(j == 2 * nkt - 1)
    def _():
        for hd in range(heads):
            cs = slice(hd * A_V_DIM, (hd + 1) * A_V_DIM)
            o = acc_ref[hd * t_new:(hd + 1) * t_new, :]
            y = _attn_epilogue(o, lam_init, g_ref[...], z_ref[:, cs].astype(F32))
            o_ref[:, cs] = y.astype(BF16)


def _attn_sample(lam, q3, kct, vc3, kn, vn, rest3, gain, lam_init):
    b, w, past = kct.shape
    t_new = kn.shape[1]
    heads = w // A_V_DIM
    rows = 2 * heads * t_new
    tk = min(past, 1024)
    assert past % tk == 0 and t_new % 16 == 0 and t_new <= LANES
    nkt = past // tk
    kern = functools.partial(_attn_sample_kernel, tk=tk, nkt=nkt, t_new=t_new, heads=heads,
                             lam_init=lam_init)
    newspec = pl.BlockSpec((None, t_new, w), lambda bi, j: (bi, 0, 0))
    return pl.pallas_call(
        kern,
        grid=(b, 2 * nkt),
        in_specs=[
            pl.BlockSpec(memory_space=pltpu.SMEM),
            newspec,
            pl.BlockSpec((None, w, tk), lambda bi, j: (bi, 0, jnp.minimum(j, nkt - 1))),
            pl.BlockSpec((None, tk * heads, A_V_DIM),
                         lambda bi, j: (bi, jnp.maximum(j - nkt, 0), 0)),
            newspec, newspec, newspec,
            pl.BlockSpec((1, A_V_DIM), lambda bi, j: (0, 0)),
        ],
        out_specs=newspec,
        out_shape=jax.ShapeDtypeStruct((b, t_new, w), BF16),
        scratch_shapes=[
            pltpu.VMEM((rows, w), BF16),
            pltpu.VMEM((rows, past + LANES), F32),
            pltpu.VMEM((rows // 2, past + LANES), BF16),
            pltpu.VMEM((rows // 2, A_V_DIM), F32),
        ],
        compiler_params=_cparams(("parallel", "arbitrary")),
        name="attn_sample",
    )(lam, q3, kct, vc3, kn, vn, rest3, gain)


def _hgrn_kernel(*refs, blk, nchunk, hp, has_s0):
    if has_s0:
        q_ref, f_ref, i_ref, og_ref, z_ref, lb_ref, gain_ref, s0_ref, y_ref, sf_ref, st_ref = refs
    else:
        q_ref, f_ref, i_ref, og_ref, z_ref, lb_ref, gain_ref, y_ref, sf_ref, st_ref = refs
    t = pl.program_id(2)

    @pl.when(t == 0)
    def _():
        st_ref[...] = s0_ref[...] if has_s0 else jnp.zeros_like(st_ref)

    lb = lb_ref[...]
    gain = gain_ref[...]
    row = lax.broadcasted_iota(jnp.int32, (blk, blk), 0)
    col = lax.broadcasted_iota(jnp.int32, (blk, blk), 1)
    causal = col <= row
    tril = jnp.where(causal, 1.0, 0.0).astype(BF16)
    ones = jnp.ones((blk, B_KEY_DIM), BF16)
    mid = (blk - 1) // 2
    nt = (((1,), (1,)), ((), ()))
    tn = (((0,), (0,)), ((), ()))

    def split3(g):
        hi = g.astype(BF16)
        r1 = g - hi.astype(F32)
        md = r1.astype(BF16)
        lo = (r1 - md.astype(F32)).astype(BF16)
        return hi, md, lo

    chunks = [slice(c * blk, (c + 1) * blk) for c in range(nchunk)]
    f = lb + (1.0 - lb) * jax.nn.sigmoid(f_ref[...].astype(F32))
    parts = split3(jnp.log(f))
    qv = _silu(q_ref[...].astype(F32))
    kv = 1.0 - f
    vv = i_ref[...]
    b = [sum(jnp.dot(tril, p[sl, :], preferred_element_type=F32) for p in parts) for sl in chunks]
    tot = [sum(lax.dot_general(p[sl, :], ones, tn, preferred_element_type=F32) for p in parts)
           for sl in chunks]
    qe, ke, qs, ks = [], [], [], []
    for sl, bc in zip(chunks, b):
        b_mid = bc[mid:mid + 1, :]
        b_last = bc[blk - 1:blk, :]
        qe.append((qv[sl, :] * jnp.exp(bc - b_mid)).astype(BF16))
        ke.append((kv[sl, :] * jnp.exp(b_mid - bc)).astype(BF16))
        qs.append((qv[sl, :] * jnp.exp(bc)).astype(BF16))
        ks.append((kv[sl, :] * jnp.exp(b_last - bc)).astype(BF16))
    lanes = [slice(h * B_KEY_DIM, (h + 1) * B_KEY_DIM) for h in range(hp)]
    o_intra, kvs = [], []
    for c, sl in enumerate(chunks):
        sc = [lax.dot_general(qe[c][:, hl], ke[c][:, hl], nt, preferred_element_type=F32)
              for hl in lanes]
        sc = [jnp.where(causal, s, 0.0).astype(BF16) for s in sc]
        o_intra.append([jnp.dot(s, vv[sl, hl], preferred_element_type=F32)
                        for s, hl in zip(sc, lanes)])
        kvs.append([lax.dot_general(ks[c][:, hl], vv[sl, hl], tn, preferred_element_type=F32)
                    for hl in lanes])
    decay = [jnp.exp(tc) for tc in tot]
    cols = []
    for h, hl in enumerate(lanes):
        state = st_ref[h]
        outs = []
        for c in range(nchunk):
            outs.append(o_intra[c][h] + jnp.dot(qs[c][:, hl], state.astype(BF16),
                                                preferred_element_type=F32))
            state = state * decay[c][hl, :] + kvs[c][h]
        st_ref[h] = state
        o = outs[0] if nchunk == 1 else jnp.concatenate(outs, axis=0)
        ms = jnp.mean(o * o, axis=1, keepdims=True)
        cols.append(o * lax.rsqrt(ms + NORM_EPS))
    o = cols[0] if hp == 1 else jnp.concatenate(cols, axis=1)
    y = o * gain * jax.nn.sigmoid(og_ref[...].astype(F32))
    y_ref[...] = (y * _silu(z_ref[...].astype(F32))).astype(BF16)

    @pl.when(t == pl.num_programs(2) - 1)
    def _():
        sf_ref[...] = st_ref[...]


def _hgrn(rest3, lb, gain, s0, blk, col0):
    b, t, _ = rest3.shape
    heads = lb.shape[1] // B_KEY_DIM
    tt = min(t, 512)
    assert t % tt == 0 and tt % blk == 0
    has_s0 = s0 is not None
    hp = heads if tt // blk == 1 else HGRN_HEADS_PER_STEP
    assert heads % hp == 0 and col0 % hp == 0
    width = hp * B_KEY_DIM
    kern = functools.partial(_hgrn_kernel, blk=blk, nchunk=tt // blk, hp=hp, has_s0=has_s0)

    def colspec(k):
        return pl.BlockSpec((None, tt, width),
                            lambda bi, hi, ti: (bi, ti, (col0 + k * heads) // hp + hi))

    vecspec = pl.BlockSpec((1, width), lambda bi, hi, ti: (0, hi))
    stspec = pl.BlockSpec((None, hp, B_KEY_DIM, B_KEY_DIM), lambda bi, hi, ti: (bi, hi, 0, 0))
    in_specs = [colspec(k) for k in range(5)] + [vecspec, vecspec]
    args = [rest3] * 5 + [lb, gain]
    if has_s0:
        in_specs.append(stspec)
        args.append(s0)
    return pl.pallas_call(
        kern,
        grid=(b, heads // hp, t // tt),
        in_specs=in_specs,
        out_specs=[
            pl.BlockSpec((None, tt, width), lambda bi, hi, ti: (bi, ti, hi)),
            stspec,
        ],
        out_shape=[
            jax.ShapeDtypeStruct((b, t, heads * B_KEY_DIM), BF16),
            jax.ShapeDtypeStruct((b, heads, B_KEY_DIM, B_KEY_DIM), F32),
        ],
        scratch_shapes=[pltpu.VMEM((hp, B_KEY_DIM, B_KEY_DIM), F32)],
        compiler_params=_cparams(("parallel", "parallel", "arbitrary")),
        name="hgrn",
    )(*args)


def _memattn_kernel(q_ref, z_ref, mk_ref, mv_ref, o_ref, kb_ref, vb_ref, *, heads):
    @pl.when(pl.program_id(1) == 0)
    def _():
        kb_ref[...] = mk_ref[...].astype(BF16)
        vb_ref[...] = mv_ref[...].astype(BF16)

    nt = (((1,), (1,)), ((), ()))
    for hd in range(heads):
        cs = slice(hd * C_HEAD_DIM, (hd + 1) * C_HEAD_DIM)
        s = lax.dot_general(q_ref[:, cs], kb_ref[:, cs], nt, preferred_element_type=F32)
        s = s * (C_HEAD_DIM ** -0.5)
        e = jnp.exp(s - jnp.max(s, axis=1, keepdims=True))
        l = jnp.sum(e, axis=1, keepdims=True)
        o = jnp.dot(e.astype(BF16), vb_ref[:, cs], preferred_element_type=F32) / l
        o_ref[:, cs] = (o * _silu(z_ref[:, cs].astype(F32))).astype(BF16)


def _memattn(rest3, mk3, mv3, qblock, zblock):
    b, t, _ = rest3.shape
    _, m, c = mk3.shape
    tq = min(t, 512)
    assert t % tq == 0
    kern = functools.partial(_memattn_kernel, heads=c // C_HEAD_DIM)
    memspec = pl.BlockSpec((None, m, c), lambda bi, ti: (bi, 0, 0))
    return pl.pallas_call(
        kern,
        grid=(b, t // tq),
        in_specs=[
            pl.BlockSpec((None, tq, c), lambda bi, ti: (bi, ti, qblock)),
            pl.BlockSpec((None, tq, c), lambda bi, ti: (bi, ti, zblock)),
            memspec, memspec,
        ],
        out_specs=pl.BlockSpec((None, tq, c), lambda bi, ti: (bi, ti, 0)),
        out_shape=jax.ShapeDtypeStruct((b, t, c), BF16),
        scratch_shapes=[pltpu.VMEM((m, c), BF16), pltpu.VMEM((m, c), BF16)],
        compiler_params=_cparams(("parallel", "arbitrary")),
        name="memattn",
    )(rest3, rest3, mk3, mv3)


def _merge_kernel(ya_ref, yb_ref, yc_ref, ga_ref, gb_ref, gc_ref, x_ref,
                  wa_ref, wb_ref, wc_ref, wo_ref, lng_ref, lnb_ref, o_ref, *, alpha):
    def branch(y_ref, w_ref, g_ref):
        return jax.nn.sigmoid(g_ref[...].astype(F32)) * jnp.dot(
            y_ref[...], w_ref[...], preferred_element_type=F32)

    merged = branch(ya_ref, wa_ref, ga_ref) + branch(yb_ref, wb_ref, gb_ref)
    merged = merged + branch(yc_ref, wc_ref, gc_ref)
    sub = jnp.dot(merged.astype(BF16), wo_ref[...], preferred_element_type=F32)
    hres = alpha * x_ref[...] + sub
    mu = jnp.mean(hres, axis=1, keepdims=True)
    cen = hres - mu
    var = jnp.mean(cen * cen, axis=1, keepdims=True)
    o_ref[...] = cen * lax.rsqrt(var + NORM_EPS) * lng_ref[...] + lnb_ref[...]


def _merge(ya, yb, yc, rest2, x2, wa, wb, wc, wo, lng, lnb, gate_block0, alpha):
    n, d = x2.shape
    w = ya.shape[1]
    tm = min(n, 256)
    assert n % tm == 0
    rows = lambda width, blk: pl.BlockSpec((tm, width), lambda i: (i, blk))
    const = lambda shape: pl.BlockSpec(shape, lambda i: (0, 0), pipeline_mode=pl.Buffered(1))
    return pl.pallas_call(
        functools.partial(_merge_kernel, alpha=alpha),
        grid=(n // tm,),
        in_specs=[
            rows(w, 0), rows(w, 0), rows(w, 0),
            rows(d, gate_block0), rows(d, gate_block0 + 1), rows(d, gate_block0 + 2),
            rows(d, 0),
            const((w, d)), const((w, d)), const((w, d)), const((d, d)),
            const((1, d)), const((1, d)),
        ],
        out_specs=rows(d, 0),
        out_shape=jax.ShapeDtypeStruct((n, d), F32),
        compiler_params=_cparams(("parallel",)),
        name="merge",
    )(ya, yb, yc, rest2, rest2, rest2, x2, wa, wb, wc, wo, lng, lnb)


def _layer(x, pos, past_k, past_v, s0, mk3, mv3, rec_block, layer_idx, lb, p):
    (w_in, lam, sub_norm, hgrn_gain, wa, wb, wc, wo, lng, lnb, alpha) = p
    n_b, t, d = x.shape
    a_width = wa.shape[0]
    heads = a_width // A_V_DIM
    x2 = x.reshape(n_b * t, d)
    prompt = past_k is None
    q_scale = A_HEAD_DIM ** -0.5 * math.log2(math.e)
    q2, k_out, v2, rest2 = _inproj(x2, w_in, _rope_tables(pos), t, a_width, q_scale,
                                   k_transposed=prompt)
    r = rest2.shape[1]
    rest3 = rest2.reshape(n_b, t, r)
    q3 = q2.reshape(n_b, t, a_width)
    v3 = v2.reshape(n_b, t, a_width)
    lam_init = 0.8 - 0.6 * math.exp(-0.3 * layer_idx)
    if prompt:
        ya = _attn_prompt(lam, q3, k_out, v3, rest3, sub_norm, lam_init)
        k5 = k_out.reshape(n_b, heads, 2, A_HEAD_DIM, t).transpose(0, 4, 1, 2, 3)
    else:
        k3 = k_out.reshape(n_b, t, a_width)
        ya = _attn_sample(lam, q3, past_k, past_v, k3, v3, rest3,
                          sub_norm, lam_init)
        k5 = k3.reshape(n_b, t, heads, 2, A_HEAD_DIM)
    yb, s_fin = _hgrn(rest3, lb, hgrn_gain, s0, rec_block, a_width // B_KEY_DIM)
    c_width = mk3.shape[2]
    c_off = a_width + 5 * lb.shape[1]
    assert c_off % c_width == 0
    yc = _memattn(rest3, mk3, mv3, c_off // c_width, c_off // c_width + 1)
    g_off = c_off + 2 * c_width
    assert g_off % d == 0
    y2 = _merge(ya.reshape(n_b * t, a_width), yb.reshape(n_b * t, -1), yc.reshape(n_b * t, c_width),
                rest2, x2, wa, wb, wc, wo, lng, lnb, g_off // d, alpha)
    return y2.reshape(n_b, t, d), k5, v3, s_fin


def kernel(x_prompt, x_sample, cache_attn_k, cache_attn_v, state_hgrn, cache_mem_k, cache_mem_v, mem_prompt, w_in, lambda_q1, lambda_k1, lambda_q2, lambda_k2, attn_sub_norm, hgrn_lb_logits, hgrn_norm, w_mem_k, w_mem_v, w_branch_a, w_branch_b, w_branch_c, w_out, ln_gamma, ln_beta):
    bp, seq, d = x_prompt.shape
    bs, t_new, _ = x_sample.shape
    depth = w_in.shape[0]
    past = cache_attn_k.shape[2]
    heads = cache_attn_k.shape[3]
    a_width = heads * A_V_DIM
    n_mem = mem_prompt.shape[1]
    c_heads = cache_mem_k.shape[3]
    c_width = c_heads * C_HEAD_DIM
    b_heads = state_hgrn.shape[2]
    alpha = (2 * depth) ** 0.25
    pos_prompt = jnp.arange(seq)
    pos_sample = past + jnp.arange(t_new)
    lower_bounds = jnp.cumsum(jax.nn.softmax(hgrn_lb_logits.astype(F32), axis=0), axis=0)

    h_p, h_s = x_prompt, x_sample
    outs = [[] for _ in range(8)]
    for l in range(depth):
        lam_init = 0.8 - 0.6 * math.exp(-0.3 * l)
        lam = (jnp.exp(jnp.sum(lambda_q1[l].astype(F32) * lambda_k1[l].astype(F32)))
               - jnp.exp(jnp.sum(lambda_q2[l].astype(F32) * lambda_k2[l].astype(F32))) + lam_init)
        params = (w_in[l].astype(BF16), lam.reshape(1), attn_sub_norm[l].reshape(1, -1),
                  hgrn_norm[l].reshape(1, -1), w_branch_a[l].astype(BF16), w_branch_b[l].astype(BF16),
                  w_branch_c[l].astype(BF16), w_out[l].astype(BF16), ln_gamma[l].reshape(1, -1),
                  ln_beta[l].reshape(1, -1), alpha)
        lb = lower_bounds[l].reshape(1, -1)
        mk_p, mv_p = _memkv(mem_prompt.reshape(bp * n_mem, d), w_mem_k[l].astype(BF16),
                            w_mem_v[l].astype(BF16))
        mk_p = mk_p.reshape(bp, n_mem, c_width)
        mv_p = mv_p.reshape(bp, n_mem, c_width)
        h_p, k_p, v_p, s_p = _layer(h_p, pos_prompt, None, None, None, mk_p, mv_p, CHUNK, l, lb, params)
        h_s, k_s, v_s, s_s = _layer(
            h_s, pos_sample,
            cache_attn_k[l].transpose(0, 2, 3, 4, 1).reshape(bs, a_width, past),
            cache_attn_v[l].reshape(bs, past * heads, A_V_DIM),
            state_hgrn[l], cache_mem_k[l].reshape(bs, n_mem, c_width),
            cache_mem_v[l].reshape(bs, n_mem, c_width), t_new, l, lb, params)
        new = (k_p, v_p.reshape(bp, seq, heads, A_V_DIM),
               s_p.astype(x_prompt.dtype), mk_p.reshape(bp, n_mem, c_heads, C_HEAD_DIM),
               mv_p.reshape(bp, n_mem, c_heads, C_HEAD_DIM),
               k_s, v_s.reshape(bs, t_new, heads, A_V_DIM),
               s_s.astype(x_sample.dtype))
        for acc, val in zip(outs, new):
            acc.append(val)
    return (h_p, h_s) + tuple(jnp.stack(o) for o in outs)
```

```python
import functools
import math

import jax
import jax.numpy as jnp
from jax import lax
from jax.experimental import pallas as pl
from jax.experimental.pallas import tpu as pltpu

F32 = jnp.float32
BF16 = jnp.bfloat16

CHUNK = 64
A_HEAD_DIM = 64
A_V_DIM = 128
ROT_DIM = 16
ROPE_THETA = 500000.0
B_KEY_DIM = 128
C_HEAD_DIM = 256
NORM_EPS = 1e-5
LANES = 128
VMEM_LIMIT = 56 * 1024 * 1024
ATTN_TILE = 512
PV_SPLIT = 4
ROW_CHUNKS = 4
NEXT_SCORES_AT = 3
HGRN_HEADS_PER_STEP = 4


def _cparams(sem):
    return pltpu.CompilerParams(dimension_semantics=sem, vmem_limit_bytes=VMEM_LIMIT)


def _silu(z):
    return z * jax.nn.sigmoid(z)


def _rope_tables(pos):
    half = ROT_DIM // 2
    inv_freq = jnp.power(ROPE_THETA, -jnp.arange(0, ROT_DIM, 2, dtype=F32) / ROT_DIM)
    ang = pos.astype(F32)[:, None] * inv_freq[None, :]
    cos, sin = jnp.cos(ang), jnp.sin(ang)
    t = pos.shape[0]
    pad = jnp.zeros((t, A_HEAD_DIM - ROT_DIM), F32)
    zero = jnp.zeros((t, half), F32)
    c64 = jnp.concatenate([cos, cos, pad + 1.0], axis=1)
    sa64 = jnp.concatenate([-sin, zero, pad], axis=1)
    sb64 = jnp.concatenate([zero, sin, pad], axis=1)
    rep = LANES // A_HEAD_DIM
    rows = tuple(jnp.tile(a, (1, rep)) for a in (c64, sa64, sb64))
    return rows, (cos.T, sin.T)


def _inproj_kernel(x_ref, w_ref, c_ref, sa_ref, sb_ref, ct_ref, st_ref,
                   q_ref, k_ref, v_ref, r_ref, xb_ref, t_ref, *, tn, nq, nk, nv, q_scale, k_transposed):
    j = pl.program_id(1)
    half = ROT_DIM // 2

    @pl.when(j == 0)
    def _():
        xb_ref[...] = x_ref[...].astype(BF16)

    tm = xb_ref.shape[0]
    chunks = [slice(r, r + tm // ROW_CHUNKS) for r in range(0, tm, tm // ROW_CHUNKS)]

    def project(rows=slice(None)):
        return jnp.dot(xb_ref[rows, :], w_ref[...], preferred_element_type=F32)

    def rope_group(acc, g, rows=slice(None)):
        xg = acc[:, g * LANES:(g + 1) * LANES]
        up = pltpu.roll(xg, LANES - half, 1)
        dn = pltpu.roll(xg, half, 1)
        return xg * c_ref[rows, :] + up * sa_ref[rows, :] + dn * sb_ref[rows, :]

    @pl.when(j < nq)
    def _():
        for rows in chunks:
            acc = project(rows)
            for g in range(tn // LANES):
                q_ref[rows, g * LANES:(g + 1) * LANES] = (
                    rope_group(acc, g, rows) * q_scale).astype(BF16)

    @pl.when((j >= nq) & (j < nq + nk))
    def _():
        if k_transposed:
            for rows in chunks:
                t_ref[rows, :] = project(rows)
            kt = t_ref[...].T
            cos, sin = ct_ref[...], st_ref[...]
            for g in range(tn // A_HEAD_DIM):
                r0 = g * A_HEAD_DIM
                lo = kt[r0:r0 + half, :]
                hi = kt[r0 + half:r0 + ROT_DIM, :]
                k_ref[r0:r0 + half, :] = lo * cos - hi * sin
                k_ref[r0 + half:r0 + ROT_DIM, :] = hi * cos + lo * sin
                k_ref[r0 + ROT_DIM:r0 + A_HEAD_DIM, :] = kt[r0 + ROT_DIM:r0 + A_HEAD_DIM, :]
        else:
            acc = project()
            for g in range(tn // LANES):
                k_ref[:, g * LANES:(g + 1) * LANES] = rope_group(acc, g)

    @pl.when((j >= nq + nk) & (j < nq + nk + nv))
    def _():
        for rows in chunks:
            v_ref[rows, :] = project(rows)

    @pl.when(j >= nq + nk + nv)
    def _():
        for rows in chunks:
            r_ref[rows, :] = project(rows).astype(BF16)


def _inproj(x2, w_bf, tables, t_len, a_width, q_scale, k_transposed):
    n, d = x2.shape
    n_in = w_bf.shape[1]
    tm = min(n, 1024)
    tn = 512
    assert n % tm == 0 and a_width % tn == 0 and n_in % tn == 0
    row_tabs, col_tabs = tables
    if t_len >= tm:
        assert t_len % tm == 0
        per = t_len // tm
    else:
        assert tm % t_len == 0 and not k_transposed
        row_tabs = tuple(jnp.tile(a, (tm // t_len, 1)) for a in row_tabs)
        col_tabs = tuple(jnp.tile(a, (1, tm // t_len)) for a in col_tabs)
        per = 1
    nq = nk = nv = a_width // tn
    nr = n_in // tn - 3 * nq
    kblk = lambda j: jnp.clip(j - nq, 0, nk - 1)
    row_spec = pl.BlockSpec((tm, LANES), lambda i, j: (i % per, 0))
    col_spec = pl.BlockSpec((ROT_DIM // 2, tm), lambda i, j: (0, i % per))
    if k_transposed:
        k_spec = pl.BlockSpec((None, tn, tm), lambda i, j: (i // per, kblk(j), i % per))
        k_shape = jax.ShapeDtypeStruct((n // t_len, a_width, t_len), F32)
    else:
        k_spec = pl.BlockSpec((tm, tn), lambda i, j: (i, kblk(j)))
        k_shape = jax.ShapeDtypeStruct((n, a_width), F32)
    kern = functools.partial(_inproj_kernel, tn=tn, nq=nq, nk=nk, nv=nv, q_scale=q_scale,
                             k_transposed=k_transposed)
    return pl.pallas_call(
        kern,
        grid=(n // tm, n_in // tn),
        in_specs=[
            pl.BlockSpec((tm, d), lambda i, j: (i, 0)),
            pl.BlockSpec((d, tn), lambda i, j: (0, j)),
            row_spec, row_spec, row_spec, col_spec, col_spec,
        ],
        out_specs=[
            pl.BlockSpec((tm, tn), lambda i, j: (i, jnp.minimum(j, nq - 1))),
            k_spec,
            pl.BlockSpec((tm, tn), lambda i, j: (i, jnp.clip(j - nq - nk, 0, nv - 1))),
            pl.BlockSpec((tm, tn), lambda i, j: (i, jnp.maximum(j - nq - nk - nv, 0))),
        ],
        out_shape=[
            jax.ShapeDtypeStruct((n, a_width), BF16),
            k_shape,
            jax.ShapeDtypeStruct((n, a_width), F32),
            jax.ShapeDtypeStruct((n, nr * tn), BF16),
        ],
        scratch_shapes=[pltpu.VMEM((tm, d), BF16), pltpu.VMEM((tm, tn), F32)],
        compiler_params=_cparams(("parallel", "arbitrary")),
        name="inproj",
    )(x2, w_bf, *row_tabs, *col_tabs)


def _memkv_kernel(x_ref, wk_ref, wv_ref, k_ref, v_ref):
    xb = x_ref[...].astype(BF16)
    k_ref[...] = jnp.dot(xb, wk_ref[...], preferred_element_type=F32)
    v_ref[...] = jnp.dot(xb, wv_ref[...], preferred_element_type=F32)


def _memkv(x2, wk_bf, wv_bf):
    n, d = x2.shape
    c = wk_bf.shape[1]
    tm = min(n, 512)
    assert n % tm == 0
    wspec = pl.BlockSpec((d, c), lambda i: (0, 0))
    ospec = pl.BlockSpec((tm, c), lambda i: (i, 0))
    return pl.pallas_call(
        _memkv_kernel,
        grid=(n // tm,),
        in_specs=[pl.BlockSpec((tm, d), lambda i: (i, 0)), wspec, wspec],
        out_specs=[ospec, ospec],
        out_shape=[jax.ShapeDtypeStruct((n, c), F32)] * 2,
        compiler_params=_cparams(("parallel",)),
        name="memkv",
    )(x2, wk_bf, wv_bf)


def _attn_epilogue(o, lam_init, gain, z):
    ms = jnp.mean(o * o, axis=1, keepdims=True)
    y = o * lax.rsqrt(ms + NORM_EPS) * gain * (1.0 - lam_init)
    return y * _silu(z)


def _attn_prompt_kernel(lam_ref, q_ref, kt_ref, v_ref, z_ref, g_ref, o_ref,
                        kb_ref, vb_ref, qa_ref, sa_ref, sb_ref, p_ref, acc_ref, m_ref, al_ref,
                        *, tq, lam_init):
    s_len = q_ref.shape[0]
    strip = 32
    kb_ref[...] = kt_ref[...].astype(BF16)
    vb_ref[:, :A_V_DIM] = v_ref[...].astype(BF16)
    vb_ref[:, A_V_DIM:] = jnp.ones((s_len, A_V_DIM), BF16)
    lane = lax.broadcasted_iota(jnp.int32, (tq, A_V_DIM), 1)

    def scores(j, s_ref, maps=(0, 1)):
        kt = kb_ref[:, pl.ds(pl.multiple_of(j * tq, tq), tq)]
        for mp in maps:
            s_ref[mp] = jnp.dot(qa_ref[mp], kt, preferred_element_type=F32)

    def softmax_pv(j, s_ref, masked, nxt_ref=None):
        start = pl.multiple_of(j * tq, tq)
        grp = tq // PV_SPLIT
        if nxt_ref is not None:
            scores(j + 1, nxt_ref, (0,))
        for mp in range(2):
            for g0 in range(0, tq, grp):
                if nxt_ref is not None and mp * tq + g0 == NEXT_SCORES_AT * grp:
                    scores(j + 1, nxt_ref, (1,))
                nk = g0 + grp if masked else tq
                for r in range(g0, g0 + grp, strip):
                    rs = slice(r, r + strip)
                    sv = s_ref[mp, rs, :nk]
                    if masked:
                        cc = lax.broadcasted_iota(jnp.int32, (strip, nk), 1) // CHUNK
                        sv = jnp.where(cc <= r // CHUNK, sv, -jnp.inf)
                    m_old = m_ref[mp, rs, :]
                    mn = jnp.maximum(m_old, jnp.max(sv, axis=1, keepdims=True))
                    m_ref[mp, rs, :] = mn
                    al_ref[mp, rs, :] = jnp.exp2(m_old - mn)
                    for c in range(nk // LANES):
                        cs = slice(c * LANES, (c + 1) * LANES)
                        p_ref[mp, rs, cs] = jnp.exp2(sv[:, cs] - mn).astype(BF16)
                gs = slice(g0, g0 + grp)
                pv = jnp.dot(p_ref[mp, gs, :nk], vb_ref[pl.ds(start, nk), :],
                             preferred_element_type=F32)
                for c in range(2):
                    cs = slice(c * A_V_DIM, (c + 1) * A_V_DIM)
                    acc_ref[mp, gs, cs] = acc_ref[mp, gs, cs] * al_ref[mp, gs, :] + pv[:, cs]

    def first_scores(qi):
        qf = q_ref[pl.ds(pl.multiple_of(qi * tq, tq), tq), :].astype(F32)
        qa_ref[0] = jnp.where(lane < A_HEAD_DIM, qf, 0.0).astype(BF16)
        qa_ref[1] = jnp.where(lane >= A_HEAD_DIM, qf, 0.0).astype(BF16)
        scores(0, sa_ref)

    def q_tile(qi, _):
        rows = pl.ds(pl.multiple_of(qi * tq, tq), tq)
        m_ref[...] = jnp.full(m_ref.shape, -jnp.inf, F32)
        acc_ref[...] = jnp.zeros(acc_ref.shape, F32)

        def full_tile(j, carry):
            @pl.when(j % 2 == 0)
            def _():
                softmax_pv(j, sa_ref, False, sb_ref)

            @pl.when(j % 2 == 1)
            def _():
                softmax_pv(j, sb_ref, False, sa_ref)

            return carry

        lax.fori_loop(0, qi, full_tile, 0)

        @pl.when(qi % 2 == 0)
        def _():
            softmax_pv(qi, sa_ref, True)

        @pl.when(qi % 2 == 1)
        def _():
            softmax_pv(qi, sb_ref, True)

        first_scores(jnp.minimum(qi + 1, nq - 1))
        o = (acc_ref[0, :, :A_V_DIM] / acc_ref[0, :, A_V_DIM:]
             - lam_ref[0] * (acc_ref[1, :, :A_V_DIM] / acc_ref[1, :, A_V_DIM:]))
        y = _attn_epilogue(o, lam_init, g_ref[...], z_ref[rows, :].astype(F32))
        o_ref[rows, :] = y.astype(BF16)
        return 0

    nq = s_len // tq
    first_scores(0)
    lax.fori_loop(0, nq, q_tile, 0)


def _attn_prompt(lam, q3, kt3, v3, rest3, gain, lam_init):
    b, s, w = q3.shape
    h = w // A_V_DIM
    tq = min(s, ATTN_TILE)
    assert s % tq == 0 and tq % (PV_SPLIT * LANES) == 0
    kern = functools.partial(_attn_prompt_kernel, tq=tq, lam_init=lam_init)
    qspec = pl.BlockSpec((None, s, A_V_DIM), lambda bi, hi: (bi, 0, hi))
    return pl.pallas_call(
        kern,
        grid=(b, h),
        in_specs=[
            pl.BlockSpec(memory_space=pltpu.SMEM),
            qspec,
            pl.BlockSpec((None, A_V_DIM, s), lambda bi, hi: (bi, hi, 0)),
            qspec, qspec,
            pl.BlockSpec((1, A_V_DIM), lambda bi, hi: (0, 0)),
        ],
        out_specs=qspec,
        out_shape=jax.ShapeDtypeStruct((b, s, w), BF16),
        scratch_shapes=[
            pltpu.VMEM((A_V_DIM, s), BF16),
            pltpu.VMEM((s, 2 * A_V_DIM), BF16),
            pltpu.VMEM((2, tq, A_V_DIM), BF16),
            pltpu.VMEM((2, tq, tq), F32),
            pltpu.VMEM((2, tq, tq), F32),
            pltpu.VMEM((2, tq, tq), BF16),
            pltpu.VMEM((2, tq, 2 * A_V_DIM), F32),
            pltpu.VMEM((2, tq, LANES), F32),
            pltpu.VMEM((2, tq, LANES), F32),
        ],
        compiler_params=_cparams(("parallel", "parallel")),
        name="attn_prompt",
    )(lam, q3, kt3, v3, rest3, gain)


def _attn_sample_kernel(lam_ref, q_ref, kc_ref, vc_ref, kn_ref, vn_ref, z_ref, g_ref, o_ref,
                        qbd_ref, s_ref, w_ref, acc_ref, *, tk, nkt, t_new, heads, lam_init):
    j = pl.program_id(1)
    past = tk * nkt
    nt = (((1,), (1,)), ((), ()))
    width = kn_ref.shape[1]

    @pl.when(j == 0)
    def _():
        lane = lax.broadcasted_iota(jnp.int32, (t_new, A_V_DIM), 1)
        zero = jnp.zeros((t_new, A_V_DIM), BF16)
        for hd in range(heads):
            qh = q_ref[:, hd * A_V_DIM:(hd + 1) * A_V_DIM].astype(F32)
            for mp in range(2):
                r0 = (2 * hd + mp) * t_new
                mine = lane >= A_HEAD_DIM if mp == 1 else lane < A_HEAD_DIM
                own = jnp.where(mine, qh, 0.0).astype(BF16)
                for hb in range(heads):
                    qbd_ref[r0:r0 + t_new, hb * A_V_DIM:(hb + 1) * A_V_DIM] = own if hb == hd else zero

    def pad_new(ref):
        new = ref[...]
        return jnp.concatenate([new, jnp.zeros((LANES - t_new, width), F32)], axis=0).astype(BF16)

    @pl.when(j < nkt)
    def _():
        s = jnp.dot(qbd_ref[...], kc_ref[...].astype(BF16), preferred_element_type=F32)
        s_ref[:, pl.ds(pl.multiple_of(j * tk, tk), tk)] = s

    @pl.when(j == nkt - 1)
    def _():
        sn = lax.dot_general(qbd_ref[...], pad_new(kn_ref), nt, preferred_element_type=F32)
        col = lax.broadcasted_iota(jnp.int32, sn.shape, 1)
        s_ref[:, past:past + LANES] = jnp.where(col < t_new, sn, -jnp.inf)
        lam = lam_ref[0]
        for hd in range(heads):
            r0 = hd * 2 * t_new
            p = []
            for mp in range(2):
                s = s_ref[r0 + mp * t_new:r0 + (mp + 1) * t_new, :]
                e = jnp.exp2(s - jnp.max(s, axis=1, keepdims=True))
                p.append(e / jnp.sum(e, axis=1, keepdims=True))
            w_ref[hd * t_new:(hd + 1) * t_new, :] = (p[0] - lam * p[1]).astype(BF16)

    @pl.when(j == nkt)
    def _():
        full = jnp.dot(w_ref[:, past:past + LANES], pad_new(vn_ref), preferred_element_type=F32)
        for hd in range(heads):
            rs = slice(hd * t_new, (hd + 1) * t_new)
            acc_ref[rs, :] = full[rs, hd * A_V_DIM:(hd + 1) * A_V_DIM]

    @pl.when(j >= nkt)
    def _():
        start = pl.multiple_of((j - nkt) * tk, tk)
        for hd in range(heads):
            rs = slice(hd * t_new, (hd + 1) * t_new)
            vh = vc_ref[pl.ds(hd, tk, stride=heads), :].astype(BF16)
            acc_ref[rs, :] += jnp.dot(w_ref[rs, pl.ds(start, tk)], vh, preferred_element_type=F32)

    @pl.when(j == 2 * nkt - 1)
    def _():
        for hd in range(heads):
            cs = slice(hd * A_V_DIM, (hd + 1) * A_V_DIM)
            o = acc_ref[hd * t_new:(hd + 1) * t_new, :]
            y = _attn_epilogue(o, lam_init, g_ref[...], z_ref[:, cs].astype(F32))
            o_ref[:, cs] = y.astype(BF16)


def _attn_sample(lam, q3, kct, vc3, kn, vn, rest3, gain, lam_init):
    b, w, past = kct.shape
    t_new = kn.shape[1]
    heads = w // A_V_DIM
    rows = 2 * heads * t_new
    tk = min(past, 2048)
    assert past % tk == 0 and t_new % 16 == 0 and t_new <= LANES
    nkt = past // tk
    kern = functools.partial(_attn_sample_kernel, tk=tk, nkt=nkt, t_new=t_new, heads=heads,
                             lam_init=lam_init)
    newspec = pl.BlockSpec((None, t_new, w), lambda bi, j: (bi, 0, 0))
    return pl.pallas_call(
        kern,
        grid=(b, 2 * nkt),
        in_specs=[
            pl.BlockSpec(memory_space=pltpu.SMEM),
            newspec,
            pl.BlockSpec((None, w, tk), lambda bi, j: (bi, 0, jnp.minimum(j, nkt - 1))),
            pl.BlockSpec((None, tk * heads, A_V_DIM),
                         lambda bi, j: (bi, jnp.maximum(j - nkt, 0), 0)),
            newspec, newspec, newspec,
            pl.BlockSpec((1, A_V_DIM), lambda bi, j: (0, 0)),
        ],
        out_specs=newspec,
        out_shape=jax.ShapeDtypeStruct((b, t_new, w), BF16),
        scratch_shapes=[
            pltpu.VMEM((rows, w), BF16),
            pltpu.VMEM((rows, past + LANES), F32),
            pltpu.VMEM((rows // 2, past + LANES), BF16),
            pltpu.VMEM((rows // 2, A_V_DIM), F32),
        ],
        compiler_params=_cparams(("parallel", "arbitrary")),
        name="attn_sample",
    )(lam, q3, kct, vc3, kn, vn, rest3, gain)


def _hgrn_kernel(*refs, blk, nchunk, hp, has_s0):
    if has_s0:
        q_ref, f_ref, i_ref, og_ref, z_ref, lb_ref, gain_ref, s0_ref, y_ref, sf_ref, st_ref = refs
    else:
        q_ref, f_ref, i_ref, og_ref, z_ref, lb_ref, gain_ref, y_ref, sf_ref, st_ref = refs
    t = pl.program_id(2)

    @pl.when(t == 0)
    def _():
        st_ref[...] = s0_ref[...] if has_s0 else jnp.zeros_like(st_ref)

    lb = lb_ref[...]
    gain = gain_ref[...]
    row = lax.broadcasted_iota(jnp.int32, (blk, blk), 0)
    col = lax.broadcasted_iota(jnp.int32, (blk, blk), 1)
    causal = col <= row
    tril = jnp.where(causal, 1.0, 0.0).astype(BF16)
    ones = jnp.ones((blk, B_KEY_DIM), BF16)
    mid = (blk - 1) // 2
    nt = (((1,), (1,)), ((), ()))
    tn = (((0,), (0,)), ((), ()))

    def split3(g):
        hi = g.astype(BF16)
        r1 = g - hi.astype(F32)
        md = r1.astype(BF16)
        lo = (r1 - md.astype(F32)).astype(BF16)
        return hi, md, lo

    chunks = [slice(c * blk, (c + 1) * blk) for c in range(nchunk)]
    f = lb + (1.0 - lb) * jax.nn.sigmoid(f_ref[...].astype(F32))
    parts = split3(jnp.log(f))
    qv = _silu(q_ref[...].astype(F32))
    kv = 1.0 - f
    vv = i_ref[...]
    b = [sum(jnp.dot(tril, p[sl, :], preferred_element_type=F32) for p in parts) for sl in chunks]
    tot = [sum(lax.dot_general(p[sl, :], ones, tn, preferred_element_type=F32) for p in parts)
           for sl in chunks]
    qe, ke, qs, ks = [], [], [], []
    for sl, bc in zip(chunks, b):
        b_mid = bc[mid:mid + 1, :]
        b_last = bc[blk - 1:blk, :]
        qe.append((qv[sl, :] * jnp.exp(bc - b_mid)).astype(BF16))
        ke.append((kv[sl, :] * jnp.exp(b_mid - bc)).astype(BF16))
        qs.append((qv[sl, :] * jnp.exp(bc)).astype(BF16))
        ks.append((kv[sl, :] * jnp.exp(b_last - bc)).astype(BF16))
    lanes = [slice(h * B_KEY_DIM, (h + 1) * B_KEY_DIM) for h in range(hp)]
    o_intra, kvs = [], []
    for c, sl in enumerate(chunks):
        sc = [lax.dot_general(qe[c][:, hl], ke[c][:, hl], nt, preferred_element_type=F32)
              for hl in lanes]
        sc = [jnp.where(causal, s, 0.0).astype(BF16) for s in sc]
        o_intra.append([jnp.dot(s, vv[sl, hl], preferred_element_type=F32)
                        for s, hl in zip(sc, lanes)])
        kvs.append([lax.dot_general(ks[c][:, hl], vv[sl, hl], tn, preferred_element_type=F32)
                    for hl in lanes])
    decay = [jnp.exp(tc) for tc in tot]
    cols = []
    for h, hl in enumerate(lanes):
        state = st_ref[h]
        outs = []
        for c in range(nchunk):
            outs.append(o_intra[c][h] + jnp.dot(qs[c][:, hl], state.astype(BF16),
                                                preferred_element_type=F32))
            state = state * decay[c][hl, :] + kvs[c][h]
        st_ref[h] = state
        o = outs[0] if nchunk == 1 else jnp.concatenate(outs, axis=0)
        ms = jnp.mean(o * o, axis=1, keepdims=True)
        cols.append(o * lax.rsqrt(ms + NORM_EPS))
    o = cols[0] if hp == 1 else jnp.concatenate(cols, axis=1)
    y = o * gain * jax.nn.sigmoid(og_ref[...].astype(F32))
    y_ref[...] = (y * _silu(z_ref[...].astype(F32))).astype(BF16)

    @pl.when(t == pl.num_programs(2) - 1)
    def _():
        sf_ref[...] = st_ref[...]


def _hgrn(rest3, lb, gain, s0, blk, col0):
    b, t, _ = rest3.shape
    heads = lb.shape[1] // B_KEY_DIM
    tt = min(t, 512)
    assert t % tt == 0 and tt % blk == 0
    has_s0 = s0 is not None
    hp = heads if tt // blk == 1 else HGRN_HEADS_PER_STEP
    assert heads % hp == 0 and col0 % hp == 0
    width = hp * B_KEY_DIM
    kern = functools.partial(_hgrn_kernel, blk=blk, nchunk=tt // blk, hp=hp, has_s0=has_s0)

    def colspec(k):
        return pl.BlockSpec((None, tt, width),
                            lambda bi, hi, ti: (bi, ti, (col0 + k * heads) // hp + hi))

    vecspec = pl.BlockSpec((1, width), lambda bi, hi, ti: (0, hi))
    stspec = pl.BlockSpec((None, hp, B_KEY_DIM, B_KEY_DIM), lambda bi, hi, ti: (bi, hi, 0, 0))
    in_specs = [colspec(k) for k in range(5)] + [vecspec, vecspec]
    args = [rest3] * 5 + [lb, gain]
    if has_s0:
        in_specs.append(stspec)
        args.append(s0)
    return pl.pallas_call(
        kern,
        grid=(b, heads // hp, t // tt),
        in_specs=in_specs,
        out_specs=[
            pl.BlockSpec((None, tt, width), lambda bi, hi, ti: (bi, ti, hi)),
            stspec,
        ],
        out_shape=[
            jax.ShapeDtypeStruct((b, t, heads * B_KEY_DIM), BF16),
            jax.ShapeDtypeStruct((b, heads, B_KEY_DIM, B_KEY_DIM), F32),
        ],
        scratch_shapes=[pltpu.VMEM((hp, B_KEY_DIM, B_KEY_DIM), F32)],
        compiler_params=_cparams(("parallel", "parallel", "arbitrary")),
        name="hgrn",
    )(*args)


def _memattn_kernel(q_ref, z_ref, mk_ref, mv_ref, o_ref, kb_ref, vb_ref, *, heads):
    @pl.when(pl.program_id(1) == 0)
    def _():
        kb_ref[...] = mk_ref[...].astype(BF16)
        vb_ref[...] = mv_ref[...].astype(BF16)

    nt = (((1,), (1,)), ((), ()))
    cols = [slice(hd * C_HEAD_DIM, (hd + 1) * C_HEAD_DIM) for hd in range(heads)]
    s = [lax.dot_general(q_ref[:, cs], kb_ref[:, cs], nt, preferred_element_type=F32)
         * (C_HEAD_DIM ** -0.5) for cs in cols]
    e = [jnp.exp(sh - jnp.max(sh, axis=1, keepdims=True)) for sh in s]
    o = [jnp.dot(eh.astype(BF16), vb_ref[:, cs], preferred_element_type=F32)
         / jnp.sum(eh, axis=1, keepdims=True) for eh, cs in zip(e, cols)]
    for oh, cs in zip(o, cols):
        o_ref[:, cs] = (oh * _silu(z_ref[:, cs].astype(F32))).astype(BF16)


def _memattn(rest3, mk3, mv3, qblock, zblock):
    b, t, _ = rest3.shape
    _, m, c = mk3.shape
    tq = min(t, 512)
    assert t % tq == 0
    kern = functools.partial(_memattn_kernel, heads=c // C_HEAD_DIM)
    memspec = pl.BlockSpec((None, m, c), lambda bi, ti: (bi, 0, 0))
    return pl.pallas_call(
        kern,
        grid=(b, t // tq),
        in_specs=[
            pl.BlockSpec((None, tq, c), lambda bi, ti: (bi, ti, qblock)),
            pl.BlockSpec((None, tq, c), lambda bi, ti: (bi, ti, zblock)),
            memspec, memspec,
        ],
        out_specs=pl.BlockSpec((None, tq, c), lambda bi, ti: (bi, ti, 0)),
        out_shape=jax.ShapeDtypeStruct((b, t, c), BF16),
        scratch_shapes=[pltpu.VMEM((m, c), BF16), pltpu.VMEM((m, c), BF16)],
        compiler_params=_cparams(("parallel", "arbitrary")),
        name="memattn",
    )(rest3, rest3, mk3, mv3)


def _merge_kernel(ya_ref, yb_ref, yc_ref, ga_ref, gb_ref, gc_ref, x_ref,
                  wa_ref, wb_ref, wc_ref, wo_ref, lng_ref, lnb_ref, o_ref, *, alpha):
    def branch(y_ref, w_ref, g_ref):
        return jax.nn.sigmoid(g_ref[...].astype(F32)) * jnp.dot(
            y_ref[...], w_ref[...], preferred_element_type=F32)

    merged = branch(ya_ref, wa_ref, ga_ref) + branch(yb_ref, wb_ref, gb_ref)
    merged = merged + branch(yc_ref, wc_ref, gc_ref)
    sub = jnp.dot(merged.astype(BF16), wo_ref[...], preferred_element_type=F32)
    hres = alpha * x_ref[...] + sub
    mu = jnp.mean(hres, axis=1, keepdims=True)
    cen = hres - mu
    var = jnp.mean(cen * cen, axis=1, keepdims=True)
    o_ref[...] = cen * lax.rsqrt(var + NORM_EPS) * lng_ref[...] + lnb_ref[...]


def _merge(ya, yb, yc, rest2, x2, wa, wb, wc, wo, lng, lnb, gate_block0, alpha):
    n, d = x2.shape
    w = ya.shape[1]
    tm = min(n, 256)
    assert n % tm == 0
    rows = lambda width, blk: pl.BlockSpec((tm, width), lambda i: (i, blk))
    const = lambda shape: pl.BlockSpec(shape, lambda i: (0, 0), pipeline_mode=pl.Buffered(1))
    return pl.pallas_call(
        functools.partial(_merge_kernel, alpha=alpha),
        grid=(n // tm,),
        in_specs=[
            rows(w, 0), rows(w, 0), rows(w, 0),
            rows(d, gate_block0), rows(d, gate_block0 + 1), rows(d, gate_block0 + 2),
            rows(d, 0),
            const((w, d)), const((w, d)), const((w, d)), const((d, d)),
            const((1, d)), const((1, d)),
        ],
        out_specs=rows(d, 0),
        out_shape=jax.ShapeDtypeStruct((n, d), F32),
        compiler_params=_cparams(("parallel",)),
        name="merge",
    )(ya, yb, yc, rest2, rest2, rest2, x2, wa, wb, wc, wo, lng, lnb)


def _layer(x, pos, past_k, past_v, s0, mk3, mv3, rec_block, layer_idx, lb, p):
    (w_in, lam, sub_norm, hgrn_gain, wa, wb, wc, wo, lng, lnb, alpha) = p
    n_b, t, d = x.shape
    a_width = wa.shape[0]
    heads = a_width // A_V_DIM
    x2 = x.reshape(n_b * t, d)
    prompt = past_k is None
    q_scale = A_HEAD_DIM ** -0.5 * math.log2(math.e)
    q2, k_out, v2, rest2 = _inproj(x2, w_in, _rope_tables(pos), t, a_width, q_scale,
                                   k_transposed=prompt)
    r = rest2.shape[1]
    rest3 = rest2.reshape(n_b, t, r)
    q3 = q2.reshape(n_b, t, a_width)
    v3 = v2.reshape(n_b, t, a_width)
    lam_init = 0.8 - 0.6 * math.exp(-0.3 * layer_idx)
    if prompt:
        ya = _attn_prompt(lam, q3, k_out, v3, rest3, sub_norm, lam_init)
        k5 = k_out.reshape(n_b, heads, 2, A_HEAD_DIM, t).transpose(0, 4, 1, 2, 3)
    else:
        k3 = k_out.reshape(n_b, t, a_width)
        ya = _attn_sample(lam, q3, past_k, past_v, k3, v3, rest3,
                          sub_norm, lam_init)
        k5 = k3.reshape(n_b, t, heads, 2, A_HEAD_DIM)
    yb, s_fin = _hgrn(rest3, lb, hgrn_gain, s0, rec_block, a_width // B_KEY_DIM)
    c_width = mk3.shape[2]
    c_off = a_width + 5 * lb.shape[1]
    assert c_off % c_width == 0
    yc = _memattn(rest3, mk3, mv3, c_off // c_width, c_off // c_width + 1)
    g_off = c_off + 2 * c_width
    assert g_off % d == 0
    y2 = _merge(ya.reshape(n_b * t, a_width), yb.reshape(n_b * t, -1), yc.reshape(n_b * t, c_width),
                rest2, x2, wa, wb, wc, wo, lng, lnb, g_off // d, alpha)
    return y2.reshape(n_b, t, d), k5, v3, s_fin


def kernel(x_prompt, x_sample, cache_attn_k, cache_attn_v, state_hgrn, cache_mem_k, cache_mem_v, mem_prompt, w_in, lambda_q1, lambda_k1, lambda_q2, lambda_k2, attn_sub_norm, hgrn_lb_logits, hgrn_norm, w_mem_k, w_mem_v, w_branch_a, w_branch_b, w_branch_c, w_out, ln_gamma, ln_beta):
    bp, seq, d = x_prompt.shape
    bs, t_new, _ = x_sample.shape
    depth = w_in.shape[0]
    past = cache_attn_k.shape[2]
    heads = cache_attn_k.shape[3]
    a_width = heads * A_V_DIM
    n_mem = mem_prompt.shape[1]
    c_heads = cache_mem_k.shape[3]
    c_width = c_heads * C_HEAD_DIM
    b_heads = state_hgrn.shape[2]
    alpha = (2 * depth) ** 0.25
    pos_prompt = jnp.arange(seq)
    pos_sample = past + jnp.arange(t_new)
    lower_bounds = jnp.cumsum(jax.nn.softmax(hgrn_lb_logits.astype(F32), axis=0), axis=0)

    h_p, h_s = x_prompt, x_sample
    outs = [[] for _ in range(8)]
    for l in range(depth):
        lam_init = 0.8 - 0.6 * math.exp(-0.3 * l)
        lam = (jnp.exp(jnp.sum(lambda_q1[l].astype(F32) * lambda_k1[l].astype(F32)))
               - jnp.exp(jnp.sum(lambda_q2[l].astype(F32) * lambda_k2[l].astype(F32))) + lam_init)
        params = (w_in[l].astype(BF16), lam.reshape(1), attn_sub_norm[l].reshape(1, -1),
                  hgrn_norm[l].reshape(1, -1), w_branch_a[l].astype(BF16), w_branch_b[l].astype(BF16),
                  w_branch_c[l].astype(BF16), w_out[l].astype(BF16), ln_gamma[l].reshape(1, -1),
                  ln_beta[l].reshape(1, -1), alpha)
        lb = lower_bounds[l].reshape(1, -1)
        mk_p, mv_p = _memkv(mem_prompt.reshape(bp * n_mem, d), w_mem_k[l].astype(BF16),
                            w_mem_v[l].astype(BF16))
        mk_p = mk_p.reshape(bp, n_mem, c_width)
        mv_p = mv_p.reshape(bp, n_mem, c_width)
        h_p, k_p, v_p, s_p = _layer(h_p, pos_prompt, None, None, None, mk_p, mv_p, CHUNK, l, lb, params)
        h_s, k_s, v_s, s_s = _layer(
            h_s, pos_sample,
            cache_attn_k[l].transpose(0, 2, 3, 4, 1).reshape(bs, a_width, past),
            cache_attn_v[l].reshape(bs, past * heads, A_V_DIM),
            state_hgrn[l], cache_mem_k[l].reshape(bs, n_mem, c_width),
            cache_mem_v[l].reshape(bs, n_mem, c_width), t_new, l, lb, params)
        new = (k_p, v_p.reshape(bp, seq, heads, A_V_DIM),
               s_p.astype(x_prompt.dtype), mk_p.reshape(bp, n_mem, c_heads, C_HEAD_DIM),
               mv_p.reshape(bp, n_mem, c_heads, C_HEAD_DIM),
               k_s, v_s.reshape(bs, t_new, heads, A_V_DIM),
               s_s.astype(x_sample.dtype))
        for acc, val in zip(outs, new):
            acc.append(val)
    return (h_p, h_s) + tuple(jnp.stack(o) for o in outs)
```

```python
import functools
import math

import jax
import jax.numpy as jnp
from jax import lax
from jax.experimental import pallas as pl
from jax.experimental.pallas import tpu as pltpu

F32 = jnp.float32
BF16 = jnp.bfloat16

CHUNK = 64
A_HEAD_DIM = 64
A_V_DIM = 128
ROT_DIM = 16
ROPE_THETA = 500000.0
B_KEY_DIM = 128
C_HEAD_DIM = 256
NORM_EPS = 1e-5
LANES = 128
VMEM_LIMIT = 56 * 1024 * 1024
ATTN_TILE = 512
PV_SPLIT = 4
ROW_CHUNKS = 4
NEXT_SCORES_AT = 3
HGRN_HEADS_PER_STEP = 8


def _cparams(sem):
    return pltpu.CompilerParams(dimension_semantics=sem, vmem_limit_bytes=VMEM_LIMIT)


def _silu(z):
    return z * jax.nn.sigmoid(z)


def _rope_tables(pos):
    half = ROT_DIM // 2
    inv_freq = jnp.power(ROPE_THETA, -jnp.arange(0, ROT_DIM, 2, dtype=F32) / ROT_DIM)
    ang = pos.astype(F32)[:, None] * inv_freq[None, :]
    cos, sin = jnp.cos(ang), jnp.sin(ang)
    t = pos.shape[0]
    pad = jnp.zeros((t, A_HEAD_DIM - ROT_DIM), F32)
    zero = jnp.zeros((t, half), F32)
    c64 = jnp.concatenate([cos, cos, pad + 1.0], axis=1)
    sa64 = jnp.concatenate([-sin, zero, pad], axis=1)
    sb64 = jnp.concatenate([zero, sin, pad], axis=1)
    rep = LANES // A_HEAD_DIM
    rows = tuple(jnp.tile(a, (1, rep)) for a in (c64, sa64, sb64))
    return rows, (cos.T, sin.T)


def _inproj_kernel(x_ref, w_ref, c_ref, sa_ref, sb_ref, ct_ref, st_ref,
                   q_ref, k_ref, v_ref, r_ref, xb_ref, t_ref, *, tn, nq, nk, nv, q_scale, k_transposed):
    j = pl.program_id(1)
    half = ROT_DIM // 2

    @pl.when(j == 0)
    def _():
        xb_ref[...] = x_ref[...].astype(BF16)

    tm = xb_ref.shape[0]
    chunks = [slice(r, r + tm // ROW_CHUNKS) for r in range(0, tm, tm // ROW_CHUNKS)]

    def project(rows=slice(None)):
        return jnp.dot(xb_ref[rows, :], w_ref[...], preferred_element_type=F32)

    def rope_group(acc, g, rows=slice(None)):
        xg = acc[:, g * LANES:(g + 1) * LANES]
        up = pltpu.roll(xg, LANES - half, 1)
        dn = pltpu.roll(xg, half, 1)
        return xg * c_ref[rows, :] + up * sa_ref[rows, :] + dn * sb_ref[rows, :]

    @pl.when(j < nq)
    def _():
        for rows in chunks:
            acc = project(rows)
            for g in range(tn // LANES):
                q_ref[rows, g * LANES:(g + 1) * LANES] = (
                    rope_group(acc, g, rows) * q_scale).astype(BF16)

    @pl.when((j >= nq) & (j < nq + nk))
    def _():
        if k_transposed:
            for rows in chunks:
                t_ref[rows, :] = project(rows)
            kt = t_ref[...].T
            cos, sin = ct_ref[...], st_ref[...]
            for g in range(tn // A_HEAD_DIM):
                r0 = g * A_HEAD_DIM
                lo = kt[r0:r0 + half, :]
                hi = kt[r0 + half:r0 + ROT_DIM, :]
                k_ref[r0:r0 + half, :] = lo * cos - hi * sin
                k_ref[r0 + half:r0 + ROT_DIM, :] = hi * cos + lo * sin
                k_ref[r0 + ROT_DIM:r0 + A_HEAD_DIM, :] = kt[r0 + ROT_DIM:r0 + A_HEAD_DIM, :]
        else:
            acc = project()
            for g in range(tn // LANES):
                k_ref[:, g * LANES:(g + 1) * LANES] = rope_group(acc, g)

    @pl.when((j >= nq + nk) & (j < nq + nk + nv))
    def _():
        for rows in chunks:
            v_ref[rows, :] = project(rows)

    @pl.when(j >= nq + nk + nv)
    def _():
        for rows in chunks:
            r_ref[rows, :] = project(rows).astype(BF16)


def _inproj(x2, w_bf, tables, t_len, a_width, q_scale, k_transposed):
    n, d = x2.shape
    n_in = w_bf.shape[1]
    tm = min(n, 1024)
    tn = 512 if tm == 1024 else 1024
    assert n % tm == 0 and a_width % tn == 0 and n_in % tn == 0
    row_tabs, col_tabs = tables
    if t_len >= tm:
        assert t_len % tm == 0
        per = t_len // tm
    else:
        assert tm % t_len == 0 and not k_transposed
        row_tabs = tuple(jnp.tile(a, (tm // t_len, 1)) for a in row_tabs)
        col_tabs = tuple(jnp.tile(a, (1, tm // t_len)) for a in col_tabs)
        per = 1
    nq = nk = nv = a_width // tn
    nr = n_in // tn - 3 * nq
    kblk = lambda j: jnp.clip(j - nq, 0, nk - 1)
    row_spec = pl.BlockSpec((tm, LANES), lambda i, j: (i % per, 0))
    col_spec = pl.BlockSpec((ROT_DIM // 2, tm), lambda i, j: (0, i % per))
    if k_transposed:
        k_spec = pl.BlockSpec((None, tn, tm), lambda i, j: (i // per, kblk(j), i % per))
        k_shape = jax.ShapeDtypeStruct((n // t_len, a_width, t_len), F32)
    else:
        k_spec = pl.BlockSpec((tm, tn), lambda i, j: (i, kblk(j)))
        k_shape = jax.ShapeDtypeStruct((n, a_width), F32)
    kern = functools.partial(_inproj_kernel, tn=tn, nq=nq, nk=nk, nv=nv, q_scale=q_scale,
                             k_transposed=k_transposed)
    return pl.pallas_call(
        kern,
        grid=(n // tm, n_in // tn),
        in_specs=[
            pl.BlockSpec((tm, d), lambda i, j: (i, 0)),
            pl.BlockSpec((d, tn), lambda i, j: (0, j)),
            row_spec, row_spec, row_spec, col_spec, col_spec,
        ],
        out_specs=[
            pl.BlockSpec((tm, tn), lambda i, j: (i, jnp.minimum(j, nq - 1))),
            k_spec,
            pl.BlockSpec((tm, tn), lambda i, j: (i, jnp.clip(j - nq - nk, 0, nv - 1))),
            pl.BlockSpec((tm, tn), lambda i, j: (i, jnp.maximum(j - nq - nk - nv, 0))),
        ],
        out_shape=[
            jax.ShapeDtypeStruct((n, a_width), BF16),
            k_shape,
            jax.ShapeDtypeStruct((n, a_width), F32),
            jax.ShapeDtypeStruct((n, nr * tn), BF16),
        ],
        scratch_shapes=[pltpu.VMEM((tm, d), BF16), pltpu.VMEM((tm, tn), F32)],
        compiler_params=_cparams(("parallel", "arbitrary")),
        name="inproj",
    )(x2, w_bf, *row_tabs, *col_tabs)


def _memkv_kernel(x_ref, wk_ref, wv_ref, k_ref, v_ref):
    xb = x_ref[...].astype(BF16)
    k_ref[...] = jnp.dot(xb, wk_ref[...], preferred_element_type=F32)
    v_ref[...] = jnp.dot(xb, wv_ref[...], preferred_element_type=F32)


def _memkv(x2, wk_bf, wv_bf):
    n, d = x2.shape
    c = wk_bf.shape[1]
    tm = min(n, 512)
    assert n % tm == 0
    wspec = pl.BlockSpec((d, c), lambda i: (0, 0))
    ospec = pl.BlockSpec((tm, c), lambda i: (i, 0))
    return pl.pallas_call(
        _memkv_kernel,
        grid=(n // tm,),
        in_specs=[pl.BlockSpec((tm, d), lambda i: (i, 0)), wspec, wspec],
        out_specs=[ospec, ospec],
        out_shape=[jax.ShapeDtypeStruct((n, c), F32)] * 2,
        compiler_params=_cparams(("parallel",)),
        name="memkv",
    )(x2, wk_bf, wv_bf)


def _attn_epilogue(o, lam_init, gain, z):
    ms = jnp.mean(o * o, axis=1, keepdims=True)
    y = o * lax.rsqrt(ms + NORM_EPS) * gain * (1.0 - lam_init)
    return y * _silu(z)


def _attn_prompt_kernel(lam_ref, q_ref, kt_ref, v_ref, z_ref, g_ref, o_ref,
                        kb_ref, vb_ref, qa_ref, sa_ref, sb_ref, p_ref, acc_ref, m_ref, al_ref,
                        *, tq, lam_init):
    s_len = q_ref.shape[0]
    strip = 32
    kb_ref[...] = kt_ref[...].astype(BF16)
    vb_ref[:, :A_V_DIM] = v_ref[...].astype(BF16)
    vb_ref[:, A_V_DIM:] = jnp.ones((s_len, A_V_DIM), BF16)
    lane = lax.broadcasted_iota(jnp.int32, (tq, A_V_DIM), 1)

    def scores(j, s_ref, maps=(0, 1)):
        kt = kb_ref[:, pl.ds(pl.multiple_of(j * tq, tq), tq)]
        for mp in maps:
            s_ref[mp] = jnp.dot(qa_ref[mp], kt, preferred_element_type=F32)

    def softmax_pv(j, s_ref, masked, nxt_ref=None, first=False):
        start = pl.multiple_of(j * tq, tq)
        grp = tq // PV_SPLIT
        if nxt_ref is not None:
            scores(j + 1, nxt_ref, (0,))
        for mp in range(2):
            for g0 in range(0, tq, grp):
                if nxt_ref is not None and mp * tq + g0 == NEXT_SCORES_AT * grp:
                    scores(j + 1, nxt_ref, (1,))
                nk = g0 + grp if masked else tq
                for r in range(g0, g0 + grp, strip):
                    rs = slice(r, r + strip)
                    sv = s_ref[mp, rs, :nk]
                    if masked:
                        cc = lax.broadcasted_iota(jnp.int32, (strip, nk), 1) // CHUNK
                        sv = jnp.where(cc <= r // CHUNK, sv, -jnp.inf)
                    mx = jnp.max(sv, axis=1, keepdims=True)
                    if first:
                        mn = jnp.broadcast_to(mx, (strip, LANES))
                    else:
                        m_old = m_ref[mp, rs, :]
                        mn = jnp.maximum(m_old, mx)
                        al_ref[mp, rs, :] = jnp.exp2(m_old - mn)
                    m_ref[mp, rs, :] = mn
                    for c in range(nk // LANES):
                        cs = slice(c * LANES, (c + 1) * LANES)
                        p_ref[mp, rs, cs] = jnp.exp2(sv[:, cs] - mn).astype(BF16)
                gs = slice(g0, g0 + grp)
                pv = jnp.dot(p_ref[mp, gs, :nk], vb_ref[pl.ds(start, nk), :],
                             preferred_element_type=F32)
                for c in range(2):
                    cs = slice(c * A_V_DIM, (c + 1) * A_V_DIM)
                    if first:
                        acc_ref[mp, gs, cs] = pv[:, cs]
                    else:
                        acc_ref[mp, gs, cs] = acc_ref[mp, gs, cs] * al_ref[mp, gs, :] + pv[:, cs]

    def first_scores(qi):
        qf = q_ref[pl.ds(pl.multiple_of(qi * tq, tq), tq), :].astype(F32)
        qa_ref[0] = jnp.where(lane < A_HEAD_DIM, qf, 0.0).astype(BF16)
        qa_ref[1] = jnp.where(lane >= A_HEAD_DIM, qf, 0.0).astype(BF16)
        scores(0, sa_ref)

    def q_tile(qi, _):
        rows = pl.ds(pl.multiple_of(qi * tq, tq), tq)

        @pl.when(qi > 0)
        def _():
            softmax_pv(0, sa_ref, False, sb_ref, first=True)

        @pl.when(qi == 0)
        def _():
            softmax_pv(0, sa_ref, True, first=True)

        def full_tile(j, carry):
            @pl.when(j % 2 == 0)
            def _():
                softmax_pv(j, sa_ref, False, sb_ref)

            @pl.when(j % 2 == 1)
            def _():
                softmax_pv(j, sb_ref, False, sa_ref)

            return carry

        lax.fori_loop(1, qi, full_tile, 0)

        @pl.when((qi > 0) & (qi % 2 == 0))
        def _():
            softmax_pv(qi, sa_ref, True)

        @pl.when(qi % 2 == 1)
        def _():
            softmax_pv(qi, sb_ref, True)

        first_scores(jnp.minimum(qi + 1, nq - 1))
        o = (acc_ref[0, :, :A_V_DIM] / acc_ref[0, :, A_V_DIM:]
             - lam_ref[0] * (acc_ref[1, :, :A_V_DIM] / acc_ref[1, :, A_V_DIM:]))
        y = _attn_epilogue(o, lam_init, g_ref[...], z_ref[rows, :].astype(F32))
        o_ref[rows, :] = y.astype(BF16)
        return 0

    nq = s_len // tq
    first_scores(0)
    lax.fori_loop(0, nq, q_tile, 0)


def _attn_prompt(lam, q3, kt3, v3, rest3, gain, lam_init):
    b, s, w = q3.shape
    h = w // A_V_DIM
    tq = min(s, ATTN_TILE)
    assert s % tq == 0 and tq % (PV_SPLIT * LANES) == 0
    kern = functools.partial(_attn_prompt_kernel, tq=tq, lam_init=lam_init)
    qspec = pl.BlockSpec((None, s, A_V_DIM), lambda bi, hi: (bi, 0, hi))
    return pl.pallas_call(
        kern,
        grid=(b, h),
        in_specs=[
            pl.BlockSpec(memory_space=pltpu.SMEM),
            qspec,
            pl.BlockSpec((None, A_V_DIM, s), lambda bi, hi: (bi, hi, 0)),
            qspec, qspec,
            pl.BlockSpec((1, A_V_DIM), lambda bi, hi: (0, 0)),
        ],
        out_specs=qspec,
        out_shape=jax.ShapeDtypeStruct((b, s, w), BF16),
        scratch_shapes=[
            pltpu.VMEM((A_V_DIM, s), BF16),
            pltpu.VMEM((s, 2 * A_V_DIM), BF16),
            pltpu.VMEM((2, tq, A_V_DIM), BF16),
            pltpu.VMEM((2, tq, tq), F32),
            pltpu.VMEM((2, tq, tq), F32),
            pltpu.VMEM((2, tq, tq), BF16),
            pltpu.VMEM((2, tq, 2 * A_V_DIM), F32),
            pltpu.VMEM((2, tq, LANES), F32),
            pltpu.VMEM((2, tq, LANES), F32),
        ],
        compiler_params=_cparams(("parallel", "parallel")),
        name="attn_prompt",
    )(lam, q3, kt3, v3, rest3, gain)


def _attn_sample_kernel(lam_ref, q_ref, kc_ref, vc_ref, kn_ref, vn_ref, z_ref, g_ref, o_ref,
                        qbd_ref, s_ref, w_ref, acc_ref, *, tk, nkt, t_new, heads, lam_init):
    j = pl.program_id(1)
    past = tk * nkt
    nt = (((1,), (1,)), ((), ()))
    width = kn_ref.shape[1]

    @pl.when(j == 0)
    def _():
        lane = lax.broadcasted_iota(jnp.int32, (t_new, A_V_DIM), 1)
        zero = jnp.zeros((t_new, A_V_DIM), BF16)
        for hd in range(heads):
            qh = q_ref[:, hd * A_V_DIM:(hd + 1) * A_V_DIM].astype(F32)
            for mp in range(2):
                r0 = (2 * hd + mp) * t_new
                mine = lane >= A_HEAD_DIM if mp == 1 else lane < A_HEAD_DIM
                own = jnp.where(mine, qh, 0.0).astype(BF16)
                for hb in range(heads):
                    qbd_ref[r0:r0 + t_new, hb * A_V_DIM:(hb + 1) * A_V_DIM] = own if hb == hd else zero

    def pad_new(ref):
        new = ref[...]
        return jnp.concatenate([new, jnp.zeros((LANES - t_new, width), F32)], axis=0).astype(BF16)

    @pl.when(j < nkt)
    def _():
        s = jnp.dot(qbd_ref[...], kc_ref[...].astype(BF16), preferred_element_type=F32)
        s_ref[:, pl.ds(pl.multiple_of(j * tk, tk), tk)] = s

    @pl.when(j == nkt - 1)
    def _():
        sn = lax.dot_general(qbd_ref[...], pad_new(kn_ref), nt, preferred_element_type=F32)
        col = lax.broadcasted_iota(jnp.int32, sn.shape, 1)
        s_ref[:, past:past + LANES] = jnp.where(col < t_new, sn, -jnp.inf)
        lam = lam_ref[0]
        for hd in range(heads):
            r0 = hd * 2 * t_new
            p = []
            for mp in range(2):
                s = s_ref[r0 + mp * t_new:r0 + (mp + 1) * t_new, :]
                e = jnp.exp2(s - jnp.max(s, axis=1, keepdims=True))
                p.append(e / jnp.sum(e, axis=1, keepdims=True))
            w_ref[hd * t_new:(hd + 1) * t_new, :] = (p[0] - lam * p[1]).astype(BF16)

    @pl.when(j == nkt)
    def _():
        full = jnp.dot(w_ref[:, past:past + LANES], pad_new(vn_ref), preferred_element_type=F32)
        for hd in range(heads):
            rs = slice(hd * t_new, (hd + 1) * t_new)
            acc_ref[rs, :] = full[rs, hd * A_V_DIM:(hd + 1) * A_V_DIM]

    @pl.when(j >= nkt)
    def _():
        start = pl.multiple_of((j - nkt) * tk, tk)
        for hd in range(heads):
            rs = slice(hd * t_new, (hd + 1) * t_new)
            vh = vc_ref[pl.ds(hd, tk, stride=heads), :].astype(BF16)
            acc_ref[rs, :] += jnp.dot(w_ref[rs, pl.ds(start, tk)], vh, preferred_element_type=F32)

    @pl.when(j == 2 * nkt - 1)
    def _():
        for hd in range(heads):
            cs = slice(hd * A_V_DIM, (hd + 1) * A_V_DIM)
            o = acc_ref[hd * t_new:(hd + 1) * t_new, :]
            y = _attn_epilogue(o, lam_init, g_ref[...], z_ref[:, cs].astype(F32))
            o_ref[:, cs] = y.astype(BF16)


def _attn_sample(lam, q3, kct, vc3, kn, vn, rest3, gain, lam_init):
    b, w, past = kct.shape
    t_new = kn.shape[1]
    heads = w // A_V_DIM
    rows = 2 * heads * t_new
    tk = min(past, 2048)
    assert past % tk == 0 and t_new % 16 == 0 and t_new <= LANES
    nkt = past // tk
    kern = functools.partial(_attn_sample_kernel, tk=tk, nkt=nkt, t_new=t_new, heads=heads,
                             lam_init=lam_init)
    newspec = pl.BlockSpec((None, t_new, w), lambda bi, j: (bi, 0, 0))
    return pl.pallas_call(
        kern,
        grid=(b, 2 * nkt),
        in_specs=[
            pl.BlockSpec(memory_space=pltpu.SMEM),
            newspec,
            pl.BlockSpec((None, w, tk), lambda bi, j: (bi, 0, jnp.minimum(j, nkt - 1))),
            pl.BlockSpec((None, tk * heads, A_V_DIM),
                         lambda bi, j: (bi, jnp.maximum(j - nkt, 0), 0)),
            newspec, newspec, newspec,
            pl.BlockSpec((1, A_V_DIM), lambda bi, j: (0, 0)),
        ],
        out_specs=newspec,
        out_shape=jax.ShapeDtypeStruct((b, t_new, w), BF16),
        scratch_shapes=[
            pltpu.VMEM((rows, w), BF16),
            pltpu.VMEM((rows, past + LANES), F32),
            pltpu.VMEM((rows // 2, past + LANES), BF16),
            pltpu.VMEM((rows // 2, A_V_DIM), F32),
        ],
        compiler_params=_cparams(("parallel", "arbitrary")),
        name="attn_sample",
    )(lam, q3, kct, vc3, kn, vn, rest3, gain)


def _hgrn_kernel(*refs, blk, nchunk, hp, has_s0):
    if has_s0:
        q_ref, f_ref, i_ref, og_ref, z_ref, lb_ref, gain_ref, s0_ref, y_ref, sf_ref, st_ref = refs
    else:
        q_ref, f_ref, i_ref, og_ref, z_ref, lb_ref, gain_ref, y_ref, sf_ref, st_ref = refs
    t = pl.program_id(2)

    @pl.when(t == 0)
    def _():
        st_ref[...] = s0_ref[...] if has_s0 else jnp.zeros_like(st_ref)

    lb = lb_ref[...]
    gain = gain_ref[...]
    row = lax.broadcasted_iota(jnp.int32, (blk, blk), 0)
    col = lax.broadcasted_iota(jnp.int32, (blk, blk), 1)
    causal = col <= row
    tril = jnp.where(causal, 1.0, 0.0).astype(BF16)
    ones = jnp.ones((blk, B_KEY_DIM), BF16)
    mid = (blk - 1) // 2
    nt = (((1,), (1,)), ((), ()))
    tn = (((0,), (0,)), ((), ()))

    def split3(g):
        hi = g.astype(BF16)
        r1 = g - hi.astype(F32)
        md = r1.astype(BF16)
        lo = (r1 - md.astype(F32)).astype(BF16)
        return hi, md, lo

    chunks = [slice(c * blk, (c + 1) * blk) for c in range(nchunk)]
    f = lb + (1.0 - lb) * jax.nn.sigmoid(f_ref[...].astype(F32))
    parts = split3(jnp.log(f))
    qv = _silu(q_ref[...].astype(F32))
    kv = 1.0 - f
    vv = i_ref[...]
    b = [sum(jnp.dot(tril, p[sl, :], preferred_element_type=F32) for p in parts) for sl in chunks]
    tot = [sum(lax.dot_general(p[sl, :], ones, tn, preferred_element_type=F32) for p in parts)
           for sl in chunks]
    qe, ke, qs, ks = [], [], [], []
    for sl, bc in zip(chunks, b):
        b_mid = bc[mid:mid + 1, :]
        b_last = bc[blk - 1:blk, :]
        qe.append((qv[sl, :] * jnp.exp(bc - b_mid)).astype(BF16))
        ke.append((kv[sl, :] * jnp.exp(b_mid - bc)).astype(BF16))
        qs.append((qv[sl, :] * jnp.exp(bc)).astype(BF16))
        ks.append((kv[sl, :] * jnp.exp(b_last - bc)).astype(BF16))
    lanes = [slice(h * B_KEY_DIM, (h + 1) * B_KEY_DIM) for h in range(hp)]
    o_intra, kvs = [], []
    for c, sl in enumerate(chunks):
        sc = [lax.dot_general(qe[c][:, hl], ke[c][:, hl], nt, preferred_element_type=F32)
              for hl in lanes]
        sc = [jnp.where(causal, s, 0.0).astype(BF16) for s in sc]
        o_intra.append([jnp.dot(s, vv[sl, hl], preferred_element_type=F32)
                        for s, hl in zip(sc, lanes)])
        kvs.append([lax.dot_general(ks[c][:, hl], vv[sl, hl], tn, preferred_element_type=F32)
                    for hl in lanes])
    decay = [jnp.exp(tc) for tc in tot]
    cols = []
    for h, hl in enumerate(lanes):
        state = st_ref[h]
        outs = []
        for c in range(nchunk):
            outs.append(o_intra[c][h] + jnp.dot(qs[c][:, hl], state.astype(BF16),
                                                preferred_element_type=F32))
            state = state * decay[c][hl, :] + kvs[c][h]
        st_ref[h] = state
        o = outs[0] if nchunk == 1 else jnp.concatenate(outs, axis=0)
        ms = jnp.mean(o * o, axis=1, keepdims=True)
        cols.append(o * lax.rsqrt(ms + NORM_EPS))
    o = cols[0] if hp == 1 else jnp.concatenate(cols, axis=1)
    y = o * gain * jax.nn.sigmoid(og_ref[...].astype(F32))
    y_ref[...] = (y * _silu(z_ref[...].astype(F32))).astype(BF16)

    @pl.when(t == pl.num_programs(2) - 1)
    def _():
        sf_ref[...] = st_ref[...]


def _hgrn(rest3, lb, gain, s0, blk, col0):
    b, t, _ = rest3.shape
    heads = lb.shape[1] // B_KEY_DIM
    tt = min(t, 512)
    assert t % tt == 0 and tt % blk == 0
    has_s0 = s0 is not None
    hp = heads if tt // blk == 1 else HGRN_HEADS_PER_STEP
    assert heads % hp == 0 and col0 % hp == 0
    width = hp * B_KEY_DIM
    kern = functools.partial(_hgrn_kernel, blk=blk, nchunk=tt // blk, hp=hp, has_s0=has_s0)

    def colspec(k):
        return pl.BlockSpec((None, tt, width),
                            lambda bi, hi, ti: (bi, ti, (col0 + k * heads) // hp + hi))

    vecspec = pl.BlockSpec((1, width), lambda bi, hi, ti: (0, hi))
    stspec = pl.BlockSpec((None, hp, B_KEY_DIM, B_KEY_DIM), lambda bi, hi, ti: (bi, hi, 0, 0))
    in_specs = [colspec(k) for k in range(5)] + [vecspec, vecspec]
    args = [rest3] * 5 + [lb, gain]
    if has_s0:
        in_specs.append(stspec)
        args.append(s0)
    return pl.pallas_call(
        kern,
        grid=(b, heads // hp, t // tt),
        in_specs=in_specs,
        out_specs=[
            pl.BlockSpec((None, tt, width), lambda bi, hi, ti: (bi, ti, hi)),
            stspec,
        ],
        out_shape=[
            jax.ShapeDtypeStruct((b, t, heads * B_KEY_DIM), BF16),
            jax.ShapeDtypeStruct((b, heads, B_KEY_DIM, B_KEY_DIM), F32),
        ],
        scratch_shapes=[pltpu.VMEM((hp, B_KEY_DIM, B_KEY_DIM), F32)],
        compiler_params=_cparams(("parallel", "parallel", "arbitrary")),
        name="hgrn",
    )(*args)


def _memattn_kernel(q_ref, z_ref, mk_ref, mv_ref, o_ref, kb_ref, vb_ref, *, heads):
    @pl.when(pl.program_id(1) == 0)
    def _():
        kb_ref[...] = mk_ref[...].astype(BF16)
        vb_ref[...] = mv_ref[...].astype(BF16)

    nt = (((1,), (1,)), ((), ()))
    cols = [slice(hd * C_HEAD_DIM, (hd + 1) * C_HEAD_DIM) for hd in range(heads)]
    s = [lax.dot_general(q_ref[:, cs], kb_ref[:, cs], nt, preferred_element_type=F32)
         * (C_HEAD_DIM ** -0.5) for cs in cols]
    e = [jnp.exp(sh - jnp.max(sh, axis=1, keepdims=True)) for sh in s]
    o = [jnp.dot(eh.astype(BF16), vb_ref[:, cs], preferred_element_type=F32)
         / jnp.sum(eh, axis=1, keepdims=True) for eh, cs in zip(e, cols)]
    for oh, cs in zip(o, cols):
        o_ref[:, cs] = (oh * _silu(z_ref[:, cs].astype(F32))).astype(BF16)


def _memattn(rest3, mk3, mv3, qblock, zblock):
    b, t, _ = rest3.shape
    _, m, c = mk3.shape
    tq = min(t, 512)
    assert t % tq == 0
    kern = functools.partial(_memattn_kernel, heads=c // C_HEAD_DIM)
    memspec = pl.BlockSpec((None, m, c), lambda bi, ti: (bi, 0, 0))
    return pl.pallas_call(
        kern,
        grid=(b, t // tq),
        in_specs=[
            pl.BlockSpec((None, tq, c), lambda bi, ti: (bi, ti, qblock)),
            pl.BlockSpec((None, tq, c), lambda bi, ti: (bi, ti, zblock)),
            memspec, memspec,
        ],
        out_specs=pl.BlockSpec((None, tq, c), lambda bi, ti: (bi, ti, 0)),
        out_shape=jax.ShapeDtypeStruct((b, t, c), BF16),
        scratch_shapes=[pltpu.VMEM((m, c), BF16), pltpu.VMEM((m, c), BF16)],
        compiler_params=_cparams(("parallel", "arbitrary")),
        name="memattn",
    )(rest3, rest3, mk3, mv3)


def _merge_kernel(ya_ref, yb_ref, yc_ref, ga_ref, gb_ref, gc_ref, x_ref,
                  wa_ref, wb_ref, wc_ref, wo_ref, lng_ref, lnb_ref, o_ref, *, alpha):
    def branch(y_ref, w_ref, g_ref):
        return jax.nn.sigmoid(g_ref[...].astype(F32)) * jnp.dot(
            y_ref[...], w_ref[...], preferred_element_type=F32)

    merged = branch(ya_ref, wa_ref, ga_ref) + branch(yb_ref, wb_ref, gb_ref)
    merged = merged + branch(yc_ref, wc_ref, gc_ref)
    sub = jnp.dot(merged.astype(BF16), wo_ref[...], preferred_element_type=F32)
    hres = alpha * x_ref[...] + sub
    mu = jnp.mean(hres, axis=1, keepdims=True)
    cen = hres - mu
    var = jnp.mean(cen * cen, axis=1, keepdims=True)
    o_ref[...] = cen * lax.rsqrt(var + NORM_EPS) * lng_ref[...] + lnb_ref[...]


def _merge(ya, yb, yc, rest2, x2, wa, wb, wc, wo, lng, lnb, gate_block0, alpha):
    n, d = x2.shape
    w = ya.shape[1]
    tm = min(n, 256)
    assert n % tm == 0
    rows = lambda width, blk: pl.BlockSpec((tm, width), lambda i: (i, blk))
    const = lambda shape: pl.BlockSpec(shape, lambda i: (0, 0), pipeline_mode=pl.Buffered(1))
    return pl.pallas_call(
        functools.partial(_merge_kernel, alpha=alpha),
        grid=(n // tm,),
        in_specs=[
            rows(w, 0), rows(w, 0), rows(w, 0),
            rows(d, gate_block0), rows(d, gate_block0 + 1), rows(d, gate_block0 + 2),
            rows(d, 0),
            const((w, d)), const((w, d)), const((w, d)), const((d, d)),
            const((1, d)), const((1, d)),
        ],
        out_specs=rows(d, 0),
        out_shape=jax.ShapeDtypeStruct((n, d), F32),
        compiler_params=_cparams(("parallel",)),
        name="merge",
    )(ya, yb, yc, rest2, rest2, rest2, x2, wa, wb, wc, wo, lng, lnb)


def _layer(x, pos, past_k, past_v, s0, mk3, mv3, rec_block, layer_idx, lb, p):
    (w_in, lam, sub_norm, hgrn_gain, wa, wb, wc, wo, lng, lnb, alpha) = p
    n_b, t, d = x.shape
    a_width = wa.shape[0]
    heads = a_width // A_V_DIM
    x2 = x.reshape(n_b * t, d)
    prompt = past_k is None
    q_scale = A_HEAD_DIM ** -0.5 * math.log2(math.e)
    q2, k_out, v2, rest2 = _inproj(x2, w_in, _rope_tables(pos), t, a_width, q_scale,
                                   k_transposed=prompt)
    r = rest2.shape[1]
    rest3 = rest2.reshape(n_b, t, r)
    q3 = q2.reshape(n_b, t, a_width)
    v3 = v2.reshape(n_b, t, a_width)
    lam_init = 0.8 - 0.6 * math.exp(-0.3 * layer_idx)
    if prompt:
        ya = _attn_prompt(lam, q3, k_out, v3, rest3, sub_norm, lam_init)
        k5 = k_out.reshape(n_b, heads, 2, A_HEAD_DIM, t).transpose(0, 4, 1, 2, 3)
    else:
        k3 = k_out.reshape(n_b, t, a_width)
        ya = _attn_sample(lam, q3, past_k, past_v, k3, v3, rest3,
                          sub_norm, lam_init)
        k5 = k3.reshape(n_b, t, heads, 2, A_HEAD_DIM)
    yb, s_fin = _hgrn(rest3, lb, hgrn_gain, s0, rec_block, a_width // B_KEY_DIM)
    c_width = mk3.shape[2]
    c_off = a_width + 5 * lb.shape[1]
    assert c_off % c_width == 0
    yc = _memattn(rest3, mk3, mv3, c_off // c_width, c_off // c_width + 1)
    g_off = c_off + 2 * c_width
    assert g_off % d == 0
    y2 = _merge(ya.reshape(n_b * t, a_width), yb.reshape(n_b * t, -1), yc.reshape(n_b * t, c_width),
                rest2, x2, wa, wb, wc, wo, lng, lnb, g_off // d, alpha)
    return y2.reshape(n_b, t, d), k5, v3, s_fin


def kernel(x_prompt, x_sample, cache_attn_k, cache_attn_v, state_hgrn, cache_mem_k, cache_mem_v, mem_prompt, w_in, lambda_q1, lambda_k1, lambda_q2, lambda_k2, attn_sub_norm, hgrn_lb_logits, hgrn_norm, w_mem_k, w_mem_v, w_branch_a, w_branch_b, w_branch_c, w_out, ln_gamma, ln_beta):
    bp, seq, d = x_prompt.shape
    bs, t_new, _ = x_sample.shape
    depth = w_in.shape[0]
    past = cache_attn_k.shape[2]
    heads = cache_attn_k.shape[3]
    a_width = heads * A_V_DIM
    n_mem = mem_prompt.shape[1]
    c_heads = cache_mem_k.shape[3]
    c_width = c_heads * C_HEAD_DIM
    b_heads = state_hgrn.shape[2]
    alpha = (2 * depth) ** 0.25
    pos_prompt = jnp.arange(seq)
    pos_sample = past + jnp.arange(t_new)
    lower_bounds = jnp.cumsum(jax.nn.softmax(hgrn_lb_logits.astype(F32), axis=0), axis=0)

    h_p, h_s = x_prompt, x_sample
    outs = [[] for _ in range(8)]
    for l in range(depth):
        lam_init = 0.8 - 0.6 * math.exp(-0.3 * l)
        lam = (jnp.exp(jnp.sum(lambda_q1[l].astype(F32) * lambda_k1[l].astype(F32)))
               - jnp.exp(jnp.sum(lambda_q2[l].astype(F32) * lambda_k2[l].astype(F32))) + lam_init)
        params = (w_in[l].astype(BF16), lam.reshape(1), attn_sub_norm[l].reshape(1, -1),
                  hgrn_norm[l].reshape(1, -1), w_branch_a[l].astype(BF16), w_branch_b[l].astype(BF16),
                  w_branch_c[l].astype(BF16), w_out[l].astype(BF16), ln_gamma[l].reshape(1, -1),
                  ln_beta[l].reshape(1, -1), alpha)
        lb = lower_bounds[l].reshape(1, -1)
        mk_p, mv_p = _memkv(mem_prompt.reshape(bp * n_mem, d), w_mem_k[l].astype(BF16),
                            w_mem_v[l].astype(BF16))
        mk_p = mk_p.reshape(bp, n_mem, c_width)
        mv_p = mv_p.reshape(bp, n_mem, c_width)
        h_p, k_p, v_p, s_p = _layer(h_p, pos_prompt, None, None, None, mk_p, mv_p, CHUNK, l, lb, params)
        h_s, k_s, v_s, s_s = _layer(
            h_s, pos_sample,
            cache_attn_k[l].transpose(0, 2, 3, 4, 1).reshape(bs, a_width, past),
            cache_attn_v[l].reshape(bs, past * heads, A_V_DIM),
            state_hgrn[l], cache_mem_k[l].reshape(bs, n_mem, c_width),
            cache_mem_v[l].reshape(bs, n_mem, c_width), t_new, l, lb, params)
        new = (k_p, v_p.reshape(bp, seq, heads, A_V_DIM),
               s_p.astype(x_prompt.dtype), mk_p.reshape(bp, n_mem, c_heads, C_HEAD_DIM),
               mv_p.reshape(bp, n_mem, c_heads, C_HEAD_DIM),
               k_s, v_s.reshape(bs, t_new, heads, A_V_DIM),
               s_s.astype(x_sample.dtype))
        for acc, val in zip(outs, new):
            acc.append(val)
    return (h_p, h_s) + tuple(jnp.stack(o) for o in outs)
```

```python
import functools
import math

import jax
import jax.numpy as jnp
from jax import lax
from jax.experimental import pallas as pl
from jax.experimental.pallas import tpu as pltpu

F32 = jnp.float32
BF16 = jnp.bfloat16

CHUNK = 64
A_HEAD_DIM = 64
A_V_DIM = 128
ROT_DIM = 16
ROPE_THETA = 500000.0
B_KEY_DIM = 128
C_HEAD_DIM = 256
NORM_EPS = 1e-5
LANES = 128
VMEM_LIMIT = 56 * 1024 * 1024
INPROJ_ROWS = 1024
INPROJ_COLS = 512
MEMKV_ROWS = 512
HGRN_ROWS = 512
MEMATTN_ROWS = 1024
MERGE_ROWS = 256
SAMPLE_KEYS = 2048
ATTN_TILE = 512
ATTN_STRIP = 32
PV_SPLIT = 4
ROW_CHUNKS = 4
NEXT_SCORES_AT = 3
HGRN_HEADS_PER_STEP = 8


def _cparams(sem):
    return pltpu.CompilerParams(dimension_semantics=sem, vmem_limit_bytes=VMEM_LIMIT)


def _silu(z):
    return z * jax.nn.sigmoid(z)


def _rope_tables(pos):
    half = ROT_DIM // 2
    inv_freq = jnp.power(ROPE_THETA, -jnp.arange(0, ROT_DIM, 2, dtype=F32) / ROT_DIM)
    ang = pos.astype(F32)[:, None] * inv_freq[None, :]
    cos, sin = jnp.cos(ang), jnp.sin(ang)
    t = pos.shape[0]
    pad = jnp.zeros((t, A_HEAD_DIM - ROT_DIM), F32)
    zero = jnp.zeros((t, half), F32)
    c64 = jnp.concatenate([cos, cos, pad + 1.0], axis=1)
    sa64 = jnp.concatenate([-sin, zero, pad], axis=1)
    sb64 = jnp.concatenate([zero, sin, pad], axis=1)
    rep = LANES // A_HEAD_DIM
    rows = tuple(jnp.tile(a, (1, rep)) for a in (c64, sa64, sb64))
    return rows, (cos.T, sin.T)


def _inproj_kernel(x_ref, w_ref, c_ref, sa_ref, sb_ref, ct_ref, st_ref,
                   q_ref, k_ref, v_ref, r_ref, xb_ref, t_ref, *, tn, nq, nk, nv, q_scale, k_transposed):
    j = pl.program_id(1)
    half = ROT_DIM // 2

    @pl.when(j == 0)
    def _():
        xb_ref[...] = x_ref[...].astype(BF16)

    tm = xb_ref.shape[0]
    chunks = [slice(r, r + tm // ROW_CHUNKS) for r in range(0, tm, tm // ROW_CHUNKS)]

    def project(rows=slice(None)):
        return jnp.dot(xb_ref[rows, :], w_ref[...], preferred_element_type=F32)

    def rope_group(acc, g, rows=slice(None)):
        xg = acc[:, g * LANES:(g + 1) * LANES]
        up = pltpu.roll(xg, LANES - half, 1)
        dn = pltpu.roll(xg, half, 1)
        return xg * c_ref[rows, :] + up * sa_ref[rows, :] + dn * sb_ref[rows, :]

    @pl.when(j < nq)
    def _():
        for rows in chunks:
            acc = project(rows)
            for g in range(tn // LANES):
                q_ref[rows, g * LANES:(g + 1) * LANES] = (
                    rope_group(acc, g, rows) * q_scale).astype(BF16)

    @pl.when((j >= nq) & (j < nq + nk))
    def _():
        if k_transposed:
            for rows in chunks:
                t_ref[rows, :] = project(rows)
            kt = t_ref[...].T
            cos, sin = ct_ref[...], st_ref[...]
            for g in range(tn // A_HEAD_DIM):
                r0 = g * A_HEAD_DIM
                lo = kt[r0:r0 + half, :]
                hi = kt[r0 + half:r0 + ROT_DIM, :]
                k_ref[r0:r0 + half, :] = lo * cos - hi * sin
                k_ref[r0 + half:r0 + ROT_DIM, :] = hi * cos + lo * sin
                k_ref[r0 + ROT_DIM:r0 + A_HEAD_DIM, :] = kt[r0 + ROT_DIM:r0 + A_HEAD_DIM, :]
        else:
            acc = project()
            for g in range(tn // LANES):
                k_ref[:, g * LANES:(g + 1) * LANES] = rope_group(acc, g)

    @pl.when((j >= nq + nk) & (j < nq + nk + nv))
    def _():
        for rows in chunks:
            v_ref[rows, :] = project(rows)

    @pl.when(j >= nq + nk + nv)
    def _():
        for rows in chunks:
            r_ref[rows, :] = project(rows).astype(BF16)


def _inproj(x2, w_bf, tables, t_len, a_width, q_scale, k_transposed):
    n, d = x2.shape
    n_in = w_bf.shape[1]
    tm = min(n, INPROJ_ROWS)
    tn = INPROJ_COLS if tm == INPROJ_ROWS else 2 * INPROJ_COLS
    assert n % tm == 0 and a_width % tn == 0 and n_in % tn == 0
    row_tabs, col_tabs = tables
    if t_len >= tm:
        assert t_len % tm == 0
        per = t_len // tm
    else:
        assert tm % t_len == 0 and not k_transposed
        row_tabs = tuple(jnp.tile(a, (tm // t_len, 1)) for a in row_tabs)
        col_tabs = tuple(jnp.tile(a, (1, tm // t_len)) for a in col_tabs)
        per = 1
    nq = nk = nv = a_width // tn
    nr = n_in // tn - 3 * nq
    kblk = lambda j: jnp.clip(j - nq, 0, nk - 1)
    row_spec = pl.BlockSpec((tm, LANES), lambda i, j: (i % per, 0))
    col_spec = pl.BlockSpec((ROT_DIM // 2, tm), lambda i, j: (0, i % per))
    if k_transposed:
        k_spec = pl.BlockSpec((None, tn, tm), lambda i, j: (i // per, kblk(j), i % per))
        k_shape = jax.ShapeDtypeStruct((n // t_len, a_width, t_len), F32)
    else:
        k_spec = pl.BlockSpec((tm, tn), lambda i, j: (i, kblk(j)))
        k_shape = jax.ShapeDtypeStruct((n, a_width), F32)
    kern = functools.partial(_inproj_kernel, tn=tn, nq=nq, nk=nk, nv=nv, q_scale=q_scale,
                             k_transposed=k_transposed)
    return pl.pallas_call(
        kern,
        grid=(n // tm, n_in // tn),
        in_specs=[
            pl.BlockSpec((tm, d), lambda i, j: (i, 0)),
            pl.BlockSpec((d, tn), lambda i, j: (0, j)),
            row_spec, row_spec, row_spec, col_spec, col_spec,
        ],
        out_specs=[
            pl.BlockSpec((tm, tn), lambda i, j: (i, jnp.minimum(j, nq - 1))),
            k_spec,
            pl.BlockSpec((tm, tn), lambda i, j: (i, jnp.clip(j - nq - nk, 0, nv - 1))),
            pl.BlockSpec((tm, tn), lambda i, j: (i, jnp.maximum(j - nq - nk - nv, 0))),
        ],
        out_shape=[
            jax.ShapeDtypeStruct((n, a_width), BF16),
            k_shape,
            jax.ShapeDtypeStruct((n, a_width), F32),
            jax.ShapeDtypeStruct((n, nr * tn), BF16),
        ],
        scratch_shapes=[pltpu.VMEM((tm, d), BF16), pltpu.VMEM((tm, tn), F32)],
        compiler_params=_cparams(("parallel", "arbitrary")),
        name="inproj",
    )(x2, w_bf, *row_tabs, *col_tabs)


def _memkv_kernel(x_ref, wk_ref, wv_ref, k_ref, v_ref):
    xb = x_ref[...].astype(BF16)
    k_ref[...] = jnp.dot(xb, wk_ref[...], preferred_element_type=F32)
    v_ref[...] = jnp.dot(xb, wv_ref[...], preferred_element_type=F32)


def _memkv(x2, wk_bf, wv_bf):
    n, d = x2.shape
    c = wk_bf.shape[1]
    tm = min(n, MEMKV_ROWS)
    assert n % tm == 0
    wspec = pl.BlockSpec((d, c), lambda i: (0, 0))
    ospec = pl.BlockSpec((tm, c), lambda i: (i, 0))
    return pl.pallas_call(
        _memkv_kernel,
        grid=(n // tm,),
        in_specs=[pl.BlockSpec((tm, d), lambda i: (i, 0)), wspec, wspec],
        out_specs=[ospec, ospec],
        out_shape=[jax.ShapeDtypeStruct((n, c), F32)] * 2,
        compiler_params=_cparams(("parallel",)),
        name="memkv",
    )(x2, wk_bf, wv_bf)


def _attn_epilogue(o, lam_init, gain, z):
    ms = jnp.mean(o * o, axis=1, keepdims=True)
    y = o * lax.rsqrt(ms + NORM_EPS) * gain * (1.0 - lam_init)
    return y * _silu(z)


def _attn_prompt_kernel(lam_ref, q_ref, kt_ref, v_ref, z_ref, g_ref, o_ref,
                        kb_ref, vb_ref, qa_ref, sa_ref, sb_ref, p_ref, acc_ref, m_ref, al_ref,
                        *, tq, lam_init):
    s_len = q_ref.shape[0]
    strip = ATTN_STRIP
    kb_ref[...] = kt_ref[...].astype(BF16)
    vb_ref[:, :A_V_DIM] = v_ref[...].astype(BF16)
    vb_ref[:, A_V_DIM:] = jnp.ones((s_len, A_V_DIM), BF16)
    lane = lax.broadcasted_iota(jnp.int32, (tq, A_V_DIM), 1)

    def scores(j, s_ref, maps=(0, 1)):
        kt = kb_ref[:, pl.ds(pl.multiple_of(j * tq, tq), tq)]
        for mp in maps:
            s_ref[mp] = jnp.dot(qa_ref[mp], kt, preferred_element_type=F32)

    def softmax_pv(j, s_ref, masked, nxt_ref=None, first=False):
        start = pl.multiple_of(j * tq, tq)
        grp = tq // PV_SPLIT
        if nxt_ref is not None:
            scores(j + 1, nxt_ref, (0,))
        for mp in range(2):
            for g0 in range(0, tq, grp):
                if nxt_ref is not None and mp * tq + g0 == NEXT_SCORES_AT * grp:
                    scores(j + 1, nxt_ref, (1,))
                nk = g0 + grp if masked else tq
                for r in range(g0, g0 + grp, strip):
                    rs = slice(r, r + strip)
                    sv = s_ref[mp, rs, :nk]
                    if masked:
                        cc = lax.broadcasted_iota(jnp.int32, (strip, nk), 1) // CHUNK
                        sv = jnp.where(cc <= r // CHUNK, sv, -jnp.inf)
                    mx = jnp.max(sv, axis=1, keepdims=True)
                    if first:
                        mn = jnp.broadcast_to(mx, (strip, LANES))
                    else:
                        m_old = m_ref[mp, rs, :]
                        mn = jnp.maximum(m_old, mx)
                        al_ref[mp, rs, :] = jnp.exp2(m_old - mn)
                    m_ref[mp, rs, :] = mn
                    for c in range(nk // LANES):
                        cs = slice(c * LANES, (c + 1) * LANES)
                        p_ref[mp, rs, cs] = jnp.exp2(sv[:, cs] - mn).astype(BF16)
                gs = slice(g0, g0 + grp)
                pv = jnp.dot(p_ref[mp, gs, :nk], vb_ref[pl.ds(start, nk), :],
                             preferred_element_type=F32)
                for c in range(2):
                    cs = slice(c * A_V_DIM, (c + 1) * A_V_DIM)
                    if first:
                        acc_ref[mp, gs, cs] = pv[:, cs]
                    else:
                        acc_ref[mp, gs, cs] = acc_ref[mp, gs, cs] * al_ref[mp, gs, :] + pv[:, cs]

    def first_scores(qi):
        qf = q_ref[pl.ds(pl.multiple_of(qi * tq, tq), tq), :].astype(F32)
        qa_ref[0] = jnp.where(lane < A_HEAD_DIM, qf, 0.0).astype(BF16)
        qa_ref[1] = jnp.where(lane >= A_HEAD_DIM, qf, 0.0).astype(BF16)
        scores(0, sa_ref)

    def q_tile(qi, _):
        rows = pl.ds(pl.multiple_of(qi * tq, tq), tq)

        @pl.when(qi > 0)
        def _():
            softmax_pv(0, sa_ref, False, sb_ref, first=True)

        @pl.when(qi == 0)
        def _():
            softmax_pv(0, sa_ref, True, first=True)

        def full_tile(j, carry):
            @pl.when(j % 2 == 0)
            def _():
                softmax_pv(j, sa_ref, False, sb_ref)

            @pl.when(j % 2 == 1)
            def _():
                softmax_pv(j, sb_ref, False, sa_ref)

            return carry

        lax.fori_loop(1, qi, full_tile, 0)

        @pl.when((qi > 0) & (qi % 2 == 0))
        def _():
            softmax_pv(qi, sa_ref, True)

        @pl.when(qi % 2 == 1)
        def _():
            softmax_pv(qi, sb_ref, True)

        first_scores(jnp.minimum(qi + 1, nq - 1))
        o = (acc_ref[0, :, :A_V_DIM] / acc_ref[0, :, A_V_DIM:]
             - lam_ref[0] * (acc_ref[1, :, :A_V_DIM] / acc_ref[1, :, A_V_DIM:]))
        y = _attn_epilogue(o, lam_init, g_ref[...], z_ref[rows, :].astype(F32))
        o_ref[rows, :] = y.astype(BF16)
        return 0

    nq = s_len // tq
    first_scores(0)
    lax.fori_loop(0, nq, q_tile, 0)


def _attn_prompt(lam, q3, kt3, v3, rest3, gain, lam_init):
    b, s, w = q3.shape
    h = w // A_V_DIM
    tq = min(s, ATTN_TILE)
    assert s % tq == 0 and tq % (PV_SPLIT * LANES) == 0
    kern = functools.partial(_attn_prompt_kernel, tq=tq, lam_init=lam_init)
    qspec = pl.BlockSpec((None, s, A_V_DIM), lambda bi, hi: (bi, 0, hi))
    return pl.pallas_call(
        kern,
        grid=(b, h),
        in_specs=[
            pl.BlockSpec(memory_space=pltpu.SMEM),
            qspec,
            pl.BlockSpec((None, A_V_DIM, s), lambda bi, hi: (bi, hi, 0)),
            qspec, qspec,
            pl.BlockSpec((1, A_V_DIM), lambda bi, hi: (0, 0)),
        ],
        out_specs=qspec,
        out_shape=jax.ShapeDtypeStruct((b, s, w), BF16),
        scratch_shapes=[
            pltpu.VMEM((A_V_DIM, s), BF16),
            pltpu.VMEM((s, 2 * A_V_DIM), BF16),
            pltpu.VMEM((2, tq, A_V_DIM), BF16),
            pltpu.VMEM((2, tq, tq), F32),
            pltpu.VMEM((2, tq, tq), F32),
            pltpu.VMEM((2, tq, tq), BF16),
            pltpu.VMEM((2, tq, 2 * A_V_DIM), F32),
            pltpu.VMEM((2, tq, LANES), F32),
            pltpu.VMEM((2, tq, LANES), F32),
        ],
        compiler_params=_cparams(("parallel", "parallel")),
        name="attn_prompt",
    )(lam, q3, kt3, v3, rest3, gain)


def _attn_sample_kernel(lam_ref, q_ref, kc_ref, vc_ref, kn_ref, vn_ref, z_ref, g_ref, o_ref,
                        qbd_ref, s_ref, w_ref, acc_ref, *, tk, nkt, t_new, heads, lam_init):
    j = pl.program_id(1)
    past = tk * nkt
    nt = (((1,), (1,)), ((), ()))
    width = kn_ref.shape[1]

    @pl.when(j == 0)
    def _():
        lane = lax.broadcasted_iota(jnp.int32, (t_new, A_V_DIM), 1)
        zero = jnp.zeros((t_new, A_V_DIM), BF16)
        for hd in range(heads):
            qh = q_ref[:, hd * A_V_DIM:(hd + 1) * A_V_DIM].astype(F32)
            for mp in range(2):
                r0 = (2 * hd + mp) * t_new
                mine = lane >= A_HEAD_DIM if mp == 1 else lane < A_HEAD_DIM
                own = jnp.where(mine, qh, 0.0).astype(BF16)
                for hb in range(heads):
                    qbd_ref[r0:r0 + t_new, hb * A_V_DIM:(hb + 1) * A_V_DIM] = own if hb == hd else zero

    def pad_new(ref):
        new = ref[...]
        return jnp.concatenate([new, jnp.zeros((LANES - t_new, width), F32)], axis=0).astype(BF16)

    @pl.when(j < nkt)
    def _():
        s = jnp.dot(qbd_ref[...], kc_ref[...].astype(BF16), preferred_element_type=F32)
        s_ref[:, pl.ds(pl.multiple_of(j * tk, tk), tk)] = s

    @pl.when(j == nkt - 1)
    def _():
        sn = lax.dot_general(qbd_ref[...], pad_new(kn_ref), nt, preferred_element_type=F32)
        col = lax.broadcasted_iota(jnp.int32, sn.shape, 1)
        s_ref[:, past:past + LANES] = jnp.where(col < t_new, sn, -jnp.inf)
        lam = lam_ref[0]
        for hd in range(heads):
            r0 = hd * 2 * t_new
            p = []
            for mp in range(2):
                s = s_ref[r0 + mp * t_new:r0 + (mp + 1) * t_new, :]
                e = jnp.exp2(s - jnp.max(s, axis=1, keepdims=True))
                p.append(e / jnp.sum(e, axis=1, keepdims=True))
            w_ref[hd * t_new:(hd + 1) * t_new, :] = (p[0] - lam * p[1]).astype(BF16)

    @pl.when(j == nkt)
    def _():
        full = jnp.dot(w_ref[:, past:past + LANES], pad_new(vn_ref), preferred_element_type=F32)
        for hd in range(heads):
            rs = slice(hd * t_new, (hd + 1) * t_new)
            acc_ref[rs, :] = full[rs, hd * A_V_DIM:(hd + 1) * A_V_DIM]

    @pl.when(j >= nkt)
    def _():
        start = pl.multiple_of((j - nkt) * tk, tk)
        for hd in range(heads):
            rs = slice(hd * t_new, (hd + 1) * t_new)
            vh = vc_ref[pl.ds(hd, tk, stride=heads), :].astype(BF16)
            acc_ref[rs, :] += jnp.dot(w_ref[rs, pl.ds(start, tk)], vh, preferred_element_type=F32)

    @pl.when(j == 2 * nkt - 1)
    def _():
        for hd in range(heads):
            cs = slice(hd * A_V_DIM, (hd + 1) * A_V_DIM)
            o = acc_ref[hd * t_new:(hd + 1) * t_new, :]
            y = _attn_epilogue(o, lam_init, g_ref[...], z_ref[:, cs].astype(F32))
            o_ref[:, cs] = y.astype(BF16)


def _attn_sample(lam, q3, kct, vc3, kn, vn, rest3, gain, lam_init):
    b, w, past = kct.shape
    t_new = kn.shape[1]
    heads = w // A_V_DIM
    rows = 2 * heads * t_new
    tk = min(past, SAMPLE_KEYS)
    assert past % tk == 0 and t_new % 16 == 0 and t_new <= LANES
    nkt = past // tk
    kern = functools.partial(_attn_sample_kernel, tk=tk, nkt=nkt, t_new=t_new, heads=heads,
                             lam_init=lam_init)
    newspec = pl.BlockSpec((None, t_new, w), lambda bi, j: (bi, 0, 0))
    return pl.pallas_call(
        kern,
        grid=(b, 2 * nkt),
        in_specs=[
            pl.BlockSpec(memory_space=pltpu.SMEM),
            newspec,
            pl.BlockSpec((None, w, tk), lambda bi, j: (bi, 0, jnp.minimum(j, nkt - 1))),
            pl.BlockSpec((None, tk * heads, A_V_DIM),
                         lambda bi, j: (bi, jnp.maximum(j - nkt, 0), 0)),
            newspec, newspec, newspec,
            pl.BlockSpec((1, A_V_DIM), lambda bi, j: (0, 0)),
        ],
        out_specs=newspec,
        out_shape=jax.ShapeDtypeStruct((b, t_new, w), BF16),
        scratch_shapes=[
            pltpu.VMEM((rows, w), BF16),
            pltpu.VMEM((rows, past + LANES), F32),
            pltpu.VMEM((rows // 2, past + LANES), BF16),
            pltpu.VMEM((rows // 2, A_V_DIM), F32),
        ],
        compiler_params=_cparams(("parallel", "arbitrary")),
        name="attn_sample",
    )(lam, q3, kct, vc3, kn, vn, rest3, gain)


def _hgrn_kernel(*refs, blk, nchunk, hp, has_s0):
    if has_s0:
        q_ref, f_ref, i_ref, og_ref, z_ref, lb_ref, gain_ref, s0_ref, y_ref, sf_ref, st_ref = refs
    else:
        q_ref, f_ref, i_ref, og_ref, z_ref, lb_ref, gain_ref, y_ref, sf_ref, st_ref = refs
    t = pl.program_id(2)

    @pl.when(t == 0)
    def _():
        st_ref[...] = s0_ref[...] if has_s0 else jnp.zeros_like(st_ref)

    lb = lb_ref[...]
    gain = gain_ref[...]
    row = lax.broadcasted_iota(jnp.int32, (blk, blk), 0)
    col = lax.broadcasted_iota(jnp.int32, (blk, blk), 1)
    causal = col <= row
    tril = jnp.where(causal, 1.0, 0.0).astype(BF16)
    ones = jnp.ones((blk, B_KEY_DIM), BF16)
    mid = (blk - 1) // 2
    nt = (((1,), (1,)), ((), ()))
    tn = (((0,), (0,)), ((), ()))

    def split3(g):
        hi = g.astype(BF16)
        r1 = g - hi.astype(F32)
        md = r1.astype(BF16)
        lo = (r1 - md.astype(F32)).astype(BF16)
        return hi, md, lo

    chunks = [slice(c * blk, (c + 1) * blk) for c in range(nchunk)]
    f = lb + (1.0 - lb) * jax.nn.sigmoid(f_ref[...].astype(F32))
    parts = split3(jnp.log(f))
    qv = _silu(q_ref[...].astype(F32))
    kv = 1.0 - f
    vv = i_ref[...]
    b = [sum(jnp.dot(tril, p[sl, :], preferred_element_type=F32) for p in parts) for sl in chunks]
    tot = [sum(lax.dot_general(p[sl, :], ones, tn, preferred_element_type=F32) for p in parts)
           for sl in chunks]
    qe, ke, qs, ks = [], [], [], []
    for sl, bc in zip(chunks, b):
        b_mid = bc[mid:mid + 1, :]
        b_last = bc[blk - 1:blk, :]
        qe.append((qv[sl, :] * jnp.exp(bc - b_mid)).astype(BF16))
        ke.append((kv[sl, :] * jnp.exp(b_mid - bc)).astype(BF16))
        qs.append((qv[sl, :] * jnp.exp(bc)).astype(BF16))
        ks.append((kv[sl, :] * jnp.exp(b_last - bc)).astype(BF16))
    lanes = [slice(h * B_KEY_DIM, (h + 1) * B_KEY_DIM) for h in range(hp)]
    o_intra, kvs = [], []
    for c, sl in enumerate(chunks):
        sc = [lax.dot_general(qe[c][:, hl], ke[c][:, hl], nt, preferred_element_type=F32)
              for hl in lanes]
        sc = [jnp.where(causal, s, 0.0).astype(BF16) for s in sc]
        o_intra.append([jnp.dot(s, vv[sl, hl], preferred_element_type=F32)
                        for s, hl in zip(sc, lanes)])
        kvs.append([lax.dot_general(ks[c][:, hl], vv[sl, hl], tn, preferred_element_type=F32)
                    for hl in lanes])
    decay = [jnp.exp(tc) for tc in tot]
    cols = []
    for h, hl in enumerate(lanes):
        state = st_ref[h]
        outs = []
        for c in range(nchunk):
            outs.append(o_intra[c][h] + jnp.dot(qs[c][:, hl], state.astype(BF16),
                                                preferred_element_type=F32))
            state = state * decay[c][hl, :] + kvs[c][h]
        st_ref[h] = state
        o = outs[0] if nchunk == 1 else jnp.concatenate(outs, axis=0)
        ms = jnp.mean(o * o, axis=1, keepdims=True)
        cols.append(o * lax.rsqrt(ms + NORM_EPS))
    o = cols[0] if hp == 1 else jnp.concatenate(cols, axis=1)
    y = o * gain * jax.nn.sigmoid(og_ref[...].astype(F32))
    y_ref[...] = (y * _silu(z_ref[...].astype(F32))).astype(BF16)

    @pl.when(t == pl.num_programs(2) - 1)
    def _():
        sf_ref[...] = st_ref[...]


def _hgrn(rest3, lb, gain, s0, blk, col0):
    b, t, _ = rest3.shape
    heads = lb.shape[1] // B_KEY_DIM
    tt = min(t, HGRN_ROWS)
    assert t % tt == 0 and tt % blk == 0
    has_s0 = s0 is not None
    hp = heads if tt // blk == 1 else HGRN_HEADS_PER_STEP
    assert heads % hp == 0 and col0 % hp == 0
    width = hp * B_KEY_DIM
    kern = functools.partial(_hgrn_kernel, blk=blk, nchunk=tt // blk, hp=hp, has_s0=has_s0)

    def colspec(k):
        return pl.BlockSpec((None, tt, width),
                            lambda bi, hi, ti: (bi, ti, (col0 + k * heads) // hp + hi))

    vecspec = pl.BlockSpec((1, width), lambda bi, hi, ti: (0, hi))
    stspec = pl.BlockSpec((None, hp, B_KEY_DIM, B_KEY_DIM), lambda bi, hi, ti: (bi, hi, 0, 0))
    in_specs = [colspec(k) for k in range(5)] + [vecspec, vecspec]
    args = [rest3] * 5 + [lb, gain]
    if has_s0:
        in_specs.append(stspec)
        args.append(s0)
    return pl.pallas_call(
        kern,
        grid=(b, heads // hp, t // tt),
        in_specs=in_specs,
        out_specs=[
            pl.BlockSpec((None, tt, width), lambda bi, hi, ti: (bi, ti, hi)),
            stspec,
        ],
        out_shape=[
            jax.ShapeDtypeStruct((b, t, heads * B_KEY_DIM), BF16),
            jax.ShapeDtypeStruct((b, heads, B_KEY_DIM, B_KEY_DIM), F32),
        ],
        scratch_shapes=[pltpu.VMEM((hp, B_KEY_DIM, B_KEY_DIM), F32)],
        compiler_params=_cparams(("parallel", "parallel", "arbitrary")),
        name="hgrn",
    )(*args)


def _memattn_kernel(q_ref, z_ref, mk_ref, mv_ref, o_ref, kb_ref, vb_ref, *, heads):
    @pl.when(pl.program_id(1) == 0)
    def _():
        kb_ref[...] = mk_ref[...].astype(BF16)
        vb_ref[...] = mv_ref[...].astype(BF16)

    nt = (((1,), (1,)), ((), ()))
    cols = [slice(hd * C_HEAD_DIM, (hd + 1) * C_HEAD_DIM) for hd in range(heads)]
    s = [lax.dot_general(q_ref[:, cs], kb_ref[:, cs], nt, preferred_element_type=F32)
         * (C_HEAD_DIM ** -0.5) for cs in cols]
    e = [jnp.exp(sh - jnp.max(sh, axis=1, keepdims=True)) for sh in s]
    o = [jnp.dot(eh.astype(BF16), vb_ref[:, cs], preferred_element_type=F32)
         / jnp.sum(eh, axis=1, keepdims=True) for eh, cs in zip(e, cols)]
    for oh, cs in zip(o, cols):
        o_ref[:, cs] = (oh * _silu(z_ref[:, cs].astype(F32))).astype(BF16)


def _memattn(rest3, mk3, mv3, qblock, zblock):
    b, t, _ = rest3.shape
    _, m, c = mk3.shape
    tq = min(t, MEMATTN_ROWS)
    assert t % tq == 0
    kern = functools.partial(_memattn_kernel, heads=c // C_HEAD_DIM)
    memspec = pl.BlockSpec((None, m, c), lambda bi, ti: (bi, 0, 0))
    return pl.pallas_call(
        kern,
        grid=(b, t // tq),
        in_specs=[
            pl.BlockSpec((None, tq, c), lambda bi, ti: (bi, ti, qblock)),
            pl.BlockSpec((None, tq, c), lambda bi, ti: (bi, ti, zblock)),
            memspec, memspec,
        ],
        out_specs=pl.BlockSpec((None, tq, c), lambda bi, ti: (bi, ti, 0)),
        out_shape=jax.ShapeDtypeStruct((b, t, c), BF16),
        scratch_shapes=[pltpu.VMEM((m, c), BF16), pltpu.VMEM((m, c), BF16)],
        compiler_params=_cparams(("parallel", "arbitrary")),
        name="memattn",
    )(rest3, rest3, mk3, mv3)


def _merge_kernel(ya_ref, yb_ref, yc_ref, ga_ref, gb_ref, gc_ref, x_ref,
                  wa_ref, wb_ref, wc_ref, wo_ref, lng_ref, lnb_ref, o_ref, *, alpha):
    def branch(y_ref, w_ref, g_ref):
        return jax.nn.sigmoid(g_ref[...].astype(F32)) * jnp.dot(
            y_ref[...], w_ref[...], preferred_element_type=F32)

    merged = branch(ya_ref, wa_ref, ga_ref) + branch(yb_ref, wb_ref, gb_ref)
    merged = merged + branch(yc_ref, wc_ref, gc_ref)
    sub = jnp.dot(merged.astype(BF16), wo_ref[...], preferred_element_type=F32)
    hres = alpha * x_ref[...] + sub
    mu = jnp.mean(hres, axis=1, keepdims=True)
    cen = hres - mu
    var = jnp.mean(cen * cen, axis=1, keepdims=True)
    o_ref[...] = cen * lax.rsqrt(var + NORM_EPS) * lng_ref[...] + lnb_ref[...]


def _merge(ya, yb, yc, rest2, x2, wa, wb, wc, wo, lng, lnb, gate_block0, alpha):
    n, d = x2.shape
    w = ya.shape[1]
    tm = min(n, MERGE_ROWS)
    assert n % tm == 0
    rows = lambda width, blk: pl.BlockSpec((tm, width), lambda i: (i, blk))
    const = lambda shape: pl.BlockSpec(shape, lambda i: (0, 0), pipeline_mode=pl.Buffered(1))
    return pl.pallas_call(
        functools.partial(_merge_kernel, alpha=alpha),
        grid=(n // tm,),
        in_specs=[
            rows(w, 0), rows(w, 0), rows(w, 0),
            rows(d, gate_block0), rows(d, gate_block0 + 1), rows(d, gate_block0 + 2),
            rows(d, 0),
            const((w, d)), const((w, d)), const((w, d)), const((d, d)),
            const((1, d)), const((1, d)),
        ],
        out_specs=rows(d, 0),
        out_shape=jax.ShapeDtypeStruct((n, d), F32),
        compiler_params=_cparams(("parallel",)),
        name="merge",
    )(ya, yb, yc, rest2, rest2, rest2, x2, wa, wb, wc, wo, lng, lnb)


def _layer(x, pos, past_k, past_v, s0, mk3, mv3, rec_block, layer_idx, lb, p):
    (w_in, lam, sub_norm, hgrn_gain, wa, wb, wc, wo, lng, lnb, alpha) = p
    n_b, t, d = x.shape
    a_width = wa.shape[0]
    heads = a_width // A_V_DIM
    x2 = x.reshape(n_b * t, d)
    prompt = past_k is None
    q_scale = A_HEAD_DIM ** -0.5 * math.log2(math.e)
    q2, k_out, v2, rest2 = _inproj(x2, w_in, _rope_tables(pos), t, a_width, q_scale,
                                   k_transposed=prompt)
    r = rest2.shape[1]
    rest3 = rest2.reshape(n_b, t, r)
    q3 = q2.reshape(n_b, t, a_width)
    v3 = v2.reshape(n_b, t, a_width)
    lam_init = 0.8 - 0.6 * math.exp(-0.3 * layer_idx)
    if prompt:
        ya = _attn_prompt(lam, q3, k_out, v3, rest3, sub_norm, lam_init)
        k5 = k_out.reshape(n_b, heads, 2, A_HEAD_DIM, t).transpose(0, 4, 1, 2, 3)
    else:
        k3 = k_out.reshape(n_b, t, a_width)
        ya = _attn_sample(lam, q3, past_k, past_v, k3, v3, rest3,
                          sub_norm, lam_init)
        k5 = k3.reshape(n_b, t, heads, 2, A_HEAD_DIM)
    yb, s_fin = _hgrn(rest3, lb, hgrn_gain, s0, rec_block, a_width // B_KEY_DIM)
    c_width = mk3.shape[2]
    c_off = a_width + 5 * lb.shape[1]
    assert c_off % c_width == 0
    yc = _memattn(rest3, mk3, mv3, c_off // c_width, c_off // c_width + 1)
    g_off = c_off + 2 * c_width
    assert g_off % d == 0
    y2 = _merge(ya.reshape(n_b * t, a_width), yb.reshape(n_b * t, -1), yc.reshape(n_b * t, c_width),
                rest2, x2, wa, wb, wc, wo, lng, lnb, g_off // d, alpha)
    return y2.reshape(n_b, t, d), k5, v3, s_fin


def kernel(x_prompt, x_sample, cache_attn_k, cache_attn_v, state_hgrn, cache_mem_k, cache_mem_v, mem_prompt, w_in, lambda_q1, lambda_k1, lambda_q2, lambda_k2, attn_sub_norm, hgrn_lb_logits, hgrn_norm, w_mem_k, w_mem_v, w_branch_a, w_branch_b, w_branch_c, w_out, ln_gamma, ln_beta):
    bp, seq, d = x_prompt.shape
    bs, t_new, _ = x_sample.shape
    depth = w_in.shape[0]
    past = cache_attn_k.shape[2]
    heads = cache_attn_k.shape[3]
    a_width = heads * A_V_DIM
    n_mem = mem_prompt.shape[1]
    c_heads = cache_mem_k.shape[3]
    c_width = c_heads * C_HEAD_DIM
    b_heads = state_hgrn.shape[2]
    alpha = (2 * depth) ** 0.25
    pos_prompt = jnp.arange(seq)
    pos_sample = past + jnp.arange(t_new)
    lower_bounds = jnp.cumsum(jax.nn.softmax(hgrn_lb_logits.astype(F32), axis=0), axis=0)

    h_p, h_s = x_prompt, x_sample
    outs = [[] for _ in range(8)]
    for l in range(depth):
        lam_init = 0.8 - 0.6 * math.exp(-0.3 * l)
        lam = (jnp.exp(jnp.sum(lambda_q1[l].astype(F32) * lambda_k1[l].astype(F32)))
               - jnp.exp(jnp.sum(lambda_q2[l].astype(F32) * lambda_k2[l].astype(F32))) + lam_init)
        params = (w_in[l].astype(BF16), lam.reshape(1), attn_sub_norm[l].reshape(1, -1),
                  hgrn_norm[l].reshape(1, -1), w_branch_a[l].astype(BF16), w_branch_b[l].astype(BF16),
                  w_branch_c[l].astype(BF16), w_out[l].astype(BF16), ln_gamma[l].reshape(1, -1),
                  ln_beta[l].reshape(1, -1), alpha)
        lb = lower_bounds[l].reshape(1, -1)
        mk_p, mv_p = _memkv(mem_prompt.reshape(bp * n_mem, d), w_mem_k[l].astype(BF16),
                            w_mem_v[l].astype(BF16))
        mk_p = mk_p.reshape(bp, n_mem, c_width)
        mv_p = mv_p.reshape(bp, n_mem, c_width)
        h_p, k_p, v_p, s_p = _layer(h_p, pos_prompt, None, None, None, mk_p, mv_p, CHUNK, l, lb, params)
        h_s, k_s, v_s, s_s = _layer(
            h_s, pos_sample,
            cache_attn_k[l].transpose(0, 2, 3, 4, 1).reshape(bs, a_width, past),
            cache_attn_v[l].reshape(bs, past * heads, A_V_DIM),
            state_hgrn[l], cache_mem_k[l].reshape(bs, n_mem, c_width),
            cache_mem_v[l].reshape(bs, n_mem, c_width), t_new, l, lb, params)
        new = (k_p, v_p.reshape(bp, seq, heads, A_V_DIM),
               s_p.astype(x_prompt.dtype), mk_p.reshape(bp, n_mem, c_heads, C_HEAD_DIM),
               mv_p.reshape(bp, n_mem, c_heads, C_HEAD_DIM),
               k_s, v_s.reshape(bs, t_new, heads, A_V_DIM),
               s_s.astype(x_sample.dtype))
        for acc, val in zip(outs, new):
            acc.append(val)
    return (h_p, h_s) + tuple(jnp.stack(o) for o in outs)
```

```python
import functools
import math

import jax
import jax.numpy as jnp
from jax import lax
from jax.experimental import pallas as pl
from jax.experimental.pallas import tpu as pltpu

F32 = jnp.float32
BF16 = jnp.bfloat16

CHUNK = 64
A_HEAD_DIM = 64
A_V_DIM = 128
ROT_DIM = 16
ROPE_THETA = 500000.0
B_KEY_DIM = 128
C_HEAD_DIM = 256
NORM_EPS = 1e-5
LANES = 128
VMEM_LIMIT = 56 * 1024 * 1024
INPROJ_ROWS = 1024
INPROJ_COLS = 512
MEMKV_ROWS = 512
HGRN_ROWS = 512
MEMATTN_ROWS = 1024
MERGE_ROWS = 256
SAMPLE_KEYS = 2048
ATTN_TILE = 512
ATTN_STRIP = 32
PV_SPLIT = 4
ROW_CHUNKS = 4
NEXT_SCORES_AT = 3
HGRN_HEADS_PER_STEP = 8


def _cparams(sem):
    return pltpu.CompilerParams(dimension_semantics=sem, vmem_limit_bytes=VMEM_LIMIT)


def _silu(z):
    return z * jax.nn.sigmoid(z)


def _rope_tables(pos):
    half = ROT_DIM // 2
    inv_freq = jnp.power(ROPE_THETA, -jnp.arange(0, ROT_DIM, 2, dtype=F32) / ROT_DIM)
    ang = pos.astype(F32)[:, None] * inv_freq[None, :]
    cos, sin = jnp.cos(ang), jnp.sin(ang)
    t = pos.shape[0]
    pad = jnp.zeros((t, A_HEAD_DIM - ROT_DIM), F32)
    zero = jnp.zeros((t, half), F32)
    c64 = jnp.concatenate([cos, cos, pad + 1.0], axis=1)
    sa64 = jnp.concatenate([-sin, zero, pad], axis=1)
    sb64 = jnp.concatenate([zero, sin, pad], axis=1)
    rep = LANES // A_HEAD_DIM
    rows = tuple(jnp.tile(a, (1, rep)) for a in (c64, sa64, sb64))
    return rows, (cos.T, sin.T)


def _inproj_kernel(x_ref, w_ref, c_ref, sa_ref, sb_ref, ct_ref, st_ref,
                   q_ref, k_ref, v_ref, r_ref, xb_ref, t_ref, *, tn, nq, nk, nv, q_scale, k_transposed):
    j = pl.program_id(1)
    half = ROT_DIM // 2

    @pl.when(j == 0)
    def _():
        xb_ref[...] = x_ref[...].astype(BF16)

    tm = xb_ref.shape[0]
    chunks = [slice(r, r + tm // ROW_CHUNKS) for r in range(0, tm, tm // ROW_CHUNKS)]

    def project(rows=slice(None)):
        return jnp.dot(xb_ref[rows, :], w_ref[...], preferred_element_type=F32)

    def rope_group(acc, g, rows=slice(None)):
        xg = acc[:, g * LANES:(g + 1) * LANES]
        up = pltpu.roll(xg, LANES - half, 1)
        dn = pltpu.roll(xg, half, 1)
        return xg * c_ref[rows, :] + up * sa_ref[rows, :] + dn * sb_ref[rows, :]

    @pl.when(j < nq)
    def _():
        for rows in chunks:
            acc = project(rows)
            for g in range(tn // LANES):
                q_ref[rows, g * LANES:(g + 1) * LANES] = (
                    rope_group(acc, g, rows) * q_scale).astype(BF16)

    @pl.when((j >= nq) & (j < nq + nk))
    def _():
        if k_transposed:
            for rows in chunks:
                t_ref[rows, :] = project(rows)
            kt = t_ref[...].T
            cos, sin = ct_ref[...], st_ref[...]
            for g in range(tn // A_HEAD_DIM):
                r0 = g * A_HEAD_DIM
                lo = kt[r0:r0 + half, :]
                hi = kt[r0 + half:r0 + ROT_DIM, :]
                k_ref[r0:r0 + half, :] = lo * cos - hi * sin
                k_ref[r0 + half:r0 + ROT_DIM, :] = hi * cos + lo * sin
                k_ref[r0 + ROT_DIM:r0 + A_HEAD_DIM, :] = kt[r0 + ROT_DIM:r0 + A_HEAD_DIM, :]
        else:
            acc = project()
            for g in range(tn // LANES):
                k_ref[:, g * LANES:(g + 1) * LANES] = rope_group(acc, g)

    @pl.when((j >= nq + nk) & (j < nq + nk + nv))
    def _():
        for rows in chunks:
            v_ref[rows, :] = project(rows)

    @pl.when(j >= nq + nk + nv)
    def _():
        for rows in chunks:
            r_ref[rows, :] = project(rows).astype(BF16)


def _inproj(x2, w_bf, tables, t_len, a_width, q_scale, k_transposed):
    n, d = x2.shape
    n_in = w_bf.shape[1]
    tm = min(n, INPROJ_ROWS)
    tn = INPROJ_COLS if tm == INPROJ_ROWS else 2 * INPROJ_COLS
    assert n % tm == 0 and a_width % tn == 0 and n_in % tn == 0
    row_tabs, col_tabs = tables
    if t_len >= tm:
        assert t_len % tm == 0
        per = t_len // tm
    else:
        assert tm % t_len == 0 and not k_transposed
        row_tabs = tuple(jnp.tile(a, (tm // t_len, 1)) for a in row_tabs)
        col_tabs = tuple(jnp.tile(a, (1, tm // t_len)) for a in col_tabs)
        per = 1
    nq = nk = nv = a_width // tn
    nr = n_in // tn - 3 * nq
    kblk = lambda j: jnp.clip(j - nq, 0, nk - 1)
    row_spec = pl.BlockSpec((tm, LANES), lambda i, j: (i % per, 0))
    col_spec = pl.BlockSpec((ROT_DIM // 2, tm), lambda i, j: (0, i % per))
    if k_transposed:
        k_spec = pl.BlockSpec((None, tn, tm), lambda i, j: (i // per, kblk(j), i % per))
        k_shape = jax.ShapeDtypeStruct((n // t_len, a_width, t_len), F32)
    else:
        k_spec = pl.BlockSpec((tm, tn), lambda i, j: (i, kblk(j)))
        k_shape = jax.ShapeDtypeStruct((n, a_width), F32)
    kern = functools.partial(_inproj_kernel, tn=tn, nq=nq, nk=nk, nv=nv, q_scale=q_scale,
                             k_transposed=k_transposed)
    return pl.pallas_call(
        kern,
        grid=(n // tm, n_in // tn),
        in_specs=[
            pl.BlockSpec((tm, d), lambda i, j: (i, 0)),
            pl.BlockSpec((d, tn), lambda i, j: (0, j)),
            row_spec, row_spec, row_spec, col_spec, col_spec,
        ],
        out_specs=[
            pl.BlockSpec((tm, tn), lambda i, j: (i, jnp.minimum(j, nq - 1))),
            k_spec,
            pl.BlockSpec((tm, tn), lambda i, j: (i, jnp.clip(j - nq - nk, 0, nv - 1))),
            pl.BlockSpec((tm, tn), lambda i, j: (i, jnp.maximum(j - nq - nk - nv, 0))),
        ],
        out_shape=[
            jax.ShapeDtypeStruct((n, a_width), BF16),
            k_shape,
            jax.ShapeDtypeStruct((n, a_width), F32),
            jax.ShapeDtypeStruct((n, nr * tn), BF16),
        ],
        scratch_shapes=[pltpu.VMEM((tm, d), BF16), pltpu.VMEM((tm, tn), F32)],
        compiler_params=_cparams(("parallel", "arbitrary")),
        name="inproj",
    )(x2, w_bf, *row_tabs, *col_tabs)


def _memkv_kernel(x_ref, wk_ref, wv_ref, k_ref, v_ref):
    xb = x_ref[...].astype(BF16)
    k_ref[...] = jnp.dot(xb, wk_ref[...], preferred_element_type=F32)
    v_ref[...] = jnp.dot(xb, wv_ref[...], preferred_element_type=F32)


def _memkv(x2, wk_bf, wv_bf):
    n, d = x2.shape
    c = wk_bf.shape[1]
    tm = min(n, MEMKV_ROWS)
    assert n % tm == 0
    wspec = pl.BlockSpec((d, c), lambda i: (0, 0))
    ospec = pl.BlockSpec((tm, c), lambda i: (i, 0))
    return pl.pallas_call(
        _memkv_kernel,
        grid=(n // tm,),
        in_specs=[pl.BlockSpec((tm, d), lambda i: (i, 0)), wspec, wspec],
        out_specs=[ospec, ospec],
        out_shape=[jax.ShapeDtypeStruct((n, c), F32)] * 2,
        compiler_params=_cparams(("parallel",)),
        name="memkv",
    )(x2, wk_bf, wv_bf)


def _attn_epilogue(o, lam_init, gain, z):
    ms = jnp.mean(o * o, axis=1, keepdims=True)
    y = o * lax.rsqrt(ms + NORM_EPS) * gain * (1.0 - lam_init)
    return y * _silu(z)


def _attn_prompt_kernel(lam_ref, q_ref, kt_ref, v_ref, z_ref, g_ref, o_ref,
                        kb_ref, vb_ref, qa_ref, sa_ref, sb_ref, p_ref, acc_ref, m_ref, al_ref,
                        *, tq, lam_init):
    s_len = q_ref.shape[0]
    strip = ATTN_STRIP
    kb_ref[...] = kt_ref[...].astype(BF16)
    vb_ref[:, :A_V_DIM] = v_ref[...].astype(BF16)
    vb_ref[:, A_V_DIM:] = jnp.ones((s_len, A_V_DIM), BF16)
    lane = lax.broadcasted_iota(jnp.int32, (tq, A_V_DIM), 1)

    def scores(j, s_ref, maps=(0, 1)):
        kt = kb_ref[:, pl.ds(pl.multiple_of(j * tq, tq), tq)]
        for mp in maps:
            s_ref[mp] = jnp.dot(qa_ref[mp], kt, preferred_element_type=F32)

    def softmax_pv(j, s_ref, masked, nxt_ref=None, first=False):
        start = pl.multiple_of(j * tq, tq)
        grp = tq // PV_SPLIT
        if nxt_ref is not None:
            scores(j + 1, nxt_ref, (0,))
        for mp in range(2):
            for g0 in range(0, tq, grp):
                if nxt_ref is not None and mp * tq + g0 == NEXT_SCORES_AT * grp:
                    scores(j + 1, nxt_ref, (1,))
                nk = g0 + grp if masked else tq
                for r in range(g0, g0 + grp, strip):
                    rs = slice(r, r + strip)
                    sv = s_ref[mp, rs, :nk]
                    if masked:
                        cc = lax.broadcasted_iota(jnp.int32, (strip, nk), 1) // CHUNK
                        sv = jnp.where(cc <= r // CHUNK, sv, -jnp.inf)
                    mx = jnp.max(sv, axis=1, keepdims=True)
                    if first:
                        mn = jnp.broadcast_to(mx, (strip, LANES))
                    else:
                        m_old = m_ref[mp, rs, :]
                        mn = jnp.maximum(m_old, mx)
                        al_ref[mp, rs, :] = jnp.exp2(m_old - mn)
                    m_ref[mp, rs, :] = mn
                    for c in range(nk // LANES):
                        cs = slice(c * LANES, (c + 1) * LANES)
                        p_ref[mp, rs, cs] = jnp.exp2(sv[:, cs] - mn).astype(BF16)
                gs = slice(g0, g0 + grp)
                pv = jnp.dot(p_ref[mp, gs, :nk], vb_ref[pl.ds(start, nk), :],
                             preferred_element_type=F32)
                for c in range(2):
                    cs = slice(c * A_V_DIM, (c + 1) * A_V_DIM)
                    if first:
                        acc_ref[mp, gs, cs] = pv[:, cs]
                    else:
                        acc_ref[mp, gs, cs] = acc_ref[mp, gs, cs] * al_ref[mp, gs, :] + pv[:, cs]

    def first_scores(qi):
        qf = q_ref[pl.ds(pl.multiple_of(qi * tq, tq), tq), :].astype(F32)
        qa_ref[0] = jnp.where(lane < A_HEAD_DIM, qf, 0.0).astype(BF16)
        qa_ref[1] = jnp.where(lane >= A_HEAD_DIM, qf, 0.0).astype(BF16)
        scores(0, sa_ref)

    def q_tile(qi, _):
        rows = pl.ds(pl.multiple_of(qi * tq, tq), tq)

        @pl.when(qi > 0)
        def _():
            softmax_pv(0, sa_ref, False, sb_ref, first=True)

        @pl.when(qi == 0)
        def _():
            softmax_pv(0, sa_ref, True, first=True)

        def full_tile(j, carry):
            @pl.when(j % 2 == 0)
            def _():
                softmax_pv(j, sa_ref, False, sb_ref)

            @pl.when(j % 2 == 1)
            def _():
                softmax_pv(j, sb_ref, False, sa_ref)

            return carry

        lax.fori_loop(1, qi, full_tile, 0)

        @pl.when((qi > 0) & (qi % 2 == 0))
        def _():
            softmax_pv(qi, sa_ref, True)

        @pl.when(qi % 2 == 1)
        def _():
            softmax_pv(qi, sb_ref, True)

        first_scores(jnp.minimum(qi + 1, nq - 1))
        o = (acc_ref[0, :, :A_V_DIM] / acc_ref[0, :, A_V_DIM:]
             - lam_ref[0] * (acc_ref[1, :, :A_V_DIM] / acc_ref[1, :, A_V_DIM:]))
        y = _attn_epilogue(o, lam_init, g_ref[...], z_ref[rows, :].astype(F32))
        o_ref[rows, :] = y.astype(BF16)
        return 0

    nq = s_len // tq
    first_scores(0)
    lax.fori_loop(0, nq, q_tile, 0)


def _attn_prompt(lam, q3, kt3, v3, rest3, gain, lam_init):
    b, s, w = q3.shape
    h = w // A_V_DIM
    tq = min(s, ATTN_TILE)
    assert s % tq == 0 and tq % (PV_SPLIT * LANES) == 0
    kern = functools.partial(_attn_prompt_kernel, tq=tq, lam_init=lam_init)
    qspec = pl.BlockSpec((None, s, A_V_DIM), lambda bi, hi: (bi, 0, hi))
    return pl.pallas_call(
        kern,
        grid=(b, h),
        in_specs=[
            pl.BlockSpec(memory_space=pltpu.SMEM),
            qspec,
            pl.BlockSpec((None, A_V_DIM, s), lambda bi, hi: (bi, hi, 0)),
            qspec, qspec,
            pl.BlockSpec((1, A_V_DIM), lambda bi, hi: (0, 0)),
        ],
        out_specs=qspec,
        out_shape=jax.ShapeDtypeStruct((b, s, w), BF16),
        scratch_shapes=[
            pltpu.VMEM((A_V_DIM, s), BF16),
            pltpu.VMEM((s, 2 * A_V_DIM), BF16),
            pltpu.VMEM((2, tq, A_V_DIM), BF16),
            pltpu.VMEM((2, tq, tq), F32),
            pltpu.VMEM((2, tq, tq), F32),
            pltpu.VMEM((2, tq, tq), BF16),
            pltpu.VMEM((2, tq, 2 * A_V_DIM), F32),
            pltpu.VMEM((2, tq, LANES), F32),
            pltpu.VMEM((2, tq, LANES), F32),
        ],
        compiler_params=_cparams(("parallel", "parallel")),
        name="attn_prompt",
    )(lam, q3, kt3, v3, rest3, gain)


def _attn_sample_kernel(lam_ref, q_ref, kc_ref, vc_ref, kn_ref, vn_ref, z_ref, g_ref, o_ref,
                        qbd_ref, s_ref, w_ref, acc_ref, *, tk, nkt, t_new, heads, lam_init):
    j = pl.program_id(1)
    past = tk * nkt
    nt = (((1,), (1,)), ((), ()))
    width = kn_ref.shape[1]

    @pl.when(j == 0)
    def _():
        lane = lax.broadcasted_iota(jnp.int32, (t_new, A_V_DIM), 1)
        zero = jnp.zeros((t_new, A_V_DIM), BF16)
        for hd in range(heads):
            qh = q_ref[:, hd * A_V_DIM:(hd + 1) * A_V_DIM].astype(F32)
            for mp in range(2):
                r0 = (2 * hd + mp) * t_new
                mine = lane >= A_HEAD_DIM if mp == 1 else lane < A_HEAD_DIM
                own = jnp.where(mine, qh, 0.0).astype(BF16)
                for hb in range(heads):
                    qbd_ref[r0:r0 + t_new, hb * A_V_DIM:(hb + 1) * A_V_DIM] = own if hb == hd else zero

    def pad_new(ref):
        new = ref[...]
        return jnp.concatenate([new, jnp.zeros((LANES - t_new, width), F32)], axis=0).astype(BF16)

    @pl.when(j < nkt)
    def _():
        s = jnp.dot(qbd_ref[...], kc_ref[...].astype(BF16), preferred_element_type=F32)
        s_ref[:, pl.ds(pl.multiple_of(j * tk, tk), tk)] = s

    @pl.when(j == nkt - 1)
    def _():
        sn = lax.dot_general(qbd_ref[...], pad_new(kn_ref), nt, preferred_element_type=F32)
        col = lax.broadcasted_iota(jnp.int32, sn.shape, 1)
        s_ref[:, past:past + LANES] = jnp.where(col < t_new, sn, -jnp.inf)
        lam = lam_ref[0]
        for hd in range(heads):
            r0 = hd * 2 * t_new
            p = []
            for mp in range(2):
                s = s_ref[r0 + mp * t_new:r0 + (mp + 1) * t_new, :]
                e = jnp.exp2(s - jnp.max(s, axis=1, keepdims=True))
                p.append(e / jnp.sum(e, axis=1, keepdims=True))
            w_ref[hd * t_new:(hd + 1) * t_new, :] = (p[0] - lam * p[1]).astype(BF16)

    @pl.when(j == nkt)
    def _():
        full = jnp.dot(w_ref[:, past:past + LANES], pad_new(vn_ref), preferred_element_type=F32)
        for hd in range(heads):
            rs = slice(hd * t_new, (hd + 1) * t_new)
            acc_ref[rs, :] = full[rs, hd * A_V_DIM:(hd + 1) * A_V_DIM]

    @pl.when(j >= nkt)
    def _():
        start = pl.multiple_of((j - nkt) * tk, tk)
        for hd in range(heads):
            rs = slice(hd * t_new, (hd + 1) * t_new)
            vh = vc_ref[pl.ds(hd, tk, stride=heads), :].astype(BF16)
            acc_ref[rs, :] += jnp.dot(w_ref[rs, pl.ds(start, tk)], vh, preferred_element_type=F32)

    @pl.when(j == 2 * nkt - 1)
    def _():
        for hd in range(heads):
            cs = slice(hd * A_V_DIM, (hd + 1) * A_V_DIM)
            o = acc_ref[hd * t_new:(hd + 1) * t_new, :]
            y = _attn_epilogue(o, lam_init, g_ref[...], z_ref[:, cs].astype(F32))
            o_ref[:, cs] = y.astype(BF16)


def _attn_sample(lam, q3, kct, vc3, kn, vn, rest3, gain, lam_init):
    b, w, past = kct.shape
    t_new = kn.shape[1]
    heads = w // A_V_DIM
    rows = 2 * heads * t_new
    tk = min(past, SAMPLE_KEYS)
    assert past % tk == 0 and t_new % 16 == 0 and t_new <= LANES
    nkt = past // tk
    kern = functools.partial(_attn_sample_kernel, tk=tk, nkt=nkt, t_new=t_new, heads=heads,
                             lam_init=lam_init)
    newspec = pl.BlockSpec((None, t_new, w), lambda bi, j: (bi, 0, 0))
    return pl.pallas_call(
        kern,
        grid=(b, 2 * nkt),
        in_specs=[
            pl.BlockSpec(memory_space=pltpu.SMEM),
            newspec,
            pl.BlockSpec((None, w, tk), lambda bi, j: (bi, 0, jnp.minimum(j, nkt - 1))),
            pl.BlockSpec((None, tk * heads, A_V_DIM),
                         lambda bi, j: (bi, jnp.maximum(j - nkt, 0), 0)),
            newspec, newspec, newspec,
            pl.BlockSpec((1, A_V_DIM), lambda bi, j: (0, 0)),
        ],
        out_specs=newspec,
        out_shape=jax.ShapeDtypeStruct((b, t_new, w), BF16),
        scratch_shapes=[
            pltpu.VMEM((rows, w), BF16),
            pltpu.VMEM((rows, past + LANES), F32),
            pltpu.VMEM((rows // 2, past + LANES), BF16),
            pltpu.VMEM((rows // 2, A_V_DIM), F32),
        ],
        compiler_params=_cparams(("parallel", "arbitrary")),
        name="attn_sample",
    )(lam, q3, kct, vc3, kn, vn, rest3, gain)


def _hgrn_kernel(*refs, blk, nchunk, hp, has_s0):
    if has_s0:
        q_ref, f_ref, i_ref, og_ref, z_ref, lb_ref, gain_ref, s0_ref, y_ref, sf_ref, st_ref = refs
    else:
        q_ref, f_ref, i_ref, og_ref, z_ref, lb_ref, gain_ref, y_ref, sf_ref, st_ref = refs
    t = pl.program_id(2)

    @pl.when(t == 0)
    def _():
        for h in range(hp):
            st_ref[h] = s0_ref[h].T if has_s0 else jnp.zeros(st_ref.shape[1:], F32)

    lb = lb_ref[...]
    gain = gain_ref[...]
    row = lax.broadcasted_iota(jnp.int32, (blk, blk), 0)
    col = lax.broadcasted_iota(jnp.int32, (blk, blk), 1)
    causal = col <= row
    tril = jnp.where(causal, 1.0, 0.0).astype(BF16)
    mid = (blk - 1) // 2
    nt = (((1,), (1,)), ((), ()))
    tn = (((0,), (0,)), ((), ()))

    def split3(g):
        hi = g.astype(BF16)
        r1 = g - hi.astype(F32)
        md = r1.astype(BF16)
        lo = (r1 - md.astype(F32)).astype(BF16)
        return hi, md, lo

    chunks = [slice(c * blk, (c + 1) * blk) for c in range(nchunk)]
    f = lb + (1.0 - lb) * jax.nn.sigmoid(f_ref[...].astype(F32))
    parts = split3(jnp.log(f))
    qv = _silu(q_ref[...].astype(F32))
    kv = 1.0 - f
    vv = i_ref[...]
    b = [sum(jnp.dot(tril, p[sl, :], preferred_element_type=F32) for p in parts) for sl in chunks]
    qe, ke, qs, ks, decay = [], [], [], [], []
    for sl, bc in zip(chunks, b):
        b_mid = bc[mid:mid + 1, :]
        b_last = bc[blk - 1:blk, :]
        decay.append(jnp.exp(b_last))
        qe.append((qv[sl, :] * jnp.exp(bc - b_mid)).astype(BF16))
        ke.append((kv[sl, :] * jnp.exp(b_mid - bc)).astype(BF16))
        qs.append((qv[sl, :] * jnp.exp(bc)).astype(BF16))
        ks.append((kv[sl, :] * jnp.exp(b_last - bc)).astype(BF16))
    lanes = [slice(h * B_KEY_DIM, (h + 1) * B_KEY_DIM) for h in range(hp)]
    o_intra, kvs = [], []
    for c, sl in enumerate(chunks):
        sc = [lax.dot_general(qe[c][:, hl], ke[c][:, hl], nt, preferred_element_type=F32)
              for hl in lanes]
        sc = [jnp.where(causal, s, 0.0).astype(BF16) for s in sc]
        o_intra.append([jnp.dot(s, vv[sl, hl], preferred_element_type=F32)
                        for s, hl in zip(sc, lanes)])
        kvs.append([lax.dot_general(vv[sl, hl], ks[c][:, hl], tn, preferred_element_type=F32)
                    for hl in lanes])
    cols = []
    for h, hl in enumerate(lanes):
        state = st_ref[h]
        outs = []
        for c in range(nchunk):
            outs.append(o_intra[c][h] + lax.dot_general(qs[c][:, hl], state.astype(BF16), nt,
                                                        preferred_element_type=F32))
            state = state * decay[c][:, hl] + kvs[c][h]
        st_ref[h] = state
        o = outs[0] if nchunk == 1 else jnp.concatenate(outs, axis=0)
        ms = jnp.mean(o * o, axis=1, keepdims=True)
        cols.append(o * lax.rsqrt(ms + NORM_EPS))
    o = cols[0] if hp == 1 else jnp.concatenate(cols, axis=1)
    y = o * gain * jax.nn.sigmoid(og_ref[...].astype(F32))
    y_ref[...] = (y * _silu(z_ref[...].astype(F32))).astype(BF16)

    @pl.when(t == pl.num_programs(2) - 1)
    def _():
        for h in range(hp):
            sf_ref[h] = st_ref[h].T


def _hgrn(rest3, lb, gain, s0, blk, col0):
    b, t, _ = rest3.shape
    heads = lb.shape[1] // B_KEY_DIM
    tt = min(t, HGRN_ROWS)
    assert t % tt == 0 and tt % blk == 0
    has_s0 = s0 is not None
    hp = heads if tt // blk == 1 else HGRN_HEADS_PER_STEP
    assert heads % hp == 0 and col0 % hp == 0
    width = hp * B_KEY_DIM
    kern = functools.partial(_hgrn_kernel, blk=blk, nchunk=tt // blk, hp=hp, has_s0=has_s0)

    def colspec(k):
        return pl.BlockSpec((None, tt, width),
                            lambda bi, hi, ti: (bi, ti, (col0 + k * heads) // hp + hi))

    vecspec = pl.BlockSpec((1, width), lambda bi, hi, ti: (0, hi))
    stspec = pl.BlockSpec((None, hp, B_KEY_DIM, B_KEY_DIM), lambda bi, hi, ti: (bi, hi, 0, 0))
    in_specs = [colspec(k) for k in range(5)] + [vecspec, vecspec]
    args = [rest3] * 5 + [lb, gain]
    if has_s0:
        in_specs.append(stspec)
        args.append(s0)
    return pl.pallas_call(
        kern,
        grid=(b, heads // hp, t // tt),
        in_specs=in_specs,
        out_specs=[
            pl.BlockSpec((None, tt, width), lambda bi, hi, ti: (bi, ti, hi)),
            stspec,
        ],
        out_shape=[
            jax.ShapeDtypeStruct((b, t, heads * B_KEY_DIM), BF16),
            jax.ShapeDtypeStruct((b, heads, B_KEY_DIM, B_KEY_DIM), F32),
        ],
        scratch_shapes=[pltpu.VMEM((hp, B_KEY_DIM, B_KEY_DIM), F32)],
        compiler_params=_cparams(("parallel", "parallel", "arbitrary")),
        name="hgrn",
    )(*args)


def _memattn_kernel(q_ref, z_ref, mk_ref, mv_ref, o_ref, kb_ref, vb_ref, *, heads):
    @pl.when(pl.program_id(1) == 0)
    def _():
        kb_ref[...] = mk_ref[...].astype(BF16)
        vb_ref[...] = mv_ref[...].astype(BF16)

    nt = (((1,), (1,)), ((), ()))
    cols = [slice(hd * C_HEAD_DIM, (hd + 1) * C_HEAD_DIM) for hd in range(heads)]
    s = [lax.dot_general(q_ref[:, cs], kb_ref[:, cs], nt, preferred_element_type=F32)
         * (C_HEAD_DIM ** -0.5) for cs in cols]
    e = [jnp.exp(sh - jnp.max(sh, axis=1, keepdims=True)) for sh in s]
    o = [jnp.dot(eh.astype(BF16), vb_ref[:, cs], preferred_element_type=F32)
         / jnp.sum(eh, axis=1, keepdims=True) for eh, cs in zip(e, cols)]
    for oh, cs in zip(o, cols):
        o_ref[:, cs] = (oh * _silu(z_ref[:, cs].astype(F32))).astype(BF16)


def _memattn(rest3, mk3, mv3, qblock, zblock):
    b, t, _ = rest3.shape
    _, m, c = mk3.shape
    tq = min(t, MEMATTN_ROWS)
    assert t % tq == 0
    kern = functools.partial(_memattn_kernel, heads=c // C_HEAD_DIM)
    memspec = pl.BlockSpec((None, m, c), lambda bi, ti: (bi, 0, 0))
    return pl.pallas_call(
        kern,
        grid=(b, t // tq),
        in_specs=[
            pl.BlockSpec((None, tq, c), lambda bi, ti: (bi, ti, qblock)),
            pl.BlockSpec((None, tq, c), lambda bi, ti: (bi, ti, zblock)),
            memspec, memspec,
        ],
        out_specs=pl.BlockSpec((None, tq, c), lambda bi, ti: (bi, ti, 0)),
        out_shape=jax.ShapeDtypeStruct((b, t, c), BF16),
        scratch_shapes=[pltpu.VMEM((m, c), BF16), pltpu.VMEM((m, c), BF16)],
        compiler_params=_cparams(("parallel", "arbitrary")),
        name="memattn",
    )(rest3, rest3, mk3, mv3)


def _merge_kernel(ya_ref, yb_ref, yc_ref, ga_ref, gb_ref, gc_ref, x_ref,
                  wa_ref, wb_ref, wc_ref, wo_ref, lng_ref, lnb_ref, o_ref, *, alpha):
    def branch(y_ref, w_ref, g_ref):
        return jax.nn.sigmoid(g_ref[...].astype(F32)) * jnp.dot(
            y_ref[...], w_ref[...], preferred_element_type=F32)

    merged = branch(ya_ref, wa_ref, ga_ref) + branch(yb_ref, wb_ref, gb_ref)
    merged = merged + branch(yc_ref, wc_ref, gc_ref)
    sub = jnp.dot(merged.astype(BF16), wo_ref[...], preferred_element_type=F32)
    hres = alpha * x_ref[...] + sub
    mu = jnp.mean(hres, axis=1, keepdims=True)
    cen = hres - mu
    var = jnp.mean(cen * cen, axis=1, keepdims=True)
    o_ref[...] = cen * lax.rsqrt(var + NORM_EPS) * lng_ref[...] + lnb_ref[...]


def _merge(ya, yb, yc, rest2, x2, wa, wb, wc, wo, lng, lnb, gate_block0, alpha):
    n, d = x2.shape
    w = ya.shape[1]
    tm = min(n, MERGE_ROWS)
    assert n % tm == 0
    rows = lambda width, blk: pl.BlockSpec((tm, width), lambda i: (i, blk))
    const = lambda shape: pl.BlockSpec(shape, lambda i: (0, 0), pipeline_mode=pl.Buffered(1))
    return pl.pallas_call(
        functools.partial(_merge_kernel, alpha=alpha),
        grid=(n // tm,),
        in_specs=[
            rows(w, 0), rows(w, 0), rows(w, 0),
            rows(d, gate_block0), rows(d, gate_block0 + 1), rows(d, gate_block0 + 2),
            rows(d, 0),
            const((w, d)), const((w, d)), const((w, d)), const((d, d)),
            const((1, d)), const((1, d)),
        ],
        out_specs=rows(d, 0),
        out_shape=jax.ShapeDtypeStruct((n, d), F32),
        compiler_params=_cparams(("parallel",)),
        name="merge",
    )(ya, yb, yc, rest2, rest2, rest2, x2, wa, wb, wc, wo, lng, lnb)


def _layer(x, pos, past_k, past_v, s0, mk3, mv3, rec_block, layer_idx, lb, p):
    (w_in, lam, sub_norm, hgrn_gain, wa, wb, wc, wo, lng, lnb, alpha) = p
    n_b, t, d = x.shape
    a_width = wa.shape[0]
    heads = a_width // A_V_DIM
    x2 = x.reshape(n_b * t, d)
    prompt = past_k is None
    q_scale = A_HEAD_DIM ** -0.5 * math.log2(math.e)
    q2, k_out, v2, rest2 = _inproj(x2, w_in, _rope_tables(pos), t, a_width, q_scale,
                                   k_transposed=prompt)
    r = rest2.shape[1]
    rest3 = rest2.reshape(n_b, t, r)
    q3 = q2.reshape(n_b, t, a_width)
    v3 = v2.reshape(n_b, t, a_width)
    lam_init = 0.8 - 0.6 * math.exp(-0.3 * layer_idx)
    if prompt:
        ya = _attn_prompt(lam, q3, k_out, v3, rest3, sub_norm, lam_init)
        k5 = k_out.reshape(n_b, heads, 2, A_HEAD_DIM, t).transpose(0, 4, 1, 2, 3)
    else:
        k3 = k_out.reshape(n_b, t, a_width)
        ya = _attn_sample(lam, q3, past_k, past_v, k3, v3, rest3,
                          sub_norm, lam_init)
        k5 = k3.reshape(n_b, t, heads, 2, A_HEAD_DIM)
    yb, s_fin = _hgrn(rest3, lb, hgrn_gain, s0, rec_block, a_width // B_KEY_DIM)
    c_width = mk3.shape[2]
    c_off = a_width + 5 * lb.shape[1]
    assert c_off % c_width == 0
    yc = _memattn(rest3, mk3, mv3, c_off // c_width, c_off // c_width + 1)
    g_off = c_off + 2 * c_width
    assert g_off % d == 0
    y2 = _merge(ya.reshape(n_b * t, a_width), yb.reshape(n_b * t, -1), yc.reshape(n_b * t, c_width),
                rest2, x2, wa, wb, wc, wo, lng, lnb, g_off // d, alpha)
    return y2.reshape(n_b, t, d), k5, v3, s_fin


def kernel(x_prompt, x_sample, cache_attn_k, cache_attn_v, state_hgrn, cache_mem_k, cache_mem_v, mem_prompt, w_in, lambda_q1, lambda_k1, lambda_q2, lambda_k2, attn_sub_norm, hgrn_lb_logits, hgrn_norm, w_mem_k, w_mem_v, w_branch_a, w_branch_b, w_branch_c, w_out, ln_gamma, ln_beta):
    bp, seq, d = x_prompt.shape
    bs, t_new, _ = x_sample.shape
    depth = w_in.shape[0]
    past = cache_attn_k.shape[2]
    heads = cache_attn_k.shape[3]
    a_width = heads * A_V_DIM
    n_mem = mem_prompt.shape[1]
    c_heads = cache_mem_k.shape[3]
    c_width = c_heads * C_HEAD_DIM
    b_heads = state_hgrn.shape[2]
    alpha = (2 * depth) ** 0.25
    pos_prompt = jnp.arange(seq)
    pos_sample = past + jnp.arange(t_new)
    lower_bounds = jnp.cumsum(jax.nn.softmax(hgrn_lb_logits.astype(F32), axis=0), axis=0)

    h_p, h_s = x_prompt, x_sample
    outs = [[] for _ in range(8)]
    for l in range(depth):
        lam_init = 0.8 - 0.6 * math.exp(-0.3 * l)
        lam = (jnp.exp(jnp.sum(lambda_q1[l].astype(F32) * lambda_k1[l].astype(F32)))
               - jnp.exp(jnp.sum(lambda_q2[l].astype(F32) * lambda_k2[l].astype(F32))) + lam_init)
        params = (w_in[l].astype(BF16), lam.reshape(1), attn_sub_norm[l].reshape(1, -1),
                  hgrn_norm[l].reshape(1, -1), w_branch_a[l].astype(BF16), w_branch_b[l].astype(BF16),
                  w_branch_c[l].astype(BF16), w_out[l].astype(BF16), ln_gamma[l].reshape(1, -1),
                  ln_beta[l].reshape(1, -1), alpha)
        lb = lower_bounds[l].reshape(1, -1)
        mk_p, mv_p = _memkv(mem_prompt.reshape(bp * n_mem, d), w_mem_k[l].astype(BF16),
                            w_mem_v[l].astype(BF16))
        mk_p = mk_p.reshape(bp, n_mem, c_width)
        mv_p = mv_p.reshape(bp, n_mem, c_width)
        h_p, k_p, v_p, s_p = _layer(h_p, pos_prompt, None, None, None, mk_p, mv_p, CHUNK, l, lb, params)
        h_s, k_s, v_s, s_s = _layer(
            h_s, pos_sample,
            cache_attn_k[l].transpose(0, 2, 3, 4, 1).reshape(bs, a_width, past),
            cache_attn_v[l].reshape(bs, past * heads, A_V_DIM),
            state_hgrn[l], cache_mem_k[l].reshape(bs, n_mem, c_width),
            cache_mem_v[l].reshape(bs, n_mem, c_width), t_new, l, lb, params)
        new = (k_p, v_p.reshape(bp, seq, heads, A_V_DIM),
               s_p.astype(x_prompt.dtype), mk_p.reshape(bp, n_mem, c_heads, C_HEAD_DIM),
               mv_p.reshape(bp, n_mem, c_heads, C_HEAD_DIM),
               k_s, v_s.reshape(bs, t_new, heads, A_V_DIM),
               s_s.astype(x_sample.dtype))
        for acc, val in zip(outs, new):
            acc.append(val)
    return (h_p, h_s) + tuple(jnp.stack(o) for o in outs)
```

```python
import functools
import math

import jax
import jax.numpy as jnp
from jax import lax
from jax.experimental import pallas as pl
from jax.experimental.pallas import tpu as pltpu

F32 = jnp.float32
BF16 = jnp.bfloat16

CHUNK = 64
A_HEAD_DIM = 64
A_V_DIM = 128
ROT_DIM = 16
ROPE_THETA = 500000.0
B_KEY_DIM = 128
C_HEAD_DIM = 256
NORM_EPS = 1e-5
LANES = 128
VMEM_LIMIT = 56 * 1024 * 1024
INPROJ_ROWS = 1024
INPROJ_COLS = 512
REST_COLS = 2048
MEMKV_ROWS = 512
HGRN_ROWS = 512
MEMATTN_ROWS = 1024
MERGE_ROWS = 256
SAMPLE_KEYS = 2048
ATTN_TILE = 512
ATTN_STRIP = 32
PV_SPLIT = 4
ROW_CHUNKS = 4
NEXT_SCORES_AT = 3
HGRN_HEADS_PER_STEP = 8


def _cparams(sem):
    return pltpu.CompilerParams(dimension_semantics=sem, vmem_limit_bytes=VMEM_LIMIT)


def _silu(z):
    return z * jax.nn.sigmoid(z)


def _rope_tables(pos):
    half = ROT_DIM // 2
    inv_freq = jnp.power(ROPE_THETA, -jnp.arange(0, ROT_DIM, 2, dtype=F32) / ROT_DIM)
    ang = pos.astype(F32)[:, None] * inv_freq[None, :]
    cos, sin = jnp.cos(ang), jnp.sin(ang)
    t = pos.shape[0]
    pad = jnp.zeros((t, A_HEAD_DIM - ROT_DIM), F32)
    zero = jnp.zeros((t, half), F32)
    c64 = jnp.concatenate([cos, cos, pad + 1.0], axis=1)
    sa64 = jnp.concatenate([-sin, zero, pad], axis=1)
    sb64 = jnp.concatenate([zero, sin, pad], axis=1)
    rep = LANES // A_HEAD_DIM
    rows = tuple(jnp.tile(a, (1, rep)) for a in (c64, sa64, sb64))
    return rows, (cos.T, sin.T)


def _inproj_kernel(x_ref, w_ref, c_ref, sa_ref, sb_ref, ct_ref, st_ref,
                   q_ref, k_ref, v_ref, xb_ref, t_ref, *, tn, nq, nk, q_scale, k_transposed):
    j = pl.program_id(1)
    half = ROT_DIM // 2

    @pl.when(j == 0)
    def _():
        xb_ref[...] = x_ref[...].astype(BF16)

    tm = xb_ref.shape[0]
    chunks = [slice(r, r + tm // ROW_CHUNKS) for r in range(0, tm, tm // ROW_CHUNKS)]

    def project(rows=slice(None)):
        return jnp.dot(xb_ref[rows, :], w_ref[...], preferred_element_type=F32)

    def rope_group(acc, g, rows=slice(None)):
        xg = acc[:, g * LANES:(g + 1) * LANES]
        up = pltpu.roll(xg, LANES - half, 1)
        dn = pltpu.roll(xg, half, 1)
        return xg * c_ref[rows, :] + up * sa_ref[rows, :] + dn * sb_ref[rows, :]

    @pl.when(j < nq)
    def _():
        for rows in chunks:
            acc = project(rows)
            for g in range(tn // LANES):
                q_ref[rows, g * LANES:(g + 1) * LANES] = (
                    rope_group(acc, g, rows) * q_scale).astype(BF16)

    @pl.when((j >= nq) & (j < nq + nk))
    def _():
        if k_transposed:
            for rows in chunks:
                t_ref[rows, :] = project(rows)
            kt = t_ref[...].T
            cos, sin = ct_ref[...], st_ref[...]
            for g in range(tn // A_HEAD_DIM):
                r0 = g * A_HEAD_DIM
                lo = kt[r0:r0 + half, :]
                hi = kt[r0 + half:r0 + ROT_DIM, :]
                k_ref[r0:r0 + half, :] = lo * cos - hi * sin
                k_ref[r0 + half:r0 + ROT_DIM, :] = hi * cos + lo * sin
                k_ref[r0 + ROT_DIM:r0 + A_HEAD_DIM, :] = kt[r0 + ROT_DIM:r0 + A_HEAD_DIM, :]
        else:
            acc = project()
            for g in range(tn // LANES):
                k_ref[:, g * LANES:(g + 1) * LANES] = rope_group(acc, g)

    @pl.when(j >= nq + nk)
    def _():
        for rows in chunks:
            v_ref[rows, :] = project(rows)


def _proj_rest_kernel(x_ref, w_ref, r_ref, xb_ref):
    @pl.when(pl.program_id(1) == 0)
    def _():
        xb_ref[...] = x_ref[...].astype(BF16)

    tm = xb_ref.shape[0]
    for r in range(0, tm, tm // ROW_CHUNKS):
        rows = slice(r, r + tm // ROW_CHUNKS)
        r_ref[rows, :] = jnp.dot(xb_ref[rows, :], w_ref[...], preferred_element_type=F32).astype(BF16)


def _proj_rest(x2, w_bf, tm):
    n, d = x2.shape
    ncols = w_bf.shape[1]
    tn = REST_COLS
    assert ncols % tn == 0 and n % tm == 0
    return pl.pallas_call(
        _proj_rest_kernel,
        grid=(n // tm, ncols // tn),
        in_specs=[
            pl.BlockSpec((tm, d), lambda i, j: (i, 0)),
            pl.BlockSpec((d, tn), lambda i, j: (0, j)),
        ],
        out_specs=pl.BlockSpec((tm, tn), lambda i, j: (i, j)),
        out_shape=jax.ShapeDtypeStruct((n, ncols), BF16),
        scratch_shapes=[pltpu.VMEM((tm, d), BF16)],
        compiler_params=_cparams(("parallel", "arbitrary")),
        name="proj_rest",
    )(x2, w_bf)


def _inproj(x2, w_qkv, w_rest, tables, t_len, q_scale, k_transposed):
    n, d = x2.shape
    n_in = w_qkv.shape[1]
    a_width = n_in // 3
    tm = min(n, INPROJ_ROWS)
    tn = INPROJ_COLS if tm == INPROJ_ROWS else 2 * INPROJ_COLS
    assert n % tm == 0 and a_width % tn == 0
    row_tabs, col_tabs = tables
    if t_len >= tm:
        assert t_len % tm == 0
        per = t_len // tm
    else:
        assert tm % t_len == 0 and not k_transposed
        row_tabs = tuple(jnp.tile(a, (tm // t_len, 1)) for a in row_tabs)
        col_tabs = tuple(jnp.tile(a, (1, tm // t_len)) for a in col_tabs)
        per = 1
    nq = nk = a_width // tn
    kblk = lambda j: jnp.clip(j - nq, 0, nk - 1)
    row_spec = pl.BlockSpec((tm, LANES), lambda i, j: (i % per, 0))
    col_spec = pl.BlockSpec((ROT_DIM // 2, tm), lambda i, j: (0, i % per))
    if k_transposed:
        k_spec = pl.BlockSpec((None, tn, tm), lambda i, j: (i // per, kblk(j), i % per))
        k_shape = jax.ShapeDtypeStruct((n // t_len, a_width, t_len), F32)
    else:
        k_spec = pl.BlockSpec((tm, tn), lambda i, j: (i, kblk(j)))
        k_shape = jax.ShapeDtypeStruct((n, a_width), F32)
    kern = functools.partial(_inproj_kernel, tn=tn, nq=nq, nk=nk, q_scale=q_scale,
                             k_transposed=k_transposed)
    q2, k_out, v2 = pl.pallas_call(
        kern,
        grid=(n // tm, n_in // tn),
        in_specs=[
            pl.BlockSpec((tm, d), lambda i, j: (i, 0)),
            pl.BlockSpec((d, tn), lambda i, j: (0, j)),
            row_spec, row_spec, row_spec, col_spec, col_spec,
        ],
        out_specs=[
            pl.BlockSpec((tm, tn), lambda i, j: (i, jnp.minimum(j, nq - 1))),
            k_spec,
            pl.BlockSpec((tm, tn), lambda i, j: (i, jnp.maximum(j - nq - nk, 0))),
        ],
        out_shape=[
            jax.ShapeDtypeStruct((n, a_width), BF16),
            k_shape,
            jax.ShapeDtypeStruct((n, a_width), F32),
        ],
        scratch_shapes=[pltpu.VMEM((tm, d), BF16), pltpu.VMEM((tm, tn), F32)],
        compiler_params=_cparams(("parallel", "arbitrary")),
        name="inproj",
    )(x2, w_qkv, *row_tabs, *col_tabs)
    return q2, k_out, v2, _proj_rest(x2, w_rest, tm)


def _memkv_kernel(x_ref, wk_ref, wv_ref, k_ref, v_ref):
    xb = x_ref[...].astype(BF16)
    k_ref[...] = jnp.dot(xb, wk_ref[...], preferred_element_type=F32)
    v_ref[...] = jnp.dot(xb, wv_ref[...], preferred_element_type=F32)


def _memkv(x2, wk_bf, wv_bf):
    n, d = x2.shape
    c = wk_bf.shape[1]
    tm = min(n, MEMKV_ROWS)
    assert n % tm == 0
    wspec = pl.BlockSpec((d, c), lambda i: (0, 0))
    ospec = pl.BlockSpec((tm, c), lambda i: (i, 0))
    return pl.pallas_call(
        _memkv_kernel,
        grid=(n // tm,),
        in_specs=[pl.BlockSpec((tm, d), lambda i: (i, 0)), wspec, wspec],
        out_specs=[ospec, ospec],
        out_shape=[jax.ShapeDtypeStruct((n, c), F32)] * 2,
        compiler_params=_cparams(("parallel",)),
        name="memkv",
    )(x2, wk_bf, wv_bf)


def _attn_epilogue(o, lam_init, gain, z):
    ms = jnp.mean(o * o, axis=1, keepdims=True)
    y = o * lax.rsqrt(ms + NORM_EPS) * gain * (1.0 - lam_init)
    return y * _silu(z)


def _attn_prompt_kernel(lam_ref, q_ref, kt_ref, v_ref, z_ref, g_ref, o_ref,
                        kb_ref, vb_ref, qa_ref, sa_ref, sb_ref, p_ref, acc_ref, m_ref, al_ref,
                        *, tq, lam_init):
    s_len = q_ref.shape[0]
    strip = ATTN_STRIP
    kb_ref[...] = kt_ref[...].astype(BF16)
    vb_ref[:, :A_V_DIM] = v_ref[...].astype(BF16)
    vb_ref[:, A_V_DIM:] = jnp.ones((s_len, A_V_DIM), BF16)
    lane = lax.broadcasted_iota(jnp.int32, (tq, A_V_DIM), 1)

    def scores(j, s_ref, maps=(0, 1)):
        kt = kb_ref[:, pl.ds(pl.multiple_of(j * tq, tq), tq)]
        for mp in maps:
            s_ref[mp] = jnp.dot(qa_ref[mp], kt, preferred_element_type=F32)

    def softmax_pv(j, s_ref, masked, nxt_ref=None, first=False):
        start = pl.multiple_of(j * tq, tq)
        grp = tq // PV_SPLIT
        if nxt_ref is not None:
            scores(j + 1, nxt_ref, (0,))
        for mp in range(2):
            for g0 in range(0, tq, grp):
                if nxt_ref is not None and mp * tq + g0 == NEXT_SCORES_AT * grp:
                    scores(j + 1, nxt_ref, (1,))
                nk = g0 + grp if masked else tq
                for r in range(g0, g0 + grp, strip):
                    rs = slice(r, r + strip)
                    sv = s_ref[mp, rs, :nk]
                    if masked:
                        cc = lax.broadcasted_iota(jnp.int32, (strip, nk), 1) // CHUNK
                        sv = jnp.where(cc <= r // CHUNK, sv, -jnp.inf)
                    mx = jnp.max(sv, axis=1, keepdims=True)
                    if first:
                        mn = jnp.broadcast_to(mx, (strip, LANES))
                    else:
                        m_old = m_ref[mp, rs, :]
                        mn = jnp.maximum(m_old, mx)
                        al_ref[mp, rs, :] = jnp.exp2(m_old - mn)
                    m_ref[mp, rs, :] = mn
                    for c in range(nk // LANES):
                        cs = slice(c * LANES, (c + 1) * LANES)
                        p_ref[mp, rs, cs] = jnp.exp2(sv[:, cs] - mn).astype(BF16)
                gs = slice(g0, g0 + grp)
                pv = jnp.dot(p_ref[mp, gs, :nk], vb_ref[pl.ds(start, nk), :],
                             preferred_element_type=F32)
                for c in range(2):
                    cs = slice(c * A_V_DIM, (c + 1) * A_V_DIM)
                    if first:
                        acc_ref[mp, gs, cs] = pv[:, cs]
                    else:
                        acc_ref[mp, gs, cs] = acc_ref[mp, gs, cs] * al_ref[mp, gs, :] + pv[:, cs]

    def first_scores(qi):
        qf = q_ref[pl.ds(pl.multiple_of(qi * tq, tq), tq), :].astype(F32)
        qa_ref[0] = jnp.where(lane < A_HEAD_DIM, qf, 0.0).astype(BF16)
        qa_ref[1] = jnp.where(lane >= A_HEAD_DIM, qf, 0.0).astype(BF16)
        scores(0, sa_ref)

    def q_tile(qi, _):
        rows = pl.ds(pl.multiple_of(qi * tq, tq), tq)

        @pl.when(qi > 0)
        def _():
            softmax_pv(0, sa_ref, False, sb_ref, first=True)

        @pl.when(qi == 0)
        def _():
            softmax_pv(0, sa_ref, True, first=True)

        def full_tile(j, carry):
            @pl.when(j % 2 == 0)
            def _():
                softmax_pv(j, sa_ref, False, sb_ref)

            @pl.when(j % 2 == 1)
            def _():
                softmax_pv(j, sb_ref, False, sa_ref)

            return carry

        lax.fori_loop(1, qi, full_tile, 0)

        @pl.when((qi > 0) & (qi % 2 == 0))
        def _():
            softmax_pv(qi, sa_ref, True)

        @pl.when(qi % 2 == 1)
        def _():
            softmax_pv(qi, sb_ref, True)

        first_scores(jnp.minimum(qi + 1, nq - 1))
        o = (acc_ref[0, :, :A_V_DIM] / acc_ref[0, :, A_V_DIM:]
             - lam_ref[0] * (acc_ref[1, :, :A_V_DIM] / acc_ref[1, :, A_V_DIM:]))
        y = _attn_epilogue(o, lam_init, g_ref[...], z_ref[rows, :].astype(F32))
        o_ref[rows, :] = y.astype(BF16)
        return 0

    nq = s_len // tq
    first_scores(0)
    lax.fori_loop(0, nq, q_tile, 0)


def _attn_prompt(lam, q3, kt3, v3, rest3, gain, lam_init):
    b, s, w = q3.shape
    h = w // A_V_DIM
    tq = min(s, ATTN_TILE)
    assert s % tq == 0 and tq % (PV_SPLIT * LANES) == 0
    kern = functools.partial(_attn_prompt_kernel, tq=tq, lam_init=lam_init)
    qspec = pl.BlockSpec((None, s, A_V_DIM), lambda bi, hi: (bi, 0, hi))
    return pl.pallas_call(
        kern,
        grid=(b, h),
        in_specs=[
            pl.BlockSpec(memory_space=pltpu.SMEM),
            qspec,
            pl.BlockSpec((None, A_V_DIM, s), lambda bi, hi: (bi, hi, 0)),
            qspec, qspec,
            pl.BlockSpec((1, A_V_DIM), lambda bi, hi: (0, 0)),
        ],
        out_specs=qspec,
        out_shape=jax.ShapeDtypeStruct((b, s, w), BF16),
        scratch_shapes=[
            pltpu.VMEM((A_V_DIM, s), BF16),
            pltpu.VMEM((s, 2 * A_V_DIM), BF16),
            pltpu.VMEM((2, tq, A_V_DIM), BF16),
            pltpu.VMEM((2, tq, tq), F32),
            pltpu.VMEM((2, tq, tq), F32),
            pltpu.VMEM((2, tq, tq), BF16),
            pltpu.VMEM((2, tq, 2 * A_V_DIM), F32),
            pltpu.VMEM((2, tq, LANES), F32),
            pltpu.VMEM((2, tq, LANES), F32),
        ],
        compiler_params=_cparams(("parallel", "parallel")),
        name="attn_prompt",
    )(lam, q3, kt3, v3, rest3, gain)


def _attn_sample_kernel(lam_ref, q_ref, kc_ref, vc_ref, kn_ref, vn_ref, z_ref, g_ref, o_ref,
                        qbd_ref, s_ref, w_ref, acc_ref, *, tk, nkt, t_new, heads, lam_init):
    j = pl.program_id(1)
    past = tk * nkt
    nt = (((1,), (1,)), ((), ()))
    width = kn_ref.shape[1]

    @pl.when(j == 0)
    def _():
        lane = lax.broadcasted_iota(jnp.int32, (t_new, A_V_DIM), 1)
        zero = jnp.zeros((t_new, A_V_DIM), BF16)
        for hd in range(heads):
            qh = q_ref[:, hd * A_V_DIM:(hd + 1) * A_V_DIM].astype(F32)
            for mp in range(2):
                r0 = (2 * hd + mp) * t_new
                mine = lane >= A_HEAD_DIM if mp == 1 else lane < A_HEAD_DIM
                own = jnp.where(mine, qh, 0.0).astype(BF16)
                for hb in range(heads):
                    qbd_ref[r0:r0 + t_new, hb * A_V_DIM:(hb + 1) * A_V_DIM] = own if hb == hd else zero

    def pad_new(ref):
        new = ref[...]
        return jnp.concatenate([new, jnp.zeros((LANES - t_new, width), F32)], axis=0).astype(BF16)

    @pl.when(j < nkt)
    def _():
        s = jnp.dot(qbd_ref[...], kc_ref[...].astype(BF16), preferred_element_type=F32)
        s_ref[:, pl.ds(pl.multiple_of(j * tk, tk), tk)] = s

    @pl.when(j == nkt - 1)
    def _():
        sn = lax.dot_general(qbd_ref[...], pad_new(kn_ref), nt, preferred_element_type=F32)
        col = lax.broadcasted_iota(jnp.int32, sn.shape, 1)
        s_ref[:, past:past + LANES] = jnp.where(col < t_new, sn, -jnp.inf)
        lam = lam_ref[0]
        for hd in range(heads):
            r0 = hd * 2 * t_new
            p = []
            for mp in range(2):
                s = s_ref[r0 + mp * t_new:r0 + (mp + 1) * t_new, :]
                e = jnp.exp2(s - jnp.max(s, axis=1, keepdims=True))
                p.append(e / jnp.sum(e, axis=1, keepdims=True))
            w_ref[hd * t_new:(hd + 1) * t_new, :] = (p[0] - lam * p[1]).astype(BF16)

    @pl.when(j == nkt)
    def _():
        full = jnp.dot(w_ref[:, past:past + LANES], pad_new(vn_ref), preferred_element_type=F32)
        for hd in range(heads):
            rs = slice(hd * t_new, (hd + 1) * t_new)
            acc_ref[rs, :] = full[rs, hd * A_V_DIM:(hd + 1) * A_V_DIM]

    @pl.when(j >= nkt)
    def _():
        start = pl.multiple_of((j - nkt) * tk, tk)
        for hd in range(heads):
            rs = slice(hd * t_new, (hd + 1) * t_new)
            vh = vc_ref[pl.ds(hd, tk, stride=heads), :].astype(BF16)
            acc_ref[rs, :] += jnp.dot(w_ref[rs, pl.ds(start, tk)], vh, preferred_element_type=F32)

    @pl.when(j == 2 * nkt - 1)
    def _():
        for hd in range(heads):
            cs = slice(hd * A_V_DIM, (hd + 1) * A_V_DIM)
            o = acc_ref[hd * t_new:(hd + 1) * t_new, :]
            y = _attn_epilogue(o, lam_init, g_ref[...], z_ref[:, cs].astype(F32))
            o_ref[:, cs] = y.astype(BF16)


def _attn_sample(lam, q3, kct, vc3, kn, vn, rest3, gain, lam_init):
    b, w, past = kct.shape
    t_new = kn.shape[1]
    heads = w // A_V_DIM
    rows = 2 * heads * t_new
    tk = min(past, SAMPLE_KEYS)
    assert past % tk == 0 and t_new % 16 == 0 and t_new <= LANES
    nkt = past // tk
    kern = functools.partial(_attn_sample_kernel, tk=tk, nkt=nkt, t_new=t_new, heads=heads,
                             lam_init=lam_init)
    newspec = pl.BlockSpec((None, t_new, w), lambda bi, j: (bi, 0, 0))
    return pl.pallas_call(
        kern,
        grid=(b, 2 * nkt),
        in_specs=[
            pl.BlockSpec(memory_space=pltpu.SMEM),
            newspec,
            pl.BlockSpec((None, w, tk), lambda bi, j: (bi, 0, jnp.minimum(j, nkt - 1))),
            pl.BlockSpec((None, tk * heads, A_V_DIM),
                         lambda bi, j: (bi, jnp.maximum(j - nkt, 0), 0)),
            newspec, newspec, newspec,
            pl.BlockSpec((1, A_V_DIM), lambda bi, j: (0, 0)),
        ],
        out_specs=newspec,
        out_shape=jax.ShapeDtypeStruct((b, t_new, w), BF16),
        scratch_shapes=[
            pltpu.VMEM((rows, w), BF16),
            pltpu.VMEM((rows, past + LANES), F32),
            pltpu.VMEM((rows // 2, past + LANES), BF16),
            pltpu.VMEM((rows // 2, A_V_DIM), F32),
        ],
        compiler_params=_cparams(("parallel", "arbitrary")),
        name="attn_sample",
    )(lam, q3, kct, vc3, kn, vn, rest3, gain)


def _hgrn_kernel(*refs, blk, nchunk, hp, has_s0):
    if has_s0:
        q_ref, f_ref, i_ref, og_ref, z_ref, lb_ref, gain_ref, s0_ref, y_ref, sf_ref, st_ref = refs
    else:
        q_ref, f_ref, i_ref, og_ref, z_ref, lb_ref, gain_ref, y_ref, sf_ref, st_ref = refs
    t = pl.program_id(2)

    @pl.when(t == 0)
    def _():
        for h in range(hp):
            st_ref[h] = s0_ref[h].T if has_s0 else jnp.zeros(st_ref.shape[1:], F32)

    lb = lb_ref[...]
    gain = gain_ref[...]
    row = lax.broadcasted_iota(jnp.int32, (blk, blk), 0)
    col = lax.broadcasted_iota(jnp.int32, (blk, blk), 1)
    causal = col <= row
    tril = jnp.where(causal, 1.0, 0.0).astype(BF16)
    mid = (blk - 1) // 2
    nt = (((1,), (1,)), ((), ()))
    tn = (((0,), (0,)), ((), ()))

    def split3(g):
        hi = g.astype(BF16)
        r1 = g - hi.astype(F32)
        md = r1.astype(BF16)
        lo = (r1 - md.astype(F32)).astype(BF16)
        return hi, md, lo

    chunks = [slice(c * blk, (c + 1) * blk) for c in range(nchunk)]
    f = lb + (1.0 - lb) * jax.nn.sigmoid(f_ref[...].astype(F32))
    parts = split3(jnp.log(f))
    qv = _silu(q_ref[...].astype(F32))
    kv = 1.0 - f
    vv = i_ref[...]
    b = [sum(jnp.dot(tril, p[sl, :], preferred_element_type=F32) for p in parts) for sl in chunks]
    qe, ke, qs, ks, decay = [], [], [], [], []
    for sl, bc in zip(chunks, b):
        b_mid = bc[mid:mid + 1, :]
        b_last = bc[blk - 1:blk, :]
        decay.append(jnp.exp(b_last))
        qe.append((qv[sl, :] * jnp.exp(bc - b_mid)).astype(BF16))
        ke.append((kv[sl, :] * jnp.exp(b_mid - bc)).astype(BF16))
        qs.append((qv[sl, :] * jnp.exp(bc)).astype(BF16))
        ks.append((kv[sl, :] * jnp.exp(b_last - bc)).astype(BF16))
    lanes = [slice(h * B_KEY_DIM, (h + 1) * B_KEY_DIM) for h in range(hp)]
    o_intra, kvs = [], []
    for c, sl in enumerate(chunks):
        sc = [lax.dot_general(qe[c][:, hl], ke[c][:, hl], nt, preferred_element_type=F32)
              for hl in lanes]
        sc = [jnp.where(causal, s, 0.0).astype(BF16) for s in sc]
        o_intra.append([jnp.dot(s, vv[sl, hl], preferred_element_type=F32)
                        for s, hl in zip(sc, lanes)])
        kvs.append([lax.dot_general(vv[sl, hl], ks[c][:, hl], tn, preferred_element_type=F32)
                    for hl in lanes])
    cols = []
    for h, hl in enumerate(lanes):
        state = st_ref[h]
        outs = []
        for c in range(nchunk):
            outs.append(o_intra[c][h] + lax.dot_general(qs[c][:, hl], state.astype(BF16), nt,
                                                        preferred_element_type=F32))
            state = state * decay[c][:, hl] + kvs[c][h]
        st_ref[h] = state
        o = outs[0] if nchunk == 1 else jnp.concatenate(outs, axis=0)
        ms = jnp.mean(o * o, axis=1, keepdims=True)
        cols.append(o * lax.rsqrt(ms + NORM_EPS))
    o = cols[0] if hp == 1 else jnp.concatenate(cols, axis=1)
    y = o * gain * jax.nn.sigmoid(og_ref[...].astype(F32))
    y_ref[...] = (y * _silu(z_ref[...].astype(F32))).astype(BF16)

    @pl.when(t == pl.num_programs(2) - 1)
    def _():
        for h in range(hp):
            sf_ref[h] = st_ref[h].T


def _hgrn(rest3, lb, gain, s0, blk, col0):
    b, t, _ = rest3.shape
    heads = lb.shape[1] // B_KEY_DIM
    tt = min(t, HGRN_ROWS)
    assert t % tt == 0 and tt % blk == 0
    has_s0 = s0 is not None
    hp = heads if tt // blk == 1 else HGRN_HEADS_PER_STEP
    assert heads % hp == 0 and col0 % hp == 0
    width = hp * B_KEY_DIM
    kern = functools.partial(_hgrn_kernel, blk=blk, nchunk=tt // blk, hp=hp, has_s0=has_s0)

    def colspec(k):
        return pl.BlockSpec((None, tt, width),
                            lambda bi, hi, ti: (bi, ti, (col0 + k * heads) // hp + hi))

    vecspec = pl.BlockSpec((1, width), lambda bi, hi, ti: (0, hi))
    stspec = pl.BlockSpec((None, hp, B_KEY_DIM, B_KEY_DIM), lambda bi, hi, ti: (bi, hi, 0, 0))
    in_specs = [colspec(k) for k in range(5)] + [vecspec, vecspec]
    args = [rest3] * 5 + [lb, gain]
    if has_s0:
        in_specs.append(stspec)
        args.append(s0)
    return pl.pallas_call(
        kern,
        grid=(b, heads // hp, t // tt),
        in_specs=in_specs,
        out_specs=[
            pl.BlockSpec((None, tt, width), lambda bi, hi, ti: (bi, ti, hi)),
            stspec,
        ],
        out_shape=[
            jax.ShapeDtypeStruct((b, t, heads * B_KEY_DIM), BF16),
            jax.ShapeDtypeStruct((b, heads, B_KEY_DIM, B_KEY_DIM), F32),
        ],
        scratch_shapes=[pltpu.VMEM((hp, B_KEY_DIM, B_KEY_DIM), F32)],
        compiler_params=_cparams(("parallel", "parallel", "arbitrary")),
        name="hgrn",
    )(*args)


def _memattn_kernel(q_ref, z_ref, mk_ref, mv_ref, o_ref, kb_ref, vb_ref, *, heads):
    @pl.when(pl.program_id(1) == 0)
    def _():
        kb_ref[...] = mk_ref[...].astype(BF16)
        vb_ref[...] = mv_ref[...].astype(BF16)

    nt = (((1,), (1,)), ((), ()))
    cols = [slice(hd * C_HEAD_DIM, (hd + 1) * C_HEAD_DIM) for hd in range(heads)]
    s = [lax.dot_general(q_ref[:, cs], kb_ref[:, cs], nt, preferred_element_type=F32)
         * (C_HEAD_DIM ** -0.5) for cs in cols]
    e = [jnp.exp(sh - jnp.max(sh, axis=1, keepdims=True)) for sh in s]
    o = [jnp.dot(eh.astype(BF16), vb_ref[:, cs], preferred_element_type=F32)
         / jnp.sum(eh, axis=1, keepdims=True) for eh, cs in zip(e, cols)]
    for oh, cs in zip(o, cols):
        o_ref[:, cs] = (oh * _silu(z_ref[:, cs].astype(F32))).astype(BF16)


def _memattn(rest3, mk3, mv3, qblock, zblock):
    b, t, _ = rest3.shape
    _, m, c = mk3.shape
    tq = min(t, MEMATTN_ROWS)
    assert t % tq == 0
    kern = functools.partial(_memattn_kernel, heads=c // C_HEAD_DIM)
    memspec = pl.BlockSpec((None, m, c), lambda bi, ti: (bi, 0, 0))
    return pl.pallas_call(
        kern,
        grid=(b, t // tq),
        in_specs=[
            pl.BlockSpec((None, tq, c), lambda bi, ti: (bi, ti, qblock)),
            pl.BlockSpec((None, tq, c), lambda bi, ti: (bi, ti, zblock)),
            memspec, memspec,
        ],
        out_specs=pl.BlockSpec((None, tq, c), lambda bi, ti: (bi, ti, 0)),
        out_shape=jax.ShapeDtypeStruct((b, t, c), BF16),
        scratch_shapes=[pltpu.VMEM((m, c), BF16), pltpu.VMEM((m, c), BF16)],
        compiler_params=_cparams(("parallel", "arbitrary")),
        name="memattn",
    )(rest3, rest3, mk3, mv3)


def _merge_kernel(ya_ref, yb_ref, yc_ref, ga_ref, gb_ref, gc_ref, x_ref,
                  wa_ref, wb_ref, wc_ref, wo_ref, lng_ref, lnb_ref, o_ref, *, alpha):
    def branch(y_ref, w_ref, g_ref):
        return jax.nn.sigmoid(g_ref[...].astype(F32)) * jnp.dot(
            y_ref[...], w_ref[...], preferred_element_type=F32)

    merged = branch(ya_ref, wa_ref, ga_ref) + branch(yb_ref, wb_ref, gb_ref)
    merged = merged + branch(yc_ref, wc_ref, gc_ref)
    sub = jnp.dot(merged.astype(BF16), wo_ref[...], preferred_element_type=F32)
    hres = alpha * x_ref[...] + sub
    mu = jnp.mean(hres, axis=1, keepdims=True)
    cen = hres - mu
    var = jnp.mean(cen * cen, axis=1, keepdims=True)
    o_ref[...] = cen * lax.rsqrt(var + NORM_EPS) * lng_ref[...] + lnb_ref[...]


def _merge(ya, yb, yc, rest2, x2, wa, wb, wc, wo, lng, lnb, gate_block0, alpha):
    n, d = x2.shape
    w = ya.shape[1]
    tm = min(n, MERGE_ROWS)
    assert n % tm == 0
    rows = lambda width, blk: pl.BlockSpec((tm, width), lambda i: (i, blk))
    const = lambda shape: pl.BlockSpec(shape, lambda i: (0, 0), pipeline_mode=pl.Buffered(1))
    return pl.pallas_call(
        functools.partial(_merge_kernel, alpha=alpha),
        grid=(n // tm,),
        in_specs=[
            rows(w, 0), rows(w, 0), rows(w, 0),
            rows(d, gate_block0), rows(d, gate_block0 + 1), rows(d, gate_block0 + 2),
            rows(d, 0),
            const((w, d)), const((w, d)), const((w, d)), const((d, d)),
            const((1, d)), const((1, d)),
        ],
        out_specs=rows(d, 0),
        out_shape=jax.ShapeDtypeStruct((n, d), F32),
        compiler_params=_cparams(("parallel",)),
        name="merge",
    )(ya, yb, yc, rest2, rest2, rest2, x2, wa, wb, wc, wo, lng, lnb)


def _layer(x, pos, past_k, past_v, s0, mk3, mv3, rec_block, layer_idx, lb, p):
    (w_qkv, w_rest, lam, sub_norm, hgrn_gain, wa, wb, wc, wo, lng, lnb, alpha) = p
    n_b, t, d = x.shape
    a_width = wa.shape[0]
    heads = a_width // A_V_DIM
    x2 = x.reshape(n_b * t, d)
    prompt = past_k is None
    q_scale = A_HEAD_DIM ** -0.5 * math.log2(math.e)
    q2, k_out, v2, rest2 = _inproj(x2, w_qkv, w_rest, _rope_tables(pos), t, q_scale,
                                   k_transposed=prompt)
    r = rest2.shape[1]
    rest3 = rest2.reshape(n_b, t, r)
    q3 = q2.reshape(n_b, t, a_width)
    v3 = v2.reshape(n_b, t, a_width)
    lam_init = 0.8 - 0.6 * math.exp(-0.3 * layer_idx)
    if prompt:
        ya = _attn_prompt(lam, q3, k_out, v3, rest3, sub_norm, lam_init)
        k5 = k_out.reshape(n_b, heads, 2, A_HEAD_DIM, t).transpose(0, 4, 1, 2, 3)
    else:
        k3 = k_out.reshape(n_b, t, a_width)
        ya = _attn_sample(lam, q3, past_k, past_v, k3, v3, rest3,
                          sub_norm, lam_init)
        k5 = k3.reshape(n_b, t, heads, 2, A_HEAD_DIM)
    yb, s_fin = _hgrn(rest3, lb, hgrn_gain, s0, rec_block, a_width // B_KEY_DIM)
    c_width = mk3.shape[2]
    c_off = a_width + 5 * lb.shape[1]
    assert c_off % c_width == 0
    yc = _memattn(rest3, mk3, mv3, c_off // c_width, c_off // c_width + 1)
    g_off = c_off + 2 * c_width
    assert g_off % d == 0
    y2 = _merge(ya.reshape(n_b * t, a_width), yb.reshape(n_b * t, -1), yc.reshape(n_b * t, c_width),
                rest2, x2, wa, wb, wc, wo, lng, lnb, g_off // d, alpha)
    return y2.reshape(n_b, t, d), k5, v3, s_fin


def kernel(x_prompt, x_sample, cache_attn_k, cache_attn_v, state_hgrn, cache_mem_k, cache_mem_v, mem_prompt, w_in, lambda_q1, lambda_k1, lambda_q2, lambda_k2, attn_sub_norm, hgrn_lb_logits, hgrn_norm, w_mem_k, w_mem_v, w_branch_a, w_branch_b, w_branch_c, w_out, ln_gamma, ln_beta):
    bp, seq, d = x_prompt.shape
    bs, t_new, _ = x_sample.shape
    depth = w_in.shape[0]
    past = cache_attn_k.shape[2]
    heads = cache_attn_k.shape[3]
    a_width = heads * A_V_DIM
    n_mem = mem_prompt.shape[1]
    c_heads = cache_mem_k.shape[3]
    c_width = c_heads * C_HEAD_DIM
    b_heads = state_hgrn.shape[2]
    alpha = (2 * depth) ** 0.25
    pos_prompt = jnp.arange(seq)
    pos_sample = past + jnp.arange(t_new)
    lower_bounds = jnp.cumsum(jax.nn.softmax(hgrn_lb_logits.astype(F32), axis=0), axis=0)

    h_p, h_s = x_prompt, x_sample
    outs = [[] for _ in range(8)]
    for l in range(depth):
        lam_init = 0.8 - 0.6 * math.exp(-0.3 * l)
        lam = (jnp.exp(jnp.sum(lambda_q1[l].astype(F32) * lambda_k1[l].astype(F32)))
               - jnp.exp(jnp.sum(lambda_q2[l].astype(F32) * lambda_k2[l].astype(F32))) + lam_init)
        params = (w_in[l][:, :3 * a_width].astype(BF16), w_in[l][:, 3 * a_width:].astype(BF16),
                  lam.reshape(1), attn_sub_norm[l].reshape(1, -1),
                  hgrn_norm[l].reshape(1, -1), w_branch_a[l].astype(BF16), w_branch_b[l].astype(BF16),
                  w_branch_c[l].astype(BF16), w_out[l].astype(BF16), ln_gamma[l].reshape(1, -1),
                  ln_beta[l].reshape(1, -1), alpha)
        lb = lower_bounds[l].reshape(1, -1)
        mk_p, mv_p = _memkv(mem_prompt.reshape(bp * n_mem, d), w_mem_k[l].astype(BF16),
                            w_mem_v[l].astype(BF16))
        mk_p = mk_p.reshape(bp, n_mem, c_width)
        mv_p = mv_p.reshape(bp, n_mem, c_width)
        h_p, k_p, v_p, s_p = _layer(h_p, pos_prompt, None, None, None, mk_p, mv_p, CHUNK, l, lb, params)
        h_s, k_s, v_s, s_s = _layer(
            h_s, pos_sample,
            cache_attn_k[l].transpose(0, 2, 3, 4, 1).reshape(bs, a_width, past),
            cache_attn_v[l].reshape(bs, past * heads, A_V_DIM),
            state_hgrn[l], cache_mem_k[l].reshape(bs, n_mem, c_width),
            cache_mem_v[l].reshape(bs, n_mem, c_width), t_new, l, lb, params)
        new = (k_p, v_p.reshape(bp, seq, heads, A_V_DIM),
               s_p.astype(x_prompt.dtype), mk_p.reshape(bp, n_mem, c_heads, C_HEAD_DIM),
               mv_p.reshape(bp, n_mem, c_heads, C_HEAD_DIM),
               k_s, v_s.reshape(bs, t_new, heads, A_V_DIM),
               s_s.astype(x_sample.dtype))
        for acc, val in zip(outs, new):
            acc.append(val)
    return (h_p, h_s) + tuple(jnp.stack(o) for o in outs)
```

```python
import functools
import math

import jax
import jax.numpy as jnp
from jax import lax
from jax.experimental import pallas as pl
from jax.experimental.pallas import tpu as pltpu

F32 = jnp.float32
BF16 = jnp.bfloat16

CHUNK = 64
A_HEAD_DIM = 64
A_V_DIM = 128
ROT_DIM = 16
ROPE_THETA = 500000.0
B_KEY_DIM = 128
C_HEAD_DIM = 256
NORM_EPS = 1e-5
LANES = 128
VMEM_LIMIT = 56 * 1024 * 1024
INPROJ_ROWS = 1024
INPROJ_COLS = 512
REST_COLS = 2048
MEMKV_ROWS = 512
HGRN_ROWS = 512
MEMATTN_ROWS = 1024
MERGE_ROWS = 256
SAMPLE_KEYS = 2048
ATTN_TILE = 512
ATTN_STRIP = 32
PV_SPLIT = 4
ROW_CHUNKS = 4
NEXT_SCORES_AT = 3
HGRN_HEADS_PER_STEP = 8


def _cparams(sem):
    return pltpu.CompilerParams(dimension_semantics=sem, vmem_limit_bytes=VMEM_LIMIT)


def _silu(z):
    return z * jax.nn.sigmoid(z)


def _rope_tables(pos):
    half = ROT_DIM // 2
    inv_freq = jnp.power(ROPE_THETA, -jnp.arange(0, ROT_DIM, 2, dtype=F32) / ROT_DIM)
    ang = pos.astype(F32)[:, None] * inv_freq[None, :]
    cos, sin = jnp.cos(ang), jnp.sin(ang)
    t = pos.shape[0]
    pad = jnp.zeros((t, A_HEAD_DIM - ROT_DIM), F32)
    zero = jnp.zeros((t, half), F32)
    c64 = jnp.concatenate([cos, cos, pad + 1.0], axis=1)
    sa64 = jnp.concatenate([-sin, zero, pad], axis=1)
    sb64 = jnp.concatenate([zero, sin, pad], axis=1)
    rep = LANES // A_HEAD_DIM
    rows = tuple(jnp.tile(a, (1, rep)) for a in (c64, sa64, sb64))
    return rows, (cos.T, sin.T)


def _inproj_kernel(x_ref, w_ref, c_ref, sa_ref, sb_ref, ct_ref, st_ref,
                   q_ref, k_ref, v_ref, xb_ref, t_ref, *, tn, nq, nk, q_scale, k_transposed):
    j = pl.program_id(1)
    half = ROT_DIM // 2

    @pl.when(j == 0)
    def _():
        xb_ref[...] = x_ref[...].astype(BF16)

    tm = xb_ref.shape[0]
    chunks = [slice(r, r + tm // ROW_CHUNKS) for r in range(0, tm, tm // ROW_CHUNKS)]

    def project(rows=slice(None)):
        return jnp.dot(xb_ref[rows, :], w_ref[...], preferred_element_type=F32)

    def rope_group(acc, g, rows=slice(None)):
        xg = acc[:, g * LANES:(g + 1) * LANES]
        up = pltpu.roll(xg, LANES - half, 1)
        dn = pltpu.roll(xg, half, 1)
        return xg * c_ref[rows, :] + up * sa_ref[rows, :] + dn * sb_ref[rows, :]

    @pl.when(j < nq)
    def _():
        for rows in chunks:
            acc = project(rows)
            for g in range(tn // LANES):
                q_ref[rows, g * LANES:(g + 1) * LANES] = (
                    rope_group(acc, g, rows) * q_scale).astype(BF16)

    @pl.when((j >= nq) & (j < nq + nk))
    def _():
        if k_transposed:
            for rows in chunks:
                t_ref[rows, :] = project(rows)
            kt = t_ref[...].T
            cos, sin = ct_ref[...], st_ref[...]
            for g in range(tn // A_HEAD_DIM):
                r0 = g * A_HEAD_DIM
                lo = kt[r0:r0 + half, :]
                hi = kt[r0 + half:r0 + ROT_DIM, :]
                k_ref[r0:r0 + half, :] = lo * cos - hi * sin
                k_ref[r0 + half:r0 + ROT_DIM, :] = hi * cos + lo * sin
                k_ref[r0 + ROT_DIM:r0 + A_HEAD_DIM, :] = kt[r0 + ROT_DIM:r0 + A_HEAD_DIM, :]
        else:
            acc = project()
            for g in range(tn // LANES):
                k_ref[:, g * LANES:(g + 1) * LANES] = rope_group(acc, g)

    @pl.when(j >= nq + nk)
    def _():
        for rows in chunks:
            v_ref[rows, :] = project(rows)


def _proj_rest_kernel(x_ref, w_ref, r_ref, xb_ref):
    @pl.when(pl.program_id(1) == 0)
    def _():
        xb_ref[...] = x_ref[...].astype(BF16)

    tm = xb_ref.shape[0]
    for r in range(0, tm, tm // ROW_CHUNKS):
        rows = slice(r, r + tm // ROW_CHUNKS)
        r_ref[rows, :] = jnp.dot(xb_ref[rows, :], w_ref[...], preferred_element_type=F32).astype(BF16)


def _proj_rest(x2, w_bf, col0, tm):
    n, d = x2.shape
    ncols = w_bf.shape[1] - col0
    tn = REST_COLS
    assert ncols % tn == 0 and col0 % LANES == 0 and n % tm == 0
    return pl.pallas_call(
        _proj_rest_kernel,
        grid=(n // tm, ncols // tn),
        in_specs=[
            pl.BlockSpec((tm, d), lambda i, j: (i, 0)),
            pl.BlockSpec((pl.Element(d), pl.Element(tn)),
                         lambda i, j: (0, pl.multiple_of(col0 + j * tn, LANES))),
        ],
        out_specs=pl.BlockSpec((tm, tn), lambda i, j: (i, j)),
        out_shape=jax.ShapeDtypeStruct((n, ncols), BF16),
        scratch_shapes=[pltpu.VMEM((tm, d), BF16)],
        compiler_params=_cparams(("parallel", "arbitrary")),
        name="proj_rest",
    )(x2, w_bf)


def _inproj(x2, w_bf, tables, t_len, a_width, q_scale, k_transposed):
    n, d = x2.shape
    n_in = 3 * a_width
    tm = min(n, INPROJ_ROWS)
    tn = INPROJ_COLS if tm == INPROJ_ROWS else 2 * INPROJ_COLS
    assert n % tm == 0 and a_width % tn == 0
    row_tabs, col_tabs = tables
    if t_len >= tm:
        assert t_len % tm == 0
        per = t_len // tm
    else:
        assert tm % t_len == 0 and not k_transposed
        row_tabs = tuple(jnp.tile(a, (tm // t_len, 1)) for a in row_tabs)
        col_tabs = tuple(jnp.tile(a, (1, tm // t_len)) for a in col_tabs)
        per = 1
    nq = nk = a_width // tn
    kblk = lambda j: jnp.clip(j - nq, 0, nk - 1)
    row_spec = pl.BlockSpec((tm, LANES), lambda i, j: (i % per, 0))
    col_spec = pl.BlockSpec((ROT_DIM // 2, tm), lambda i, j: (0, i % per))
    if k_transposed:
        k_spec = pl.BlockSpec((None, tn, tm), lambda i, j: (i // per, kblk(j), i % per))
        k_shape = jax.ShapeDtypeStruct((n // t_len, a_width, t_len), F32)
    else:
        k_spec = pl.BlockSpec((tm, tn), lambda i, j: (i, kblk(j)))
        k_shape = jax.ShapeDtypeStruct((n, a_width), F32)
    kern = functools.partial(_inproj_kernel, tn=tn, nq=nq, nk=nk, q_scale=q_scale,
                             k_transposed=k_transposed)
    q2, k_out, v2 = pl.pallas_call(
        kern,
        grid=(n // tm, n_in // tn),
        in_specs=[
            pl.BlockSpec((tm, d), lambda i, j: (i, 0)),
            pl.BlockSpec((d, tn), lambda i, j: (0, j)),
            row_spec, row_spec, row_spec, col_spec, col_spec,
        ],
        out_specs=[
            pl.BlockSpec((tm, tn), lambda i, j: (i, jnp.minimum(j, nq - 1))),
            k_spec,
            pl.BlockSpec((tm, tn), lambda i, j: (i, jnp.maximum(j - nq - nk, 0))),
        ],
        out_shape=[
            jax.ShapeDtypeStruct((n, a_width), BF16),
            k_shape,
            jax.ShapeDtypeStruct((n, a_width), F32),
        ],
        scratch_shapes=[pltpu.VMEM((tm, d), BF16), pltpu.VMEM((tm, tn), F32)],
        compiler_params=_cparams(("parallel", "arbitrary")),
        name="inproj",
    )(x2, w_bf, *row_tabs, *col_tabs)
    return q2, k_out, v2, _proj_rest(x2, w_bf, n_in, tm)


def _memkv_kernel(x_ref, wk_ref, wv_ref, k_ref, v_ref):
    xb = x_ref[...].astype(BF16)
    k_ref[...] = jnp.dot(xb, wk_ref[...], preferred_element_type=F32)
    v_ref[...] = jnp.dot(xb, wv_ref[...], preferred_element_type=F32)


def _memkv(x2, wk_bf, wv_bf):
    n, d = x2.shape
    c = wk_bf.shape[1]
    tm = min(n, MEMKV_ROWS)
    assert n % tm == 0
    wspec = pl.BlockSpec((d, c), lambda i: (0, 0))
    ospec = pl.BlockSpec((tm, c), lambda i: (i, 0))
    return pl.pallas_call(
        _memkv_kernel,
        grid=(n // tm,),
        in_specs=[pl.BlockSpec((tm, d), lambda i: (i, 0)), wspec, wspec],
        out_specs=[ospec, ospec],
        out_shape=[jax.ShapeDtypeStruct((n, c), F32)] * 2,
        compiler_params=_cparams(("parallel",)),
        name="memkv",
    )(x2, wk_bf, wv_bf)


def _attn_epilogue(o, lam_init, gain, z):
    ms = jnp.mean(o * o, axis=1, keepdims=True)
    y = o * lax.rsqrt(ms + NORM_EPS) * gain * (1.0 - lam_init)
    return y * _silu(z)


def _attn_prompt_kernel(lam_ref, q_ref, kt_ref, v_ref, z_ref, g_ref, o_ref,
                        kb_ref, vb_ref, qa_ref, sa_ref, sb_ref, p_ref, acc_ref, m_ref, al_ref,
                        *, tq, lam_init):
    s_len = q_ref.shape[0]
    strip = ATTN_STRIP
    kb_ref[...] = kt_ref[...].astype(BF16)
    vb_ref[:, :A_V_DIM] = v_ref[...].astype(BF16)
    vb_ref[:, A_V_DIM:] = jnp.ones((s_len, A_V_DIM), BF16)
    lane = lax.broadcasted_iota(jnp.int32, (tq, A_V_DIM), 1)

    def scores(j, s_ref, maps=(0, 1)):
        kt = kb_ref[:, pl.ds(pl.multiple_of(j * tq, tq), tq)]
        for mp in maps:
            s_ref[mp] = jnp.dot(qa_ref[mp], kt, preferred_element_type=F32)

    def softmax_pv(j, s_ref, masked, nxt_ref=None, first=False):
        start = pl.multiple_of(j * tq, tq)
        grp = tq // PV_SPLIT
        if nxt_ref is not None:
            scores(j + 1, nxt_ref, (0,))
        for mp in range(2):
            for g0 in range(0, tq, grp):
                if nxt_ref is not None and mp * tq + g0 == NEXT_SCORES_AT * grp:
                    scores(j + 1, nxt_ref, (1,))
                nk = g0 + grp if masked else tq
                for r in range(g0, g0 + grp, strip):
                    rs = slice(r, r + strip)
                    sv = s_ref[mp, rs, :nk]
                    if masked:
                        cc = lax.broadcasted_iota(jnp.int32, (strip, nk), 1) // CHUNK
                        sv = jnp.where(cc <= r // CHUNK, sv, -jnp.inf)
                    mx = jnp.max(sv, axis=1, keepdims=True)
                    if first:
                        mn = jnp.broadcast_to(mx, (strip, LANES))
                    else:
                        m_old = m_ref[mp, rs, :]
                        mn = jnp.maximum(m_old, mx)
                        al_ref[mp, rs, :] = jnp.exp2(m_old - mn)
                    m_ref[mp, rs, :] = mn
                    for c in range(nk // LANES):
                        cs = slice(c * LANES, (c + 1) * LANES)
                        p_ref[mp, rs, cs] = jnp.exp2(sv[:, cs] - mn).astype(BF16)
                gs = slice(g0, g0 + grp)
                pv = jnp.dot(p_ref[mp, gs, :nk], vb_ref[pl.ds(start, nk), :],
                             preferred_element_type=F32)
                for c in range(2):
                    cs = slice(c * A_V_DIM, (c + 1) * A_V_DIM)
                    if first:
                        acc_ref[mp, gs, cs] = pv[:, cs]
                    else:
                        acc_ref[mp, gs, cs] = acc_ref[mp, gs, cs] * al_ref[mp, gs, :] + pv[:, cs]

    def first_scores(qi):
        qf = q_ref[pl.ds(pl.multiple_of(qi * tq, tq), tq), :].astype(F32)
        qa_ref[0] = jnp.where(lane < A_HEAD_DIM, qf, 0.0).astype(BF16)
        qa_ref[1] = jnp.where(lane >= A_HEAD_DIM, qf, 0.0).astype(BF16)
        scores(0, sa_ref)

    def q_tile(qi, _):
        rows = pl.ds(pl.multiple_of(qi * tq, tq), tq)

        @pl.when(qi > 0)
        def _():
            softmax_pv(0, sa_ref, False, sb_ref, first=True)

        @pl.when(qi == 0)
        def _():
            softmax_pv(0, sa_ref, True, first=True)

        def full_tile(j, carry):
            @pl.when(j % 2 == 0)
            def _():
                softmax_pv(j, sa_ref, False, sb_ref)

            @pl.when(j % 2 == 1)
            def _():
                softmax_pv(j, sb_ref, False, sa_ref)

            return carry

        lax.fori_loop(1, qi, full_tile, 0)

        @pl.when((qi > 0) & (qi % 2 == 0))
        def _():
            softmax_pv(qi, sa_ref, True)

        @pl.when(qi % 2 == 1)
        def _():
            softmax_pv(qi, sb_ref, True)

        first_scores(jnp.minimum(qi + 1, nq - 1))
        o = (acc_ref[0, :, :A_V_DIM] / acc_ref[0, :, A_V_DIM:]
             - lam_ref[0] * (acc_ref[1, :, :A_V_DIM] / acc_ref[1, :, A_V_DIM:]))
        y = _attn_epilogue(o, lam_init, g_ref[...], z_ref[rows, :].astype(F32))
        o_ref[rows, :] = y.astype(BF16)
        return 0

    nq = s_len // tq
    first_scores(0)
    lax.fori_loop(0, nq, q_tile, 0)


def _attn_prompt(lam, q3, kt3, v3, rest3, gain, lam_init):
    b, s, w = q3.shape
    h = w // A_V_DIM
    tq = min(s, ATTN_TILE)
    assert s % tq == 0 and tq % (PV_SPLIT * LANES) == 0
    kern = functools.partial(_attn_prompt_kernel, tq=tq, lam_init=lam_init)
    qspec = pl.BlockSpec((None, s, A_V_DIM), lambda bi, hi: (bi, 0, hi))
    return pl.pallas_call(
        kern,
        grid=(b, h),
        in_specs=[
            pl.BlockSpec(memory_space=pltpu.SMEM),
            qspec,
            pl.BlockSpec((None, A_V_DIM, s), lambda bi, hi: (bi, hi, 0)),
            qspec, qspec,
            pl.BlockSpec((1, A_V_DIM), lambda bi, hi: (0, 0)),
        ],
        out_specs=qspec,
        out_shape=jax.ShapeDtypeStruct((b, s, w), BF16),
        scratch_shapes=[
            pltpu.VMEM((A_V_DIM, s), BF16),
            pltpu.VMEM((s, 2 * A_V_DIM), BF16),
            pltpu.VMEM((2, tq, A_V_DIM), BF16),
            pltpu.VMEM((2, tq, tq), F32),
            pltpu.VMEM((2, tq, tq), F32),
            pltpu.VMEM((2, tq, tq), BF16),
            pltpu.VMEM((2, tq, 2 * A_V_DIM), F32),
            pltpu.VMEM((2, tq, LANES), F32),
            pltpu.VMEM((2, tq, LANES), F32),
        ],
        compiler_params=_cparams(("parallel", "parallel")),
        name="attn_prompt",
    )(lam, q3, kt3, v3, rest3, gain)


def _attn_sample_kernel(lam_ref, q_ref, kc_ref, vc_ref, kn_ref, vn_ref, z_ref, g_ref, o_ref,
                        qbd_ref, s_ref, w_ref, acc_ref, *, tk, nkt, t_new, heads, lam_init):
    j = pl.program_id(1)
    past = tk * nkt
    nt = (((1,), (1,)), ((), ()))
    width = kn_ref.shape[1]

    @pl.when(j == 0)
    def _():
        lane = lax.broadcasted_iota(jnp.int32, (t_new, A_V_DIM), 1)
        zero = jnp.zeros((t_new, A_V_DIM), BF16)
        for hd in range(heads):
            qh = q_ref[:, hd * A_V_DIM:(hd + 1) * A_V_DIM].astype(F32)
            for mp in range(2):
                r0 = (2 * hd + mp) * t_new
                mine = lane >= A_HEAD_DIM if mp == 1 else lane < A_HEAD_DIM
                own = jnp.where(mine, qh, 0.0).astype(BF16)
                for hb in range(heads):
                    qbd_ref[r0:r0 + t_new, hb * A_V_DIM:(hb + 1) * A_V_DIM] = own if hb == hd else zero

    def pad_new(ref):
        new = ref[...]
        return jnp.concatenate([new, jnp.zeros((LANES - t_new, width), F32)], axis=0).astype(BF16)

    @pl.when(j < nkt)
    def _():
        s = jnp.dot(qbd_ref[...], kc_ref[...].astype(BF16), preferred_element_type=F32)
        s_ref[:, pl.ds(pl.multiple_of(j * tk, tk), tk)] = s

    @pl.when(j == nkt - 1)
    def _():
        sn = lax.dot_general(qbd_ref[...], pad_new(kn_ref), nt, preferred_element_type=F32)
        col = lax.broadcasted_iota(jnp.int32, sn.shape, 1)
        s_ref[:, past:past + LANES] = jnp.where(col < t_new, sn, -jnp.inf)
        lam = lam_ref[0]
        for hd in range(heads):
            r0 = hd * 2 * t_new
            p = []
            for mp in range(2):
                s = s_ref[r0 + mp * t_new:r0 + (mp + 1) * t_new, :]
                e = jnp.exp2(s - jnp.max(s, axis=1, keepdims=True))
                p.append(e / jnp.sum(e, axis=1, keepdims=True))
            w_ref[hd * t_new:(hd + 1) * t_new, :] = (p[0] - lam * p[1]).astype(BF16)

    @pl.when(j == nkt)
    def _():
        full = jnp.dot(w_ref[:, past:past + LANES], pad_new(vn_ref), preferred_element_type=F32)
        for hd in range(heads):
            rs = slice(hd * t_new, (hd + 1) * t_new)
            acc_ref[rs, :] = full[rs, hd * A_V_DIM:(hd + 1) * A_V_DIM]

    @pl.when(j >= nkt)
    def _():
        start = pl.multiple_of((j - nkt) * tk, tk)
        for hd in range(heads):
            rs = slice(hd * t_new, (hd + 1) * t_new)
            vh = vc_ref[pl.ds(hd, tk, stride=heads), :].astype(BF16)
            acc_ref[rs, :] += jnp.dot(w_ref[rs, pl.ds(start, tk)], vh, preferred_element_type=F32)

    @pl.when(j == 2 * nkt - 1)
    def _():
        for hd in range(heads):
            cs = slice(hd * A_V_DIM, (hd + 1) * A_V_DIM)
            o = acc_ref[hd * t_new:(hd + 1) * t_new, :]
            y = _attn_epilogue(o, lam_init, g_ref[...], z_ref[:, cs].astype(F32))
            o_ref[:, cs] = y.astype(BF16)


def _attn_sample(lam, q3, kct, vc3, kn, vn, rest3, gain, lam_init):
    b, w, past = kct.shape
    t_new = kn.shape[1]
    heads = w // A_V_DIM
    rows = 2 * heads * t_new
    tk = min(past, SAMPLE_KEYS)
    assert past % tk == 0 and t_new % 16 == 0 and t_new <= LANES
    nkt = past // tk
    kern = functools.partial(_attn_sample_kernel, tk=tk, nkt=nkt, t_new=t_new, heads=heads,
                             lam_init=lam_init)
    newspec = pl.BlockSpec((None, t_new, w), lambda bi, j: (bi, 0, 0))
    return pl.pallas_call(
        kern,
        grid=(b, 2 * nkt),
        in_specs=[
            pl.BlockSpec(memory_space=pltpu.SMEM),
            newspec,
            pl.BlockSpec((None, w, tk), lambda bi, j: (bi, 0, jnp.minimum(j, nkt - 1))),
            pl.BlockSpec((None, tk * heads, A_V_DIM),
                         lambda bi, j: (bi, jnp.maximum(j - nkt, 0), 0)),
            newspec, newspec, newspec,
            pl.BlockSpec((1, A_V_DIM), lambda bi, j: (0, 0)),
        ],
        out_specs=newspec,
        out_shape=jax.ShapeDtypeStruct((b, t_new, w), BF16),
        scratch_shapes=[
            pltpu.VMEM((rows, w), BF16),
            pltpu.VMEM((rows, past + LANES), F32),
            pltpu.VMEM((rows // 2, past + LANES), BF16),
            pltpu.VMEM((rows // 2, A_V_DIM), F32),
        ],
        compiler_params=_cparams(("parallel", "arbitrary")),
        name="attn_sample",
    )(lam, q3, kct, vc3, kn, vn, rest3, gain)


def _hgrn_kernel(*refs, blk, nchunk, hp, has_s0):
    if has_s0:
        q_ref, f_ref, i_ref, og_ref, z_ref, lb_ref, gain_ref, s0_ref, y_ref, sf_ref, st_ref = refs
    else:
        q_ref, f_ref, i_ref, og_ref, z_ref, lb_ref, gain_ref, y_ref, sf_ref, st_ref = refs
    t = pl.program_id(2)

    @pl.when(t == 0)
    def _():
        for h in range(hp):
            st_ref[h] = s0_ref[h].T if has_s0 else jnp.zeros(st_ref.shape[1:], F32)

    lb = lb_ref[...]
    gain = gain_ref[...]
    row = lax.broadcasted_iota(jnp.int32, (blk, blk), 0)
    col = lax.broadcasted_iota(jnp.int32, (blk, blk), 1)
    causal = col <= row
    tril = jnp.where(causal, 1.0, 0.0).astype(BF16)
    mid = (blk - 1) // 2
    nt = (((1,), (1,)), ((), ()))
    tn = (((0,), (0,)), ((), ()))

    def split3(g):
        hi = g.astype(BF16)
        r1 = g - hi.astype(F32)
        md = r1.astype(BF16)
        lo = (r1 - md.astype(F32)).astype(BF16)
        return hi, md, lo

    chunks = [slice(c * blk, (c + 1) * blk) for c in range(nchunk)]
    f = lb + (1.0 - lb) * jax.nn.sigmoid(f_ref[...].astype(F32))
    parts = split3(jnp.log(f))
    qv = _silu(q_ref[...].astype(F32))
    kv = 1.0 - f
    vv = i_ref[...]
    b = [sum(jnp.dot(tril, p[sl, :], preferred_element_type=F32) for p in parts) for sl in chunks]
    qe, ke, qs, ks, decay = [], [], [], [], []
    for sl, bc in zip(chunks, b):
        b_mid = bc[mid:mid + 1, :]
        b_last = bc[blk - 1:blk, :]
        decay.append(jnp.exp(b_last))
        qe.append((qv[sl, :] * jnp.exp(bc - b_mid)).astype(BF16))
        ke.append((kv[sl, :] * jnp.exp(b_mid - bc)).astype(BF16))
        qs.append((qv[sl, :] * jnp.exp(bc)).astype(BF16))
        ks.append((kv[sl, :] * jnp.exp(b_last - bc)).astype(BF16))
    lanes = [slice(h * B_KEY_DIM, (h + 1) * B_KEY_DIM) for h in range(hp)]
    o_intra, kvs = [], []
    for c, sl in enumerate(chunks):
        sc = [lax.dot_general(qe[c][:, hl], ke[c][:, hl], nt, preferred_element_type=F32)
              for hl in lanes]
        sc = [jnp.where(causal, s, 0.0).astype(BF16) for s in sc]
        o_intra.append([jnp.dot(s, vv[sl, hl], preferred_element_type=F32)
                        for s, hl in zip(sc, lanes)])
        kvs.append([lax.dot_general(vv[sl, hl], ks[c][:, hl], tn, preferred_element_type=F32)
                    for hl in lanes])
    cols = []
    for h, hl in enumerate(lanes):
        state = st_ref[h]
        outs = []
        for c in range(nchunk):
            outs.append(o_intra[c][h] + lax.dot_general(qs[c][:, hl], state.astype(BF16), nt,
                                                        preferred_element_type=F32))
            state = state * decay[c][:, hl] + kvs[c][h]
        st_ref[h] = state
        o = outs[0] if nchunk == 1 else jnp.concatenate(outs, axis=0)
        ms = jnp.mean(o * o, axis=1, keepdims=True)
        cols.append(o * lax.rsqrt(ms + NORM_EPS))
    o = cols[0] if hp == 1 else jnp.concatenate(cols, axis=1)
    y = o * gain * jax.nn.sigmoid(og_ref[...].astype(F32))
    y_ref[...] = (y * _silu(z_ref[...].astype(F32))).astype(BF16)

    @pl.when(t == pl.num_programs(2) - 1)
    def _():
        for h in range(hp):
            sf_ref[h] = st_ref[h].T


def _hgrn(rest3, lb, gain, s0, blk, col0):
    b, t, _ = rest3.shape
    heads = lb.shape[1] // B_KEY_DIM
    tt = min(t, HGRN_ROWS)
    assert t % tt == 0 and tt % blk == 0
    has_s0 = s0 is not None
    hp = heads if tt // blk == 1 else HGRN_HEADS_PER_STEP
    assert heads % hp == 0 and col0 % hp == 0
    width = hp * B_KEY_DIM
    kern = functools.partial(_hgrn_kernel, blk=blk, nchunk=tt // blk, hp=hp, has_s0=has_s0)

    def colspec(k):
        return pl.BlockSpec((None, tt, width),
                            lambda bi, hi, ti: (bi, ti, (col0 + k * heads) // hp + hi))

    vecspec = pl.BlockSpec((1, width), lambda bi, hi, ti: (0, hi))
    stspec = pl.BlockSpec((None, hp, B_KEY_DIM, B_KEY_DIM), lambda bi, hi, ti: (bi, hi, 0, 0))
    in_specs = [colspec(k) for k in range(5)] + [vecspec, vecspec]
    args = [rest3] * 5 + [lb, gain]
    if has_s0:
        in_specs.append(stspec)
        args.append(s0)
    return pl.pallas_call(
        kern,
        grid=(b, heads // hp, t // tt),
        in_specs=in_specs,
        out_specs=[
            pl.BlockSpec((None, tt, width), lambda bi, hi, ti: (bi, ti, hi)),
            stspec,
        ],
        out_shape=[
            jax.ShapeDtypeStruct((b, t, heads * B_KEY_DIM), BF16),
            jax.ShapeDtypeStruct((b, heads, B_KEY_DIM, B_KEY_DIM), F32),
        ],
        scratch_shapes=[pltpu.VMEM((hp, B_KEY_DIM, B_KEY_DIM), F32)],
        compiler_params=_cparams(("parallel", "parallel", "arbitrary")),
        name="hgrn",
    )(*args)


def _memattn_kernel(q_ref, z_ref, mk_ref, mv_ref, o_ref, kb_ref, vb_ref, *, heads):
    @pl.when(pl.program_id(1) == 0)
    def _():
        kb_ref[...] = mk_ref[...].astype(BF16)
        vb_ref[...] = mv_ref[...].astype(BF16)

    nt = (((1,), (1,)), ((), ()))
    cols = [slice(hd * C_HEAD_DIM, (hd + 1) * C_HEAD_DIM) for hd in range(heads)]
    s = [lax.dot_general(q_ref[:, cs], kb_ref[:, cs], nt, preferred_element_type=F32)
         * (C_HEAD_DIM ** -0.5) for cs in cols]
    e = [jnp.exp(sh - jnp.max(sh, axis=1, keepdims=True)) for sh in s]
    o = [jnp.dot(eh.astype(BF16), vb_ref[:, cs], preferred_element_type=F32)
         / jnp.sum(eh, axis=1, keepdims=True) for eh, cs in zip(e, cols)]
    for oh, cs in zip(o, cols):
        o_ref[:, cs] = (oh * _silu(z_ref[:, cs].astype(F32))).astype(BF16)


def _memattn(rest3, mk3, mv3, qblock, zblock):
    b, t, _ = rest3.shape
    _, m, c = mk3.shape
    tq = min(t, MEMATTN_ROWS)
    assert t % tq == 0
    kern = functools.partial(_memattn_kernel, heads=c // C_HEAD_DIM)
    memspec = pl.BlockSpec((None, m, c), lambda bi, ti: (bi, 0, 0))
    return pl.pallas_call(
        kern,
        grid=(b, t // tq),
        in_specs=[
            pl.BlockSpec((None, tq, c), lambda bi, ti: (bi, ti, qblock)),
            pl.BlockSpec((None, tq, c), lambda bi, ti: (bi, ti, zblock)),
            memspec, memspec,
        ],
        out_specs=pl.BlockSpec((None, tq, c), lambda bi, ti: (bi, ti, 0)),
        out_shape=jax.ShapeDtypeStruct((b, t, c), BF16),
        scratch_shapes=[pltpu.VMEM((m, c), BF16), pltpu.VMEM((m, c), BF16)],
        compiler_params=_cparams(("parallel", "arbitrary")),
        name="memattn",
    )(rest3, rest3, mk3, mv3)


def _merge_kernel(ya_ref, yb_ref, yc_ref, ga_ref, gb_ref, gc_ref, x_ref,
                  wa_ref, wb_ref, wc_ref, wo_ref, lng_ref, lnb_ref, o_ref, *, alpha):
    def branch(y_ref, w_ref, g_ref):
        return jax.nn.sigmoid(g_ref[...].astype(F32)) * jnp.dot(
            y_ref[...], w_ref[...], preferred_element_type=F32)

    merged = branch(ya_ref, wa_ref, ga_ref) + branch(yb_ref, wb_ref, gb_ref)
    merged = merged + branch(yc_ref, wc_ref, gc_ref)
    sub = jnp.dot(merged.astype(BF16), wo_ref[...], preferred_element_type=F32)
    hres = alpha * x_ref[...] + sub
    mu = jnp.mean(hres, axis=1, keepdims=True)
    cen = hres - mu
    var = jnp.mean(cen * cen, axis=1, keepdims=True)
    o_ref[...] = cen * lax.rsqrt(var + NORM_EPS) * lng_ref[...] + lnb_ref[...]


def _merge(ya, yb, yc, rest2, x2, wa, wb, wc, wo, lng, lnb, gate_block0, alpha):
    n, d = x2.shape
    w = ya.shape[1]
    tm = min(n, MERGE_ROWS)
    assert n % tm == 0
    rows = lambda width, blk: pl.BlockSpec((tm, width), lambda i: (i, blk))
    const = lambda shape: pl.BlockSpec(shape, lambda i: (0, 0), pipeline_mode=pl.Buffered(1))
    return pl.pallas_call(
        functools.partial(_merge_kernel, alpha=alpha),
        grid=(n // tm,),
        in_specs=[
            rows(w, 0), rows(w, 0), rows(w, 0),
            rows(d, gate_block0), rows(d, gate_block0 + 1), rows(d, gate_block0 + 2),
            rows(d, 0),
            const((w, d)), const((w, d)), const((w, d)), const((d, d)),
            const((1, d)), const((1, d)),
        ],
        out_specs=rows(d, 0),
        out_shape=jax.ShapeDtypeStruct((n, d), F32),
        compiler_params=_cparams(("parallel",)),
        name="merge",
    )(ya, yb, yc, rest2, rest2, rest2, x2, wa, wb, wc, wo, lng, lnb)


def _layer(x, pos, past_k, past_v, s0, mk3, mv3, rec_block, layer_idx, lb, p):
    (w_in, lam, sub_norm, hgrn_gain, wa, wb, wc, wo, lng, lnb, alpha) = p
    n_b, t, d = x.shape
    a_width = wa.shape[0]
    heads = a_width // A_V_DIM
    x2 = x.reshape(n_b * t, d)
    prompt = past_k is None
    q_scale = A_HEAD_DIM ** -0.5 * math.log2(math.e)
    q2, k_out, v2, rest2 = _inproj(x2, w_in, _rope_tables(pos), t, a_width, q_scale,
                                   k_transposed=prompt)
    r = rest2.shape[1]
    rest3 = rest2.reshape(n_b, t, r)
    q3 = q2.reshape(n_b, t, a_width)
    v3 = v2.reshape(n_b, t, a_width)
    lam_init = 0.8 - 0.6 * math.exp(-0.3 * layer_idx)
    if prompt:
        ya = _attn_prompt(lam, q3, k_out, v3, rest3, sub_norm, lam_init)
        k5 = k_out.reshape(n_b, heads, 2, A_HEAD_DIM, t).transpose(0, 4, 1, 2, 3)
    else:
        k3 = k_out.reshape(n_b, t, a_width)
        ya = _attn_sample(lam, q3, past_k, past_v, k3, v3, rest3,
                          sub_norm, lam_init)
        k5 = k3.reshape(n_b, t, heads, 2, A_HEAD_DIM)
    yb, s_fin = _hgrn(rest3, lb, hgrn_gain, s0, rec_block, a_width // B_KEY_DIM)
    c_width = mk3.shape[2]
    c_off = a_width + 5 * lb.shape[1]
    assert c_off % c_width == 0
    yc = _memattn(rest3, mk3, mv3, c_off // c_width, c_off // c_width + 1)
    g_off = c_off + 2 * c_width
    assert g_off % d == 0
    y2 = _merge(ya.reshape(n_b * t, a_width), yb.reshape(n_b * t, -1), yc.reshape(n_b * t, c_width),
                rest2, x2, wa, wb, wc, wo, lng, lnb, g_off // d, alpha)
    return y2.reshape(n_b, t, d), k5, v3, s_fin


def kernel(x_prompt, x_sample, cache_attn_k, cache_attn_v, state_hgrn, cache_mem_k, cache_mem_v, mem_prompt, w_in, lambda_q1, lambda_k1, lambda_q2, lambda_k2, attn_sub_norm, hgrn_lb_logits, hgrn_norm, w_mem_k, w_mem_v, w_branch_a, w_branch_b, w_branch_c, w_out, ln_gamma, ln_beta):
    bp, seq, d = x_prompt.shape
    bs, t_new, _ = x_sample.shape
    depth = w_in.shape[0]
    past = cache_attn_k.shape[2]
    heads = cache_attn_k.shape[3]
    a_width = heads * A_V_DIM
    n_mem = mem_prompt.shape[1]
    c_heads = cache_mem_k.shape[3]
    c_width = c_heads * C_HEAD_DIM
    b_heads = state_hgrn.shape[2]
    alpha = (2 * depth) ** 0.25
    pos_prompt = jnp.arange(seq)
    pos_sample = past + jnp.arange(t_new)
    lower_bounds = jnp.cumsum(jax.nn.softmax(hgrn_lb_logits.astype(F32), axis=0), axis=0)

    h_p, h_s = x_prompt, x_sample
    outs = [[] for _ in range(8)]
    for l in range(depth):
        lam_init = 0.8 - 0.6 * math.exp(-0.3 * l)
        lam = (jnp.exp(jnp.sum(lambda_q1[l].astype(F32) * lambda_k1[l].astype(F32)))
               - jnp.exp(jnp.sum(lambda_q2[l].astype(F32) * lambda_k2[l].astype(F32))) + lam_init)
        params = (w_in[l].astype(BF16), lam.reshape(1), attn_sub_norm[l].reshape(1, -1),
                  hgrn_norm[l].reshape(1, -1), w_branch_a[l].astype(BF16), w_branch_b[l].astype(BF16),
                  w_branch_c[l].astype(BF16), w_out[l].astype(BF16), ln_gamma[l].reshape(1, -1),
                  ln_beta[l].reshape(1, -1), alpha)
        lb = lower_bounds[l].reshape(1, -1)
        mk_p, mv_p = _memkv(mem_prompt.reshape(bp * n_mem, d), w_mem_k[l].astype(BF16),
                            w_mem_v[l].astype(BF16))
        mk_p = mk_p.reshape(bp, n_mem, c_width)
        mv_p = mv_p.reshape(bp, n_mem, c_width)
        h_p, k_p, v_p, s_p = _layer(h_p, pos_prompt, None, None, None, mk_p, mv_p, CHUNK, l, lb, params)
        h_s, k_s, v_s, s_s = _layer(
            h_s, pos_sample,
            cache_attn_k[l].transpose(0, 2, 3, 4, 1).reshape(bs, a_width, past),
            cache_attn_v[l].reshape(bs, past * heads, A_V_DIM),
            state_hgrn[l], cache_mem_k[l].reshape(bs, n_mem, c_width),
            cache_mem_v[l].reshape(bs, n_mem, c_width), t_new, l, lb, params)
        new = (k_p, v_p.reshape(bp, seq, heads, A_V_DIM),
               s_p.astype(x_prompt.dtype), mk_p.reshape(bp, n_mem, c_heads, C_HEAD_DIM),
               mv_p.reshape(bp, n_mem, c_heads, C_HEAD_DIM),
               k_s, v_s.reshape(bs, t_new, heads, A_V_DIM),
               s_s.astype(x_sample.dtype))
        for acc, val in zip(outs, new):
            acc.append(val)
    return (h_p, h_s) + tuple(jnp.stack(o) for o in outs)
```

```python
import functools
import math

import jax
import jax.numpy as jnp
from jax import lax
from jax.experimental import pallas as pl
from jax.experimental.pallas import tpu as pltpu

F32 = jnp.float32
BF16 = jnp.bfloat16

CHUNK = 64
A_HEAD_DIM = 64
A_V_DIM = 128
ROT_DIM = 16
ROPE_THETA = 500000.0
B_KEY_DIM = 128
C_HEAD_DIM = 256
NORM_EPS = 1e-5
LANES = 128
VMEM_LIMIT = 56 * 1024 * 1024
INPROJ_ROWS = 1024
QKV_ROWS = 512
INPROJ_COLS = 512
REST_COLS = 2048
MEMKV_ROWS = 512
HGRN_ROWS = 512
MEMATTN_ROWS = 1024
MERGE_ROWS = 256
SAMPLE_KEYS = 2048
ATTN_TILE = 512
ATTN_STRIP = 32
PV_SPLIT = 4
ROW_CHUNKS = 4
NEXT_SCORES_AT = 3
HGRN_HEADS_PER_STEP = 8


def _cparams(sem):
    return pltpu.CompilerParams(dimension_semantics=sem, vmem_limit_bytes=VMEM_LIMIT)


def _silu(z):
    return z * jax.nn.sigmoid(z)


def _rope_tables(pos):
    half = ROT_DIM // 2
    inv_freq = jnp.power(ROPE_THETA, -jnp.arange(0, ROT_DIM, 2, dtype=F32) / ROT_DIM)
    ang = pos.astype(F32)[:, None] * inv_freq[None, :]
    cos, sin = jnp.cos(ang), jnp.sin(ang)
    t = pos.shape[0]
    pad = jnp.zeros((t, A_HEAD_DIM - ROT_DIM), F32)
    zero = jnp.zeros((t, half), F32)
    c64 = jnp.concatenate([cos, cos, pad + 1.0], axis=1)
    sa64 = jnp.concatenate([-sin, zero, pad], axis=1)
    sb64 = jnp.concatenate([zero, sin, pad], axis=1)
    rep = LANES // A_HEAD_DIM
    rows = tuple(jnp.tile(a, (1, rep)) for a in (c64, sa64, sb64))
    return rows, (cos.T, sin.T)


def _inproj_kernel(x_ref, w_ref, c_ref, sa_ref, sb_ref, ct_ref, st_ref,
                   q_ref, k_ref, v_ref, xb_ref, t_ref, *, tn, a_width, q_scale, k_transposed):
    half = ROT_DIM // 2
    xb_ref[...] = x_ref[...].astype(BF16)

    def project(col0):
        return jnp.dot(xb_ref[...], w_ref[:, col0:col0 + tn], preferred_element_type=F32)

    def rope_group(acc, g):
        xg = acc[:, g * LANES:(g + 1) * LANES]
        up = pltpu.roll(xg, LANES - half, 1)
        dn = pltpu.roll(xg, half, 1)
        return xg * c_ref[...] + up * sa_ref[...] + dn * sb_ref[...]

    for c0 in range(0, a_width, tn):
        acc = project(c0)
        for g in range(tn // LANES):
            cs = slice(c0 + g * LANES, c0 + (g + 1) * LANES)
            q_ref[:, cs] = (rope_group(acc, g) * q_scale).astype(BF16)

    for c0 in range(0, a_width, tn):
        if k_transposed:
            t_ref[c0 // tn] = project(a_width + c0)
            kt = t_ref[c0 // tn].T
            cos, sin = ct_ref[...], st_ref[...]
            for g in range(tn // A_HEAD_DIM):
                r0 = g * A_HEAD_DIM
                lo = kt[r0:r0 + half, :]
                hi = kt[r0 + half:r0 + ROT_DIM, :]
                k_ref[c0 + r0:c0 + r0 + half, :] = lo * cos - hi * sin
                k_ref[c0 + r0 + half:c0 + r0 + ROT_DIM, :] = hi * cos + lo * sin
                k_ref[c0 + r0 + ROT_DIM:c0 + r0 + A_HEAD_DIM, :] = kt[r0 + ROT_DIM:r0 + A_HEAD_DIM, :]
        else:
            acc = project(a_width + c0)
            for g in range(tn // LANES):
                k_ref[:, c0 + g * LANES:c0 + (g + 1) * LANES] = rope_group(acc, g)

    for c0 in range(0, a_width, tn):
        v_ref[:, c0:c0 + tn] = project(2 * a_width + c0)


def _proj_rest_kernel(x_ref, w_ref, r_ref, xb_ref):
    @pl.when(pl.program_id(1) == 0)
    def _():
        xb_ref[...] = x_ref[...].astype(BF16)

    tm = xb_ref.shape[0]
    for r in range(0, tm, tm // ROW_CHUNKS):
        rows = slice(r, r + tm // ROW_CHUNKS)
        r_ref[rows, :] = jnp.dot(xb_ref[rows, :], w_ref[...], preferred_element_type=F32).astype(BF16)


def _proj_rest(x2, w_bf, col0, tm):
    n, d = x2.shape
    ncols = w_bf.shape[1] - col0
    tn = REST_COLS
    assert ncols % tn == 0 and col0 % LANES == 0 and n % tm == 0
    return pl.pallas_call(
        _proj_rest_kernel,
        grid=(n // tm, ncols // tn),
        in_specs=[
            pl.BlockSpec((tm, d), lambda i, j: (i, 0)),
            pl.BlockSpec((pl.Element(d), pl.Element(tn)),
                         lambda i, j: (0, pl.multiple_of(col0 + j * tn, LANES))),
        ],
        out_specs=pl.BlockSpec((tm, tn), lambda i, j: (i, j)),
        out_shape=jax.ShapeDtypeStruct((n, ncols), BF16),
        scratch_shapes=[pltpu.VMEM((tm, d), BF16)],
        compiler_params=_cparams(("parallel", "arbitrary")),
        name="proj_rest",
    )(x2, w_bf)


def _inproj(x2, w_bf, tables, t_len, a_width, q_scale, k_transposed):
    n, d = x2.shape
    n_in = 3 * a_width
    tm = min(n, QKV_ROWS)
    tn = INPROJ_COLS
    assert n % tm == 0 and a_width % tn == 0
    row_tabs, col_tabs = tables
    if t_len >= tm:
        assert t_len % tm == 0
        per = t_len // tm
    else:
        assert tm % t_len == 0 and not k_transposed
        row_tabs = tuple(jnp.tile(a, (tm // t_len, 1)) for a in row_tabs)
        col_tabs = tuple(jnp.tile(a, (1, tm // t_len)) for a in col_tabs)
        per = 1
    row_spec = pl.BlockSpec((tm, LANES), lambda i: (i % per, 0))
    col_spec = pl.BlockSpec((ROT_DIM // 2, tm), lambda i: (0, i % per))
    rows = pl.BlockSpec((tm, a_width), lambda i: (i, 0))
    if k_transposed:
        k_spec = pl.BlockSpec((None, a_width, tm), lambda i: (i // per, 0, i % per))
        k_shape = jax.ShapeDtypeStruct((n // t_len, a_width, t_len), F32)
    else:
        k_spec = rows
        k_shape = jax.ShapeDtypeStruct((n, a_width), F32)
    kern = functools.partial(_inproj_kernel, tn=tn, a_width=a_width, q_scale=q_scale,
                             k_transposed=k_transposed)
    q2, k_out, v2 = pl.pallas_call(
        kern,
        grid=(n // tm,),
        in_specs=[
            pl.BlockSpec((tm, d), lambda i: (i, 0)),
            pl.BlockSpec((d, n_in), lambda i: (0, 0), pipeline_mode=pl.Buffered(1)),
            row_spec, row_spec, row_spec, col_spec, col_spec,
        ],
        out_specs=[rows, k_spec, rows],
        out_shape=[
            jax.ShapeDtypeStruct((n, a_width), BF16),
            k_shape,
            jax.ShapeDtypeStruct((n, a_width), F32),
        ],
        scratch_shapes=[pltpu.VMEM((tm, d), BF16), pltpu.VMEM((a_width // tn, tm, tn), F32)],
        compiler_params=_cparams(("parallel",)),
        name="inproj",
    )(x2, w_bf, *row_tabs, *col_tabs)
    return q2, k_out, v2, _proj_rest(x2, w_bf, n_in, min(n, INPROJ_ROWS))


def _memkv_kernel(x_ref, wk_ref, wv_ref, k_ref, v_ref):
    xb = x_ref[...].astype(BF16)
    k_ref[...] = jnp.dot(xb, wk_ref[...], preferred_element_type=F32)
    v_ref[...] = jnp.dot(xb, wv_ref[...], preferred_element_type=F32)


def _memkv(x2, wk_bf, wv_bf):
    n, d = x2.shape
    c = wk_bf.shape[1]
    tm = min(n, MEMKV_ROWS)
    assert n % tm == 0
    wspec = pl.BlockSpec((d, c), lambda i: (0, 0))
    ospec = pl.BlockSpec((tm, c), lambda i: (i, 0))
    return pl.pallas_call(
        _memkv_kernel,
        grid=(n // tm,),
        in_specs=[pl.BlockSpec((tm, d), lambda i: (i, 0)), wspec, wspec],
        out_specs=[ospec, ospec],
        out_shape=[jax.ShapeDtypeStruct((n, c), F32)] * 2,
        compiler_params=_cparams(("parallel",)),
        name="memkv",
    )(x2, wk_bf, wv_bf)


def _attn_epilogue(o, lam_init, gain, z):
    ms = jnp.mean(o * o, axis=1, keepdims=True)
    y = o * lax.rsqrt(ms + NORM_EPS) * gain * (1.0 - lam_init)
    return y * _silu(z)


def _attn_prompt_kernel(lam_ref, q_ref, kt_ref, v_ref, z_ref, g_ref, o_ref,
                        kb_ref, vb_ref, qa_ref, sa_ref, sb_ref, p_ref, acc_ref, m_ref, al_ref,
                        *, tq, lam_init):
    s_len = q_ref.shape[0]
    strip = ATTN_STRIP
    kb_ref[...] = kt_ref[...].astype(BF16)
    vb_ref[:, :A_V_DIM] = v_ref[...].astype(BF16)
    vb_ref[:, A_V_DIM:] = jnp.ones((s_len, A_V_DIM), BF16)
    lane = lax.broadcasted_iota(jnp.int32, (tq, A_V_DIM), 1)

    def scores(j, s_ref, maps=(0, 1)):
        kt = kb_ref[:, pl.ds(pl.multiple_of(j * tq, tq), tq)]
        for mp in maps:
            s_ref[mp] = jnp.dot(qa_ref[mp], kt, preferred_element_type=F32)

    def softmax_pv(j, s_ref, masked, nxt_ref=None, first=False):
        start = pl.multiple_of(j * tq, tq)
        grp = tq // PV_SPLIT
        if nxt_ref is not None:
            scores(j + 1, nxt_ref, (0,))
        for mp in range(2):
            for g0 in range(0, tq, grp):
                if nxt_ref is not None and mp * tq + g0 == NEXT_SCORES_AT * grp:
                    scores(j + 1, nxt_ref, (1,))
                nk = g0 + grp if masked else tq
                for r in range(g0, g0 + grp, strip):
                    rs = slice(r, r + strip)
                    sv = s_ref[mp, rs, :nk]
                    if masked:
                        cc = lax.broadcasted_iota(jnp.int32, (strip, nk), 1) // CHUNK
                        sv = jnp.where(cc <= r // CHUNK, sv, -jnp.inf)
                    mx = jnp.max(sv, axis=1, keepdims=True)
                    if first:
                        mn = jnp.broadcast_to(mx, (strip, LANES))
                    else:
                        m_old = m_ref[mp, rs, :]
                        mn = jnp.maximum(m_old, mx)
                        al_ref[mp, rs, :] = jnp.exp2(m_old - mn)
                    m_ref[mp, rs, :] = mn
                    for c in range(nk // LANES):
                        cs = slice(c * LANES, (c + 1) * LANES)
                        p_ref[mp, rs, cs] = jnp.exp2(sv[:, cs] - mn).astype(BF16)
                gs = slice(g0, g0 + grp)
                pv = jnp.dot(p_ref[mp, gs, :nk], vb_ref[pl.ds(start, nk), :],
                             preferred_element_type=F32)
                for c in range(2):
                    cs = slice(c * A_V_DIM, (c + 1) * A_V_DIM)
                    if first:
                        acc_ref[mp, gs, cs] = pv[:, cs]
                    else:
                        acc_ref[mp, gs, cs] = acc_ref[mp, gs, cs] * al_ref[mp, gs, :] + pv[:, cs]

    def first_scores(qi):
        qf = q_ref[pl.ds(pl.multiple_of(qi * tq, tq), tq), :].astype(F32)
        qa_ref[0] = jnp.where(lane < A_HEAD_DIM, qf, 0.0).astype(BF16)
        qa_ref[1] = jnp.where(lane >= A_HEAD_DIM, qf, 0.0).astype(BF16)
        scores(0, sa_ref)

    def q_tile(qi, _):
        rows = pl.ds(pl.multiple_of(qi * tq, tq), tq)

        @pl.when(qi > 0)
        def _():
            softmax_pv(0, sa_ref, False, sb_ref, first=True)

        @pl.when(qi == 0)
        def _():
            softmax_pv(0, sa_ref, True, first=True)

        def full_tile(j, carry):
            @pl.when(j % 2 == 0)
            def _():
                softmax_pv(j, sa_ref, False, sb_ref)

            @pl.when(j % 2 == 1)
            def _():
                softmax_pv(j, sb_ref, False, sa_ref)

            return carry

        lax.fori_loop(1, qi, full_tile, 0)

        @pl.when((qi > 0) & (qi % 2 == 0))
        def _():
            softmax_pv(qi, sa_ref, True)

        @pl.when(qi % 2 == 1)
        def _():
            softmax_pv(qi, sb_ref, True)

        first_scores(jnp.minimum(qi + 1, nq - 1))
        o = (acc_ref[0, :, :A_V_DIM] / acc_ref[0, :, A_V_DIM:]
             - lam_ref[0] * (acc_ref[1, :, :A_V_DIM] / acc_ref[1, :, A_V_DIM:]))
        y = _attn_epilogue(o, lam_init, g_ref[...], z_ref[rows, :].astype(F32))
        o_ref[rows, :] = y.astype(BF16)
        return 0

    nq = s_len // tq
    first_scores(0)
    lax.fori_loop(0, nq, q_tile, 0)


def _attn_prompt(lam, q3, kt3, v3, rest3, gain, lam_init):
    b, s, w = q3.shape
    h = w // A_V_DIM
    tq = min(s, ATTN_TILE)
    assert s % tq == 0 and tq % (PV_SPLIT * LANES) == 0
    kern = functools.partial(_attn_prompt_kernel, tq=tq, lam_init=lam_init)
    qspec = pl.BlockSpec((None, s, A_V_DIM), lambda bi, hi: (bi, 0, hi))
    return pl.pallas_call(
        kern,
        grid=(b, h),
        in_specs=[
            pl.BlockSpec(memory_space=pltpu.SMEM),
            qspec,
            pl.BlockSpec((None, A_V_DIM, s), lambda bi, hi: (bi, hi, 0)),
            qspec, qspec,
            pl.BlockSpec((1, A_V_DIM), lambda bi, hi: (0, 0)),
        ],
        out_specs=qspec,
        out_shape=jax.ShapeDtypeStruct((b, s, w), BF16),
        scratch_shapes=[
            pltpu.VMEM((A_V_DIM, s), BF16),
            pltpu.VMEM((s, 2 * A_V_DIM), BF16),
            pltpu.VMEM((2, tq, A_V_DIM), BF16),
            pltpu.VMEM((2, tq, tq), F32),
            pltpu.VMEM((2, tq, tq), F32),
            pltpu.VMEM((2, tq, tq), BF16),
            pltpu.VMEM((2, tq, 2 * A_V_DIM), F32),
            pltpu.VMEM((2, tq, LANES), F32),
            pltpu.VMEM((2, tq, LANES), F32),
        ],
        compiler_params=_cparams(("parallel", "parallel")),
        name="attn_prompt",
    )(lam, q3, kt3, v3, rest3, gain)


def _attn_sample_kernel(lam_ref, q_ref, kc_ref, vc_ref, kn_ref, vn_ref, z_ref, g_ref, o_ref,
                        qbd_ref, s_ref, w_ref, acc_ref, *, tk, nkt, t_new, heads, lam_init):
    j = pl.program_id(1)
    past = tk * nkt
    nt = (((1,), (1,)), ((), ()))
    width = kn_ref.shape[1]

    @pl.when(j == 0)
    def _():
        lane = lax.broadcasted_iota(jnp.int32, (t_new, A_V_DIM), 1)
        zero = jnp.zeros((t_new, A_V_DIM), BF16)
        for hd in range(heads):
            qh = q_ref[:, hd * A_V_DIM:(hd + 1) * A_V_DIM].astype(F32)
            for mp in range(2):
                r0 = (2 * hd + mp) * t_new
                mine = lane >= A_HEAD_DIM if mp == 1 else lane < A_HEAD_DIM
                own = jnp.where(mine, qh, 0.0).astype(BF16)
                for hb in range(heads):
                    qbd_ref[r0:r0 + t_new, hb * A_V_DIM:(hb + 1) * A_V_DIM] = own if hb == hd else zero

    def pad_new(ref):
        new = ref[...]
        return jnp.concatenate([new, jnp.zeros((LANES - t_new, width), F32)], axis=0).astype(BF16)

    @pl.when(j < nkt)
    def _():
        s = jnp.dot(qbd_ref[...], kc_ref[...].astype(BF16), preferred_element_type=F32)
        s_ref[:, pl.ds(pl.multiple_of(j * tk, tk), tk)] = s

    @pl.when(j == nkt - 1)
    def _():
        sn = lax.dot_general(qbd_ref[...], pad_new(kn_ref), nt, preferred_element_type=F32)
        col = lax.broadcasted_iota(jnp.int32, sn.shape, 1)
        s_ref[:, past:past + LANES] = jnp.where(col < t_new, sn, -jnp.inf)
        lam = lam_ref[0]
        for hd in range(heads):
            r0 = hd * 2 * t_new
            p = []
            for mp in range(2):
                s = s_ref[r0 + mp * t_new:r0 + (mp + 1) * t_new, :]
                e = jnp.exp2(s - jnp.max(s, axis=1, keepdims=True))
                p.append(e / jnp.sum(e, axis=1, keepdims=True))
            w_ref[hd * t_new:(hd + 1) * t_new, :] = (p[0] - lam * p[1]).astype(BF16)

    @pl.when(j == nkt)
    def _():
        full = jnp.dot(w_ref[:, past:past + LANES], pad_new(vn_ref), preferred_element_type=F32)
        for hd in range(heads):
            rs = slice(hd * t_new, (hd + 1) * t_new)
            acc_ref[rs, :] = full[rs, hd * A_V_DIM:(hd + 1) * A_V_DIM]

    @pl.when(j >= nkt)
    def _():
        start = pl.multiple_of((j - nkt) * tk, tk)
        for hd in range(heads):
            rs = slice(hd * t_new, (hd + 1) * t_new)
            vh = vc_ref[pl.ds(hd, tk, stride=heads), :].astype(BF16)
            acc_ref[rs, :] += jnp.dot(w_ref[rs, pl.ds(start, tk)], vh, preferred_element_type=F32)

    @pl.when(j == 2 * nkt - 1)
    def _():
        for hd in range(heads):
            cs = slice(hd * A_V_DIM, (hd + 1) * A_V_DIM)
            o = acc_ref[hd * t_new:(hd + 1) * t_new, :]
            y = _attn_epilogue(o, lam_init, g_ref[...], z_ref[:, cs].astype(F32))
            o_ref[:, cs] = y.astype(BF16)


def _attn_sample(lam, q3, kct, vc3, kn, vn, rest3, gain, lam_init):
    b, w, past = kct.shape
    t_new = kn.shape[1]
    heads = w // A_V_DIM
    rows = 2 * heads * t_new
    tk = min(past, SAMPLE_KEYS)
    assert past % tk == 0 and t_new % 16 == 0 and t_new <= LANES
    nkt = past // tk
    kern = functools.partial(_attn_sample_kernel, tk=tk, nkt=nkt, t_new=t_new, heads=heads,
                             lam_init=lam_init)
    newspec = pl.BlockSpec((None, t_new, w), lambda bi, j: (bi, 0, 0))
    return pl.pallas_call(
        kern,
        grid=(b, 2 * nkt),
        in_specs=[
            pl.BlockSpec(memory_space=pltpu.SMEM),
            newspec,
            pl.BlockSpec((None, w, tk), lambda bi, j: (bi, 0, jnp.minimum(j, nkt - 1))),
            pl.BlockSpec((None, tk * heads, A_V_DIM),
                         lambda bi, j: (bi, jnp.maximum(j - nkt, 0), 0)),
            newspec, newspec, newspec,
            pl.BlockSpec((1, A_V_DIM), lambda bi, j: (0, 0)),
        ],
        out_specs=newspec,
        out_shape=jax.ShapeDtypeStruct((b, t_new, w), BF16),
        scratch_shapes=[
            pltpu.VMEM((rows, w), BF16),
            pltpu.VMEM((rows, past + LANES), F32),
            pltpu.VMEM((rows // 2, past + LANES), BF16),
            pltpu.VMEM((rows // 2, A_V_DIM), F32),
        ],
        compiler_params=_cparams(("parallel", "arbitrary")),
        name="attn_sample",
    )(lam, q3, kct, vc3, kn, vn, rest3, gain)


def _hgrn_kernel(*refs, blk, nchunk, hp, has_s0):
    if has_s0:
        q_ref, f_ref, i_ref, og_ref, z_ref, lb_ref, gain_ref, s0_ref, y_ref, sf_ref, st_ref = refs
    else:
        q_ref, f_ref, i_ref, og_ref, z_ref, lb_ref, gain_ref, y_ref, sf_ref, st_ref = refs
    t = pl.program_id(2)

    @pl.when(t == 0)
    def _():
        for h in range(hp):
            st_ref[h] = s0_ref[h].T if has_s0 else jnp.zeros(st_ref.shape[1:], F32)

    lb = lb_ref[...]
    gain = gain_ref[...]
    row = lax.broadcasted_iota(jnp.int32, (blk, blk), 0)
    col = lax.broadcasted_iota(jnp.int32, (blk, blk), 1)
    causal = col <= row
    tril = jnp.where(causal, 1.0, 0.0).astype(BF16)
    mid = (blk - 1) // 2
    nt = (((1,), (1,)), ((), ()))
    tn = (((0,), (0,)), ((), ()))

    def split3(g):
        hi = g.astype(BF16)
        r1 = g - hi.astype(F32)
        md = r1.astype(BF16)
        lo = (r1 - md.astype(F32)).astype(BF16)
        return hi, md, lo

    chunks = [slice(c * blk, (c + 1) * blk) for c in range(nchunk)]
    f = lb + (1.0 - lb) * jax.nn.sigmoid(f_ref[...].astype(F32))
    parts = split3(jnp.log(f))
    qv = _silu(q_ref[...].astype(F32))
    kv = 1.0 - f
    vv = i_ref[...]
    b = [sum(jnp.dot(tril, p[sl, :], preferred_element_type=F32) for p in parts) for sl in chunks]
    qe, ke, qs, ks, decay = [], [], [], [], []
    for sl, bc in zip(chunks, b):
        b_mid = bc[mid:mid + 1, :]
        b_last = bc[blk - 1:blk, :]
        decay.append(jnp.exp(b_last))
        qe.append((qv[sl, :] * jnp.exp(bc - b_mid)).astype(BF16))
        ke.append((kv[sl, :] * jnp.exp(b_mid - bc)).astype(BF16))
        qs.append((qv[sl, :] * jnp.exp(bc)).astype(BF16))
        ks.append((kv[sl, :] * jnp.exp(b_last - bc)).astype(BF16))
    lanes = [slice(h * B_KEY_DIM, (h + 1) * B_KEY_DIM) for h in range(hp)]
    o_intra, kvs = [], []
    for c, sl in enumerate(chunks):
        sc = [lax.dot_general(qe[c][:, hl], ke[c][:, hl], nt, preferred_element_type=F32)
              for hl in lanes]
        sc = [jnp.where(causal, s, 0.0).astype(BF16) for s in sc]
        o_intra.append([jnp.dot(s, vv[sl, hl], preferred_element_type=F32)
                        for s, hl in zip(sc, lanes)])
        kvs.append([lax.dot_general(vv[sl, hl], ks[c][:, hl], tn, preferred_element_type=F32)
                    for hl in lanes])
    cols = []
    for h, hl in enumerate(lanes):
        state = st_ref[h]
        outs = []
        for c in range(nchunk):
            outs.append(o_intra[c][h] + lax.dot_general(qs[c][:, hl], state.astype(BF16), nt,
                                                        preferred_element_type=F32))
            state = state * decay[c][:, hl] + kvs[c][h]
        st_ref[h] = state
        o = outs[0] if nchunk == 1 else jnp.concatenate(outs, axis=0)
        ms = jnp.mean(o * o, axis=1, keepdims=True)
        cols.append(o * lax.rsqrt(ms + NORM_EPS))
    o = cols[0] if hp == 1 else jnp.concatenate(cols, axis=1)
    y = o * gain * jax.nn.sigmoid(og_ref[...].astype(F32))
    y_ref[...] = (y * _silu(z_ref[...].astype(F32))).astype(BF16)

    @pl.when(t == pl.num_programs(2) - 1)
    def _():
        for h in range(hp):
            sf_ref[h] = st_ref[h].T


def _hgrn(rest3, lb, gain, s0, blk, col0):
    b, t, _ = rest3.shape
    heads = lb.shape[1] // B_KEY_DIM
    tt = min(t, HGRN_ROWS)
    assert t % tt == 0 and tt % blk == 0
    has_s0 = s0 is not None
    hp = heads if tt // blk == 1 else HGRN_HEADS_PER_STEP
    assert heads % hp == 0 and col0 % hp == 0
    width = hp * B_KEY_DIM
    kern = functools.partial(_hgrn_kernel, blk=blk, nchunk=tt // blk, hp=hp, has_s0=has_s0)

    def colspec(k):
        return pl.BlockSpec((None, tt, width),
                            lambda bi, hi, ti: (bi, ti, (col0 + k * heads) // hp + hi))

    vecspec = pl.BlockSpec((1, width), lambda bi, hi, ti: (0, hi))
    stspec = pl.BlockSpec((None, hp, B_KEY_DIM, B_KEY_DIM), lambda bi, hi, ti: (bi, hi, 0, 0))
    in_specs = [colspec(k) for k in range(5)] + [vecspec, vecspec]
    args = [rest3] * 5 + [lb, gain]
    if has_s0:
        in_specs.append(stspec)
        args.append(s0)
    return pl.pallas_call(
        kern,
        grid=(b, heads // hp, t // tt),
        in_specs=in_specs,
        out_specs=[
            pl.BlockSpec((None, tt, width), lambda bi, hi, ti: (bi, ti, hi)),
            stspec,
        ],
        out_shape=[
            jax.ShapeDtypeStruct((b, t, heads * B_KEY_DIM), BF16),
            jax.ShapeDtypeStruct((b, heads, B_KEY_DIM, B_KEY_DIM), F32),
        ],
        scratch_shapes=[pltpu.VMEM((hp, B_KEY_DIM, B_KEY_DIM), F32)],
        compiler_params=_cparams(("parallel", "parallel", "arbitrary")),
        name="hgrn",
    )(*args)


def _memattn_kernel(q_ref, z_ref, mk_ref, mv_ref, o_ref, kb_ref, vb_ref, *, heads):
    @pl.when(pl.program_id(1) == 0)
    def _():
        kb_ref[...] = mk_ref[...].astype(BF16)
        vb_ref[...] = mv_ref[...].astype(BF16)

    nt = (((1,), (1,)), ((), ()))
    cols = [slice(hd * C_HEAD_DIM, (hd + 1) * C_HEAD_DIM) for hd in range(heads)]
    s = [lax.dot_general(q_ref[:, cs], kb_ref[:, cs], nt, preferred_element_type=F32)
         * (C_HEAD_DIM ** -0.5) for cs in cols]
    e = [jnp.exp(sh - jnp.max(sh, axis=1, keepdims=True)) for sh in s]
    o = [jnp.dot(eh.astype(BF16), vb_ref[:, cs], preferred_element_type=F32)
         / jnp.sum(eh, axis=1, keepdims=True) for eh, cs in zip(e, cols)]
    for oh, cs in zip(o, cols):
        o_ref[:, cs] = (oh * _silu(z_ref[:, cs].astype(F32))).astype(BF16)


def _memattn(rest3, mk3, mv3, qblock, zblock):
    b, t, _ = rest3.shape
    _, m, c = mk3.shape
    tq = min(t, MEMATTN_ROWS)
    assert t % tq == 0
    kern = functools.partial(_memattn_kernel, heads=c // C_HEAD_DIM)
    memspec = pl.BlockSpec((None, m, c), lambda bi, ti: (bi, 0, 0))
    return pl.pallas_call(
        kern,
        grid=(b, t // tq),
        in_specs=[
            pl.BlockSpec((None, tq, c), lambda bi, ti: (bi, ti, qblock)),
            pl.BlockSpec((None, tq, c), lambda bi, ti: (bi, ti, zblock)),
            memspec, memspec,
        ],
        out_specs=pl.BlockSpec((None, tq, c), lambda bi, ti: (bi, ti, 0)),
        out_shape=jax.ShapeDtypeStruct((b, t, c), BF16),
        scratch_shapes=[pltpu.VMEM((m, c), BF16), pltpu.VMEM((m, c), BF16)],
        compiler_params=_cparams(("parallel", "arbitrary")),
        name="memattn",
    )(rest3, rest3, mk3, mv3)


def _merge_kernel(ya_ref, yb_ref, yc_ref, ga_ref, gb_ref, gc_ref, x_ref,
                  wa_ref, wb_ref, wc_ref, wo_ref, lng_ref, lnb_ref, o_ref, *, alpha):
    def branch(y_ref, w_ref, g_ref):
        return jax.nn.sigmoid(g_ref[...].astype(F32)) * jnp.dot(
            y_ref[...], w_ref[...], preferred_element_type=F32)

    merged = branch(ya_ref, wa_ref, ga_ref) + branch(yb_ref, wb_ref, gb_ref)
    merged = merged + branch(yc_ref, wc_ref, gc_ref)
    sub = jnp.dot(merged.astype(BF16), wo_ref[...], preferred_element_type=F32)
    hres = alpha * x_ref[...] + sub
    mu = jnp.mean(hres, axis=1, keepdims=True)
    cen = hres - mu
    var = jnp.mean(cen * cen, axis=1, keepdims=True)
    o_ref[...] = cen * lax.rsqrt(var + NORM_EPS) * lng_ref[...] + lnb_ref[...]


def _merge(ya, yb, yc, rest2, x2, wa, wb, wc, wo, lng, lnb, gate_block0, alpha):
    n, d = x2.shape
    w = ya.shape[1]
    tm = min(n, MERGE_ROWS)
    assert n % tm == 0
    rows = lambda width, blk: pl.BlockSpec((tm, width), lambda i: (i, blk))
    const = lambda shape: pl.BlockSpec(shape, lambda i: (0, 0), pipeline_mode=pl.Buffered(1))
    return pl.pallas_call(
        functools.partial(_merge_kernel, alpha=alpha),
        grid=(n // tm,),
        in_specs=[
            rows(w, 0), rows(w, 0), rows(w, 0),
            rows(d, gate_block0), rows(d, gate_block0 + 1), rows(d, gate_block0 + 2),
            rows(d, 0),
            const((w, d)), const((w, d)), const((w, d)), const((d, d)),
            const((1, d)), const((1, d)),
        ],
        out_specs=rows(d, 0),
        out_shape=jax.ShapeDtypeStruct((n, d), F32),
        compiler_params=_cparams(("parallel",)),
        name="merge",
    )(ya, yb, yc, rest2, rest2, rest2, x2, wa, wb, wc, wo, lng, lnb)


def _layer(x, pos, past_k, past_v, s0, mk3, mv3, rec_block, layer_idx, lb, p):
    (w_in, lam, sub_norm, hgrn_gain, wa, wb, wc, wo, lng, lnb, alpha) = p
    n_b, t, d = x.shape
    a_width = wa.shape[0]
    heads = a_width // A_V_DIM
    x2 = x.reshape(n_b * t, d)
    prompt = past_k is None
    q_scale = A_HEAD_DIM ** -0.5 * math.log2(math.e)
    q2, k_out, v2, rest2 = _inproj(x2, w_in, _rope_tables(pos), t, a_width, q_scale,
                                   k_transposed=prompt)
    r = rest2.shape[1]
    rest3 = rest2.reshape(n_b, t, r)
    q3 = q2.reshape(n_b, t, a_width)
    v3 = v2.reshape(n_b, t, a_width)
    lam_init = 0.8 - 0.6 * math.exp(-0.3 * layer_idx)
    if prompt:
        ya = _attn_prompt(lam, q3, k_out, v3, rest3, sub_norm, lam_init)
        k5 = k_out.reshape(n_b, heads, 2, A_HEAD_DIM, t).transpose(0, 4, 1, 2, 3)
    else:
        k3 = k_out.reshape(n_b, t, a_width)
        ya = _attn_sample(lam, q3, past_k, past_v, k3, v3, rest3,
                          sub_norm, lam_init)
        k5 = k3.reshape(n_b, t, heads, 2, A_HEAD_DIM)
    yb, s_fin = _hgrn(rest3, lb, hgrn_gain, s0, rec_block, a_width // B_KEY_DIM)
    c_width = mk3.shape[2]
    c_off = a_width + 5 * lb.shape[1]
    assert c_off % c_width == 0
    yc = _memattn(rest3, mk3, mv3, c_off // c_width, c_off // c_width + 1)
    g_off = c_off + 2 * c_width
    assert g_off % d == 0
    y2 = _merge(ya.reshape(n_b * t, a_width), yb.reshape(n_b * t, -1), yc.reshape(n_b * t, c_width),
                rest2, x2, wa, wb, wc, wo, lng, lnb, g_off // d, alpha)
    return y2.reshape(n_b, t, d), k5, v3, s_fin


def kernel(x_prompt, x_sample, cache_attn_k, cache_attn_v, state_hgrn, cache_mem_k, cache_mem_v, mem_prompt, w_in, lambda_q1, lambda_k1, lambda_q2, lambda_k2, attn_sub_norm, hgrn_lb_logits, hgrn_norm, w_mem_k, w_mem_v, w_branch_a, w_branch_b, w_branch_c, w_out, ln_gamma, ln_beta):
    bp, seq, d = x_prompt.shape
    bs, t_new, _ = x_sample.shape
    depth = w_in.shape[0]
    past = cache_attn_k.shape[2]
    heads = cache_attn_k.shape[3]
    a_width = heads * A_V_DIM
    n_mem = mem_prompt.shape[1]
    c_heads = cache_mem_k.shape[3]
    c_width = c_heads * C_HEAD_DIM
    b_heads = state_hgrn.shape[2]
    alpha = (2 * depth) ** 0.25
    pos_prompt = jnp.arange(seq)
    pos_sample = past + jnp.arange(t_new)
    lower_bounds = jnp.cumsum(jax.nn.softmax(hgrn_lb_logits.astype(F32), axis=0), axis=0)

    h_p, h_s = x_prompt, x_sample
    outs = [[] for _ in range(8)]
    for l in range(depth):
        lam_init = 0.8 - 0.6 * math.exp(-0.3 * l)
        lam = (jnp.exp(jnp.sum(lambda_q1[l].astype(F32) * lambda_k1[l].astype(F32)))
               - jnp.exp(jnp.sum(lambda_q2[l].astype(F32) * lambda_k2[l].astype(F32))) + lam_init)
        params = (w_in[l].astype(BF16), lam.reshape(1), attn_sub_norm[l].reshape(1, -1),
                  hgrn_norm[l].reshape(1, -1), w_branch_a[l].astype(BF16), w_branch_b[l].astype(BF16),
                  w_branch_c[l].astype(BF16), w_out[l].astype(BF16), ln_gamma[l].reshape(1, -1),
                  ln_beta[l].reshape(1, -1), alpha)
        lb = lower_bounds[l].reshape(1, -1)
        mk_p, mv_p = _memkv(mem_prompt.reshape(bp * n_mem, d), w_mem_k[l].astype(BF16),
                            w_mem_v[l].astype(BF16))
        mk_p = mk_p.reshape(bp, n_mem, c_width)
        mv_p = mv_p.reshape(bp, n_mem, c_width)
        h_p, k_p, v_p, s_p = _layer(h_p, pos_prompt, None, None, None, mk_p, mv_p, CHUNK, l, lb, params)
        h_s, k_s, v_s, s_s = _layer(
            h_s, pos_sample,
            cache_attn_k[l].transpose(0, 2, 3, 4, 1).reshape(bs, a_width, past),
            cache_attn_v[l].reshape(bs, past * heads, A_V_DIM),
            state_hgrn[l], cache_mem_k[l].reshape(bs, n_mem, c_width),
            cache_mem_v[l].reshape(bs, n_mem, c_width), t_new, l, lb, params)
        new = (k_p, v_p.reshape(bp, seq, heads, A_V_DIM),
               s_p.astype(x_prompt.dtype), mk_p.reshape(bp, n_mem, c_heads, C_HEAD_DIM),
               mv_p.reshape(bp, n_mem, c_heads, C_HEAD_DIM),
               k_s, v_s.reshape(bs, t_new, heads, A_V_DIM),
               s_s.astype(x_sample.dtype))
        for acc, val in zip(outs, new):
            acc.append(val)
    return (h_p, h_s) + tuple(jnp.stack(o) for o in outs)
```

```python
import functools
import math

import jax
import jax.numpy as jnp
from jax import lax
from jax.experimental import pallas as pl
from jax.experimental.pallas import tpu as pltpu

F32 = jnp.float32
BF16 = jnp.bfloat16

CHUNK = 64
A_HEAD_DIM = 64
A_V_DIM = 128
ROT_DIM = 16
ROPE_THETA = 500000.0
B_KEY_DIM = 128
C_HEAD_DIM = 256
NORM_EPS = 1e-5
LANES = 128
VMEM_LIMIT = 56 * 1024 * 1024
INPROJ_ROWS = 1024
QKV_ROWS = 512
INPROJ_COLS = 512
REST_COLS = 2048
MEMKV_ROWS = 512
HGRN_ROWS = 512
MEMATTN_ROWS = 1024
MERGE_ROWS = 256
SAMPLE_KEYS = 2048
ATTN_TILE = 512
ATTN_STRIP = 32
PV_SPLIT = 4
ROW_CHUNKS = 4
NEXT_SCORES_AT = 3


def _cparams(sem):
    return pltpu.CompilerParams(dimension_semantics=sem, vmem_limit_bytes=VMEM_LIMIT)


def _silu(z):
    return z * jax.nn.sigmoid(z)


def _rope_tables(pos):
    half = ROT_DIM // 2
    inv_freq = jnp.power(ROPE_THETA, -jnp.arange(0, ROT_DIM, 2, dtype=F32) / ROT_DIM)
    ang = pos.astype(F32)[:, None] * inv_freq[None, :]
    cos, sin = jnp.cos(ang), jnp.sin(ang)
    t = pos.shape[0]
    pad = jnp.zeros((t, A_HEAD_DIM - ROT_DIM), F32)
    zero = jnp.zeros((t, half), F32)
    c64 = jnp.concatenate([cos, cos, pad + 1.0], axis=1)
    sa64 = jnp.concatenate([-sin, zero, pad], axis=1)
    sb64 = jnp.concatenate([zero, sin, pad], axis=1)
    rep = LANES // A_HEAD_DIM
    rows = tuple(jnp.tile(a, (1, rep)) for a in (c64, sa64, sb64))
    return rows, (cos.T, sin.T)


def _inproj_kernel(x_ref, w_ref, c_ref, sa_ref, sb_ref, ct_ref, st_ref,
                   q_ref, k_ref, v_ref, xb_ref, t_ref, *, tn, a_width, q_scale, k_transposed):
    half = ROT_DIM // 2
    xb_ref[...] = x_ref[...].astype(BF16)

    def project(col0):
        return jnp.dot(xb_ref[...], w_ref[:, col0:col0 + tn], preferred_element_type=F32)

    def rope_group(acc, g):
        xg = acc[:, g * LANES:(g + 1) * LANES]
        up = pltpu.roll(xg, LANES - half, 1)
        dn = pltpu.roll(xg, half, 1)
        return xg * c_ref[...] + up * sa_ref[...] + dn * sb_ref[...]

    for c0 in range(0, a_width, tn):
        acc = project(c0)
        for g in range(tn // LANES):
            cs = slice(c0 + g * LANES, c0 + (g + 1) * LANES)
            q_ref[:, cs] = (rope_group(acc, g) * q_scale).astype(BF16)

    for c0 in range(0, a_width, tn):
        if k_transposed:
            t_ref[c0 // tn] = project(a_width + c0)
            kt = t_ref[c0 // tn].T
            cos, sin = ct_ref[...], st_ref[...]
            for g in range(tn // A_HEAD_DIM):
                r0 = g * A_HEAD_DIM
                lo = kt[r0:r0 + half, :]
                hi = kt[r0 + half:r0 + ROT_DIM, :]
                k_ref[c0 + r0:c0 + r0 + half, :] = lo * cos - hi * sin
                k_ref[c0 + r0 + half:c0 + r0 + ROT_DIM, :] = hi * cos + lo * sin
                k_ref[c0 + r0 + ROT_DIM:c0 + r0 + A_HEAD_DIM, :] = kt[r0 + ROT_DIM:r0 + A_HEAD_DIM, :]
        else:
            acc = project(a_width + c0)
            for g in range(tn // LANES):
                k_ref[:, c0 + g * LANES:c0 + (g + 1) * LANES] = rope_group(acc, g)

    for c0 in range(0, a_width, tn):
        v_ref[:, c0:c0 + tn] = project(2 * a_width + c0)


def _proj_rest_kernel(x_ref, w_ref, r_ref, xb_ref):
    @pl.when(pl.program_id(1) == 0)
    def _():
        xb_ref[...] = x_ref[...].astype(BF16)

    tm = xb_ref.shape[0]
    for r in range(0, tm, tm // ROW_CHUNKS):
        rows = slice(r, r + tm // ROW_CHUNKS)
        r_ref[rows, :] = jnp.dot(xb_ref[rows, :], w_ref[...], preferred_element_type=F32).astype(BF16)


def _proj_rest(x2, w_bf, col0, tm):
    n, d = x2.shape
    ncols = w_bf.shape[1] - col0
    tn = REST_COLS
    assert ncols % tn == 0 and col0 % LANES == 0 and n % tm == 0
    return pl.pallas_call(
        _proj_rest_kernel,
        grid=(n // tm, ncols // tn),
        in_specs=[
            pl.BlockSpec((tm, d), lambda i, j: (i, 0)),
            pl.BlockSpec((pl.Element(d), pl.Element(tn)),
                         lambda i, j: (0, pl.multiple_of(col0 + j * tn, LANES))),
        ],
        out_specs=pl.BlockSpec((tm, tn), lambda i, j: (i, j)),
        out_shape=jax.ShapeDtypeStruct((n, ncols), BF16),
        scratch_shapes=[pltpu.VMEM((tm, d), BF16)],
        compiler_params=_cparams(("parallel", "arbitrary")),
        name="proj_rest",
    )(x2, w_bf)


def _inproj(x2, w_bf, tables, t_len, a_width, q_scale, k_transposed):
    n, d = x2.shape
    n_in = 3 * a_width
    tm = min(n, QKV_ROWS)
    tn = INPROJ_COLS
    assert n % tm == 0 and a_width % tn == 0
    row_tabs, col_tabs = tables
    if t_len >= tm:
        assert t_len % tm == 0
        per = t_len // tm
    else:
        assert tm % t_len == 0 and not k_transposed
        row_tabs = tuple(jnp.tile(a, (tm // t_len, 1)) for a in row_tabs)
        col_tabs = tuple(jnp.tile(a, (1, tm // t_len)) for a in col_tabs)
        per = 1
    row_spec = pl.BlockSpec((tm, LANES), lambda i: (i % per, 0))
    col_spec = pl.BlockSpec((ROT_DIM // 2, tm), lambda i: (0, i % per))
    rows = pl.BlockSpec((tm, a_width), lambda i: (i, 0))
    if k_transposed:
        k_spec = pl.BlockSpec((None, a_width, tm), lambda i: (i // per, 0, i % per))
        k_shape = jax.ShapeDtypeStruct((n // t_len, a_width, t_len), F32)
    else:
        k_spec = rows
        k_shape = jax.ShapeDtypeStruct((n, a_width), F32)
    kern = functools.partial(_inproj_kernel, tn=tn, a_width=a_width, q_scale=q_scale,
                             k_transposed=k_transposed)
    q2, k_out, v2 = pl.pallas_call(
        kern,
        grid=(n // tm,),
        in_specs=[
            pl.BlockSpec((tm, d), lambda i: (i, 0)),
            pl.BlockSpec((d, n_in), lambda i: (0, 0), pipeline_mode=pl.Buffered(1)),
            row_spec, row_spec, row_spec, col_spec, col_spec,
        ],
        out_specs=[rows, k_spec, rows],
        out_shape=[
            jax.ShapeDtypeStruct((n, a_width), BF16),
            k_shape,
            jax.ShapeDtypeStruct((n, a_width), F32),
        ],
        scratch_shapes=[pltpu.VMEM((tm, d), BF16), pltpu.VMEM((a_width // tn, tm, tn), F32)],
        compiler_params=_cparams(("parallel",)),
        name="inproj",
    )(x2, w_bf, *row_tabs, *col_tabs)
    return q2, k_out, v2, _proj_rest(x2, w_bf, n_in, min(n, INPROJ_ROWS))


def _memkv_kernel(x_ref, wk_ref, wv_ref, k_ref, v_ref):
    xb = x_ref[...].astype(BF16)
    k_ref[...] = jnp.dot(xb, wk_ref[...], preferred_element_type=F32)
    v_ref[...] = jnp.dot(xb, wv_ref[...], preferred_element_type=F32)


def _memkv(x2, wk_bf, wv_bf):
    n, d = x2.shape
    c = wk_bf.shape[1]
    tm = min(n, MEMKV_ROWS)
    assert n % tm == 0
    wspec = pl.BlockSpec((d, c), lambda i: (0, 0))
    ospec = pl.BlockSpec((tm, c), lambda i: (i, 0))
    return pl.pallas_call(
        _memkv_kernel,
        grid=(n // tm,),
        in_specs=[pl.BlockSpec((tm, d), lambda i: (i, 0)), wspec, wspec],
        out_specs=[ospec, ospec],
        out_shape=[jax.ShapeDtypeStruct((n, c), F32)] * 2,
        compiler_params=_cparams(("parallel",)),
        name="memkv",
    )(x2, wk_bf, wv_bf)


def _attn_epilogue(o, lam_init, gain, z):
    ms = jnp.mean(o * o, axis=1, keepdims=True)
    y = o * lax.rsqrt(ms + NORM_EPS) * gain * (1.0 - lam_init)
    return y * _silu(z)


def _attn_prompt_kernel(lam_ref, q_ref, kt_ref, v_ref, z_ref, g_ref, o_ref,
                        kb_ref, vb_ref, qa_ref, sa_ref, sb_ref, p_ref, acc_ref, m_ref, al_ref,
                        *, tq, lam_init):
    s_len = q_ref.shape[0]
    strip = ATTN_STRIP
    kb_ref[...] = kt_ref[...].astype(BF16)
    vb_ref[:, :A_V_DIM] = v_ref[...].astype(BF16)
    vb_ref[:, A_V_DIM:] = jnp.ones((s_len, A_V_DIM), BF16)
    lane = lax.broadcasted_iota(jnp.int32, (tq, A_V_DIM), 1)

    def scores(j, s_ref, maps=(0, 1)):
        kt = kb_ref[:, pl.ds(pl.multiple_of(j * tq, tq), tq)]
        for mp in maps:
            s_ref[mp] = jnp.dot(qa_ref[mp], kt, preferred_element_type=F32)

    def softmax_pv(j, s_ref, masked, nxt_ref=None, first=False):
        start = pl.multiple_of(j * tq, tq)
        grp = tq // PV_SPLIT
        if nxt_ref is not None:
            scores(j + 1, nxt_ref, (0,))
        for mp in range(2):
            for g0 in range(0, tq, grp):
                if nxt_ref is not None and mp * tq + g0 == NEXT_SCORES_AT * grp:
                    scores(j + 1, nxt_ref, (1,))
                nk = g0 + grp if masked else tq
                for r in range(g0, g0 + grp, strip):
                    rs = slice(r, r + strip)
                    sv = s_ref[mp, rs, :nk]
                    if masked:
                        cc = lax.broadcasted_iota(jnp.int32, (strip, nk), 1) // CHUNK
                        sv = jnp.where(cc <= r // CHUNK, sv, -jnp.inf)
                    mx = jnp.max(sv, axis=1, keepdims=True)
                    if first:
                        mn = jnp.broadcast_to(mx, (strip, LANES))
                    else:
                        m_old = m_ref[mp, rs, :]
                        mn = jnp.maximum(m_old, mx)
                        al_ref[mp, rs, :] = jnp.exp2(m_old - mn)
                    m_ref[mp, rs, :] = mn
                    for c in range(nk // LANES):
                        cs = slice(c * LANES, (c + 1) * LANES)
                        p_ref[mp, rs, cs] = jnp.exp2(sv[:, cs] - mn).astype(BF16)
                gs = slice(g0, g0 + grp)
                pv = jnp.dot(p_ref[mp, gs, :nk], vb_ref[pl.ds(start, nk), :],
                             preferred_element_type=F32)
                for c in range(2):
                    cs = slice(c * A_V_DIM, (c + 1) * A_V_DIM)
                    if first:
                        acc_ref[mp, gs, cs] = pv[:, cs]
                    else:
                        acc_ref[mp, gs, cs] = acc_ref[mp, gs, cs] * al_ref[mp, gs, :] + pv[:, cs]

    def first_scores(qi):
        qf = q_ref[pl.ds(pl.multiple_of(qi * tq, tq), tq), :].astype(F32)
        qa_ref[0] = jnp.where(lane < A_HEAD_DIM, qf, 0.0).astype(BF16)
        qa_ref[1] = jnp.where(lane >= A_HEAD_DIM, qf, 0.0).astype(BF16)
        scores(0, sa_ref)

    def q_tile(qi, _):
        rows = pl.ds(pl.multiple_of(qi * tq, tq), tq)

        @pl.when(qi > 0)
        def _():
            softmax_pv(0, sa_ref, False, sb_ref, first=True)

        @pl.when(qi == 0)
        def _():
            softmax_pv(0, sa_ref, True, first=True)

        def full_tile(j, carry):
            @pl.when(j % 2 == 0)
            def _():
                softmax_pv(j, sa_ref, False, sb_ref)

            @pl.when(j % 2 == 1)
            def _():
                softmax_pv(j, sb_ref, False, sa_ref)

            return carry

        lax.fori_loop(1, qi, full_tile, 0)

        @pl.when((qi > 0) & (qi % 2 == 0))
        def _():
            softmax_pv(qi, sa_ref, True)

        @pl.when(qi % 2 == 1)
        def _():
            softmax_pv(qi, sb_ref, True)

        first_scores(jnp.minimum(qi + 1, nq - 1))
        o = (acc_ref[0, :, :A_V_DIM] / acc_ref[0, :, A_V_DIM:]
             - lam_ref[0] * (acc_ref[1, :, :A_V_DIM] / acc_ref[1, :, A_V_DIM:]))
        y = _attn_epilogue(o, lam_init, g_ref[...], z_ref[rows, :].astype(F32))
        o_ref[rows, :] = y.astype(BF16)
        return 0

    nq = s_len // tq
    first_scores(0)
    lax.fori_loop(0, nq, q_tile, 0)


def _attn_prompt(lam, q3, kt3, v3, rest3, gain, lam_init):
    b, s, w = q3.shape
    h = w // A_V_DIM
    tq = min(s, ATTN_TILE)
    assert s % tq == 0 and tq % (PV_SPLIT * LANES) == 0
    kern = functools.partial(_attn_prompt_kernel, tq=tq, lam_init=lam_init)
    qspec = pl.BlockSpec((None, s, A_V_DIM), lambda bi, hi: (bi, 0, hi))
    return pl.pallas_call(
        kern,
        grid=(b, h),
        in_specs=[
            pl.BlockSpec(memory_space=pltpu.SMEM),
            qspec,
            pl.BlockSpec((None, A_V_DIM, s), lambda bi, hi: (bi, hi, 0)),
            qspec, qspec,
            pl.BlockSpec((1, A_V_DIM), lambda bi, hi: (0, 0)),
        ],
        out_specs=qspec,
        out_shape=jax.ShapeDtypeStruct((b, s, w), BF16),
        scratch_shapes=[
            pltpu.VMEM((A_V_DIM, s), BF16),
            pltpu.VMEM((s, 2 * A_V_DIM), BF16),
            pltpu.VMEM((2, tq, A_V_DIM), BF16),
            pltpu.VMEM((2, tq, tq), F32),
            pltpu.VMEM((2, tq, tq), F32),
            pltpu.VMEM((2, tq, tq), BF16),
            pltpu.VMEM((2, tq, 2 * A_V_DIM), F32),
            pltpu.VMEM((2, tq, LANES), F32),
            pltpu.VMEM((2, tq, LANES), F32),
        ],
        compiler_params=_cparams(("parallel", "parallel")),
        name="attn_prompt",
    )(lam, q3, kt3, v3, rest3, gain)


def _attn_sample_kernel(lam_ref, q_ref, kc_ref, vc_ref, kn_ref, vn_ref, z_ref, g_ref, o_ref,
                        qbd_ref, s_ref, w_ref, acc_ref, *, tk, nkt, t_new, heads, lam_init):
    j = pl.program_id(1)
    past = tk * nkt
    nt = (((1,), (1,)), ((), ()))
    width = kn_ref.shape[1]

    @pl.when(j == 0)
    def _():
        lane = lax.broadcasted_iota(jnp.int32, (t_new, A_V_DIM), 1)
        zero = jnp.zeros((t_new, A_V_DIM), BF16)
        for hd in range(heads):
            qh = q_ref[:, hd * A_V_DIM:(hd + 1) * A_V_DIM].astype(F32)
            for mp in range(2):
                r0 = (2 * hd + mp) * t_new
                mine = lane >= A_HEAD_DIM if mp == 1 else lane < A_HEAD_DIM
                own = jnp.where(mine, qh, 0.0).astype(BF16)
                for hb in range(heads):
                    qbd_ref[r0:r0 + t_new, hb * A_V_DIM:(hb + 1) * A_V_DIM] = own if hb == hd else zero

    def pad_new(ref):
        new = ref[...]
        return jnp.concatenate([new, jnp.zeros((LANES - t_new, width), F32)], axis=0).astype(BF16)

    @pl.when(j < nkt)
    def _():
        s = jnp.dot(qbd_ref[...], kc_ref[...].astype(BF16), preferred_element_type=F32)
        s_ref[:, pl.ds(pl.multiple_of(j * tk, tk), tk)] = s

    @pl.when(j == nkt - 1)
    def _():
        sn = lax.dot_general(qbd_ref[...], pad_new(kn_ref), nt, preferred_element_type=F32)
        col = lax.broadcasted_iota(jnp.int32, sn.shape, 1)
        s_ref[:, past:past + LANES] = jnp.where(col < t_new, sn, -jnp.inf)
        lam = lam_ref[0]
        for hd in range(heads):
            r0 = hd * 2 * t_new
            p = []
            for mp in range(2):
                s = s_ref[r0 + mp * t_new:r0 + (mp + 1) * t_new, :]
                e = jnp.exp2(s - jnp.max(s, axis=1, keepdims=True))
                p.append(e / jnp.sum(e, axis=1, keepdims=True))
            w_ref[hd * t_new:(hd + 1) * t_new, :] = (p[0] - lam * p[1]).astype(BF16)

    @pl.when(j == nkt)
    def _():
        full = jnp.dot(w_ref[:, past:past + LANES], pad_new(vn_ref), preferred_element_type=F32)
        for hd in range(heads):
            rs = slice(hd * t_new, (hd + 1) * t_new)
            acc_ref[rs, :] = full[rs, hd * A_V_DIM:(hd + 1) * A_V_DIM]

    @pl.when(j >= nkt)
    def _():
        start = pl.multiple_of((j - nkt) * tk, tk)
        for hd in range(heads):
            rs = slice(hd * t_new, (hd + 1) * t_new)
            vh = vc_ref[pl.ds(hd, tk, stride=heads), :].astype(BF16)
            acc_ref[rs, :] += jnp.dot(w_ref[rs, pl.ds(start, tk)], vh, preferred_element_type=F32)

    @pl.when(j == 2 * nkt - 1)
    def _():
        for hd in range(heads):
            cs = slice(hd * A_V_DIM, (hd + 1) * A_V_DIM)
            o = acc_ref[hd * t_new:(hd + 1) * t_new, :]
            y = _attn_epilogue(o, lam_init, g_ref[...], z_ref[:, cs].astype(F32))
            o_ref[:, cs] = y.astype(BF16)


def _attn_sample(lam, q3, kct, vc3, kn, vn, rest3, gain, lam_init):
    b, w, past = kct.shape
    t_new = kn.shape[1]
    heads = w // A_V_DIM
    rows = 2 * heads * t_new
    tk = min(past, SAMPLE_KEYS)
    assert past % tk == 0 and t_new % 16 == 0 and t_new <= LANES
    nkt = past // tk
    kern = functools.partial(_attn_sample_kernel, tk=tk, nkt=nkt, t_new=t_new, heads=heads,
                             lam_init=lam_init)
    newspec = pl.BlockSpec((None, t_new, w), lambda bi, j: (bi, 0, 0))
    return pl.pallas_call(
        kern,
        grid=(b, 2 * nkt),
        in_specs=[
            pl.BlockSpec(memory_space=pltpu.SMEM),
            newspec,
            pl.BlockSpec((None, w, tk), lambda bi, j: (bi, 0, jnp.minimum(j, nkt - 1))),
            pl.BlockSpec((None, tk * heads, A_V_DIM),
                         lambda bi, j: (bi, jnp.maximum(j - nkt, 0), 0)),
            newspec, newspec, newspec,
            pl.BlockSpec((1, A_V_DIM), lambda bi, j: (0, 0)),
        ],
        out_specs=newspec,
        out_shape=jax.ShapeDtypeStruct((b, t_new, w), BF16),
        scratch_shapes=[
            pltpu.VMEM((rows, w), BF16),
            pltpu.VMEM((rows, past + LANES), F32),
            pltpu.VMEM((rows // 2, past + LANES), BF16),
            pltpu.VMEM((rows // 2, A_V_DIM), F32),
        ],
        compiler_params=_cparams(("parallel", "arbitrary")),
        name="attn_sample",
    )(lam, q3, kct, vc3, kn, vn, rest3, gain)


def _hgrn_kernel(*refs, blk, nchunk, hp, has_s0):
    if has_s0:
        c_ref, lb_ref, gain_ref, s0_ref, y_ref, sf_ref, st_ref = refs
    else:
        c_ref, lb_ref, gain_ref, y_ref, sf_ref, st_ref = refs
    width = hp * B_KEY_DIM
    q_ref, f_ref, i_ref, og_ref, z_ref = (c_ref.at[0, :, k * width:(k + 1) * width] for k in range(5))
    t = pl.program_id(1)

    @pl.when(t == 0)
    def _():
        for h in range(hp):
            st_ref[h] = s0_ref[h].T if has_s0 else jnp.zeros(st_ref.shape[1:], F32)

    lb = lb_ref[...]
    gain = gain_ref[...]
    row = lax.broadcasted_iota(jnp.int32, (blk, blk), 0)
    col = lax.broadcasted_iota(jnp.int32, (blk, blk), 1)
    causal = col <= row
    tril = jnp.where(causal, 1.0, 0.0).astype(BF16)
    mid = (blk - 1) // 2
    nt = (((1,), (1,)), ((), ()))
    tn = (((0,), (0,)), ((), ()))

    def split3(g):
        hi = g.astype(BF16)
        r1 = g - hi.astype(F32)
        md = r1.astype(BF16)
        lo = (r1 - md.astype(F32)).astype(BF16)
        return hi, md, lo

    chunks = [slice(c * blk, (c + 1) * blk) for c in range(nchunk)]
    f = lb + (1.0 - lb) * jax.nn.sigmoid(f_ref[...].astype(F32))
    parts = split3(jnp.log(f))
    qv = _silu(q_ref[...].astype(F32))
    kv = 1.0 - f
    vv = i_ref[...]
    b = [sum(jnp.dot(tril, p[sl, :], preferred_element_type=F32) for p in parts) for sl in chunks]
    qe, ke, qs, ks, decay = [], [], [], [], []
    for sl, bc in zip(chunks, b):
        b_mid = bc[mid:mid + 1, :]
        b_last = bc[blk - 1:blk, :]
        decay.append(jnp.exp(b_last))
        qe.append((qv[sl, :] * jnp.exp(bc - b_mid)).astype(BF16))
        ke.append((kv[sl, :] * jnp.exp(b_mid - bc)).astype(BF16))
        qs.append((qv[sl, :] * jnp.exp(bc)).astype(BF16))
        ks.append((kv[sl, :] * jnp.exp(b_last - bc)).astype(BF16))
    lanes = [slice(h * B_KEY_DIM, (h + 1) * B_KEY_DIM) for h in range(hp)]
    o_intra, kvs = [], []
    for c, sl in enumerate(chunks):
        sc = [lax.dot_general(qe[c][:, hl], ke[c][:, hl], nt, preferred_element_type=F32)
              for hl in lanes]
        sc = [jnp.where(causal, s, 0.0).astype(BF16) for s in sc]
        o_intra.append([jnp.dot(s, vv[sl, hl], preferred_element_type=F32)
                        for s, hl in zip(sc, lanes)])
        kvs.append([lax.dot_general(vv[sl, hl], ks[c][:, hl], tn, preferred_element_type=F32)
                    for hl in lanes])
    cols = []
    for h, hl in enumerate(lanes):
        state = st_ref[h]
        outs = []
        for c in range(nchunk):
            outs.append(o_intra[c][h] + lax.dot_general(qs[c][:, hl], state.astype(BF16), nt,
                                                        preferred_element_type=F32))
            state = state * decay[c][:, hl] + kvs[c][h]
        st_ref[h] = state
        o = outs[0] if nchunk == 1 else jnp.concatenate(outs, axis=0)
        ms = jnp.mean(o * o, axis=1, keepdims=True)
        cols.append(o * lax.rsqrt(ms + NORM_EPS))
    o = cols[0] if hp == 1 else jnp.concatenate(cols, axis=1)
    y = o * gain * jax.nn.sigmoid(og_ref[...].astype(F32))
    y_ref[...] = (y * _silu(z_ref[...].astype(F32))).astype(BF16)

    @pl.when(t == pl.num_programs(1) - 1)
    def _():
        for h in range(hp):
            sf_ref[h] = st_ref[h].T


def _hgrn(rest3, lb, gain, s0, blk, col0):
    b, t, _ = rest3.shape
    width = lb.shape[1]
    heads = width // B_KEY_DIM
    tt = min(t, HGRN_ROWS)
    assert t % tt == 0 and tt % blk == 0
    has_s0 = s0 is not None
    kern = functools.partial(_hgrn_kernel, blk=blk, nchunk=tt // blk, hp=heads, has_s0=has_s0)
    window = pl.BlockSpec(
        (pl.Element(1), pl.Element(tt), pl.Element(5 * width)),
        lambda bi, ti: (bi, pl.multiple_of(ti * tt, tt), col0 * B_KEY_DIM))
    vecspec = pl.BlockSpec((1, width), lambda bi, ti: (0, 0))
    stspec = pl.BlockSpec((None, heads, B_KEY_DIM, B_KEY_DIM), lambda bi, ti: (bi, 0, 0, 0))
    in_specs = [window, vecspec, vecspec]
    args = [rest3, lb, gain]
    if has_s0:
        in_specs.append(stspec)
        args.append(s0)
    return pl.pallas_call(
        kern,
        grid=(b, t // tt),
        in_specs=in_specs,
        out_specs=[
            pl.BlockSpec((None, tt, width), lambda bi, ti: (bi, ti, 0)),
            stspec,
        ],
        out_shape=[
            jax.ShapeDtypeStruct((b, t, width), BF16),
            jax.ShapeDtypeStruct((b, heads, B_KEY_DIM, B_KEY_DIM), F32),
        ],
        scratch_shapes=[pltpu.VMEM((heads, B_KEY_DIM, B_KEY_DIM), F32)],
        compiler_params=_cparams(("parallel", "arbitrary")),
        name="hgrn",
    )(*args)


def _memattn_kernel(q_ref, z_ref, mk_ref, mv_ref, o_ref, kb_ref, vb_ref, *, heads):
    @pl.when(pl.program_id(1) == 0)
    def _():
        kb_ref[...] = mk_ref[...].astype(BF16)
        vb_ref[...] = mv_ref[...].astype(BF16)

    nt = (((1,), (1,)), ((), ()))
    cols = [slice(hd * C_HEAD_DIM, (hd + 1) * C_HEAD_DIM) for hd in range(heads)]
    s = [lax.dot_general(q_ref[:, cs], kb_ref[:, cs], nt, preferred_element_type=F32)
         * (C_HEAD_DIM ** -0.5) for cs in cols]
    e = [jnp.exp(sh - jnp.max(sh, axis=1, keepdims=True)) for sh in s]
    o = [jnp.dot(eh.astype(BF16), vb_ref[:, cs], preferred_element_type=F32)
         / jnp.sum(eh, axis=1, keepdims=True) for eh, cs in zip(e, cols)]
    for oh, cs in zip(o, cols):
        o_ref[:, cs] = (oh * _silu(z_ref[:, cs].astype(F32))).astype(BF16)


def _memattn(rest3, mk3, mv3, qblock, zblock):
    b, t, _ = rest3.shape
    _, m, c = mk3.shape
    tq = min(t, MEMATTN_ROWS)
    assert t % tq == 0
    kern = functools.partial(_memattn_kernel, heads=c // C_HEAD_DIM)
    memspec = pl.BlockSpec((None, m, c), lambda bi, ti: (bi, 0, 0))
    return pl.pallas_call(
        kern,
        grid=(b, t // tq),
        in_specs=[
            pl.BlockSpec((None, tq, c), lambda bi, ti: (bi, ti, qblock)),
            pl.BlockSpec((None, tq, c), lambda bi, ti: (bi, ti, zblock)),
            memspec, memspec,
        ],
        out_specs=pl.BlockSpec((None, tq, c), lambda bi, ti: (bi, ti, 0)),
        out_shape=jax.ShapeDtypeStruct((b, t, c), BF16),
        scratch_shapes=[pltpu.VMEM((m, c), BF16), pltpu.VMEM((m, c), BF16)],
        compiler_params=_cparams(("parallel", "arbitrary")),
        name="memattn",
    )(rest3, rest3, mk3, mv3)


def _merge_kernel(ya_ref, yb_ref, yc_ref, g_ref, x_ref,
                  wa_ref, wb_ref, wc_ref, wo_ref, lng_ref, lnb_ref, o_ref, *, alpha):
    d = x_ref.shape[1]
    ga_ref, gb_ref, gc_ref = (g_ref.at[:, k * d:(k + 1) * d] for k in range(3))

    def branch(y_ref, w_ref, g_ref):
        return jax.nn.sigmoid(g_ref[...].astype(F32)) * jnp.dot(
            y_ref[...], w_ref[...], preferred_element_type=F32)

    merged = branch(ya_ref, wa_ref, ga_ref) + branch(yb_ref, wb_ref, gb_ref)
    merged = merged + branch(yc_ref, wc_ref, gc_ref)
    sub = jnp.dot(merged.astype(BF16), wo_ref[...], preferred_element_type=F32)
    hres = alpha * x_ref[...] + sub
    mu = jnp.mean(hres, axis=1, keepdims=True)
    cen = hres - mu
    var = jnp.mean(cen * cen, axis=1, keepdims=True)
    o_ref[...] = cen * lax.rsqrt(var + NORM_EPS) * lng_ref[...] + lnb_ref[...]


def _merge(ya, yb, yc, rest2, x2, wa, wb, wc, wo, lng, lnb, gate_block0, alpha):
    n, d = x2.shape
    w = ya.shape[1]
    tm = min(n, MERGE_ROWS)
    assert n % tm == 0
    rows = lambda width, blk: pl.BlockSpec((tm, width), lambda i: (i, blk))
    const = lambda shape: pl.BlockSpec(shape, lambda i: (0, 0), pipeline_mode=pl.Buffered(1))
    return pl.pallas_call(
        functools.partial(_merge_kernel, alpha=alpha),
        grid=(n // tm,),
        in_specs=[
            rows(w, 0), rows(w, 0), rows(w, 0),
            pl.BlockSpec((pl.Element(tm), pl.Element(3 * d)),
                         lambda i: (pl.multiple_of(i * tm, tm), gate_block0 * d)),
            rows(d, 0),
            const((w, d)), const((w, d)), const((w, d)), const((d, d)),
            const((1, d)), const((1, d)),
        ],
        out_specs=rows(d, 0),
        out_shape=jax.ShapeDtypeStruct((n, d), F32),
        compiler_params=_cparams(("parallel",)),
        name="merge",
    )(ya, yb, yc, rest2, x2, wa, wb, wc, wo, lng, lnb)


def _layer(x, pos, past_k, past_v, s0, mk3, mv3, rec_block, layer_idx, lb, p):
    (w_in, lam, sub_norm, hgrn_gain, wa, wb, wc, wo, lng, lnb, alpha) = p
    n_b, t, d = x.shape
    a_width = wa.shape[0]
    heads = a_width // A_V_DIM
    x2 = x.reshape(n_b * t, d)
    prompt = past_k is None
    q_scale = A_HEAD_DIM ** -0.5 * math.log2(math.e)
    q2, k_out, v2, rest2 = _inproj(x2, w_in, _rope_tables(pos), t, a_width, q_scale,
                                   k_transposed=prompt)
    r = rest2.shape[1]
    rest3 = rest2.reshape(n_b, t, r)
    q3 = q2.reshape(n_b, t, a_width)
    v3 = v2.reshape(n_b, t, a_width)
    lam_init = 0.8 - 0.6 * math.exp(-0.3 * layer_idx)
    if prompt:
        ya = _attn_prompt(lam, q3, k_out, v3, rest3, sub_norm, lam_init)
        k5 = k_out.reshape(n_b, heads, 2, A_HEAD_DIM, t).transpose(0, 4, 1, 2, 3)
    else:
        k3 = k_out.reshape(n_b, t, a_width)
        ya = _attn_sample(lam, q3, past_k, past_v, k3, v3, rest3,
                          sub_norm, lam_init)
        k5 = k3.reshape(n_b, t, heads, 2, A_HEAD_DIM)
    yb, s_fin = _hgrn(rest3, lb, hgrn_gain, s0, rec_block, a_width // B_KEY_DIM)
    c_width = mk3.shape[2]
    c_off = a_width + 5 * lb.shape[1]
    assert c_off % c_width == 0
    yc = _memattn(rest3, mk3, mv3, c_off // c_width, c_off // c_width + 1)
    g_off = c_off + 2 * c_width
    assert g_off % d == 0
    y2 = _merge(ya.reshape(n_b * t, a_width), yb.reshape(n_b * t, -1), yc.reshape(n_b * t, c_width),
                rest2, x2, wa, wb, wc, wo, lng, lnb, g_off // d, alpha)
    return y2.reshape(n_b, t, d), k5, v3, s_fin


def kernel(x_prompt, x_sample, cache_attn_k, cache_attn_v, state_hgrn, cache_mem_k, cache_mem_v, mem_prompt, w_in, lambda_q1, lambda_k1, lambda_q2, lambda_k2, attn_sub_norm, hgrn_lb_logits, hgrn_norm, w_mem_k, w_mem_v, w_branch_a, w_branch_b, w_branch_c, w_out, ln_gamma, ln_beta):
    bp, seq, d = x_prompt.shape
    bs, t_new, _ = x_sample.shape
    depth = w_in.shape[0]
    past = cache_attn_k.shape[2]
    heads = cache_attn_k.shape[3]
    a_width = heads * A_V_DIM
    n_mem = mem_prompt.shape[1]
    c_heads = cache_mem_k.shape[3]
    c_width = c_heads * C_HEAD_DIM
    b_heads = state_hgrn.shape[2]
    alpha = (2 * depth) ** 0.25
    pos_prompt = jnp.arange(seq)
    pos_sample = past + jnp.arange(t_new)
    lower_bounds = jnp.cumsum(jax.nn.softmax(hgrn_lb_logits.astype(F32), axis=0), axis=0)

    h_p, h_s = x_prompt, x_sample
    outs = [[] for _ in range(8)]
    for l in range(depth):
        lam_init = 0.8 - 0.6 * math.exp(-0.3 * l)
        lam = (jnp.exp(jnp.sum(lambda_q1[l].astype(F32) * lambda_k1[l].astype(F32)))
               - jnp.exp(jnp.sum(lambda_q2[l].astype(F32) * lambda_k2[l].astype(F32))) + lam_init)
        params = (w_in[l].astype(BF16), lam.reshape(1), attn_sub_norm[l].reshape(1, -1),
                  hgrn_norm[l].reshape(1, -1), w_branch_a[l].astype(BF16), w_branch_b[l].astype(BF16),
                  w_branch_c[l].astype(BF16), w_out[l].astype(BF16), ln_gamma[l].reshape(1, -1),
                  ln_beta[l].reshape(1, -1), alpha)
        lb = lower_bounds[l].reshape(1, -1)
        mk_p, mv_p = _memkv(mem_prompt.reshape(bp * n_mem, d), w_mem_k[l].astype(BF16),
                            w_mem_v[l].astype(BF16))
        mk_p = mk_p.reshape(bp, n_mem, c_width)
        mv_p = mv_p.reshape(bp, n_mem, c_width)
        h_p, k_p, v_p, s_p = _layer(h_p, pos_prompt, None, None, None, mk_p, mv_p, CHUNK, l, lb, params)
        h_s, k_s, v_s, s_s = _layer(
            h_s, pos_sample,
            cache_attn_k[l].transpose(0, 2, 3, 4, 1).reshape(bs, a_width, past),
            cache_attn_v[l].reshape(bs, past * heads, A_V_DIM),
            state_hgrn[l], cache_mem_k[l].reshape(bs, n_mem, c_width),
            cache_mem_v[l].reshape(bs, n_mem, c_width), t_new, l, lb, params)
        new = (k_p, v_p.reshape(bp, seq, heads, A_V_DIM),
               s_p.astype(x_prompt.dtype), mk_p.reshape(bp, n_mem, c_heads, C_HEAD_DIM),
               mv_p.reshape(bp, n_mem, c_heads, C_HEAD_DIM),
               k_s, v_s.reshape(bs, t_new, heads, A_V_DIM),
               s_s.astype(x_sample.dtype))
        for acc, val in zip(outs, new):
            acc.append(val)
    return (h_p, h_s) + tuple(jnp.stack(o) for o in outs)
```

```python
import functools
import math

import jax
import jax.numpy as jnp
from jax import lax
from jax.experimental import pallas as pl
from jax.experimental.pallas import tpu as pltpu

F32 = jnp.float32
BF16 = jnp.bfloat16

CHUNK = 64
A_HEAD_DIM = 64
A_V_DIM = 128
ROT_DIM = 16
ROPE_THETA = 500000.0
B_KEY_DIM = 128
C_HEAD_DIM = 256
NORM_EPS = 1e-5
LANES = 128
VMEM_LIMIT = 56 * 1024 * 1024
INPROJ_ROWS = 1024
QKV_ROWS = 512
INPROJ_COLS = 512
REST_COLS = 2048
MEMKV_ROWS = 512
HGRN_ROWS = 512
MEMATTN_ROWS = 1024
MERGE_ROWS = 256
SAMPLE_KEYS = 2048
ATTN_TILE = 512
ATTN_STRIP = 32
PV_SPLIT = 4
ROW_CHUNKS = 4
NEXT_SCORES_AT = 3


def _cparams(sem):
    return pltpu.CompilerParams(dimension_semantics=sem, vmem_limit_bytes=VMEM_LIMIT)


def _silu(z):
    return z * jax.nn.sigmoid(z)


def _rope_tables(pos):
    half = ROT_DIM // 2
    inv_freq = jnp.power(ROPE_THETA, -jnp.arange(0, ROT_DIM, 2, dtype=F32) / ROT_DIM)
    ang = pos.astype(F32)[:, None] * inv_freq[None, :]
    cos, sin = jnp.cos(ang), jnp.sin(ang)
    t = pos.shape[0]
    pad = jnp.zeros((t, A_HEAD_DIM - ROT_DIM), F32)
    zero = jnp.zeros((t, half), F32)
    c64 = jnp.concatenate([cos, cos, pad + 1.0], axis=1)
    sa64 = jnp.concatenate([-sin, zero, pad], axis=1)
    sb64 = jnp.concatenate([zero, sin, pad], axis=1)
    rep = LANES // A_HEAD_DIM
    rows = tuple(jnp.tile(a, (1, rep)) for a in (c64, sa64, sb64))
    return rows, (cos.T, sin.T)


def _inproj_kernel(x_ref, w_ref, c_ref, sa_ref, sb_ref, ct_ref, st_ref,
                   q_ref, k_ref, v_ref, xb_ref, t_ref, *, tn, a_width, q_scale, k_transposed):
    half = ROT_DIM // 2
    xb_ref[...] = x_ref[...].astype(BF16)

    def project(col0):
        return jnp.dot(xb_ref[...], w_ref[:, col0:col0 + tn], preferred_element_type=F32)

    def rope_group(acc, g):
        xg = acc[:, g * LANES:(g + 1) * LANES]
        up = pltpu.roll(xg, LANES - half, 1)
        dn = pltpu.roll(xg, half, 1)
        return xg * c_ref[...] + up * sa_ref[...] + dn * sb_ref[...]

    for c0 in range(0, a_width, tn):
        acc = project(c0)
        for g in range(tn // LANES):
            cs = slice(c0 + g * LANES, c0 + (g + 1) * LANES)
            q_ref[:, cs] = (rope_group(acc, g) * q_scale).astype(BF16)

    for c0 in range(0, a_width, tn):
        if k_transposed:
            t_ref[c0 // tn] = project(a_width + c0)
            kt = t_ref[c0 // tn].T
            cos, sin = ct_ref[...], st_ref[...]
            for g in range(tn // A_HEAD_DIM):
                r0 = g * A_HEAD_DIM
                lo = kt[r0:r0 + half, :]
                hi = kt[r0 + half:r0 + ROT_DIM, :]
                k_ref[c0 + r0:c0 + r0 + half, :] = lo * cos - hi * sin
                k_ref[c0 + r0 + half:c0 + r0 + ROT_DIM, :] = hi * cos + lo * sin
                k_ref[c0 + r0 + ROT_DIM:c0 + r0 + A_HEAD_DIM, :] = kt[r0 + ROT_DIM:r0 + A_HEAD_DIM, :]
        else:
            acc = project(a_width + c0)
            for g in range(tn // LANES):
                k_ref[:, c0 + g * LANES:c0 + (g + 1) * LANES] = rope_group(acc, g)

    for c0 in range(0, a_width, tn):
        v_ref[:, c0:c0 + tn] = project(2 * a_width + c0)


def _proj_rest_kernel(x_ref, w_ref, r_ref, xb_ref):
    @pl.when(pl.program_id(1) == 0)
    def _():
        xb_ref[...] = x_ref[...].astype(BF16)

    tm = xb_ref.shape[0]
    for r in range(0, tm, tm // ROW_CHUNKS):
        rows = slice(r, r + tm // ROW_CHUNKS)
        r_ref[rows, :] = jnp.dot(xb_ref[rows, :], w_ref[...], preferred_element_type=F32).astype(BF16)


def _proj_rest(x2, w_bf, col0, tm):
    n, d = x2.shape
    ncols = w_bf.shape[1] - col0
    tn = REST_COLS
    assert ncols % tn == 0 and col0 % LANES == 0 and n % tm == 0
    return pl.pallas_call(
        _proj_rest_kernel,
        grid=(n // tm, ncols // tn),
        in_specs=[
            pl.BlockSpec((tm, d), lambda i, j: (i, 0)),
            pl.BlockSpec((pl.Element(d), pl.Element(tn)),
                         lambda i, j: (0, pl.multiple_of(col0 + j * tn, LANES))),
        ],
        out_specs=pl.BlockSpec((tm, tn), lambda i, j: (i, j)),
        out_shape=jax.ShapeDtypeStruct((n, ncols), BF16),
        scratch_shapes=[pltpu.VMEM((tm, d), BF16)],
        compiler_params=_cparams(("parallel", "arbitrary")),
        name="proj_rest",
    )(x2, w_bf)


def _inproj(x2, w_bf, tables, t_len, a_width, q_scale, k_transposed):
    n, d = x2.shape
    n_in = 3 * a_width
    tm = min(n, QKV_ROWS)
    tn = INPROJ_COLS
    assert n % tm == 0 and a_width % tn == 0
    row_tabs, col_tabs = tables
    if t_len >= tm:
        assert t_len % tm == 0
        per = t_len // tm
    else:
        assert tm % t_len == 0 and not k_transposed
        row_tabs = tuple(jnp.tile(a, (tm // t_len, 1)) for a in row_tabs)
        col_tabs = tuple(jnp.tile(a, (1, tm // t_len)) for a in col_tabs)
        per = 1
    row_spec = pl.BlockSpec((tm, LANES), lambda i: (i % per, 0))
    col_spec = pl.BlockSpec((ROT_DIM // 2, tm), lambda i: (0, i % per))
    rows = pl.BlockSpec((tm, a_width), lambda i: (i, 0))
    if k_transposed:
        k_spec = pl.BlockSpec((None, a_width, tm), lambda i: (i // per, 0, i % per))
        k_shape = jax.ShapeDtypeStruct((n // t_len, a_width, t_len), F32)
    else:
        k_spec = rows
        k_shape = jax.ShapeDtypeStruct((n, a_width), F32)
    kern = functools.partial(_inproj_kernel, tn=tn, a_width=a_width, q_scale=q_scale,
                             k_transposed=k_transposed)
    q2, k_out, v2 = pl.pallas_call(
        kern,
        grid=(n // tm,),
        in_specs=[
            pl.BlockSpec((tm, d), lambda i: (i, 0)),
            pl.BlockSpec((d, n_in), lambda i: (0, 0), pipeline_mode=pl.Buffered(1)),
            row_spec, row_spec, row_spec, col_spec, col_spec,
        ],
        out_specs=[rows, k_spec, rows],
        out_shape=[
            jax.ShapeDtypeStruct((n, a_width), BF16),
            k_shape,
            jax.ShapeDtypeStruct((n, a_width), F32),
        ],
        scratch_shapes=[pltpu.VMEM((tm, d), BF16), pltpu.VMEM((a_width // tn, tm, tn), F32)],
        compiler_params=_cparams(("parallel",)),
        name="inproj",
    )(x2, w_bf, *row_tabs, *col_tabs)
    return q2, k_out, v2, _proj_rest(x2, w_bf, n_in, min(n, INPROJ_ROWS))


def _memkv_kernel(x_ref, wk_ref, wv_ref, k_ref, v_ref):
    xb = x_ref[...].astype(BF16)
    k_ref[...] = jnp.dot(xb, wk_ref[...], preferred_element_type=F32)
    v_ref[...] = jnp.dot(xb, wv_ref[...], preferred_element_type=F32)


def _memkv(x2, wk_bf, wv_bf):
    n, d = x2.shape
    c = wk_bf.shape[1]
    tm = min(n, MEMKV_ROWS)
    assert n % tm == 0
    wspec = pl.BlockSpec((d, c), lambda i: (0, 0))
    ospec = pl.BlockSpec((tm, c), lambda i: (i, 0))
    return pl.pallas_call(
        _memkv_kernel,
        grid=(n // tm,),
        in_specs=[pl.BlockSpec((tm, d), lambda i: (i, 0)), wspec, wspec],
        out_specs=[ospec, ospec],
        out_shape=[jax.ShapeDtypeStruct((n, c), F32)] * 2,
        compiler_params=_cparams(("parallel",)),
        name="memkv",
    )(x2, wk_bf, wv_bf)


def _attn_epilogue(o, lam_init, gain, z):
    ms = jnp.mean(o * o, axis=1, keepdims=True)
    y = o * lax.rsqrt(ms + NORM_EPS) * gain * (1.0 - lam_init)
    return y * _silu(z)


def _attn_prompt_kernel(lam_ref, q_ref, kt_ref, v_ref, z_ref, g_ref, o_ref,
                        kb_ref, vb_ref, qa_ref, sa_ref, sb_ref, p_ref, acc_ref, m_ref, al_ref, l_ref,
                        *, tq, lam_init):
    s_len = q_ref.shape[0]
    strip = ATTN_STRIP
    kb_ref[...] = kt_ref[...].astype(BF16)
    vb_ref[...] = v_ref[...].astype(BF16)
    lane = lax.broadcasted_iota(jnp.int32, (tq, A_V_DIM), 1)

    def scores(j, s_ref, maps=(0, 1)):
        kt = kb_ref[:, pl.ds(pl.multiple_of(j * tq, tq), tq)]
        for mp in maps:
            s_ref[mp] = jnp.dot(qa_ref[mp], kt, preferred_element_type=F32)

    def softmax_pv(j, s_ref, masked, nxt_ref=None, first=False):
        start = pl.multiple_of(j * tq, tq)
        grp = tq // PV_SPLIT
        if nxt_ref is not None:
            scores(j + 1, nxt_ref, (0,))
        for mp in range(2):
            for g0 in range(0, tq, grp):
                if nxt_ref is not None and mp * tq + g0 == NEXT_SCORES_AT * grp:
                    scores(j + 1, nxt_ref, (1,))
                nk = g0 + grp if masked else tq
                for r in range(g0, g0 + grp, strip):
                    rs = slice(r, r + strip)
                    sv = s_ref[mp, rs, :nk]
                    if masked:
                        cc = lax.broadcasted_iota(jnp.int32, (strip, nk), 1) // CHUNK
                        sv = jnp.where(cc <= r // CHUNK, sv, -jnp.inf)
                    mx = jnp.max(sv, axis=1, keepdims=True)
                    if first:
                        mn = jnp.broadcast_to(mx, (strip, LANES))
                    else:
                        m_old = m_ref[mp, rs, :]
                        mn = jnp.maximum(m_old, mx)
                        al = jnp.exp2(m_old - mn)
                        al_ref[mp, rs, :] = al
                    m_ref[mp, rs, :] = mn
                    lsum = None
                    for c in range(nk // LANES):
                        cs = slice(c * LANES, (c + 1) * LANES)
                        e = jnp.exp2(sv[:, cs] - mn)
                        p_ref[mp, rs, cs] = e.astype(BF16)
                        lsum = e if lsum is None else lsum + e
                    l_ref[mp, rs, :] = lsum if first else l_ref[mp, rs, :] * al + lsum
                gs = slice(g0, g0 + grp)
                pv = jnp.dot(p_ref[mp, gs, :nk], vb_ref[pl.ds(start, nk), :],
                             preferred_element_type=F32)
                if first:
                    acc_ref[mp, gs, :] = pv
                else:
                    acc_ref[mp, gs, :] = acc_ref[mp, gs, :] * al_ref[mp, gs, :] + pv

    def first_scores(qi):
        qf = q_ref[pl.ds(pl.multiple_of(qi * tq, tq), tq), :].astype(F32)
        qa_ref[0] = jnp.where(lane < A_HEAD_DIM, qf, 0.0).astype(BF16)
        qa_ref[1] = jnp.where(lane >= A_HEAD_DIM, qf, 0.0).astype(BF16)
        scores(0, sa_ref)

    def q_tile(qi, _):
        rows = pl.ds(pl.multiple_of(qi * tq, tq), tq)

        @pl.when(qi > 0)
        def _():
            softmax_pv(0, sa_ref, False, sb_ref, first=True)

        @pl.when(qi == 0)
        def _():
            softmax_pv(0, sa_ref, True, first=True)

        def full_tile(j, carry):
            @pl.when(j % 2 == 0)
            def _():
                softmax_pv(j, sa_ref, False, sb_ref)

            @pl.when(j % 2 == 1)
            def _():
                softmax_pv(j, sb_ref, False, sa_ref)

            return carry

        lax.fori_loop(1, qi, full_tile, 0)

        @pl.when((qi > 0) & (qi % 2 == 0))
        def _():
            softmax_pv(qi, sa_ref, True)

        @pl.when(qi % 2 == 1)
        def _():
            softmax_pv(qi, sb_ref, True)

        first_scores(jnp.minimum(qi + 1, nq - 1))
        l0 = jnp.sum(l_ref[0], axis=1, keepdims=True)
        l1 = jnp.sum(l_ref[1], axis=1, keepdims=True)
        o = acc_ref[0] / l0 - lam_ref[0] * (acc_ref[1] / l1)
        y = _attn_epilogue(o, lam_init, g_ref[...], z_ref[rows, :].astype(F32))
        o_ref[rows, :] = y.astype(BF16)
        return 0

    nq = s_len // tq
    first_scores(0)
    lax.fori_loop(0, nq, q_tile, 0)


def _attn_prompt(lam, q3, kt3, v3, rest3, gain, lam_init):
    b, s, w = q3.shape
    h = w // A_V_DIM
    tq = min(s, ATTN_TILE)
    assert s % tq == 0 and tq % (PV_SPLIT * LANES) == 0
    kern = functools.partial(_attn_prompt_kernel, tq=tq, lam_init=lam_init)
    qspec = pl.BlockSpec((None, s, A_V_DIM), lambda bi, hi: (bi, 0, hi))
    return pl.pallas_call(
        kern,
        grid=(b, h),
        in_specs=[
            pl.BlockSpec(memory_space=pltpu.SMEM),
            qspec,
            pl.BlockSpec((None, A_V_DIM, s), lambda bi, hi: (bi, hi, 0)),
            qspec, qspec,
            pl.BlockSpec((1, A_V_DIM), lambda bi, hi: (0, 0)),
        ],
        out_specs=qspec,
        out_shape=jax.ShapeDtypeStruct((b, s, w), BF16),
        scratch_shapes=[
            pltpu.VMEM((A_V_DIM, s), BF16),
            pltpu.VMEM((s, A_V_DIM), BF16),
            pltpu.VMEM((2, tq, A_V_DIM), BF16),
            pltpu.VMEM((2, tq, tq), F32),
            pltpu.VMEM((2, tq, tq), F32),
            pltpu.VMEM((2, tq, tq), BF16),
            pltpu.VMEM((2, tq, A_V_DIM), F32),
            pltpu.VMEM((2, tq, LANES), F32),
            pltpu.VMEM((2, tq, LANES), F32),
            pltpu.VMEM((2, tq, LANES), F32),
        ],
        compiler_params=_cparams(("parallel", "parallel")),
        name="attn_prompt",
    )(lam, q3, kt3, v3, rest3, gain)


def _attn_sample_kernel(lam_ref, q_ref, kc_ref, vc_ref, kn_ref, vn_ref, z_ref, g_ref, o_ref,
                        qbd_ref, s_ref, w_ref, acc_ref, *, tk, nkt, t_new, heads, lam_init):
    j = pl.program_id(1)
    past = tk * nkt
    nt = (((1,), (1,)), ((), ()))
    width = kn_ref.shape[1]

    @pl.when(j == 0)
    def _():
        lane = lax.broadcasted_iota(jnp.int32, (t_new, A_V_DIM), 1)
        zero = jnp.zeros((t_new, A_V_DIM), BF16)
        for hd in range(heads):
            qh = q_ref[:, hd * A_V_DIM:(hd + 1) * A_V_DIM].astype(F32)
            for mp in range(2):
                r0 = (2 * hd + mp) * t_new
                mine = lane >= A_HEAD_DIM if mp == 1 else lane < A_HEAD_DIM
                own = jnp.where(mine, qh, 0.0).astype(BF16)
                for hb in range(heads):
                    qbd_ref[r0:r0 + t_new, hb * A_V_DIM:(hb + 1) * A_V_DIM] = own if hb == hd else zero

    def pad_new(ref):
        new = ref[...]
        return jnp.concatenate([new, jnp.zeros((LANES - t_new, width), F32)], axis=0).astype(BF16)

    @pl.when(j < nkt)
    def _():
        s = jnp.dot(qbd_ref[...], kc_ref[...].astype(BF16), preferred_element_type=F32)
        s_ref[:, pl.ds(pl.multiple_of(j * tk, tk), tk)] = s

    @pl.when(j == nkt - 1)
    def _():
        sn = lax.dot_general(qbd_ref[...], pad_new(kn_ref), nt, preferred_element_type=F32)
        col = lax.broadcasted_iota(jnp.int32, sn.shape, 1)
        s_ref[:, past:past + LANES] = jnp.where(col < t_new, sn, -jnp.inf)
        lam = lam_ref[0]
        for hd in range(heads):
            r0 = hd * 2 * t_new
            p = []
            for mp in range(2):
                s = s_ref[r0 + mp * t_new:r0 + (mp + 1) * t_new, :]
                e = jnp.exp2(s - jnp.max(s, axis=1, keepdims=True))
                p.append(e / jnp.sum(e, axis=1, keepdims=True))
            w_ref[hd * t_new:(hd + 1) * t_new, :] = (p[0] - lam * p[1]).astype(BF16)

    @pl.when(j == nkt)
    def _():
        full = jnp.dot(w_ref[:, past:past + LANES], pad_new(vn_ref), preferred_element_type=F32)
        for hd in range(heads):
            rs = slice(hd * t_new, (hd + 1) * t_new)
            acc_ref[rs, :] = full[rs, hd * A_V_DIM:(hd + 1) * A_V_DIM]

    @pl.when(j >= nkt)
    def _():
        start = pl.multiple_of((j - nkt) * tk, tk)
        for hd in range(heads):
            rs = slice(hd * t_new, (hd + 1) * t_new)
            vh = vc_ref[pl.ds(hd, tk, stride=heads), :].astype(BF16)
            acc_ref[rs, :] += jnp.dot(w_ref[rs, pl.ds(start, tk)], vh, preferred_element_type=F32)

    @pl.when(j == 2 * nkt - 1)
    def _():
        for hd in range(heads):
            cs = slice(hd * A_V_DIM, (hd + 1) * A_V_DIM)
            o = acc_ref[hd * t_new:(hd + 1) * t_new, :]
            y = _attn_epilogue(o, lam_init, g_ref[...], z_ref[:, cs].astype(F32))
            o_ref[:, cs] = y.astype(BF16)


def _attn_sample(lam, q3, kct, vc3, kn, vn, rest3, gain, lam_init):
    b, w, past = kct.shape
    t_new = kn.shape[1]
    heads = w // A_V_DIM
    rows = 2 * heads * t_new
    tk = min(past, SAMPLE_KEYS)
    assert past % tk == 0 and t_new % 16 == 0 and t_new <= LANES
    nkt = past // tk
    kern = functools.partial(_attn_sample_kernel, tk=tk, nkt=nkt, t_new=t_new, heads=heads,
                             lam_init=lam_init)
    newspec = pl.BlockSpec((None, t_new, w), lambda bi, j: (bi, 0, 0))
    return pl.pallas_call(
        kern,
        grid=(b, 2 * nkt),
        in_specs=[
            pl.BlockSpec(memory_space=pltpu.SMEM),
            newspec,
            pl.BlockSpec((None, w, tk), lambda bi, j: (bi, 0, jnp.minimum(j, nkt - 1))),
            pl.BlockSpec((None, tk * heads, A_V_DIM),
                         lambda bi, j: (bi, jnp.maximum(j - nkt, 0), 0)),
            newspec, newspec, newspec,
            pl.BlockSpec((1, A_V_DIM), lambda bi, j: (0, 0)),
        ],
        out_specs=newspec,
        out_shape=jax.ShapeDtypeStruct((b, t_new, w), BF16),
        scratch_shapes=[
            pltpu.VMEM((rows, w), BF16),
            pltpu.VMEM((rows, past + LANES), F32),
            pltpu.VMEM((rows // 2, past + LANES), BF16),
            pltpu.VMEM((rows // 2, A_V_DIM), F32),
        ],
        compiler_params=_cparams(("parallel", "arbitrary")),
        name="attn_sample",
    )(lam, q3, kct, vc3, kn, vn, rest3, gain)


def _hgrn_kernel(*refs, blk, nchunk, hp, has_s0):
    if has_s0:
        c_ref, lb_ref, gain_ref, s0_ref, y_ref, sf_ref, st_ref = refs
    else:
        c_ref, lb_ref, gain_ref, y_ref, sf_ref, st_ref = refs
    width = hp * B_KEY_DIM
    q_ref, f_ref, i_ref, og_ref, z_ref = (c_ref.at[0, :, k * width:(k + 1) * width] for k in range(5))
    t = pl.program_id(1)

    @pl.when(t == 0)
    def _():
        for h in range(hp):
            st_ref[h] = s0_ref[h].T if has_s0 else jnp.zeros(st_ref.shape[1:], F32)

    lb = lb_ref[...]
    gain = gain_ref[...]
    row = lax.broadcasted_iota(jnp.int32, (blk, blk), 0)
    col = lax.broadcasted_iota(jnp.int32, (blk, blk), 1)
    causal = col <= row
    tril = jnp.where(causal, 1.0, 0.0).astype(BF16)
    mid = (blk - 1) // 2
    nt = (((1,), (1,)), ((), ()))
    tn = (((0,), (0,)), ((), ()))

    def split3(g):
        hi = g.astype(BF16)
        r1 = g - hi.astype(F32)
        md = r1.astype(BF16)
        lo = (r1 - md.astype(F32)).astype(BF16)
        return hi, md, lo

    chunks = [slice(c * blk, (c + 1) * blk) for c in range(nchunk)]
    f = lb + (1.0 - lb) * jax.nn.sigmoid(f_ref[...].astype(F32))
    parts = split3(jnp.log(f))
    qv = _silu(q_ref[...].astype(F32))
    kv = 1.0 - f
    vv = i_ref[...]
    b = [sum(jnp.dot(tril, p[sl, :], preferred_element_type=F32) for p in parts) for sl in chunks]
    qe, ke, qs, ks, decay = [], [], [], [], []
    for sl, bc in zip(chunks, b):
        b_mid = bc[mid:mid + 1, :]
        b_last = bc[blk - 1:blk, :]
        decay.append(jnp.exp(b_last))
        qe.append((qv[sl, :] * jnp.exp(bc - b_mid)).astype(BF16))
        ke.append((kv[sl, :] * jnp.exp(b_mid - bc)).astype(BF16))
        qs.append((qv[sl, :] * jnp.exp(bc)).astype(BF16))
        ks.append((kv[sl, :] * jnp.exp(b_last - bc)).astype(BF16))
    lanes = [slice(h * B_KEY_DIM, (h + 1) * B_KEY_DIM) for h in range(hp)]
    o_intra, kvs = [], []
    for c, sl in enumerate(chunks):
        sc = [lax.dot_general(qe[c][:, hl], ke[c][:, hl], nt, preferred_element_type=F32)
              for hl in lanes]
        sc = [jnp.where(causal, s, 0.0).astype(BF16) for s in sc]
        o_intra.append([jnp.dot(s, vv[sl, hl], preferred_element_type=F32)
                        for s, hl in zip(sc, lanes)])
        kvs.append([lax.dot_general(vv[sl, hl], ks[c][:, hl], tn, preferred_element_type=F32)
                    for hl in lanes])
    cols = []
    for h, hl in enumerate(lanes):
        state = st_ref[h]
        outs = []
        for c in range(nchunk):
            outs.append(o_intra[c][h] + lax.dot_general(qs[c][:, hl], state.astype(BF16), nt,
                                                        preferred_element_type=F32))
            state = state * decay[c][:, hl] + kvs[c][h]
        st_ref[h] = state
        o = outs[0] if nchunk == 1 else jnp.concatenate(outs, axis=0)
        ms = jnp.mean(o * o, axis=1, keepdims=True)
        cols.append(o * lax.rsqrt(ms + NORM_EPS))
    o = cols[0] if hp == 1 else jnp.concatenate(cols, axis=1)
    y = o * gain * jax.nn.sigmoid(og_ref[...].astype(F32))
    y_ref[...] = (y * _silu(z_ref[...].astype(F32))).astype(BF16)

    @pl.when(t == pl.num_programs(1) - 1)
    def _():
        for h in range(hp):
            sf_ref[h] = st_ref[h].T


def _hgrn(rest3, lb, gain, s0, blk, col0):
    b, t, _ = rest3.shape
    width = lb.shape[1]
    heads = width // B_KEY_DIM
    tt = min(t, HGRN_ROWS)
    assert t % tt == 0 and tt % blk == 0
    has_s0 = s0 is not None
    kern = functools.partial(_hgrn_kernel, blk=blk, nchunk=tt // blk, hp=heads, has_s0=has_s0)
    window = pl.BlockSpec(
        (pl.Element(1), pl.Element(tt), pl.Element(5 * width)),
        lambda bi, ti: (bi, pl.multiple_of(ti * tt, tt), col0 * B_KEY_DIM))
    vecspec = pl.BlockSpec((1, width), lambda bi, ti: (0, 0))
    stspec = pl.BlockSpec((None, heads, B_KEY_DIM, B_KEY_DIM), lambda bi, ti: (bi, 0, 0, 0))
    in_specs = [window, vecspec, vecspec]
    args = [rest3, lb, gain]
    if has_s0:
        in_specs.append(stspec)
        args.append(s0)
    return pl.pallas_call(
        kern,
        grid=(b, t // tt),
        in_specs=in_specs,
        out_specs=[
            pl.BlockSpec((None, tt, width), lambda bi, ti: (bi, ti, 0)),
            stspec,
        ],
        out_shape=[
            jax.ShapeDtypeStruct((b, t, width), BF16),
            jax.ShapeDtypeStruct((b, heads, B_KEY_DIM, B_KEY_DIM), F32),
        ],
        scratch_shapes=[pltpu.VMEM((heads, B_KEY_DIM, B_KEY_DIM), F32)],
        compiler_params=_cparams(("parallel", "arbitrary")),
        name="hgrn",
    )(*args)


def _memattn_kernel(q_ref, z_ref, mk_ref, mv_ref, o_ref, kb_ref, vb_ref, *, heads):
    @pl.when(pl.program_id(1) == 0)
    def _():
        kb_ref[...] = mk_ref[...].astype(BF16)
        vb_ref[...] = mv_ref[...].astype(BF16)

    nt = (((1,), (1,)), ((), ()))
    cols = [slice(hd * C_HEAD_DIM, (hd + 1) * C_HEAD_DIM) for hd in range(heads)]
    s = [lax.dot_general(q_ref[:, cs], kb_ref[:, cs], nt, preferred_element_type=F32)
         * (C_HEAD_DIM ** -0.5) for cs in cols]
    e = [jnp.exp(sh - jnp.max(sh, axis=1, keepdims=True)) for sh in s]
    o = [jnp.dot(eh.astype(BF16), vb_ref[:, cs], preferred_element_type=F32)
         / jnp.sum(eh, axis=1, keepdims=True) for eh, cs in zip(e, cols)]
    for oh, cs in zip(o, cols):
        o_ref[:, cs] = (oh * _silu(z_ref[:, cs].astype(F32))).astype(BF16)


def _memattn(rest3, mk3, mv3, qblock, zblock):
    b, t, _ = rest3.shape
    _, m, c = mk3.shape
    tq = min(t, MEMATTN_ROWS)
    assert t % tq == 0
    kern = functools.partial(_memattn_kernel, heads=c // C_HEAD_DIM)
    memspec = pl.BlockSpec((None, m, c), lambda bi, ti: (bi, 0, 0))
    return pl.pallas_call(
        kern,
        grid=(b, t // tq),
        in_specs=[
            pl.BlockSpec((None, tq, c), lambda bi, ti: (bi, ti, qblock)),
            pl.BlockSpec((None, tq, c), lambda bi, ti: (bi, ti, zblock)),
            memspec, memspec,
        ],
        out_specs=pl.BlockSpec((None, tq, c), lambda bi, ti: (bi, ti, 0)),
        out_shape=jax.ShapeDtypeStruct((b, t, c), BF16),
        scratch_shapes=[pltpu.VMEM((m, c), BF16), pltpu.VMEM((m, c), BF16)],
        compiler_params=_cparams(("parallel", "arbitrary")),
        name="memattn",
    )(rest3, rest3, mk3, mv3)


def _merge_kernel(ya_ref, yb_ref, yc_ref, g_ref, x_ref,
                  wa_ref, wb_ref, wc_ref, wo_ref, lng_ref, lnb_ref, o_ref, *, alpha):
    d = x_ref.shape[1]
    ga_ref, gb_ref, gc_ref = (g_ref.at[:, k * d:(k + 1) * d] for k in range(3))

    def branch(y_ref, w_ref, g_ref):
        return jax.nn.sigmoid(g_ref[...].astype(F32)) * jnp.dot(
            y_ref[...], w_ref[...], preferred_element_type=F32)

    merged = branch(ya_ref, wa_ref, ga_ref) + branch(yb_ref, wb_ref, gb_ref)
    merged = merged + branch(yc_ref, wc_ref, gc_ref)
    sub = jnp.dot(merged.astype(BF16), wo_ref[...], preferred_element_type=F32)
    hres = alpha * x_ref[...] + sub
    mu = jnp.mean(hres, axis=1, keepdims=True)
    cen = hres - mu
    var = jnp.mean(cen * cen, axis=1, keepdims=True)
    o_ref[...] = cen * lax.rsqrt(var + NORM_EPS) * lng_ref[...] + lnb_ref[...]


def _merge(ya, yb, yc, rest2, x2, wa, wb, wc, wo, lng, lnb, gate_block0, alpha):
    n, d = x2.shape
    w = ya.shape[1]
    tm = min(n, MERGE_ROWS)
    assert n % tm == 0
    rows = lambda width, blk: pl.BlockSpec((tm, width), lambda i: (i, blk))
    const = lambda shape: pl.BlockSpec(shape, lambda i: (0, 0), pipeline_mode=pl.Buffered(1))
    return pl.pallas_call(
        functools.partial(_merge_kernel, alpha=alpha),
        grid=(n // tm,),
        in_specs=[
            rows(w, 0), rows(w, 0), rows(w, 0),
            pl.BlockSpec((pl.Element(tm), pl.Element(3 * d)),
                         lambda i: (pl.multiple_of(i * tm, tm), gate_block0 * d)),
            rows(d, 0),
            const((w, d)), const((w, d)), const((w, d)), const((d, d)),
            const((1, d)), const((1, d)),
        ],
        out_specs=rows(d, 0),
        out_shape=jax.ShapeDtypeStruct((n, d), F32),
        compiler_params=_cparams(("parallel",)),
        name="merge",
    )(ya, yb, yc, rest2, x2, wa, wb, wc, wo, lng, lnb)


def _layer(x, pos, past_k, past_v, s0, mk3, mv3, rec_block, layer_idx, lb, p):
    (w_in, lam, sub_norm, hgrn_gain, wa, wb, wc, wo, lng, lnb, alpha) = p
    n_b, t, d = x.shape
    a_width = wa.shape[0]
    heads = a_width // A_V_DIM
    x2 = x.reshape(n_b * t, d)
    prompt = past_k is None
    q_scale = A_HEAD_DIM ** -0.5 * math.log2(math.e)
    q2, k_out, v2, rest2 = _inproj(x2, w_in, _rope_tables(pos), t, a_width, q_scale,
                                   k_transposed=prompt)
    r = rest2.shape[1]
    rest3 = rest2.reshape(n_b, t, r)
    q3 = q2.reshape(n_b, t, a_width)
    v3 = v2.reshape(n_b, t, a_width)
    lam_init = 0.8 - 0.6 * math.exp(-0.3 * layer_idx)
    if prompt:
        ya = _attn_prompt(lam, q3, k_out, v3, rest3, sub_norm, lam_init)
        k5 = k_out.reshape(n_b, heads, 2, A_HEAD_DIM, t).transpose(0, 4, 1, 2, 3)
    else:
        k3 = k_out.reshape(n_b, t, a_width)
        ya = _attn_sample(lam, q3, past_k, past_v, k3, v3, rest3,
                          sub_norm, lam_init)
        k5 = k3.reshape(n_b, t, heads, 2, A_HEAD_DIM)
    yb, s_fin = _hgrn(rest3, lb, hgrn_gain, s0, rec_block, a_width // B_KEY_DIM)
    c_width = mk3.shape[2]
    c_off = a_width + 5 * lb.shape[1]
    assert c_off % c_width == 0
    yc = _memattn(rest3, mk3, mv3, c_off // c_width, c_off // c_width + 1)
    g_off = c_off + 2 * c_width
    assert g_off % d == 0
    y2 = _merge(ya.reshape(n_b * t, a_width), yb.reshape(n_b * t, -1), yc.reshape(n_b * t, c_width),
                rest2, x2, wa, wb, wc, wo, lng, lnb, g_off // d, alpha)
    return y2.reshape(n_b, t, d), k5, v3, s_fin


def kernel(x_prompt, x_sample, cache_attn_k, cache_attn_v, state_hgrn, cache_mem_k, cache_mem_v, mem_prompt, w_in, lambda_q1, lambda_k1, lambda_q2, lambda_k2, attn_sub_norm, hgrn_lb_logits, hgrn_norm, w_mem_k, w_mem_v, w_branch_a, w_branch_b, w_branch_c, w_out, ln_gamma, ln_beta):
    bp, seq, d = x_prompt.shape
    bs, t_new, _ = x_sample.shape
    depth = w_in.shape[0]
    past = cache_attn_k.shape[2]
    heads = cache_attn_k.shape[3]
    a_width = heads * A_V_DIM
    n_mem = mem_prompt.shape[1]
    c_heads = cache_mem_k.shape[3]
    c_width = c_heads * C_HEAD_DIM
    b_heads = state_hgrn.shape[2]
    alpha = (2 * depth) ** 0.25
    pos_prompt = jnp.arange(seq)
    pos_sample = past + jnp.arange(t_new)
    lower_bounds = jnp.cumsum(jax.nn.softmax(hgrn_lb_logits.astype(F32), axis=0), axis=0)

    h_p, h_s = x_prompt, x_sample
    outs = [[] for _ in range(8)]
    for l in range(depth):
        lam_init = 0.8 - 0.6 * math.exp(-0.3 * l)
        lam = (jnp.exp(jnp.sum(lambda_q1[l].astype(F32) * lambda_k1[l].astype(F32)))
               - jnp.exp(jnp.sum(lambda_q2[l].astype(F32) * lambda_k2[l].astype(F32))) + lam_init)
        params = (w_in[l].astype(BF16), lam.reshape(1), attn_sub_norm[l].reshape(1, -1),
                  hgrn_norm[l].reshape(1, -1), w_branch_a[l].astype(BF16), w_branch_b[l].astype(BF16),
                  w_branch_c[l].astype(BF16), w_out[l].astype(BF16), ln_gamma[l].reshape(1, -1),
                  ln_beta[l].reshape(1, -1), alpha)
        lb = lower_bounds[l].reshape(1, -1)
        mk_p, mv_p = _memkv(mem_prompt.reshape(bp * n_mem, d), w_mem_k[l].astype(BF16),
                            w_mem_v[l].astype(BF16))
        mk_p = mk_p.reshape(bp, n_mem, c_width)
        mv_p = mv_p.reshape(bp, n_mem, c_width)
        h_p, k_p, v_p, s_p = _layer(h_p, pos_prompt, None, None, None, mk_p, mv_p, CHUNK, l, lb, params)
        h_s, k_s, v_s, s_s = _layer(
            h_s, pos_sample,
            cache_attn_k[l].transpose(0, 2, 3, 4, 1).reshape(bs, a_width, past),
            cache_attn_v[l].reshape(bs, past * heads, A_V_DIM),
            state_hgrn[l], cache_mem_k[l].reshape(bs, n_mem, c_width),
            cache_mem_v[l].reshape(bs, n_mem, c_width), t_new, l, lb, params)
        new = (k_p, v_p.reshape(bp, seq, heads, A_V_DIM),
               s_p.astype(x_prompt.dtype), mk_p.reshape(bp, n_mem, c_heads, C_HEAD_DIM),
               mv_p.reshape(bp, n_mem, c_heads, C_HEAD_DIM),
               k_s, v_s.reshape(bs, t_new, heads, A_V_DIM),
               s_s.astype(x_sample.dtype))
        for acc, val in zip(outs, new):
            acc.append(val)
    return (h_p, h_s) + tuple(jnp.stack(o) for o in outs)
```

```python
import functools
import math

import jax
import jax.numpy as jnp
from jax import lax
from jax.experimental import pallas as pl
from jax.experimental.pallas import tpu as pltpu

F32 = jnp.float32
BF16 = jnp.bfloat16

CHUNK = 64
A_HEAD_DIM = 64
A_V_DIM = 128
ROT_DIM = 16
ROPE_THETA = 500000.0
B_KEY_DIM = 128
C_HEAD_DIM = 256
NORM_EPS = 1e-5
LANES = 128
VMEM_LIMIT = 56 * 1024 * 1024
INPROJ_ROWS = 2048
QKV_ROWS = 512
INPROJ_COLS = 512
REST_COLS = 2048
MEMKV_ROWS = 512
HGRN_ROWS = 512
MEMATTN_ROWS = 1024
MERGE_ROWS = 256
SAMPLE_KEYS = 2048
ATTN_TILE = 512
ATTN_STRIP = 32
PV_SPLIT = 4
ROW_CHUNKS = 4
NEXT_SCORES_AT = 3


def _cparams(sem):
    return pltpu.CompilerParams(dimension_semantics=sem, vmem_limit_bytes=VMEM_LIMIT)


def _silu(z):
    return z * jax.nn.sigmoid(z)


def _rope_tables(pos):
    half = ROT_DIM // 2
    inv_freq = jnp.power(ROPE_THETA, -jnp.arange(0, ROT_DIM, 2, dtype=F32) / ROT_DIM)
    ang = pos.astype(F32)[:, None] * inv_freq[None, :]
    cos, sin = jnp.cos(ang), jnp.sin(ang)
    t = pos.shape[0]
    pad = jnp.zeros((t, A_HEAD_DIM - ROT_DIM), F32)
    zero = jnp.zeros((t, half), F32)
    c64 = jnp.concatenate([cos, cos, pad + 1.0], axis=1)
    sa64 = jnp.concatenate([-sin, zero, pad], axis=1)
    sb64 = jnp.concatenate([zero, sin, pad], axis=1)
    rep = LANES // A_HEAD_DIM
    rows = tuple(jnp.tile(a, (1, rep)) for a in (c64, sa64, sb64))
    return rows, (cos.T, sin.T)


def _inproj_kernel(x_ref, w_ref, c_ref, sa_ref, sb_ref, ct_ref, st_ref,
                   q_ref, k_ref, v_ref, xb_ref, t_ref, *, tn, a_width, q_scale, k_transposed):
    half = ROT_DIM // 2
    xb_ref[...] = x_ref[...].astype(BF16)

    def project(col0):
        return jnp.dot(xb_ref[...], w_ref[:, col0:col0 + tn], preferred_element_type=F32)

    def rope_group(acc, g):
        xg = acc[:, g * LANES:(g + 1) * LANES]
        up = pltpu.roll(xg, LANES - half, 1)
        dn = pltpu.roll(xg, half, 1)
        return xg * c_ref[...] + up * sa_ref[...] + dn * sb_ref[...]

    for c0 in range(0, a_width, tn):
        acc = project(c0)
        for g in range(tn // LANES):
            cs = slice(c0 + g * LANES, c0 + (g + 1) * LANES)
            q_ref[:, cs] = (rope_group(acc, g) * q_scale).astype(BF16)

    for c0 in range(0, a_width, tn):
        if k_transposed:
            t_ref[c0 // tn] = project(a_width + c0)
            kt = t_ref[c0 // tn].T
            cos, sin = ct_ref[...], st_ref[...]
            for g in range(tn // A_HEAD_DIM):
                r0 = g * A_HEAD_DIM
                lo = kt[r0:r0 + half, :]
                hi = kt[r0 + half:r0 + ROT_DIM, :]
                k_ref[c0 + r0:c0 + r0 + half, :] = lo * cos - hi * sin
                k_ref[c0 + r0 + half:c0 + r0 + ROT_DIM, :] = hi * cos + lo * sin
                k_ref[c0 + r0 + ROT_DIM:c0 + r0 + A_HEAD_DIM, :] = kt[r0 + ROT_DIM:r0 + A_HEAD_DIM, :]
        else:
            acc = project(a_width + c0)
            for g in range(tn // LANES):
                k_ref[:, c0 + g * LANES:c0 + (g + 1) * LANES] = rope_group(acc, g)

    for c0 in range(0, a_width, tn):
        v_ref[:, c0:c0 + tn] = project(2 * a_width + c0)


def _proj_rest_kernel(xb_ref, w_ref, r_ref):
    tm = xb_ref.shape[0]
    for r in range(0, tm, tm // ROW_CHUNKS):
        rows = slice(r, r + tm // ROW_CHUNKS)
        r_ref[rows, :] = jnp.dot(xb_ref[rows, :], w_ref[...], preferred_element_type=F32).astype(BF16)


def _proj_rest(xb2, w_bf, col0):
    n, d = xb2.shape
    ncols = w_bf.shape[1] - col0
    tm = min(n, INPROJ_ROWS)
    tn = REST_COLS
    assert ncols % tn == 0 and col0 % LANES == 0 and n % tm == 0
    return pl.pallas_call(
        _proj_rest_kernel,
        grid=(n // tm, ncols // tn),
        in_specs=[
            pl.BlockSpec((tm, d), lambda i, j: (i, 0)),
            pl.BlockSpec((pl.Element(d), pl.Element(tn)),
                         lambda i, j: (0, pl.multiple_of(col0 + j * tn, LANES))),
        ],
        out_specs=pl.BlockSpec((tm, tn), lambda i, j: (i, j)),
        out_shape=jax.ShapeDtypeStruct((n, ncols), BF16),
        compiler_params=_cparams(("parallel", "arbitrary")),
        name="proj_rest",
    )(xb2, w_bf)


def _inproj(x2, w_bf, tables, t_len, a_width, q_scale, k_transposed):
    n, d = x2.shape
    n_in = 3 * a_width
    tm = min(n, QKV_ROWS)
    tn = INPROJ_COLS
    assert n % tm == 0 and a_width % tn == 0
    row_tabs, col_tabs = tables
    if t_len >= tm:
        assert t_len % tm == 0
        per = t_len // tm
    else:
        assert tm % t_len == 0 and not k_transposed
        row_tabs = tuple(jnp.tile(a, (tm // t_len, 1)) for a in row_tabs)
        col_tabs = tuple(jnp.tile(a, (1, tm // t_len)) for a in col_tabs)
        per = 1
    row_spec = pl.BlockSpec((tm, LANES), lambda i: (i % per, 0))
    col_spec = pl.BlockSpec((ROT_DIM // 2, tm), lambda i: (0, i % per))
    rows = pl.BlockSpec((tm, a_width), lambda i: (i, 0))
    if k_transposed:
        k_spec = pl.BlockSpec((None, a_width, tm), lambda i: (i // per, 0, i % per))
        k_shape = jax.ShapeDtypeStruct((n // t_len, a_width, t_len), F32)
    else:
        k_spec = rows
        k_shape = jax.ShapeDtypeStruct((n, a_width), F32)
    kern = functools.partial(_inproj_kernel, tn=tn, a_width=a_width, q_scale=q_scale,
                             k_transposed=k_transposed)
    q2, k_out, v2, xb2 = pl.pallas_call(
        kern,
        grid=(n // tm,),
        in_specs=[
            pl.BlockSpec((tm, d), lambda i: (i, 0)),
            pl.BlockSpec((d, n_in), lambda i: (0, 0), pipeline_mode=pl.Buffered(1)),
            row_spec, row_spec, row_spec, col_spec, col_spec,
        ],
        out_specs=[rows, k_spec, rows, pl.BlockSpec((tm, d), lambda i: (i, 0))],
        out_shape=[
            jax.ShapeDtypeStruct((n, a_width), BF16),
            k_shape,
            jax.ShapeDtypeStruct((n, a_width), F32),
            jax.ShapeDtypeStruct((n, d), BF16),
        ],
        scratch_shapes=[pltpu.VMEM((a_width // tn, tm, tn), F32)],
        compiler_params=_cparams(("parallel",)),
        name="inproj",
    )(x2, w_bf, *row_tabs, *col_tabs)
    return q2, k_out, v2, _proj_rest(xb2, w_bf, n_in)


def _memkv_kernel(x_ref, wk_ref, wv_ref, k_ref, v_ref):
    xb = x_ref[...].astype(BF16)
    k_ref[...] = jnp.dot(xb, wk_ref[...], preferred_element_type=F32)
    v_ref[...] = jnp.dot(xb, wv_ref[...], preferred_element_type=F32)


def _memkv(x2, wk_bf, wv_bf):
    n, d = x2.shape
    c = wk_bf.shape[1]
    tm = min(n, MEMKV_ROWS)
    assert n % tm == 0
    wspec = pl.BlockSpec((d, c), lambda i: (0, 0))
    ospec = pl.BlockSpec((tm, c), lambda i: (i, 0))
    return pl.pallas_call(
        _memkv_kernel,
        grid=(n // tm,),
        in_specs=[pl.BlockSpec((tm, d), lambda i: (i, 0)), wspec, wspec],
        out_specs=[ospec, ospec],
        out_shape=[jax.ShapeDtypeStruct((n, c), F32)] * 2,
        compiler_params=_cparams(("parallel",)),
        name="memkv",
    )(x2, wk_bf, wv_bf)


def _attn_epilogue(o, lam_init, gain, z):
    ms = jnp.mean(o * o, axis=1, keepdims=True)
    y = o * lax.rsqrt(ms + NORM_EPS) * gain * (1.0 - lam_init)
    return y * _silu(z)


def _attn_prompt_kernel(lam_ref, q_ref, kt_ref, v_ref, z_ref, g_ref, o_ref,
                        kb_ref, vb_ref, qa_ref, sa_ref, sb_ref, p_ref, acc_ref, m_ref, al_ref,
                        *, tq, lam_init):
    s_len = q_ref.shape[0]
    strip = ATTN_STRIP
    kb_ref[...] = kt_ref[...].astype(BF16)
    vb_ref[:, :A_V_DIM] = v_ref[...].astype(BF16)
    vb_ref[:, A_V_DIM:] = jnp.ones((s_len, A_V_DIM), BF16)
    lane = lax.broadcasted_iota(jnp.int32, (tq, A_V_DIM), 1)

    def scores(j, s_ref, maps=(0, 1)):
        kt = kb_ref[:, pl.ds(pl.multiple_of(j * tq, tq), tq)]
        for mp in maps:
            s_ref[mp] = jnp.dot(qa_ref[mp], kt, preferred_element_type=F32)

    def softmax_pv(j, s_ref, masked, nxt_ref=None, first=False):
        start = pl.multiple_of(j * tq, tq)
        grp = tq // PV_SPLIT
        if nxt_ref is not None:
            scores(j + 1, nxt_ref, (0,))
        for mp in range(2):
            for g0 in range(0, tq, grp):
                if nxt_ref is not None and mp * tq + g0 == NEXT_SCORES_AT * grp:
                    scores(j + 1, nxt_ref, (1,))
                nk = g0 + grp if masked else tq
                for r in range(g0, g0 + grp, strip):
                    rs = slice(r, r + strip)
                    sv = s_ref[mp, rs, :nk]
                    if masked:
                        cc = lax.broadcasted_iota(jnp.int32, (strip, nk), 1) // CHUNK
                        sv = jnp.where(cc <= r // CHUNK, sv, -jnp.inf)
                    mx = jnp.max(sv, axis=1, keepdims=True)
                    if first:
                        mn = jnp.broadcast_to(mx, (strip, LANES))
                    else:
                        m_old = m_ref[mp, rs, :]
                        mn = jnp.maximum(m_old, mx)
                        al_ref[mp, rs, :] = jnp.exp2(m_old - mn)
                    m_ref[mp, rs, :] = mn
                    for c in range(nk // LANES):
                        cs = slice(c * LANES, (c + 1) * LANES)
                        p_ref[mp, rs, cs] = jnp.exp2(sv[:, cs] - mn).astype(BF16)
                gs = slice(g0, g0 + grp)
                pv = jnp.dot(p_ref[mp, gs, :nk], vb_ref[pl.ds(start, nk), :],
                             preferred_element_type=F32)
                for c in range(2):
                    cs = slice(c * A_V_DIM, (c + 1) * A_V_DIM)
                    if first:
                        acc_ref[mp, gs, cs] = pv[:, cs]
                    else:
                        acc_ref[mp, gs, cs] = acc_ref[mp, gs, cs] * al_ref[mp, gs, :] + pv[:, cs]

    def first_scores(qi):
        qf = q_ref[pl.ds(pl.multiple_of(qi * tq, tq), tq), :].astype(F32)
        qa_ref[0] = jnp.where(lane < A_HEAD_DIM, qf, 0.0).astype(BF16)
        qa_ref[1] = jnp.where(lane >= A_HEAD_DIM, qf, 0.0).astype(BF16)
        scores(0, sa_ref)

    def q_tile(qi, _):
        rows = pl.ds(pl.multiple_of(qi * tq, tq), tq)

        @pl.when(qi > 0)
        def _():
            softmax_pv(0, sa_ref, False, sb_ref, first=True)

        @pl.when(qi == 0)
        def _():
            softmax_pv(0, sa_ref, True, first=True)

        def full_tile(j, carry):
            @pl.when(j % 2 == 0)
            def _():
                softmax_pv(j, sa_ref, False, sb_ref)

            @pl.when(j % 2 == 1)
            def _():
                softmax_pv(j, sb_ref, False, sa_ref)

            return carry

        lax.fori_loop(1, qi, full_tile, 0)

        @pl.when((qi > 0) & (qi % 2 == 0))
        def _():
            softmax_pv(qi, sa_ref, True)

        @pl.when(qi % 2 == 1)
        def _():
            softmax_pv(qi, sb_ref, True)

        first_scores(jnp.minimum(qi + 1, nq - 1))
        o = (acc_ref[0, :, :A_V_DIM] / acc_ref[0, :, A_V_DIM:]
             - lam_ref[0] * (acc_ref[1, :, :A_V_DIM] / acc_ref[1, :, A_V_DIM:]))
        y = _attn_epilogue(o, lam_init, g_ref[...], z_ref[rows, :].astype(F32))
        o_ref[rows, :] = y.astype(BF16)
        return 0

    nq = s_len // tq
    first_scores(0)
    lax.fori_loop(0, nq, q_tile, 0)


def _attn_prompt(lam, q3, kt3, v3, rest3, gain, lam_init):
    b, s, w = q3.shape
    h = w // A_V_DIM
    tq = min(s, ATTN_TILE)
    assert s % tq == 0 and tq % (PV_SPLIT * LANES) == 0
    kern = functools.partial(_attn_prompt_kernel, tq=tq, lam_init=lam_init)
    qspec = pl.BlockSpec((None, s, A_V_DIM), lambda bi, hi: (bi, 0, hi))
    return pl.pallas_call(
        kern,
        grid=(b, h),
        in_specs=[
            pl.BlockSpec(memory_space=pltpu.SMEM),
            qspec,
            pl.BlockSpec((None, A_V_DIM, s), lambda bi, hi: (bi, hi, 0)),
            qspec, qspec,
            pl.BlockSpec((1, A_V_DIM), lambda bi, hi: (0, 0)),
        ],
        out_specs=qspec,
        out_shape=jax.ShapeDtypeStruct((b, s, w), BF16),
        scratch_shapes=[
            pltpu.VMEM((A_V_DIM, s), BF16),
            pltpu.VMEM((s, 2 * A_V_DIM), BF16),
            pltpu.VMEM((2, tq, A_V_DIM), BF16),
            pltpu.VMEM((2, tq, tq), F32),
            pltpu.VMEM((2, tq, tq), F32),
            pltpu.VMEM((2, tq, tq), BF16),
            pltpu.VMEM((2, tq, 2 * A_V_DIM), F32),
            pltpu.VMEM((2, tq, LANES), F32),
            pltpu.VMEM((2, tq, LANES), F32),
        ],
        compiler_params=_cparams(("parallel", "parallel")),
        name="attn_prompt",
    )(lam, q3, kt3, v3, rest3, gain)


def _attn_sample_kernel(lam_ref, q_ref, kc_ref, vc_ref, kn_ref, vn_ref, z_ref, g_ref, o_ref,
                        qbd_ref, s_ref, w_ref, acc_ref, *, tk, nkt, t_new, heads, lam_init):
    j = pl.program_id(1)
    past = tk * nkt
    nt = (((1,), (1,)), ((), ()))
    width = kn_ref.shape[1]

    @pl.when(j == 0)
    def _():
        lane = lax.broadcasted_iota(jnp.int32, (t_new, A_V_DIM), 1)
        zero = jnp.zeros((t_new, A_V_DIM), BF16)
        for hd in range(heads):
            qh = q_ref[:, hd * A_V_DIM:(hd + 1) * A_V_DIM].astype(F32)
            for mp in range(2):
                r0 = (2 * hd + mp) * t_new
                mine = lane >= A_HEAD_DIM if mp == 1 else lane < A_HEAD_DIM
                own = jnp.where(mine, qh, 0.0).astype(BF16)
                for hb in range(heads):
                    qbd_ref[r0:r0 + t_new, hb * A_V_DIM:(hb + 1) * A_V_DIM] = own if hb == hd else zero

    def pad_new(ref):
        new = ref[...]
        return jnp.concatenate([new, jnp.zeros((LANES - t_new, width), F32)], axis=0).astype(BF16)

    @pl.when(j < nkt)
    def _():
        s = jnp.dot(qbd_ref[...], kc_ref[...].astype(BF16), preferred_element_type=F32)
        s_ref[:, pl.ds(pl.multiple_of(j * tk, tk), tk)] = s

    @pl.when(j == nkt - 1)
    def _():
        sn = lax.dot_general(qbd_ref[...], pad_new(kn_ref), nt, preferred_element_type=F32)
        col = lax.broadcasted_iota(jnp.int32, sn.shape, 1)
        s_ref[:, past:past + LANES] = jnp.where(col < t_new, sn, -jnp.inf)
        lam = lam_ref[0]
        for hd in range(heads):
            r0 = hd * 2 * t_new
            p = []
            for mp in range(2):
                s = s_ref[r0 + mp * t_new:r0 + (mp + 1) * t_new, :]
                e = jnp.exp2(s - jnp.max(s, axis=1, keepdims=True))
                p.append(e / jnp.sum(e, axis=1, keepdims=True))
            w_ref[hd * t_new:(hd + 1) * t_new, :] = (p[0] - lam * p[1]).astype(BF16)

    @pl.when(j == nkt)
    def _():
        full = jnp.dot(w_ref[:, past:past + LANES], pad_new(vn_ref), preferred_element_type=F32)
        for hd in range(heads):
            rs = slice(hd * t_new, (hd + 1) * t_new)
            acc_ref[rs, :] = full[rs, hd * A_V_DIM:(hd + 1) * A_V_DIM]

    @pl.when(j >= nkt)
    def _():
        start = pl.multiple_of((j - nkt) * tk, tk)
        for hd in range(heads):
            rs = slice(hd * t_new, (hd + 1) * t_new)
            vh = vc_ref[pl.ds(hd, tk, stride=heads), :].astype(BF16)
            acc_ref[rs, :] += jnp.dot(w_ref[rs, pl.ds(start, tk)], vh, preferred_element_type=F32)

    @pl.when(j == 2 * nkt - 1)
    def _():
        for hd in range(heads):
            cs = slice(hd * A_V_DIM, (hd + 1) * A_V_DIM)
            o = acc_ref[hd * t_new:(hd + 1) * t_new, :]
            y = _attn_epilogue(o, lam_init, g_ref[...], z_ref[:, cs].astype(F32))
            o_ref[:, cs] = y.astype(BF16)


def _attn_sample(lam, q3, kct, vc3, kn, vn, rest3, gain, lam_init):
    b, w, past = kct.shape
    t_new = kn.shape[1]
    heads = w // A_V_DIM
    rows = 2 * heads * t_new
    tk = min(past, SAMPLE_KEYS)
    assert past % tk == 0 and t_new % 16 == 0 and t_new <= LANES
    nkt = past // tk
    kern = functools.partial(_attn_sample_kernel, tk=tk, nkt=nkt, t_new=t_new, heads=heads,
                             lam_init=lam_init)
    newspec = pl.BlockSpec((None, t_new, w), lambda bi, j: (bi, 0, 0))
    return pl.pallas_call(
        kern,
        grid=(b, 2 * nkt),
        in_specs=[
            pl.BlockSpec(memory_space=pltpu.SMEM),
            newspec,
            pl.BlockSpec((None, w, tk), lambda bi, j: (bi, 0, jnp.minimum(j, nkt - 1))),
            pl.BlockSpec((None, tk * heads, A_V_DIM),
                         lambda bi, j: (bi, jnp.maximum(j - nkt, 0), 0)),
            newspec, newspec, newspec,
            pl.BlockSpec((1, A_V_DIM), lambda bi, j: (0, 0)),
        ],
        out_specs=newspec,
        out_shape=jax.ShapeDtypeStruct((b, t_new, w), BF16),
        scratch_shapes=[
            pltpu.VMEM((rows, w), BF16),
            pltpu.VMEM((rows, past + LANES), F32),
            pltpu.VMEM((rows // 2, past + LANES), BF16),
            pltpu.VMEM((rows // 2, A_V_DIM), F32),
        ],
        compiler_params=_cparams(("parallel", "arbitrary")),
        name="attn_sample",
    )(lam, q3, kct, vc3, kn, vn, rest3, gain)


def _hgrn_kernel(*refs, blk, nchunk, hp, has_s0):
    if has_s0:
        c_ref, lb_ref, gain_ref, s0_ref, y_ref, sf_ref, st_ref = refs
    else:
        c_ref, lb_ref, gain_ref, y_ref, sf_ref, st_ref = refs
    width = hp * B_KEY_DIM
    q_ref, f_ref, i_ref, og_ref, z_ref = (c_ref.at[0, :, k * width:(k + 1) * width] for k in range(5))
    t = pl.program_id(1)

    @pl.when(t == 0)
    def _():
        for h in range(hp):
            st_ref[h] = s0_ref[h].T if has_s0 else jnp.zeros(st_ref.shape[1:], F32)

    lb = lb_ref[...]
    gain = gain_ref[...]
    row = lax.broadcasted_iota(jnp.int32, (blk, blk), 0)
    col = lax.broadcasted_iota(jnp.int32, (blk, blk), 1)
    causal = col <= row
    tril = jnp.where(causal, 1.0, 0.0).astype(BF16)
    mid = (blk - 1) // 2
    nt = (((1,), (1,)), ((), ()))
    tn = (((0,), (0,)), ((), ()))

    def split3(g):
        hi = g.astype(BF16)
        r1 = g - hi.astype(F32)
        md = r1.astype(BF16)
        lo = (r1 - md.astype(F32)).astype(BF16)
        return hi, md, lo

    chunks = [slice(c * blk, (c + 1) * blk) for c in range(nchunk)]
    f = lb + (1.0 - lb) * jax.nn.sigmoid(f_ref[...].astype(F32))
    parts = split3(jnp.log(f))
    qv = _silu(q_ref[...].astype(F32))
    kv = 1.0 - f
    vv = i_ref[...]
    b = [sum(jnp.dot(tril, p[sl, :], preferred_element_type=F32) for p in parts) for sl in chunks]
    qe, ke, qs, ks, decay = [], [], [], [], []
    for sl, bc in zip(chunks, b):
        b_mid = bc[mid:mid + 1, :]
        b_last = bc[blk - 1:blk, :]
        decay.append(jnp.exp(b_last))
        qe.append((qv[sl, :] * jnp.exp(bc - b_mid)).astype(BF16))
        ke.append((kv[sl, :] * jnp.exp(b_mid - bc)).astype(BF16))
        qs.append((qv[sl, :] * jnp.exp(bc)).astype(BF16))
        ks.append((kv[sl, :] * jnp.exp(b_last - bc)).astype(BF16))
    lanes = [slice(h * B_KEY_DIM, (h + 1) * B_KEY_DIM) for h in range(hp)]
    o_intra, kvs = [], []
    for c, sl in enumerate(chunks):
        sc = [lax.dot_general(qe[c][:, hl], ke[c][:, hl], nt, preferred_element_type=F32)
              for hl in lanes]
        sc = [jnp.where(causal, s, 0.0).astype(BF16) for s in sc]
        o_intra.append([jnp.dot(s, vv[sl, hl], preferred_element_type=F32)
                        for s, hl in zip(sc, lanes)])
        kvs.append([lax.dot_general(vv[sl, hl], ks[c][:, hl], tn, preferred_element_type=F32)
                    for hl in lanes])
    cols = []
    for h, hl in enumerate(lanes):
        state = st_ref[h]
        outs = []
        for c in range(nchunk):
            outs.append(o_intra[c][h] + lax.dot_general(qs[c][:, hl], state.astype(BF16), nt,
                                                        preferred_element_type=F32))
            state = state * decay[c][:, hl] + kvs[c][h]
        st_ref[h] = state
        o = outs[0] if nchunk == 1 else jnp.concatenate(outs, axis=0)
        ms = jnp.mean(o * o, axis=1, keepdims=True)
        cols.append(o * lax.rsqrt(ms + NORM_EPS))
    o = cols[0] if hp == 1 else jnp.concatenate(cols, axis=1)
    y = o * gain * jax.nn.sigmoid(og_ref[...].astype(F32))
    y_ref[...] = (y * _silu(z_ref[...].astype(F32))).astype(BF16)

    @pl.when(t == pl.num_programs(1) - 1)
    def _():
        for h in range(hp):
            sf_ref[h] = st_ref[h].T


def _hgrn(rest3, lb, gain, s0, blk, col0):
    b, t, _ = rest3.shape
    width = lb.shape[1]
    heads = width // B_KEY_DIM
    tt = min(t, HGRN_ROWS)
    assert t % tt == 0 and tt % blk == 0
    has_s0 = s0 is not None
    kern = functools.partial(_hgrn_kernel, blk=blk, nchunk=tt // blk, hp=heads, has_s0=has_s0)
    window = pl.BlockSpec(
        (pl.Element(1), pl.Element(tt), pl.Element(5 * width)),
        lambda bi, ti: (bi, pl.multiple_of(ti * tt, tt), col0 * B_KEY_DIM))
    vecspec = pl.BlockSpec((1, width), lambda bi, ti: (0, 0))
    stspec = pl.BlockSpec((None, heads, B_KEY_DIM, B_KEY_DIM), lambda bi, ti: (bi, 0, 0, 0))
    in_specs = [window, vecspec, vecspec]
    args = [rest3, lb, gain]
    if has_s0:
        in_specs.append(stspec)
        args.append(s0)
    return pl.pallas_call(
        kern,
        grid=(b, t // tt),
        in_specs=in_specs,
        out_specs=[
            pl.BlockSpec((None, tt, width), lambda bi, ti: (bi, ti, 0)),
            stspec,
        ],
        out_shape=[
            jax.ShapeDtypeStruct((b, t, width), BF16),
            jax.ShapeDtypeStruct((b, heads, B_KEY_DIM, B_KEY_DIM), F32),
        ],
        scratch_shapes=[pltpu.VMEM((heads, B_KEY_DIM, B_KEY_DIM), F32)],
        compiler_params=_cparams(("parallel", "arbitrary")),
        name="hgrn",
    )(*args)


def _memattn_kernel(q_ref, z_ref, mk_ref, mv_ref, o_ref, kb_ref, vb_ref, *, heads):
    @pl.when(pl.program_id(1) == 0)
    def _():
        kb_ref[...] = mk_ref[...].astype(BF16)
        vb_ref[...] = mv_ref[...].astype(BF16)

    nt = (((1,), (1,)), ((), ()))
    cols = [slice(hd * C_HEAD_DIM, (hd + 1) * C_HEAD_DIM) for hd in range(heads)]
    s = [lax.dot_general(q_ref[:, cs], kb_ref[:, cs], nt, preferred_element_type=F32)
         * (C_HEAD_DIM ** -0.5) for cs in cols]
    e = [jnp.exp(sh - jnp.max(sh, axis=1, keepdims=True)) for sh in s]
    o = [jnp.dot(eh.astype(BF16), vb_ref[:, cs], preferred_element_type=F32)
         / jnp.sum(eh, axis=1, keepdims=True) for eh, cs in zip(e, cols)]
    for oh, cs in zip(o, cols):
        o_ref[:, cs] = (oh * _silu(z_ref[:, cs].astype(F32))).astype(BF16)


def _memattn(rest3, mk3, mv3, qblock, zblock):
    b, t, _ = rest3.shape
    _, m, c = mk3.shape
    tq = min(t, MEMATTN_ROWS)
    assert t % tq == 0
    kern = functools.partial(_memattn_kernel, heads=c // C_HEAD_DIM)
    memspec = pl.BlockSpec((None, m, c), lambda bi, ti: (bi, 0, 0))
    return pl.pallas_call(
        kern,
        grid=(b, t // tq),
        in_specs=[
            pl.BlockSpec((None, tq, c), lambda bi, ti: (bi, ti, qblock)),
            pl.BlockSpec((None, tq, c), lambda bi, ti: (bi, ti, zblock)),
            memspec, memspec,
        ],
        out_specs=pl.BlockSpec((None, tq, c), lambda bi, ti: (bi, ti, 0)),
        out_shape=jax.ShapeDtypeStruct((b, t, c), BF16),
        scratch_shapes=[pltpu.VMEM((m, c), BF16), pltpu.VMEM((m, c), BF16)],
        compiler_params=_cparams(("parallel", "arbitrary")),
        name="memattn",
    )(rest3, rest3, mk3, mv3)


def _merge_kernel(ya_ref, yb_ref, yc_ref, g_ref, x_ref,
                  wa_ref, wb_ref, wc_ref, wo_ref, lng_ref, lnb_ref, o_ref, *, alpha):
    d = x_ref.shape[1]
    ga_ref, gb_ref, gc_ref = (g_ref.at[:, k * d:(k + 1) * d] for k in range(3))

    def branch(y_ref, w_ref, g_ref):
        return jax.nn.sigmoid(g_ref[...].astype(F32)) * jnp.dot(
            y_ref[...], w_ref[...], preferred_element_type=F32)

    merged = branch(ya_ref, wa_ref, ga_ref) + branch(yb_ref, wb_ref, gb_ref)
    merged = merged + branch(yc_ref, wc_ref, gc_ref)
    sub = jnp.dot(merged.astype(BF16), wo_ref[...], preferred_element_type=F32)
    hres = alpha * x_ref[...] + sub
    mu = jnp.mean(hres, axis=1, keepdims=True)
    cen = hres - mu
    var = jnp.mean(cen * cen, axis=1, keepdims=True)
    o_ref[...] = cen * lax.rsqrt(var + NORM_EPS) * lng_ref[...] + lnb_ref[...]


def _merge(ya, yb, yc, rest2, x2, wa, wb, wc, wo, lng, lnb, gate_block0, alpha):
    n, d = x2.shape
    w = ya.shape[1]
    tm = min(n, MERGE_ROWS)
    assert n % tm == 0
    rows = lambda width, blk: pl.BlockSpec((tm, width), lambda i: (i, blk))
    const = lambda shape: pl.BlockSpec(shape, lambda i: (0, 0), pipeline_mode=pl.Buffered(1))
    return pl.pallas_call(
        functools.partial(_merge_kernel, alpha=alpha),
        grid=(n // tm,),
        in_specs=[
            rows(w, 0), rows(w, 0), rows(w, 0),
            pl.BlockSpec((pl.Element(tm), pl.Element(3 * d)),
                         lambda i: (pl.multiple_of(i * tm, tm), gate_block0 * d)),
            rows(d, 0),
            const((w, d)), const((w, d)), const((w, d)), const((d, d)),
            const((1, d)), const((1, d)),
        ],
        out_specs=rows(d, 0),
        out_shape=jax.ShapeDtypeStruct((n, d), F32),
        compiler_params=_cparams(("parallel",)),
        name="merge",
    )(ya, yb, yc, rest2, x2, wa, wb, wc, wo, lng, lnb)


def _layer(x, pos, past_k, past_v, s0, mk3, mv3, rec_block, layer_idx, lb, p):
    (w_in, lam, sub_norm, hgrn_gain, wa, wb, wc, wo, lng, lnb, alpha) = p
    n_b, t, d = x.shape
    a_width = wa.shape[0]
    heads = a_width // A_V_DIM
    x2 = x.reshape(n_b * t, d)
    prompt = past_k is None
    q_scale = A_HEAD_DIM ** -0.5 * math.log2(math.e)
    q2, k_out, v2, rest2 = _inproj(x2, w_in, _rope_tables(pos), t, a_width, q_scale,
                                   k_transposed=prompt)
    r = rest2.shape[1]
    rest3 = rest2.reshape(n_b, t, r)
    q3 = q2.reshape(n_b, t, a_width)
    v3 = v2.reshape(n_b, t, a_width)
    lam_init = 0.8 - 0.6 * math.exp(-0.3 * layer_idx)
    if prompt:
        ya = _attn_prompt(lam, q3, k_out, v3, rest3, sub_norm, lam_init)
        k5 = k_out.reshape(n_b, heads, 2, A_HEAD_DIM, t).transpose(0, 4, 1, 2, 3)
    else:
        k3 = k_out.reshape(n_b, t, a_width)
        ya = _attn_sample(lam, q3, past_k, past_v, k3, v3, rest3,
                          sub_norm, lam_init)
        k5 = k3.reshape(n_b, t, heads, 2, A_HEAD_DIM)
    yb, s_fin = _hgrn(rest3, lb, hgrn_gain, s0, rec_block, a_width // B_KEY_DIM)
    c_width = mk3.shape[2]
    c_off = a_width + 5 * lb.shape[1]
    assert c_off % c_width == 0
    yc = _memattn(rest3, mk3, mv3, c_off // c_width, c_off // c_width + 1)
    g_off = c_off + 2 * c_width
    assert g_off % d == 0
    y2 = _merge(ya.reshape(n_b * t, a_width), yb.reshape(n_b * t, -1), yc.reshape(n_b * t, c_width),
                rest2, x2, wa, wb, wc, wo, lng, lnb, g_off // d, alpha)
    return y2.reshape(n_b, t, d), k5, v3, s_fin


def kernel(x_prompt, x_sample, cache_attn_k, cache_attn_v, state_hgrn, cache_mem_k, cache_mem_v, mem_prompt, w_in, lambda_q1, lambda_k1, lambda_q2, lambda_k2, attn_sub_norm, hgrn_lb_logits, hgrn_norm, w_mem_k, w_mem_v, w_branch_a, w_branch_b, w_branch_c, w_out, ln_gamma, ln_beta):
    bp, seq, d = x_prompt.shape
    bs, t_new, _ = x_sample.shape
    depth = w_in.shape[0]
    past = cache_attn_k.shape[2]
    heads = cache_attn_k.shape[3]
    a_width = heads * A_V_DIM
    n_mem = mem_prompt.shape[1]
    c_heads = cache_mem_k.shape[3]
    c_width = c_heads * C_HEAD_DIM
    b_heads = state_hgrn.shape[2]
    alpha = (2 * depth) ** 0.25
    pos_prompt = jnp.arange(seq)
    pos_sample = past + jnp.arange(t_new)
    lower_bounds = jnp.cumsum(jax.nn.softmax(hgrn_lb_logits.astype(F32), axis=0), axis=0)

    h_p, h_s = x_prompt, x_sample
    outs = [[] for _ in range(8)]
    for l in range(depth):
        lam_init = 0.8 - 0.6 * math.exp(-0.3 * l)
        lam = (jnp.exp(jnp.sum(lambda_q1[l].astype(F32) * lambda_k1[l].astype(F32)))
               - jnp.exp(jnp.sum(lambda_q2[l].astype(F32) * lambda_k2[l].astype(F32))) + lam_init)
        params = (w_in[l].astype(BF16), lam.reshape(1), attn_sub_norm[l].reshape(1, -1),
                  hgrn_norm[l].reshape(1, -1), w_branch_a[l].astype(BF16), w_branch_b[l].astype(BF16),
                  w_branch_c[l].astype(BF16), w_out[l].astype(BF16), ln_gamma[l].reshape(1, -1),
                  ln_beta[l].reshape(1, -1), alpha)
        lb = lower_bounds[l].reshape(1, -1)
        mk_p, mv_p = _memkv(mem_prompt.reshape(bp * n_mem, d), w_mem_k[l].astype(BF16),
                            w_mem_v[l].astype(BF16))
        mk_p = mk_p.reshape(bp, n_mem, c_width)
        mv_p = mv_p.reshape(bp, n_mem, c_width)
        h_p, k_p, v_p, s_p = _layer(h_p, pos_prompt, None, None, None, mk_p, mv_p, CHUNK, l, lb, params)
        h_s, k_s, v_s, s_s = _layer(
            h_s, pos_sample,
            cache_attn_k[l].transpose(0, 2, 3, 4, 1).reshape(bs, a_width, past),
            cache_attn_v[l].reshape(bs, past * heads, A_V_DIM),
            state_hgrn[l], cache_mem_k[l].reshape(bs, n_mem, c_width),
            cache_mem_v[l].reshape(bs, n_mem, c_width), t_new, l, lb, params)
        new = (k_p, v_p.reshape(bp, seq, heads, A_V_DIM),
               s_p.astype(x_prompt.dtype), mk_p.reshape(bp, n_mem, c_heads, C_HEAD_DIM),
               mv_p.reshape(bp, n_mem, c_heads, C_HEAD_DIM),
               k_s, v_s.reshape(bs, t_new, heads, A_V_DIM),
               s_s.astype(x_sample.dtype))
        for acc, val in zip(outs, new):
            acc.append(val)
    return (h_p, h_s) + tuple(jnp.stack(o) for o in outs)
```

```python
import functools
import math

import jax
import jax.numpy as jnp
from jax import lax
from jax.experimental import pallas as pl
from jax.experimental.pallas import tpu as pltpu

F32 = jnp.float32
BF16 = jnp.bfloat16

CHUNK = 64
A_HEAD_DIM = 64
A_V_DIM = 128
ROT_DIM = 16
ROPE_THETA = 500000.0
B_KEY_DIM = 128
C_HEAD_DIM = 256
NORM_EPS = 1e-5
LANES = 128
VMEM_LIMIT = 56 * 1024 * 1024
INPROJ_ROWS = 2048
QKV_ROWS = 512
INPROJ_COLS = 512
REST_COLS = 2048
MEMKV_ROWS = 512
HGRN_ROWS = 512
MEMATTN_ROWS = 1024
MERGE_ROWS = 256
SAMPLE_KEYS = 2048
ATTN_TILE = 512
ATTN_STRIP = 32
PV_SPLIT = 4
ROW_CHUNKS = 4
NEXT_SCORES_AT = 3


def _cparams(sem):
    return pltpu.CompilerParams(dimension_semantics=sem, vmem_limit_bytes=VMEM_LIMIT)


def _silu(z):
    return z * jax.nn.sigmoid(z)


def _rope_tables(pos):
    half = ROT_DIM // 2
    inv_freq = jnp.power(ROPE_THETA, -jnp.arange(0, ROT_DIM, 2, dtype=F32) / ROT_DIM)
    ang = pos.astype(F32)[:, None] * inv_freq[None, :]
    cos, sin = jnp.cos(ang), jnp.sin(ang)
    t = pos.shape[0]
    pad = jnp.zeros((t, A_HEAD_DIM - ROT_DIM), F32)
    zero = jnp.zeros((t, half), F32)
    c64 = jnp.concatenate([cos, cos, pad + 1.0], axis=1)
    sa64 = jnp.concatenate([-sin, zero, pad], axis=1)
    sb64 = jnp.concatenate([zero, sin, pad], axis=1)
    rep = LANES // A_HEAD_DIM
    rows = tuple(jnp.tile(a, (1, rep)) for a in (c64, sa64, sb64))
    return rows, (cos.T, sin.T)


def _inproj_kernel(x_ref, w_ref, c_ref, sa_ref, sb_ref, ct_ref, st_ref,
                   q_ref, k_ref, v_ref, xb_ref, t_ref, *, tn, a_width, q_scale, k_transposed):
    half = ROT_DIM // 2
    xb_ref[...] = x_ref[...].astype(BF16)

    def project(col0):
        return jnp.dot(xb_ref[...], w_ref[:, col0:col0 + tn], preferred_element_type=F32)

    def rope_group(acc, g):
        xg = acc[:, g * LANES:(g + 1) * LANES]
        up = pltpu.roll(xg, LANES - half, 1)
        dn = pltpu.roll(xg, half, 1)
        return xg * c_ref[...] + up * sa_ref[...] + dn * sb_ref[...]

    for c0 in range(0, a_width, tn):
        acc = project(c0)
        for g in range(tn // LANES):
            cs = slice(c0 + g * LANES, c0 + (g + 1) * LANES)
            q_ref[:, cs] = (rope_group(acc, g) * q_scale).astype(BF16)

    for c0 in range(0, a_width, tn):
        if k_transposed:
            t_ref[c0 // tn] = project(a_width + c0)
            kt = t_ref[c0 // tn].T
            cos, sin = ct_ref[...], st_ref[...]
            for g in range(tn // A_HEAD_DIM):
                r0 = g * A_HEAD_DIM
                lo = kt[r0:r0 + half, :]
                hi = kt[r0 + half:r0 + ROT_DIM, :]
                k_ref[c0 + r0:c0 + r0 + half, :] = lo * cos - hi * sin
                k_ref[c0 + r0 + half:c0 + r0 + ROT_DIM, :] = hi * cos + lo * sin
                k_ref[c0 + r0 + ROT_DIM:c0 + r0 + A_HEAD_DIM, :] = kt[r0 + ROT_DIM:r0 + A_HEAD_DIM, :]
        else:
            acc = project(a_width + c0)
            for g in range(tn // LANES):
                k_ref[:, c0 + g * LANES:c0 + (g + 1) * LANES] = rope_group(acc, g)

    for c0 in range(0, a_width, tn):
        v_ref[:, c0:c0 + tn] = project(2 * a_width + c0)


def _proj_rest_kernel(xb_ref, w_ref, r_ref):
    tm = xb_ref.shape[0]
    for r in range(0, tm, tm // ROW_CHUNKS):
        rows = slice(r, r + tm // ROW_CHUNKS)
        r_ref[rows, :] = jnp.dot(xb_ref[rows, :], w_ref[...], preferred_element_type=F32).astype(BF16)


def _proj_rest(xb2, w_bf, col0):
    n, d = xb2.shape
    ncols = w_bf.shape[1] - col0
    tm = min(n, INPROJ_ROWS)
    tn = REST_COLS
    assert ncols % tn == 0 and col0 % LANES == 0 and n % tm == 0
    return pl.pallas_call(
        _proj_rest_kernel,
        grid=(n // tm, ncols // tn),
        in_specs=[
            pl.BlockSpec((tm, d), lambda i, j: (i, 0)),
            pl.BlockSpec((pl.Element(d), pl.Element(tn)),
                         lambda i, j: (0, pl.multiple_of(col0 + j * tn, LANES))),
        ],
        out_specs=pl.BlockSpec((tm, tn), lambda i, j: (i, j)),
        out_shape=jax.ShapeDtypeStruct((n, ncols), BF16),
        compiler_params=_cparams(("parallel", "arbitrary")),
        name="proj_rest",
    )(xb2, w_bf)


def _inproj(x2, w_bf, tables, t_len, a_width, q_scale, k_transposed):
    n, d = x2.shape
    n_in = 3 * a_width
    tm = min(n, QKV_ROWS)
    tn = INPROJ_COLS
    assert n % tm == 0 and a_width % tn == 0
    row_tabs, col_tabs = tables
    if t_len >= tm:
        assert t_len % tm == 0
        per = t_len // tm
    else:
        assert tm % t_len == 0 and not k_transposed
        row_tabs = tuple(jnp.tile(a, (tm // t_len, 1)) for a in row_tabs)
        col_tabs = tuple(jnp.tile(a, (1, tm // t_len)) for a in col_tabs)
        per = 1
    row_spec = pl.BlockSpec((tm, LANES), lambda i: (i % per, 0))
    col_spec = pl.BlockSpec((ROT_DIM // 2, tm), lambda i: (0, i % per))
    rows = pl.BlockSpec((tm, a_width), lambda i: (i, 0))
    if k_transposed:
        k_spec = pl.BlockSpec((None, a_width, tm), lambda i: (i // per, 0, i % per))
        k_shape = jax.ShapeDtypeStruct((n // t_len, a_width, t_len), F32)
    else:
        k_spec = rows
        k_shape = jax.ShapeDtypeStruct((n, a_width), F32)
    kern = functools.partial(_inproj_kernel, tn=tn, a_width=a_width, q_scale=q_scale,
                             k_transposed=k_transposed)
    q2, k_out, v2, xb2 = pl.pallas_call(
        kern,
        grid=(n // tm,),
        in_specs=[
            pl.BlockSpec((tm, d), lambda i: (i, 0)),
            pl.BlockSpec((d, n_in), lambda i: (0, 0), pipeline_mode=pl.Buffered(1)),
            row_spec, row_spec, row_spec, col_spec, col_spec,
        ],
        out_specs=[rows, k_spec, rows, pl.BlockSpec((tm, d), lambda i: (i, 0))],
        out_shape=[
            jax.ShapeDtypeStruct((n, a_width), BF16),
            k_shape,
            jax.ShapeDtypeStruct((n, a_width), F32),
            jax.ShapeDtypeStruct((n, d), BF16),
        ],
        scratch_shapes=[pltpu.VMEM((a_width // tn, tm, tn), F32)],
        compiler_params=_cparams(("parallel",)),
        name="inproj",
    )(x2, w_bf, *row_tabs, *col_tabs)
    return q2, k_out, v2, _proj_rest(xb2, w_bf, n_in)


def _memkv_kernel(x_ref, wk_ref, wv_ref, k_ref, v_ref):
    xb = x_ref[...].astype(BF16)
    k_ref[...] = jnp.dot(xb, wk_ref[...], preferred_element_type=F32)
    v_ref[...] = jnp.dot(xb, wv_ref[...], preferred_element_type=F32)


def _memkv(x2, wk_bf, wv_bf):
    n, d = x2.shape
    c = wk_bf.shape[1]
    tm = min(n, MEMKV_ROWS)
    assert n % tm == 0
    wspec = pl.BlockSpec((d, c), lambda i: (0, 0))
    ospec = pl.BlockSpec((tm, c), lambda i: (i, 0))
    return pl.pallas_call(
        _memkv_kernel,
        grid=(n // tm,),
        in_specs=[pl.BlockSpec((tm, d), lambda i: (i, 0)), wspec, wspec],
        out_specs=[ospec, ospec],
        out_shape=[jax.ShapeDtypeStruct((n, c), F32)] * 2,
        compiler_params=_cparams(("parallel",)),
        name="memkv",
    )(x2, wk_bf, wv_bf)


def _attn_epilogue(o, lam_init, gain, z):
    ms = jnp.mean(o * o, axis=1, keepdims=True)
    y = o * lax.rsqrt(ms + NORM_EPS) * gain * (1.0 - lam_init)
    return y * _silu(z)


def _attn_prompt_kernel(lam_ref, q_ref, kt_ref, v_ref, z_ref, g_ref, o_ref,
                        kb_ref, vb_ref, qa_ref, sa_ref, sb_ref, p_ref, acc_ref, m_ref, al_ref,
                        *, tq, lam_init):
    s_len = q_ref.shape[0]
    strip = ATTN_STRIP
    kb_ref[...] = kt_ref[...].astype(BF16)
    vb_ref[:, :A_V_DIM] = v_ref[...].astype(BF16)
    vb_ref[:, A_V_DIM:] = jnp.ones((s_len, A_V_DIM), BF16)
    lane = lax.broadcasted_iota(jnp.int32, (tq, A_V_DIM), 1)

    def scores(j, s_ref, maps=(0, 1)):
        kt = kb_ref[:, pl.ds(pl.multiple_of(j * tq, tq), tq)]
        for mp in maps:
            s_ref[mp] = jnp.dot(qa_ref[mp], kt, preferred_element_type=F32)

    def softmax_pv(j, s_ref, masked, nxt_ref=None, first=False):
        start = pl.multiple_of(j * tq, tq)
        grp = tq // PV_SPLIT
        if nxt_ref is not None:
            scores(j + 1, nxt_ref, (0,))
        for mp in range(2):
            for g0 in range(0, tq, grp):
                if nxt_ref is not None and mp * tq + g0 == NEXT_SCORES_AT * grp:
                    scores(j + 1, nxt_ref, (1,))
                nk = g0 + grp if masked else tq
                for r in range(g0, g0 + grp, strip):
                    rs = slice(r, r + strip)
                    sv = s_ref[mp, rs, :nk]
                    if masked:
                        cc = lax.broadcasted_iota(jnp.int32, (strip, nk), 1) // CHUNK
                        sv = jnp.where(cc <= r // CHUNK, sv, -jnp.inf)
                    mx = jnp.max(sv, axis=1, keepdims=True)
                    if first:
                        mn = jnp.broadcast_to(mx, (strip, LANES))
                    else:
                        m_old = m_ref[mp, rs, :]
                        mn = jnp.maximum(m_old, mx)
                        al_ref[mp, rs, :] = jnp.exp2(m_old - mn)
                    m_ref[mp, rs, :] = mn
                    for c in range(nk // LANES):
                        cs = slice(c * LANES, (c + 1) * LANES)
                        p_ref[mp, rs, cs] = jnp.exp2(sv[:, cs] - mn).astype(BF16)
                gs = slice(g0, g0 + grp)
                pv = jnp.dot(p_ref[mp, gs, :nk], vb_ref[pl.ds(start, nk), :],
                             preferred_element_type=F32)
                for c in range(2):
                    cs = slice(c * A_V_DIM, (c + 1) * A_V_DIM)
                    if first:
                        acc_ref[mp, gs, cs] = pv[:, cs]
                    else:
                        acc_ref[mp, gs, cs] = acc_ref[mp, gs, cs] * al_ref[mp, gs, :] + pv[:, cs]

    def first_scores(qi):
        qf = q_ref[pl.ds(pl.multiple_of(qi * tq, tq), tq), :].astype(F32)
        qa_ref[0] = jnp.where(lane < A_HEAD_DIM, qf, 0.0).astype(BF16)
        qa_ref[1] = jnp.where(lane >= A_HEAD_DIM, qf, 0.0).astype(BF16)
        scores(0, sa_ref)

    def q_tile(qi, _):
        rows = pl.ds(pl.multiple_of(qi * tq, tq), tq)

        @pl.when(qi > 0)
        def _():
            softmax_pv(0, sa_ref, False, sb_ref, first=True)

        @pl.when(qi == 0)
        def _():
            softmax_pv(0, sa_ref, True, first=True)

        def full_tile(j, carry):
            @pl.when(j % 2 == 0)
            def _():
                softmax_pv(j, sa_ref, False, sb_ref)

            @pl.when(j % 2 == 1)
            def _():
                softmax_pv(j, sb_ref, False, sa_ref)

            return carry

        lax.fori_loop(1, qi, full_tile, 0)

        @pl.when((qi > 0) & (qi % 2 == 0))
        def _():
            softmax_pv(qi, sa_ref, True)

        @pl.when(qi % 2 == 1)
        def _():
            softmax_pv(qi, sb_ref, True)

        first_scores(jnp.minimum(qi + 1, nq - 1))
        o = (acc_ref[0, :, :A_V_DIM] / acc_ref[0, :, A_V_DIM:]
             - lam_ref[0] * (acc_ref[1, :, :A_V_DIM] / acc_ref[1, :, A_V_DIM:]))
        y = _attn_epilogue(o, lam_init, g_ref[...], z_ref[rows, :].astype(F32))
        o_ref[rows, :] = y.astype(BF16)
        return 0

    nq = s_len // tq
    first_scores(0)
    lax.fori_loop(0, nq, q_tile, 0)


def _attn_prompt(lam, q3, kt3, v3, rest3, gain, lam_init):
    b, s, w = q3.shape
    h = w // A_V_DIM
    tq = min(s, ATTN_TILE)
    assert s % tq == 0 and tq % (PV_SPLIT * LANES) == 0
    kern = functools.partial(_attn_prompt_kernel, tq=tq, lam_init=lam_init)
    qspec = pl.BlockSpec((None, s, A_V_DIM), lambda bi, hi: (bi, 0, hi))
    return pl.pallas_call(
        kern,
        grid=(b, h),
        in_specs=[
            pl.BlockSpec(memory_space=pltpu.SMEM),
            qspec,
            pl.BlockSpec((None, A_V_DIM, s), lambda bi, hi: (bi, hi, 0)),
            qspec, qspec,
            pl.BlockSpec((1, A_V_DIM), lambda bi, hi: (0, 0)),
        ],
        out_specs=qspec,
        out_shape=jax.ShapeDtypeStruct((b, s, w), BF16),
        scratch_shapes=[
            pltpu.VMEM((A_V_DIM, s), BF16),
            pltpu.VMEM((s, 2 * A_V_DIM), BF16),
            pltpu.VMEM((2, tq, A_V_DIM), BF16),
            pltpu.VMEM((2, tq, tq), F32),
            pltpu.VMEM((2, tq, tq), F32),
            pltpu.VMEM((2, tq, tq), BF16),
            pltpu.VMEM((2, tq, 2 * A_V_DIM), F32),
            pltpu.VMEM((2, tq, LANES), F32),
            pltpu.VMEM((2, tq, LANES), F32),
        ],
        compiler_params=_cparams(("parallel", "parallel")),
        name="attn_prompt",
    )(lam, q3, kt3, v3, rest3, gain)


def _attn_sample_kernel(lam_ref, q_ref, kc_ref, vc_ref, kn_ref, vn_ref, z_ref, g_ref, o_ref,
                        qbd_ref, s_ref, w_ref, acc_ref, *, tk, nkt, t_new, heads, lam_init):
    j = pl.program_id(1)
    past = tk * nkt
    nt = (((1,), (1,)), ((), ()))
    width = kn_ref.shape[1]

    @pl.when(j == 0)
    def _():
        lane = lax.broadcasted_iota(jnp.int32, (t_new, A_V_DIM), 1)
        zero = jnp.zeros((t_new, A_V_DIM), BF16)
        for hd in range(heads):
            qh = q_ref[:, hd * A_V_DIM:(hd + 1) * A_V_DIM].astype(F32)
            for mp in range(2):
                r0 = (2 * hd + mp) * t_new
                mine = lane >= A_HEAD_DIM if mp == 1 else lane < A_HEAD_DIM
                own = jnp.where(mine, qh, 0.0).astype(BF16)
                for hb in range(heads):
                    qbd_ref[r0:r0 + t_new, hb * A_V_DIM:(hb + 1) * A_V_DIM] = own if hb == hd else zero

    def pad_new(ref):
        new = ref[...]
        return jnp.concatenate([new, jnp.zeros((LANES - t_new, width), F32)], axis=0).astype(BF16)

    @pl.when(j < nkt)
    def _():
        s = jnp.dot(qbd_ref[...], kc_ref[...].astype(BF16), preferred_element_type=F32)
        s_ref[:, pl.ds(pl.multiple_of(j * tk, tk), tk)] = s

    @pl.when(j == nkt - 1)
    def _():
        sn = lax.dot_general(qbd_ref[...], pad_new(kn_ref), nt, preferred_element_type=F32)
        col = lax.broadcasted_iota(jnp.int32, sn.shape, 1)
        s_ref[:, past:past + LANES] = jnp.where(col < t_new, sn, -jnp.inf)
        lam = lam_ref[0]
        for hd in range(heads):
            r0 = hd * 2 * t_new
            p = []
            for mp in range(2):
                s = s_ref[r0 + mp * t_new:r0 + (mp + 1) * t_new, :]
                e = jnp.exp2(s - jnp.max(s, axis=1, keepdims=True))
                p.append(e / jnp.sum(e, axis=1, keepdims=True))
            w_ref[hd * t_new:(hd + 1) * t_new, :] = (p[0] - lam * p[1]).astype(BF16)

    @pl.when(j == nkt)
    def _():
        full = jnp.dot(w_ref[:, past:past + LANES], pad_new(vn_ref), preferred_element_type=F32)
        for hd in range(heads):
            rs = slice(hd * t_new, (hd + 1) * t_new)
            acc_ref[rs, :] = full[rs, hd * A_V_DIM:(hd + 1) * A_V_DIM]

    @pl.when(j >= nkt)
    def _():
        start = pl.multiple_of((j - nkt) * tk, tk)
        for hd in range(heads):
            rs = slice(hd * t_new, (hd + 1) * t_new)
            vh = vc_ref[pl.ds(hd, tk, stride=heads), :].astype(BF16)
            acc_ref[rs, :] += jnp.dot(w_ref[rs, pl.ds(start, tk)], vh, preferred_element_type=F32)

    @pl.when(j == 2 * nkt - 1)
    def _():
        for hd in range(heads):
            cs = slice(hd * A_V_DIM, (hd + 1) * A_V_DIM)
            o = acc_ref[hd * t_new:(hd + 1) * t_new, :]
            y = _attn_epilogue(o, lam_init, g_ref[...], z_ref[:, cs].astype(F32))
            o_ref[:, cs] = y.astype(BF16)


def _attn_sample(lam, q3, kct, vc3, kn, vn, rest3, gain, lam_init):
    b, w, past = kct.shape
    t_new = kn.shape[1]
    heads = w // A_V_DIM
    rows = 2 * heads * t_new
    tk = min(past, SAMPLE_KEYS)
    assert past % tk == 0 and t_new % 16 == 0 and t_new <= LANES
    nkt = past // tk
    kern = functools.partial(_attn_sample_kernel, tk=tk, nkt=nkt, t_new=t_new, heads=heads,
                             lam_init=lam_init)
    newspec = pl.BlockSpec((None, t_new, w), lambda bi, j: (bi, 0, 0))
    return pl.pallas_call(
        kern,
        grid=(b, 2 * nkt),
        in_specs=[
            pl.BlockSpec(memory_space=pltpu.SMEM),
            newspec,
            pl.BlockSpec((None, w, tk), lambda bi, j: (bi, 0, jnp.minimum(j, nkt - 1))),
            pl.BlockSpec((None, tk * heads, A_V_DIM),
                         lambda bi, j: (bi, jnp.maximum(j - nkt, 0), 0)),
            newspec, newspec, newspec,
            pl.BlockSpec((1, A_V_DIM), lambda bi, j: (0, 0)),
        ],
        out_specs=newspec,
        out_shape=jax.ShapeDtypeStruct((b, t_new, w), BF16),
        scratch_shapes=[
            pltpu.VMEM((rows, w), BF16),
            pltpu.VMEM((rows, past + LANES), F32),
            pltpu.VMEM((rows // 2, past + LANES), BF16),
            pltpu.VMEM((rows // 2, A_V_DIM), F32),
        ],
        compiler_params=_cparams(("parallel", "arbitrary")),
        name="attn_sample",
    )(lam, q3, kct, vc3, kn, vn, rest3, gain)


def _hgrn_kernel(*refs, blk, nchunk, hp, has_s0):
    if has_s0:
        c_ref, lb_ref, gain_ref, s0_ref, y_ref, sf_ref, st_ref = refs
    else:
        c_ref, lb_ref, gain_ref, y_ref, sf_ref, st_ref = refs
    width = hp * B_KEY_DIM
    q_ref, f_ref, i_ref, og_ref, z_ref = (c_ref.at[0, :, k * width:(k + 1) * width] for k in range(5))
    t = pl.program_id(1)

    @pl.when(t == 0)
    def _():
        for h in range(hp):
            st_ref[h] = s0_ref[h].T if has_s0 else jnp.zeros(st_ref.shape[1:], F32)

    lb = lb_ref[...]
    gain = gain_ref[...]
    row = lax.broadcasted_iota(jnp.int32, (blk, blk), 0)
    col = lax.broadcasted_iota(jnp.int32, (blk, blk), 1)
    causal = col <= row
    tril = jnp.where(causal, 1.0, 0.0).astype(BF16)
    mid = (blk - 1) // 2
    nt = (((1,), (1,)), ((), ()))
    tn = (((0,), (0,)), ((), ()))

    def split3(g):
        hi = g.astype(BF16)
        r1 = g - hi.astype(F32)
        md = r1.astype(BF16)
        lo = (r1 - md.astype(F32)).astype(BF16)
        return hi, md, lo

    chunks = [slice(c * blk, (c + 1) * blk) for c in range(nchunk)]
    f = lb + (1.0 - lb) * jax.nn.sigmoid(f_ref[...].astype(F32))
    parts = split3(jnp.log(f))
    qv = _silu(q_ref[...].astype(F32))
    kv = 1.0 - f
    vv = i_ref[...]
    b = [sum(jnp.dot(tril, p[sl, :], preferred_element_type=F32) for p in parts) for sl in chunks]
    qe, ke, qs, ks, decay = [], [], [], [], []
    for sl, bc in zip(chunks, b):
        b_mid = bc[mid:mid + 1, :]
        b_last = bc[blk - 1:blk, :]
        decay.append(jnp.exp(b_last))
        qe.append((qv[sl, :] * jnp.exp(bc - b_mid)).astype(BF16))
        ke.append((kv[sl, :] * jnp.exp(b_mid - bc)).astype(BF16))
        qs.append((qv[sl, :] * jnp.exp(bc)).astype(BF16))
        ks.append((kv[sl, :] * jnp.exp(b_last - bc)).astype(BF16))
    lanes = [slice(h * B_KEY_DIM, (h + 1) * B_KEY_DIM) for h in range(hp)]
    o_intra, kvs = [], []
    for c, sl in enumerate(chunks):
        sc = [lax.dot_general(qe[c][:, hl], ke[c][:, hl], nt, preferred_element_type=F32)
              for hl in lanes]
        sc = [jnp.where(causal, s, 0.0).astype(BF16) for s in sc]
        o_intra.append([jnp.dot(s, vv[sl, hl], preferred_element_type=F32)
                        for s, hl in zip(sc, lanes)])
        kvs.append([lax.dot_general(vv[sl, hl], ks[c][:, hl], tn, preferred_element_type=F32)
                    for hl in lanes])
    cols = []
    for h, hl in enumerate(lanes):
        state = st_ref[h]
        outs = []
        for c in range(nchunk):
            outs.append(o_intra[c][h] + lax.dot_general(qs[c][:, hl], state.astype(BF16), nt,
                                                        preferred_element_type=F32))
            state = state * decay[c][:, hl] + kvs[c][h]
        st_ref[h] = state
        o = outs[0] if nchunk == 1 else jnp.concatenate(outs, axis=0)
        ms = jnp.mean(o * o, axis=1, keepdims=True)
        cols.append(o * lax.rsqrt(ms + NORM_EPS))
    o = cols[0] if hp == 1 else jnp.concatenate(cols, axis=1)
    y = o * gain * jax.nn.sigmoid(og_ref[...].astype(F32))
    y_ref[...] = (y * _silu(z_ref[...].astype(F32))).astype(BF16)

    @pl.when(t == pl.num_programs(1) - 1)
    def _():
        for h in range(hp):
            sf_ref[h] = st_ref[h].T


def _hgrn(rest3, lb, gain, s0, blk, col0):
    b, t, _ = rest3.shape
    width = lb.shape[1]
    heads = width // B_KEY_DIM
    tt = min(t, HGRN_ROWS)
    assert t % tt == 0 and tt % blk == 0
    has_s0 = s0 is not None
    kern = functools.partial(_hgrn_kernel, blk=blk, nchunk=tt // blk, hp=heads, has_s0=has_s0)
    window = pl.BlockSpec(
        (pl.Element(1), pl.Element(tt), pl.Element(5 * width)),
        lambda bi, ti: (bi, pl.multiple_of(ti * tt, tt), col0 * B_KEY_DIM))
    vecspec = pl.BlockSpec((1, width), lambda bi, ti: (0, 0))
    stspec = pl.BlockSpec((None, heads, B_KEY_DIM, B_KEY_DIM), lambda bi, ti: (bi, 0, 0, 0))
    in_specs = [window, vecspec, vecspec]
    args = [rest3, lb, gain]
    if has_s0:
        in_specs.append(stspec)
        args.append(s0)
    return pl.pallas_call(
        kern,
        grid=(b, t // tt),
        in_specs=in_specs,
        out_specs=[
            pl.BlockSpec((None, tt, width), lambda bi, ti: (bi, ti, 0)),
            stspec,
        ],
        out_shape=[
            jax.ShapeDtypeStruct((b, t, width), BF16),
            jax.ShapeDtypeStruct((b, heads, B_KEY_DIM, B_KEY_DIM), F32),
        ],
        scratch_shapes=[pltpu.VMEM((heads, B_KEY_DIM, B_KEY_DIM), F32)],
        compiler_params=_cparams(("parallel", "arbitrary")),
        name="hgrn",
    )(*args)


def _memattn_kernel(q_ref, z_ref, mk_ref, mv_ref, o_ref, kb_ref, vb_ref, *, heads):
    @pl.when(pl.program_id(1) == 0)
    def _():
        kb_ref[...] = mk_ref[...].astype(BF16)
        vb_ref[...] = mv_ref[...].astype(BF16)

    nt = (((1,), (1,)), ((), ()))
    cols = [slice(hd * C_HEAD_DIM, (hd + 1) * C_HEAD_DIM) for hd in range(heads)]
    s = [lax.dot_general(q_ref[:, cs], kb_ref[:, cs], nt, preferred_element_type=F32)
         * (C_HEAD_DIM ** -0.5) for cs in cols]
    e = [jnp.exp(sh - jnp.max(sh, axis=1, keepdims=True)) for sh in s]
    o = [jnp.dot(eh.astype(BF16), vb_ref[:, cs], preferred_element_type=F32)
         / jnp.sum(eh, axis=1, keepdims=True) for eh, cs in zip(e, cols)]
    for oh, cs in zip(o, cols):
        o_ref[:, cs] = (oh * _silu(z_ref[:, cs].astype(F32))).astype(BF16)


def _memattn(rest3, mk3, mv3, qblock, zblock):
    b, t, _ = rest3.shape
    _, m, c = mk3.shape
    tq = min(t, MEMATTN_ROWS)
    assert t % tq == 0
    kern = functools.partial(_memattn_kernel, heads=c // C_HEAD_DIM)
    memspec = pl.BlockSpec((None, m, c), lambda bi, ti: (bi, 0, 0))
    return pl.pallas_call(
        kern,
        grid=(b, t // tq),
        in_specs=[
            pl.BlockSpec((None, tq, c), lambda bi, ti: (bi, ti, qblock)),
            pl.BlockSpec((None, tq, c), lambda bi, ti: (bi, ti, zblock)),
            memspec, memspec,
        ],
        out_specs=pl.BlockSpec((None, tq, c), lambda bi, ti: (bi, ti, 0)),
        out_shape=jax.ShapeDtypeStruct((b, t, c), BF16),
        scratch_shapes=[pltpu.VMEM((m, c), BF16), pltpu.VMEM((m, c), BF16)],
        compiler_params=_cparams(("parallel", "arbitrary")),
        name="memattn",
    )(rest3, rest3, mk3, mv3)


def _merge_kernel(ya_ref, yb_ref, yc_ref, g_ref, x_ref,
                  wa_ref, wb_ref, wc_ref, wo_ref, lng_ref, lnb_ref, o_ref, h_ref, *, alpha):
    i = pl.program_id(0)
    d = x_ref.shape[1]
    ga_ref, gb_ref, gc_ref = (g_ref.at[:, k * d:(k + 1) * d] for k in range(3))

    @pl.when(i == 0)
    def _():
        h_ref[1] = jnp.zeros(h_ref.shape[1:], F32)

    def branch(y_ref, w_ref, g_ref):
        return jax.nn.sigmoid(g_ref[...].astype(F32)) * jnp.dot(
            y_ref[...], w_ref[...], preferred_element_type=F32)

    def step(slot):
        hres = h_ref[1 - slot]
        mu = jnp.mean(hres, axis=1, keepdims=True)
        cen = hres - mu
        var = jnp.mean(cen * cen, axis=1, keepdims=True)
        o_ref[...] = cen * lax.rsqrt(var + NORM_EPS) * lng_ref[...] + lnb_ref[...]
        merged = branch(ya_ref, wa_ref, ga_ref) + branch(yb_ref, wb_ref, gb_ref)
        merged = merged + branch(yc_ref, wc_ref, gc_ref)
        sub = jnp.dot(merged.astype(BF16), wo_ref[...], preferred_element_type=F32)
        h_ref[slot] = alpha * x_ref[...] + sub

    for slot in range(2):
        pl.when(i % 2 == slot)(functools.partial(step, slot))


def _merge(ya, yb, yc, rest2, x2, wa, wb, wc, wo, lng, lnb, gate_block0, alpha):
    n, d = x2.shape
    w = ya.shape[1]
    tm = min(n, MERGE_ROWS)
    assert n % tm == 0
    nt = n // tm
    cur = lambda i: jnp.minimum(i, nt - 1)
    rows = lambda width, blk: pl.BlockSpec((tm, width), lambda i: (cur(i), blk))
    const = lambda shape: pl.BlockSpec(shape, lambda i: (0, 0), pipeline_mode=pl.Buffered(1))
    return pl.pallas_call(
        functools.partial(_merge_kernel, alpha=alpha),
        grid=(nt + 1,),
        in_specs=[
            rows(w, 0), rows(w, 0), rows(w, 0),
            pl.BlockSpec((pl.Element(tm), pl.Element(3 * d)),
                         lambda i: (pl.multiple_of(cur(i) * tm, tm), gate_block0 * d)),
            rows(d, 0),
            const((w, d)), const((w, d)), const((w, d)), const((d, d)),
            const((1, d)), const((1, d)),
        ],
        out_specs=pl.BlockSpec((tm, d), lambda i: (jnp.maximum(i - 1, 0), 0)),
        out_shape=jax.ShapeDtypeStruct((n, d), F32),
        scratch_shapes=[pltpu.VMEM((2, tm, d), F32)],
        compiler_params=_cparams(("arbitrary",)),
        name="merge",
    )(ya, yb, yc, rest2, x2, wa, wb, wc, wo, lng, lnb)


def _layer(x, pos, past_k, past_v, s0, mk3, mv3, rec_block, layer_idx, lb, p):
    (w_in, lam, sub_norm, hgrn_gain, wa, wb, wc, wo, lng, lnb, alpha) = p
    n_b, t, d = x.shape
    a_width = wa.shape[0]
    heads = a_width // A_V_DIM
    x2 = x.reshape(n_b * t, d)
    prompt = past_k is None
    q_scale = A_HEAD_DIM ** -0.5 * math.log2(math.e)
    q2, k_out, v2, rest2 = _inproj(x2, w_in, _rope_tables(pos), t, a_width, q_scale,
                                   k_transposed=prompt)
    r = rest2.shape[1]
    rest3 = rest2.reshape(n_b, t, r)
    q3 = q2.reshape(n_b, t, a_width)
    v3 = v2.reshape(n_b, t, a_width)
    lam_init = 0.8 - 0.6 * math.exp(-0.3 * layer_idx)
    if prompt:
        ya = _attn_prompt(lam, q3, k_out, v3, rest3, sub_norm, lam_init)
        k5 = k_out.reshape(n_b, heads, 2, A_HEAD_DIM, t).transpose(0, 4, 1, 2, 3)
    else:
        k3 = k_out.reshape(n_b, t, a_width)
        ya = _attn_sample(lam, q3, past_k, past_v, k3, v3, rest3,
                          sub_norm, lam_init)
        k5 = k3.reshape(n_b, t, heads, 2, A_HEAD_DIM)
    yb, s_fin = _hgrn(rest3, lb, hgrn_gain, s0, rec_block, a_width // B_KEY_DIM)
    c_width = mk3.shape[2]
    c_off = a_width + 5 * lb.shape[1]
    assert c_off % c_width == 0
    yc = _memattn(rest3, mk3, mv3, c_off // c_width, c_off // c_width + 1)
    g_off = c_off + 2 * c_width
    assert g_off % d == 0
    y2 = _merge(ya.reshape(n_b * t, a_width), yb.reshape(n_b * t, -1), yc.reshape(n_b * t, c_width),
                rest2, x2, wa, wb, wc, wo, lng, lnb, g_off // d, alpha)
    return y2.reshape(n_b, t, d), k5, v3, s_fin


def kernel(x_prompt, x_sample, cache_attn_k, cache_attn_v, state_hgrn, cache_mem_k, cache_mem_v, mem_prompt, w_in, lambda_q1, lambda_k1, lambda_q2, lambda_k2, attn_sub_norm, hgrn_lb_logits, hgrn_norm, w_mem_k, w_mem_v, w_branch_a, w_branch_b, w_branch_c, w_out, ln_gamma, ln_beta):
    bp, seq, d = x_prompt.shape
    bs, t_new, _ = x_sample.shape
    depth = w_in.shape[0]
    past = cache_attn_k.shape[2]
    heads = cache_attn_k.shape[3]
    a_width = heads * A_V_DIM
    n_mem = mem_prompt.shape[1]
    c_heads = cache_mem_k.shape[3]
    c_width = c_heads * C_HEAD_DIM
    b_heads = state_hgrn.shape[2]
    alpha = (2 * depth) ** 0.25
    pos_prompt = jnp.arange(seq)
    pos_sample = past + jnp.arange(t_new)
    lower_bounds = jnp.cumsum(jax.nn.softmax(hgrn_lb_logits.astype(F32), axis=0), axis=0)

    h_p, h_s = x_prompt, x_sample
    outs = [[] for _ in range(8)]
    for l in range(depth):
        lam_init = 0.8 - 0.6 * math.exp(-0.3 * l)
        lam = (jnp.exp(jnp.sum(lambda_q1[l].astype(F32) * lambda_k1[l].astype(F32)))
               - jnp.exp(jnp.sum(lambda_q2[l].astype(F32) * lambda_k2[l].astype(F32))) + lam_init)
        params = (w_in[l].astype(BF16), lam.reshape(1), attn_sub_norm[l].reshape(1, -1),
                  hgrn_norm[l].reshape(1, -1), w_branch_a[l].astype(BF16), w_branch_b[l].astype(BF16),
                  w_branch_c[l].astype(BF16), w_out[l].astype(BF16), ln_gamma[l].reshape(1, -1),
                  ln_beta[l].reshape(1, -1), alpha)
        lb = lower_bounds[l].reshape(1, -1)
        mk_p, mv_p = _memkv(mem_prompt.reshape(bp * n_mem, d), w_mem_k[l].astype(BF16),
                            w_mem_v[l].astype(BF16))
        mk_p = mk_p.reshape(bp, n_mem, c_width)
        mv_p = mv_p.reshape(bp, n_mem, c_width)
        h_p, k_p, v_p, s_p = _layer(h_p, pos_prompt, None, None, None, mk_p, mv_p, CHUNK, l, lb, params)
        h_s, k_s, v_s, s_s = _layer(
            h_s, pos_sample,
            cache_attn_k[l].transpose(0, 2, 3, 4, 1).reshape(bs, a_width, past),
            cache_attn_v[l].reshape(bs, past * heads, A_V_DIM),
            state_hgrn[l], cache_mem_k[l].reshape(bs, n_mem, c_width),
            cache_mem_v[l].reshape(bs, n_mem, c_width), t_new, l, lb, params)
        new = (k_p, v_p.reshape(bp, seq, heads, A_V_DIM),
               s_p.astype(x_prompt.dtype), mk_p.reshape(bp, n_mem, c_heads, C_HEAD_DIM),
               mv_p.reshape(bp, n_mem, c_heads, C_HEAD_DIM),
               k_s, v_s.reshape(bs, t_new, heads, A_V_DIM),
               s_s.astype(x_sample.dtype))
        for acc, val in zip(outs, new):
            acc.append(val)
    return (h_p, h_s) + tuple(jnp.stack(o) for o in outs)
```
